```python
import math
import jax
import jax.numpy as jnp
from jax import lax
import numpy as np

D_MODEL = 2048
BATCH = 8
SEQ = 2048
DEPTH = 2
DEC_BATCH = 32
DEC_SEQ = 32
PAST_LEN = 4096

CHUNK = 64
BAND_CHUNKS = 8
BAND_PAST = BAND_CHUNKS * CHUNK
N_MEM = 256
EPS = 1e-6
NEG_INF = -1e30
Q_BLOCK = 128

BR_W = D_MODEL // 2
HD_A = 128
H_A = BR_W // HD_A
REL_CLIP = 128
DIFF_HD = 64
H_B = BR_W // (2 * DIFF_HD)
ROT_DIM = DIFF_HD // 4
ROPE_THETA = 500000.0
H_M = 4
HD_M = BR_W // H_M
N_BRANCH = 3
D_FF = ((8 * D_MODEL // 3 + 127) // 128) * 128
CONV_W = 3
SPLITS = (BR_W, 2 * BR_W, 3 * BR_W, 4 * BR_W, 5 * BR_W, 6 * BR_W, 7 * BR_W)
N_IN = 7 * BR_W + N_BRANCH * D_MODEL

kernel_name = 'hybrid_chunk_stream_encoder_step'


def rms_norm(x, g):
    xf = x.astype(jnp.float32)
    y = xf * lax.rsqrt(jnp.mean(xf * xf, axis=-1, keepdims=True) + EPS)
    return (y * g.astype(jnp.float32)).astype(x.dtype)


def rope_partial(x, pos):
    half = ROT_DIM // 2
    inv_freq = jnp.exp(jnp.arange(half, dtype=jnp.float32) * (-2.0 * math.log(ROPE_THETA) / ROT_DIM))
    ang = pos.astype(jnp.float32)[:, None] * inv_freq[None, :]
    shape = (ang.shape[0],) + (1,) * (x.ndim - 3) + (half,)
    cos = jnp.cos(ang).reshape(shape)
    sin = jnp.sin(ang).reshape(shape)
    xf = x.astype(jnp.float32)
    x1 = xf[..., :half]
    x2 = xf[..., half:ROT_DIM]
    out = jnp.concatenate([x1 * cos - x2 * sin, x2 * cos + x1 * sin, xf[..., ROT_DIM:]], axis=-1)
    return out.astype(x.dtype)


def mixer_inputs(h, pos, P, l):
    B, T, _ = h.shape
    z = h @ P['w_in'][l]
    qa, ka, va, qb, kb, vb, qm, gt = jnp.split(z, SPLITS, axis=-1)
    qa = rms_norm(qa.reshape(B, T, H_A, HD_A), P['a_q_norm_g'][l])
    ka = rms_norm(ka.reshape(B, T, H_A, HD_A), P['a_k_norm_g'][l])
    va = va.reshape(B, T, H_A, HD_A)
    qb = rope_partial(rms_norm(qb.reshape(B, T, H_B, 2, DIFF_HD), P['b_q_norm_g'][l]), pos)
    kb = rope_partial(rms_norm(kb.reshape(B, T, H_B, 2, DIFF_HD), P['b_k_norm_g'][l]), pos)
    kb = kb.reshape(B, T, H_B, 2 * DIFF_HD)
    vb = vb.reshape(B, T, H_B, 2 * DIFF_HD)
    qm = rms_norm(qm.reshape(B, T, H_M, HD_M), P['m_q_norm_g'][l])
    gates = jax.nn.sigmoid(gt.reshape(B, T, N_BRANCH, D_MODEL) + P['gate_b'][l])
    return qa, ka, va, qb, kb, vb, qm, gates


def band_attention(q, k, v, rel, valid, bias_tab):
    s = jnp.einsum('bqhd,bkhd->bhqk', q, k).astype(jnp.float32) * (HD_A ** -0.5)
    bias = bias_tab[:, jnp.clip(rel, -REL_CLIP, REL_CLIP) + REL_CLIP].astype(jnp.float32)
    s = jnp.where(valid, s + bias, NEG_INF)
    p = jax.nn.softmax(s, axis=-1)
    return jnp.einsum('bhqk,bkhd->bqhd', p.astype(v.dtype), v)


def chunk_band_prompt(q, k, v, bias_tab):
    B, T, H, D = q.shape
    band = BAND_PAST + CHUNK
    pad = jnp.zeros((B, BAND_PAST, H, D), k.dtype)
    kp = jnp.concatenate([pad, k], axis=1)
    vp = jnp.concatenate([pad.astype(v.dtype), v], axis=1)
    r = jnp.arange(band)
    rel = r[None, :] - BAND_PAST - jnp.arange(CHUNK)[:, None]

    def one_chunk(c):
        start = c * CHUNK
        qc = lax.dynamic_slice_in_dim(q, start, CHUNK, axis=1)
        kc = lax.dynamic_slice_in_dim(kp, start, band, axis=1)
        vc = lax.dynamic_slice_in_dim(vp, start, band, axis=1)
        valid = jnp.broadcast_to((start - BAND_PAST + r >= 0)[None, :], (CHUNK, band))
        return band_attention(qc, kc, vc, rel, valid, bias_tab)

    out = lax.map(one_chunk, jnp.arange(T // CHUNK))
    return jnp.swapaxes(out, 0, 1).reshape(B, T, H * D)


def diff_lambda(P, l):
    lam_init = 0.8 - 0.6 * math.exp(-0.3 * l)
    q1 = P['b_lam_q1'][l].astype(jnp.float32)
    k1 = P['b_lam_k1'][l].astype(jnp.float32)
    q2 = P['b_lam_q2'][l].astype(jnp.float32)
    k2 = P['b_lam_k2'][l].astype(jnp.float32)
    lam = jnp.exp(jnp.sum(q1 * k1)) - jnp.exp(jnp.sum(q2 * k2)) + lam_init
    return lam, lam_init


def diff_attention(q, k, v, valid, lam):
    s = jnp.einsum('bqhmd,bkhmd->bhmqk', q, k).astype(jnp.float32) * (DIFF_HD ** -0.5)
    s = jnp.where(valid, s, NEG_INF)
    p = jax.nn.softmax(s, axis=-1)
    w = p[:, :, 0] - lam * p[:, :, 1]
    return jnp.einsum('bhqk,bkhe->bqhe', w.astype(v.dtype), v)


def diff_prompt(q, k, v, lam):
    B, T = q.shape[:2]
    k_chunk = jnp.arange(T) // CHUNK

    def one_block(i):
        start = i * Q_BLOCK
        qb = lax.dynamic_slice_in_dim(q, start, Q_BLOCK, axis=1)
        q_chunk = (start + jnp.arange(Q_BLOCK)) // CHUNK
        valid = k_chunk[None, :] <= q_chunk[:, None]
        return diff_attention(qb, k, v, valid, lam)

    out = lax.map(one_block, jnp.arange(T // Q_BLOCK))
    return jnp.swapaxes(out, 0, 1).reshape(B, T, H_B, 2 * DIFF_HD)


def diff_post(o, lam_init, P, l):
    B, T = o.shape[:2]
    o = rms_norm(o, P['b_subln_g'][l]) * (1.0 - lam_init)
    return o.reshape(B, T, BR_W)


def memory_kv(mem, P, l):
    B, N, _ = mem.shape
    hm = rms_norm(mem, P['mem_norm_g'][l])
    mk, mv = jnp.split(hm @ P['w_mem_kv'][l], 2, axis=-1)
    mk = rms_norm(mk.reshape(B, N, H_M, HD_M), P['m_k_norm_g'][l])
    return mk, mv.reshape(B, N, H_M, HD_M)


def memory_attention(q, k, v):
    B, T = q.shape[:2]
    s = jnp.einsum('bqhd,bkhd->bhqk', q, k).astype(jnp.float32) * (HD_M ** -0.5)
    p = jax.nn.softmax(s, axis=-1)
    return jnp.einsum('bhqk,bkhd->bqhd', p.astype(v.dtype), v).reshape(B, T, BR_W)


def merge_branches(outs, gates, P, l):
    w_br = P['w_branch'][l]
    merged = sum(gates[:, :, n] * (o @ w_br[n]) for n, o in enumerate(outs))
    return merged @ P['w_out'][l]


def conv_ffn(x, prev, P, l):
    h = rms_norm(x, P['norm_ffn_g'][l])
    a, b = jnp.split(h @ P['w_ffn_up'][l], 2, axis=-1)
    T = a.shape[1]
    ap = jnp.concatenate([prev.astype(a.dtype), a], axis=1)
    w = P['ffn_conv_w'][l]
    c = P['ffn_conv_b'][l] + sum(ap[:, j:j + T] * w[j] for j in range(CONV_W))
    y = (jax.nn.gelu(c, approximate=False) * b) @ P['w_ffn_down'][l]
    return y, ap[:, T:]


def setup_inputs(seed: int = 0) -> dict:
    key = jax.random.key(seed)
    keys = jax.random.split(key, 34)

    def nrm(i, shape, scale):
        return jax.random.normal(keys[i], shape, jnp.float32) * scale

    def gain(i, shape):
        return 1.0 + nrm(i, shape, 0.02)

    a_len = min(BAND_PAST, PAST_LEN)
    return {
        'x_prompt': nrm(0, (BATCH, SEQ, D_MODEL), 1.0),
        'x_sample': nrm(1, (DEC_BATCH, DEC_SEQ, D_MODEL), 1.0),
        'cache_a_k': nrm(2, (DEPTH, DEC_BATCH, a_len, H_A, HD_A), 1.0),
        'cache_a_v': nrm(3, (DEPTH, DEC_BATCH, a_len, H_A, HD_A), 1.0),
        'cache_b_k': nrm(4, (DEPTH, DEC_BATCH, PAST_LEN, H_B, 2 * DIFF_HD), 1.0),
        'cache_b_v': nrm(5, (DEPTH, DEC_BATCH, PAST_LEN, H_B, 2 * DIFF_HD), 1.0),
        'cache_mem_k': nrm(6, (DEPTH, DEC_BATCH, N_MEM, H_M, HD_M), 1.0),
        'cache_mem_v': nrm(7, (DEPTH, DEC_BATCH, N_MEM, H_M, HD_M), 1.0),
        'state_ffn_conv': nrm(8, (DEPTH, DEC_BATCH, CONV_W - 1, D_FF), 1.0),
        'mem_prompt': nrm(9, (BATCH, N_MEM, D_MODEL), 1.0),
        'norm_mix_g': gain(10, (DEPTH, D_MODEL)),
        'w_in': nrm(11, (DEPTH, D_MODEL, N_IN), D_MODEL ** -0.5),
        'a_q_norm_g': gain(12, (DEPTH, HD_A)),
        'a_k_norm_g': gain(13, (DEPTH, HD_A)),
        'a_rel_bias': nrm(14, (DEPTH, H_A, 2 * REL_CLIP + 1), 0.5),
        'b_q_norm_g': gain(15, (DEPTH, DIFF_HD)),
        'b_k_norm_g': gain(16, (DEPTH, DIFF_HD)),
        'b_lam_q1': nrm(17, (DEPTH, DIFF_HD), 0.1),
        'b_lam_k1': nrm(18, (DEPTH, DIFF_HD), 0.1),
        'b_lam_q2': nrm(19, (DEPTH, DIFF_HD), 0.1),
        'b_lam_k2': nrm(20, (DEPTH, DIFF_HD), 0.1),
        'b_subln_g': gain(21, (DEPTH, 2 * DIFF_HD)),
        'm_q_norm_g': gain(22, (DEPTH, HD_M)),
        'm_k_norm_g': gain(23, (DEPTH, HD_M)),
        'mem_norm_g': gain(24, (DEPTH, D_MODEL)),
        'w_mem_kv': nrm(25, (DEPTH, D_MODEL, 2 * BR_W), D_MODEL ** -0.5),
        'gate_b': nrm(26, (DEPTH, N_BRANCH, D_MODEL), 0.01),
        'w_branch': nrm(27, (DEPTH, N_BRANCH, BR_W, D_MODEL), BR_W ** -0.5),
        'w_out': nrm(28, (DEPTH, D_MODEL, D_MODEL), D_MODEL ** -0.5),
        'norm_ffn_g': gain(29, (DEPTH, D_MODEL)),
        'w_ffn_up': nrm(30, (DEPTH, D_MODEL, 2 * D_FF), D_MODEL ** -0.5),
        'ffn_conv_w': nrm(31, (DEPTH, CONV_W, D_FF), 0.5),
        'ffn_conv_b': nrm(32, (DEPTH, D_FF), 0.01),
        'w_ffn_down': nrm(33, (DEPTH, D_FF, D_MODEL), D_FF ** -0.5),
    }


def reference(x_prompt, x_sample, cache_a_k, cache_a_v, cache_b_k, cache_b_v, cache_mem_k,
              cache_mem_v, state_ffn_conv, mem_prompt, norm_mix_g, w_in, a_q_norm_g, a_k_norm_g,
              a_rel_bias, b_q_norm_g, b_k_norm_g, b_lam_q1, b_lam_k1, b_lam_q2, b_lam_k2,
              b_subln_g, m_q_norm_g, m_k_norm_g, mem_norm_g, w_mem_kv, gate_b, w_branch, w_out,
              norm_ffn_g, w_ffn_up, ffn_conv_w, ffn_conv_b, w_ffn_down):
    P = {'norm_mix_g': norm_mix_g, 'w_in': w_in, 'a_q_norm_g': a_q_norm_g,
         'a_k_norm_g': a_k_norm_g, 'b_q_norm_g': b_q_norm_g, 'b_k_norm_g': b_k_norm_g,
         'b_lam_q1': b_lam_q1, 'b_lam_k1': b_lam_k1, 'b_lam_q2': b_lam_q2, 'b_lam_k2': b_lam_k2,
         'b_subln_g': b_subln_g, 'm_q_norm_g': m_q_norm_g, 'm_k_norm_g': m_k_norm_g,
         'mem_norm_g': mem_norm_g, 'w_mem_kv': w_mem_kv, 'gate_b': gate_b, 'w_branch': w_branch,
         'w_out': w_out, 'norm_ffn_g': norm_ffn_g, 'w_ffn_up': w_ffn_up,
         'ffn_conv_w': ffn_conv_w, 'ffn_conv_b': ffn_conv_b, 'w_ffn_down': w_ffn_down}

    Bp, Tp, _ = x_prompt.shape
    Bs, Ts, _ = x_sample.shape
    pos_p = jnp.arange(Tp, dtype=jnp.int32)
    pos_s = PAST_LEN + jnp.arange(Ts, dtype=jnp.int32)
    a_keep = min(BAND_PAST, Tp)

    a_len = cache_a_k.shape[2]
    key_pos_a = jnp.concatenate([PAST_LEN - a_len + jnp.arange(a_len, dtype=jnp.int32), pos_s])
    rel_s = key_pos_a[None, :] - pos_s[:, None]
    q_chunk_s = pos_s // CHUNK
    k_chunk_a = key_pos_a // CHUNK
    valid_a_s = (k_chunk_a[None, :] <= q_chunk_s[:, None]) & (k_chunk_a[None, :] >= q_chunk_s[:, None] - BAND_CHUNKS)
    key_pos_b = jnp.concatenate([jnp.arange(PAST_LEN, dtype=jnp.int32), pos_s])
    valid_b_s = (key_pos_b // CHUNK)[None, :] <= q_chunk_s[:, None]

    xp = x_prompt
    xs = x_sample
    ak_p, av_p, bk_p, bv_p, mk_p, mv_p, cv_p = [], [], [], [], [], [], []
    ak_s, av_s, bk_s, bv_s, cv_s = [], [], [], [], []
    for l in range(DEPTH):
        lam, lam_init = diff_lambda(P, l)

        h = rms_norm(xp, norm_mix_g[l])
        qa, ka, va, qb, kb, vb, qm, gates = mixer_inputs(h, pos_p, P, l)
        oa = chunk_band_prompt(qa, ka, va, a_rel_bias[l])
        ob = diff_post(diff_prompt(qb, kb.reshape(Bp, Tp, H_B, 2, DIFF_HD), vb, lam), lam_init, P, l)
        mk, mv = memory_kv(mem_prompt, P, l)
        om = memory_attention(qm, mk, mv)
        xp = xp + merge_branches((oa, ob, om), gates, P, l)
        f, conv_new = conv_ffn(xp, jnp.zeros((Bp, CONV_W - 1, D_FF), xp.dtype), P, l)
        xp = xp + f
        ak_p.append(ka[:, Tp - a_keep:])
        av_p.append(va[:, Tp - a_keep:])
        bk_p.append(kb)
        bv_p.append(vb)
        mk_p.append(mk)
        mv_p.append(mv)
        cv_p.append(conv_new)

        h = rms_norm(xs, norm_mix_g[l])
        qa, ka, va, qb, kb, vb, qm, gates = mixer_inputs(h, pos_s, P, l)
        ka_band = jnp.concatenate([cache_a_k[l].astype(ka.dtype), ka], axis=1)
        va_band = jnp.concatenate([cache_a_v[l].astype(va.dtype), va], axis=1)
        oa = band_attention(qa, ka_band, va_band, rel_s, valid_a_s, a_rel_bias[l]).reshape(Bs, Ts, BR_W)
        kb_all = jnp.concatenate([cache_b_k[l].astype(kb.dtype), kb], axis=1)
        vb_all = jnp.concatenate([cache_b_v[l].astype(vb.dtype), vb], axis=1)
        ob = diff_attention(qb, kb_all.reshape(Bs, PAST_LEN + Ts, H_B, 2, DIFF_HD), vb_all, valid_b_s, lam)
        ob = diff_post(ob, lam_init, P, l)
        om = memory_attention(qm, cache_mem_k[l].astype(qm.dtype), cache_mem_v[l].astype(qm.dtype))
        xs = xs + merge_branches((oa, ob, om), gates, P, l)
        f, conv_new = conv_ffn(xs, state_ffn_conv[l], P, l)
        xs = xs + f
        ak_s.append(ka)
        av_s.append(va)
        bk_s.append(kb)
        bv_s.append(vb)
        cv_s.append(conv_new)

    return (xp, xs,
            jnp.stack(ak_p), jnp.stack(av_p), jnp.stack(bk_p), jnp.stack(bv_p),
            jnp.stack(mk_p), jnp.stack(mv_p), jnp.stack(cv_p),
            jnp.stack(ak_s), jnp.stack(av_s), jnp.stack(bk_s), jnp.stack(bv_s), jnp.stack(cv_s))
```

```python
import functools
import math

import numpy as np
import jax
import jax.numpy as jnp
from jax import lax
from jax.experimental import pallas as pl
from jax.experimental.pallas import tpu as pltpu

F32 = jnp.float32
BF16 = jnp.bfloat16

CHUNK = 64
BAND_CHUNKS = 8
BAND_PAST = BAND_CHUNKS * CHUNK
REL_CLIP = 128
HD_A = 128
DIFF_HD = 64
ROT_DIM = DIFF_HD // 4
ROPE_THETA = 500000.0
H_M = 4
N_BRANCH = 3
CONV_W = 3
EPS = 1e-6
NEG_INF = -1e30

V7X_LANES = 128
V7X_MXU_DIM = 256
V7X_VMEM_BYTES = 64 * 1024 * 1024
MIB = 1024 * 1024


def _compiler_params(semantics, vmem_estimate_bytes):
    limit = min(int(vmem_estimate_bytes * 1.25) + 8 * MIB, V7X_VMEM_BYTES - 4 * MIB)
    return pltpu.CompilerParams(dimension_semantics=semantics, vmem_limit_bytes=limit)


def _row_tile(m, target):
    t = min(m, target)
    assert m % t == 0, (m, t)
    return t


def _rmsnorm_kernel(x_ref, g_ref, o_ref):
    x = x_ref[...]
    ms = jnp.mean(x * x, axis=-1, keepdims=True)
    o_ref[...] = (x * lax.rsqrt(ms + EPS) * g_ref[...]).astype(o_ref.dtype)


def rmsnorm_cast(x, g):
    m, d = x.shape
    tm = _row_tile(m, 512)
    return pl.pallas_call(
        _rmsnorm_kernel,
        grid=(m // tm,),
        in_specs=[pl.BlockSpec((tm, d), lambda i: (i, 0)), pl.BlockSpec((1, d), lambda i: (0, 0))],
        out_specs=pl.BlockSpec((tm, d), lambda i: (i, 0)),
        out_shape=jax.ShapeDtypeStruct((m, d), BF16),
        compiler_params=_compiler_params(("parallel",), 2 * tm * d * 6),
        name="rmsnorm_cast",
    )(x, g.reshape(1, d))


def _dot(a, b):
    return jnp.dot(a, b, preferred_element_type=F32)


def _dot_nt(a, b):
    return lax.dot_general(a, b, (((1,), (1,)), ((), ())), preferred_element_type=F32)


def _proj_plain_kernel(h_ref, w_ref, o_ref):
    o_ref[...] = _dot(h_ref[...], w_ref[...]).astype(o_ref.dtype)


def _proj_headnorm_kernel(h_ref, w_ref, g_ref, o_ref, *, hd):
    acc = _dot(h_ref[...], w_ref[...])
    g = g_ref[...]
    for k in range(acc.shape[1] // hd):
        s = acc[:, k * hd:(k + 1) * hd]
        ms = jnp.mean(s * s, axis=-1, keepdims=True)
        o_ref[:, k * hd:(k + 1) * hd] = (s * lax.rsqrt(ms + EPS) * g).astype(o_ref.dtype)


def _proj_norm_rope_kernel(h_ref, w_ref, g_ref, tab_ref, o_ref):
    acc = _dot(h_ref[...], w_ref[...])
    g = g_ref[...]
    cos = tab_ref[:, 0:V7X_LANES]
    sin_up = tab_ref[:, V7X_LANES:2 * V7X_LANES]
    sin_dn = tab_ref[:, 2 * V7X_LANES:3 * V7X_LANES]
    half = ROT_DIM // 2
    left = lax.broadcasted_iota(jnp.int32, (acc.shape[0], V7X_LANES), 1) < DIFF_HD
    for k in range(acc.shape[1] // V7X_LANES):
        s = acc[:, k * V7X_LANES:(k + 1) * V7X_LANES]
        sq = s * s
        tot = jnp.sum(sq, axis=-1, keepdims=True)
        lsum = jnp.sum(jnp.where(left, sq, 0.0), axis=-1, keepdims=True)
        ms = jnp.where(left, lsum, tot - lsum) * (1.0 / DIFF_HD)
        y = s * lax.rsqrt(ms + EPS) * g
        out = (y * cos + pltpu.roll(y, half, 1) * sin_up
               + pltpu.roll(y, V7X_LANES - half, 1) * sin_dn)
        o_ref[:, k * V7X_LANES:(k + 1) * V7X_LANES] = out.astype(o_ref.dtype)


def _proj_gate_kernel(h_ref, w_ref, b_ref, o_ref):
    acc = _dot(h_ref[...], w_ref[...])
    o_ref[...] = jax.nn.sigmoid(acc + b_ref[...]).astype(o_ref.dtype)


def _proj_residual_kernel(h_ref, w_ref, x_ref, o_ref):
    o_ref[...] = x_ref[...] + _dot(h_ref[...], w_ref[...])


def _proj_call(kernel_fn, h, w, col0, ncols, out_dtype, extras, *, tm=1024, tn=512, name):
    m, k = h.shape
    tm = _row_tile(m, tm)
    tn = min(tn, ncols)
    assert ncols % tn == 0 and col0 % tn == 0, (ncols, col0, tn)
    cb = col0 // tn
    in_specs = [pl.BlockSpec((tm, k), lambda i, j: (i, 0)),
                pl.BlockSpec((k, tn), lambda i, j: (0, j + cb))]
    in_specs += [spec for _, spec in extras]
    extra_bytes = sum(int(np.prod(spec.block_shape)) * a.dtype.itemsize for a, spec in extras)
    vmem = 2 * (tm * k * 2 + k * tn * 2 + tm * tn * 4 + extra_bytes) + tm * tn * 8
    return pl.pallas_call(
        kernel_fn,
        grid=(m // tm, ncols // tn),
        in_specs=in_specs,
        out_specs=pl.BlockSpec((tm, tn), lambda i, j: (i, j)),
        out_shape=jax.ShapeDtypeStruct((m, ncols), out_dtype),
        compiler_params=_compiler_params(("parallel", "arbitrary"), vmem),
        name=name,
    )(h, w, *[a for a, _ in extras])


def _rope_table(pos):
    half = ROT_DIM // 2
    inv_freq = jnp.exp(jnp.arange(half, dtype=F32) * (-2.0 * math.log(ROPE_THETA) / ROT_DIM))
    ang = pos.astype(F32)[:, None] * inv_freq[None, :]
    cos = jnp.cos(ang)
    sin = jnp.sin(ang)
    p = pos.shape[0]
    rest = DIFF_HD - ROT_DIM
    c64 = jnp.concatenate([cos, cos, jnp.ones((p, rest), F32)], axis=1)
    up64 = jnp.concatenate([jnp.zeros((p, half), F32), sin, jnp.zeros((p, rest), F32)], axis=1)
    dn64 = jnp.concatenate([-sin, jnp.zeros((p, half + rest), F32)], axis=1)
    return jnp.concatenate([c64, c64, up64, up64, dn64, dn64], axis=1)


A_QBLK = 4 * CHUNK
A_KBLK = BAND_PAST + A_QBLK


def _softmax_pv(s, v):
    m = jnp.max(s, axis=-1, keepdims=True)
    p = jnp.exp(s - m)
    l = jnp.sum(p, axis=-1, keepdims=True)
    return _dot(p.astype(BF16), v), l


def _attn_a_prompt_kernel(q_ref, k_ref, v_ref, bias_ref, o_ref, kp, vp, *, t):
    kp[0:BAND_PAST, :] = jnp.zeros((BAND_PAST, HD_A), BF16)
    vp[0:BAND_PAST, :] = jnp.zeros((BAND_PAST, HD_A), BF16)
    kp[BAND_PAST:, :] = k_ref[0].astype(BF16)
    vp[BAND_PAST:, :] = v_ref[0].astype(BF16)
    scale = HD_A ** -0.5
    col = lax.broadcasted_iota(jnp.int32, (A_QBLK, A_KBLK), 1)

    def body(i, carry):
        r0 = pl.multiple_of(i * A_QBLK, A_QBLK)
        q = q_ref[0, pl.ds(r0, A_QBLK), :]
        kb = kp[pl.ds(r0, A_KBLK), :]
        vb = vp[pl.ds(r0, A_KBLK), :]
        s = _dot_nt(q, kb) * scale + bias_ref[0]
        s = jnp.where(col >= BAND_PAST - r0, s, NEG_INF)
        o, l = _softmax_pv(s, vb)
        o_ref[0, pl.ds(r0, A_QBLK), :] = (o / l).astype(o_ref.dtype)
        return carry

    lax.fori_loop(0, t // A_QBLK, body, 0)


def attn_a_prompt(qa, ka, va, bias, b, t):
    h = qa.shape[1] // HD_A
    assert t % A_QBLK == 0
    q3, k3, v3 = (a.reshape(b, t, h * HD_A) for a in (qa, ka, va))
    spec = pl.BlockSpec((1, t, HD_A), lambda bi, hi: (bi, 0, hi))
    vmem = 2 * (t * HD_A * (2 + 4 + 4 + 2) + A_QBLK * A_KBLK * 4) + 2 * (t + BAND_PAST) * HD_A * 2
    out = pl.pallas_call(
        functools.partial(_attn_a_prompt_kernel, t=t),
        grid=(b, h),
        in_specs=[spec, spec, spec, pl.BlockSpec((1, A_QBLK, A_KBLK), lambda bi, hi: (hi, 0, 0))],
        out_specs=spec,
        out_shape=jax.ShapeDtypeStruct((b, t, h * HD_A), BF16),
        scratch_shapes=[pltpu.VMEM((t + BAND_PAST, HD_A), BF16), pltpu.VMEM((t + BAND_PAST, HD_A), BF16)],
        compiler_params=_compiler_params(("parallel", "parallel"), vmem),
        name="attn_a_prompt",
    )(q3, k3, v3, bias)
    return out.reshape(b * t, h * HD_A)


def _attn_a_sample_kernel(q_ref, kc_ref, vc_ref, kn_ref, vn_ref, bc_ref, bn_ref, o_ref, *, heads):
    scale = HD_A ** -0.5
    for h in range(heads):
        sl = slice(h * HD_A, (h + 1) * HD_A)
        q = q_ref[0, :, sl]
        kc = kc_ref[0, 0, :, h, :].astype(BF16)
        vc = vc_ref[0, 0, :, h, :].astype(BF16)
        kn = kn_ref[0, :, sl].astype(BF16)
        vn = vn_ref[0, :, sl].astype(BF16)
        sc = _dot_nt(q, kc) * scale + bc_ref[h]
        sn = _dot_nt(q, kn) * scale + bn_ref[h]
        m = jnp.maximum(jnp.max(sc, axis=-1, keepdims=True), jnp.max(sn, axis=-1, keepdims=True))
        pc = jnp.exp(sc - m)
        pn = jnp.exp(sn - m)
        l = jnp.sum(pc, axis=-1, keepdims=True) + jnp.sum(pn, axis=-1, keepdims=True)
        o = _dot(pc.astype(BF16), vc) + _dot(pn.astype(BF16), vn)
        o_ref[0, :, sl] = (o / l).astype(o_ref.dtype)


def attn_a_sample(qa, ka, va, cache_k, cache_v, layer, bias_c, bias_n, b, t):
    heads = qa.shape[1] // HD_A
    a_len = cache_k.shape[2]
    q3, k3, v3 = (a.reshape(b, t, heads * HD_A) for a in (qa, ka, va))
    new_spec = pl.BlockSpec((1, t, heads * HD_A), lambda bi: (bi, 0, 0))
    cache_spec = pl.BlockSpec((1, 1, a_len, heads, HD_A), lambda bi: (layer, bi, 0, 0, 0))
    vmem = 2 * (2 * a_len * heads * HD_A * 4 + t * heads * HD_A * 12) + 2 * bias_c.size * 4 + 8 * MIB
    out = pl.pallas_call(
        functools.partial(_attn_a_sample_kernel, heads=heads),
        grid=(b,),
        in_specs=[new_spec, cache_spec, cache_spec, new_spec, new_spec,
                  pl.BlockSpec(bias_c.shape, lambda bi: (0, 0, 0)),
                  pl.BlockSpec(bias_n.shape, lambda bi: (0, 0, 0))],
        out_specs=new_spec,
        out_shape=jax.ShapeDtypeStruct((b, t, heads * HD_A), BF16),
        compiler_params=_compiler_params(("parallel",), vmem),
        name="attn_a_sample",
    )(q3, cache_k, cache_v, k3, v3, bias_c, bias_n)
    return out.reshape(b * t, heads * HD_A)


def _diff_lambda(lam_ref, lam_init):
    v = lam_ref[...]
    d1 = jnp.sum(v[0:1] * v[1:2], axis=-1, keepdims=True)
    d2 = jnp.sum(v[2:3] * v[3:4], axis=-1, keepdims=True)
    return jnp.exp(d1) - jnp.exp(d2) + lam_init


def _stack_diff_queries(q):
    lane = lax.broadcasted_iota(jnp.int32, q.shape, 1)
    qs = q * jnp.asarray(DIFF_HD ** -0.5, q.dtype)
    zero = jnp.zeros_like(qs)
    return jnp.concatenate([jnp.where(lane < DIFF_HD, qs, zero), jnp.where(lane >= DIFF_HD, qs, zero)], axis=0)


def _online_step(carry, s, v):
    m, l, acc = carry
    m_new = jnp.maximum(m, jnp.max(s, axis=-1, keepdims=True))
    alpha = jnp.exp(m - m_new)
    p = jnp.exp(s - m_new)
    l = alpha * l + jnp.sum(p, axis=-1, keepdims=True)
    acc = alpha * acc + _dot(p.astype(BF16), v)
    return m_new, l, acc


def _diff_finish(carry, lam, g, post_scale, tq):
    _, l, acc = carry
    o = acc[:tq] / l[:tq] - lam * (acc[tq:] / l[tq:])
    ms = jnp.mean(o * o, axis=-1, keepdims=True)
    return (o * lax.rsqrt(ms + EPS) * g) * post_scale


def _attn_b_prompt_kernel(lam_ref, q_ref, k_ref, v_ref, g_ref, o_ref, kb, vb, *, tq, lam_init):
    qi = pl.program_id(2)

    @pl.when(qi == 0)
    def _():
        kb[...] = k_ref[0].astype(BF16)
        vb[...] = v_ref[0].astype(BF16)

    q2 = _stack_diff_queries(q_ref[0])
    row = lax.broadcasted_iota(jnp.int32, (2 * tq, tq), 0)
    col = lax.broadcasted_iota(jnp.int32, (2 * tq, tq), 1)
    row = jnp.where(row >= tq, row - tq, row)
    diag_ok = (col // CHUNK) <= (row // CHUNK)

    def block(j, carry, masked):
        k0 = pl.multiple_of(j * tq, tq)
        s = _dot_nt(q2, kb[pl.ds(k0, tq), :])
        if masked:
            s = jnp.where(diag_ok, s, NEG_INF)
        return _online_step(carry, s, vb[pl.ds(k0, tq), :])

    init = (jnp.full((2 * tq, 1), NEG_INF, F32), jnp.zeros((2 * tq, 1), F32),
            jnp.zeros((2 * tq, 2 * DIFF_HD), F32))
    carry = lax.fori_loop(0, qi, lambda j, c: block(j, c, False), init)
    carry = block(qi, carry, True)
    lam = _diff_lambda(lam_ref, lam_init)
    o_ref[0] = _diff_finish(carry, lam, g_ref[...], 1.0 - lam_init, tq).astype(o_ref.dtype)


def attn_b_prompt(qb, kb, vb, lam_params, subln_g, lam_init, b, t, *, tq=256):
    hd = 2 * DIFF_HD
    heads = qb.shape[1] // hd
    tq = min(tq, t)
    assert t % tq == 0 and tq % CHUNK == 0
    q3, k3, v3 = (a.reshape(b, t, heads * hd) for a in (qb, kb, vb))
    kv_spec = pl.BlockSpec((1, t, hd), lambda bi, hi, qi: (bi, 0, hi))
    q_spec = pl.BlockSpec((1, tq, hd), lambda bi, hi, qi: (bi, qi, hi))
    vmem = 2 * (2 * t * hd * 4 + 2 * tq * hd * 2) + 2 * t * hd * 2 + 16 * tq * tq * 4
    out = pl.pallas_call(
        functools.partial(_attn_b_prompt_kernel, tq=tq, lam_init=lam_init),
        grid=(b, heads, t // tq),
        in_specs=[pl.BlockSpec(lam_params.shape, lambda bi, hi, qi: (0, 0)), q_spec, kv_spec, kv_spec,
                  pl.BlockSpec((1, hd), lambda bi, hi, qi: (0, 0))],
        out_specs=q_spec,
        out_shape=jax.ShapeDtypeStruct((b, t, heads * hd), BF16),
        scratch_shapes=[pltpu.VMEM((t, hd), BF16), pltpu.VMEM((t, hd), BF16)],
        compiler_params=_compiler_params(("parallel", "parallel", "arbitrary"), vmem),
        name="attn_b_prompt",
    )(lam_params, q3, k3, v3, subln_g.reshape(1, hd))
    return out.reshape(b * t, heads * hd)


def _attn_b_sample_kernel(lam_ref, q_ref, kc_ref, vc_ref, kn_ref, vn_ref, g_ref, o_ref,
                          m_scr, l_scr, acc_scr, *, heads, t, lam_init):
    kv = pl.program_id(1)
    last = pl.num_programs(1) - 1
    hd = 2 * DIFF_HD

    @pl.when(kv == 0)
    def _():
        m_scr[...] = jnp.full(m_scr.shape, NEG_INF, F32)
        l_scr[...] = jnp.zeros(l_scr.shape, F32)
        acc_scr[...] = jnp.zeros(acc_scr.shape, F32)

    for h in range(heads):
        q2 = _stack_diff_queries(q_ref[0, :, h * hd:(h + 1) * hd])
        carry = (m_scr[h], l_scr[h], acc_scr[h])
        k = kc_ref[0, 0, :, h, :].astype(BF16)
        v = vc_ref[0, 0, :, h, :].astype(BF16)
        m, l, acc = _online_step(carry, _dot_nt(q2, k), v)
        m_scr[h] = m
        l_scr[h] = l
        acc_scr[h] = acc

    @pl.when(kv == last)
    def _():
        lam = _diff_lambda(lam_ref, lam_init)
        for h in range(heads):
            sl = slice(h * hd, (h + 1) * hd)
            q2 = _stack_diff_queries(q_ref[0, :, sl])
            carry = (m_scr[h], l_scr[h], acc_scr[h])
            carry = _online_step(carry, _dot_nt(q2, kn_ref[0, :, sl].astype(BF16)), vn_ref[0, :, sl].astype(BF16))
            o_ref[0, :, sl] = _diff_finish(carry, lam, g_ref[...], 1.0 - lam_init, t).astype(o_ref.dtype)


def attn_b_sample(qb, kb, vb, cache_k, cache_v, layer, lam_params, subln_g, lam_init, b, t, *, tk=1024):
    hd = 2 * DIFF_HD
    heads = qb.shape[1] // hd
    past = cache_k.shape[2]
    tk = min(tk, past)
    assert past % tk == 0
    q3, k3, v3 = (a.reshape(b, t, heads * hd) for a in (qb, kb, vb))
    new_spec = pl.BlockSpec((1, t, heads * hd), lambda bi, kv: (bi, 0, 0))
    cache_spec = pl.BlockSpec((1, 1, tk, heads, hd), lambda bi, kv: (layer, bi, kv, 0, 0))
    vmem = 2 * (2 * tk * heads * hd * 4 + t * heads * hd * 12) + 16 * 2 * t * tk * 4
    out = pl.pallas_call(
        functools.partial(_attn_b_sample_kernel, heads=heads, t=t, lam_init=lam_init),
        grid=(b, past // tk),
        in_specs=[pl.BlockSpec(lam_params.shape, lambda bi, kv: (0, 0)), new_spec, cache_spec, cache_spec,
                  new_spec, new_spec, pl.BlockSpec((1, hd), lambda bi, kv: (0, 0))],
        out_specs=new_spec,
        out_shape=jax.ShapeDtypeStruct((b, t, heads * hd), BF16),
        scratch_shapes=[pltpu.VMEM((heads, 2 * t, 1), F32), pltpu.VMEM((heads, 2 * t, 1), F32),
                        pltpu.VMEM((heads, 2 * t, hd), F32)],
        compiler_params=_compiler_params(("parallel", "arbitrary"), vmem),
        name="attn_b_sample",
    )(lam_params, q3, cache_k, cache_v, k3, v3, subln_g.reshape(1, hd))
    return out.reshape(b * t, heads * hd)


def _attn_m_kernel(q_ref, k_ref, v_ref, o_ref, *, heads, hd, native):
    scale = hd ** -0.5
    for h in range(heads):
        sl = slice(h * hd, (h + 1) * hd)
        if native:
            k = k_ref[0, 0, :, h, :]
            v = v_ref[0, 0, :, h, :]
        else:
            k = k_ref[0, :, sl]
            v = v_ref[0, :, sl]
        s = _dot_nt(q_ref[0, :, sl], k.astype(BF16)) * scale
        o, l = _softmax_pv(s, v.astype(BF16))
        o_ref[0, :, sl] = (o / l).astype(o_ref.dtype)


def attn_m(qm, mem_k, mem_v, b, t, *, layer=None, tq=512):
    width = qm.shape[1]
    hd = width // H_M
    tq = min(tq, t)
    assert t % tq == 0
    q3 = qm.reshape(b, t, width)
    q_spec = pl.BlockSpec((1, tq, width), lambda bi, qi: (bi, qi, 0))
    if layer is None:
        n = mem_k.shape[0] // b
        mem_k = mem_k.reshape(b, n, width)
        mem_v = mem_v.reshape(b, n, width)
        kv_spec = pl.BlockSpec((1, n, width), lambda bi, qi: (bi, 0, 0))
        kv_bytes = n * width * 4
    else:
        n = mem_k.shape[2]
        kv_spec = pl.BlockSpec((1, 1, n, H_M, hd), lambda bi, qi: (layer, bi, 0, 0, 0))
        kv_bytes = n * 8 * hd * 4
    vmem = 2 * (2 * kv_bytes + 2 * tq * width * 2) + 8 * tq * n * 4
    out = pl.pallas_call(
        functools.partial(_attn_m_kernel, heads=H_M, hd=hd, native=layer is not None),
        grid=(b, t // tq),
        in_specs=[q_spec, kv_spec, kv_spec],
        out_specs=q_spec,
        out_shape=jax.ShapeDtypeStruct((b, t, width), BF16),
        compiler_params=_compiler_params(("parallel", "arbitrary"), vmem),
        name="attn_m",
    )(q3, mem_k, mem_v)
    return out.reshape(b * t, width)


def _merge_kernel(oa_ref, ob_ref, om_ref, w_ref, ga_ref, gb_ref, gm_ref, o_ref):
    acc = ga_ref[...].astype(F32) * _dot(oa_ref[...], w_ref[0])
    acc = acc + gb_ref[...].astype(F32) * _dot(ob_ref[...], w_ref[1])
    acc = acc + gm_ref[...].astype(F32) * _dot(om_ref[...], w_ref[2])
    o_ref[...] = acc.astype(o_ref.dtype)


def merge_branches(oa, ob, om, gates, w_br, *, tm=1024, tn=512):
    m, bw = oa.shape
    d = w_br.shape[2]
    tm = _row_tile(m, tm)
    assert d % tn == 0
    nj = d // tn
    o_spec = pl.BlockSpec((tm, bw), lambda i, j: (i, 0))
    gate_specs = [pl.BlockSpec((tm, tn), functools.partial(lambda i, j, n: (i, n * nj + j), n=n))
                  for n in range(N_BRANCH)]
    vmem = 2 * (3 * tm * bw * 2 + 3 * bw * tn * 2 + 3 * tm * tn * 2 + tm * tn * 2) + 3 * tm * tn * 4
    return pl.pallas_call(
        _merge_kernel,
        grid=(m // tm, nj),
        in_specs=[o_spec, o_spec, o_spec, pl.BlockSpec((N_BRANCH, bw, tn), lambda i, j: (0, 0, j))] + gate_specs,
        out_specs=pl.BlockSpec((tm, tn), lambda i, j: (i, j)),
        out_shape=jax.ShapeDtypeStruct((m, d), BF16),
        compiler_params=_compiler_params(("parallel", "arbitrary"), vmem),
        name="merge_branches",
    )(oa, ob, om, w_br, gates, gates, gates)


def _ffn_up_kernel(h_ref, wa_ref, wb_ref, cw_ref, cb_ref, st_ref, g_ref, cn_ref, carry, *, nb, tb, blocks_per_seq):
    i = pl.program_id(0)
    j = pl.program_id(1)
    h = h_ref[...]
    a = _dot(h, wa_ref[...])
    bgate = _dot(h, wb_ref[...])
    tm, tn = a.shape
    if nb == 1:
        @pl.when((i % blocks_per_seq) == 0)
        def _():
            carry[j] = st_ref[0]

        prev = carry[j]
        p0 = jnp.broadcast_to(prev[0:1], (tm, tn))
        p1 = jnp.broadcast_to(prev[1:2], (tm, tn))
        carry[j] = a[tm - 2:tm]
        cn_ref[0] = a[tm - 2:tm]
        trow = lax.broadcasted_iota(jnp.int32, (tm, tn), 0)
    else:
        st = st_ref[...]
        p0 = jnp.broadcast_to(st[:, 0:1, :], (nb, tb, tn)).reshape(tm, tn)
        p1 = jnp.broadcast_to(st[:, 1:2, :], (nb, tb, tn)).reshape(tm, tn)
        cn_ref[...] = a.reshape(nb, tb, tn)[:, tb - 2:tb, :]
        trow = lax.broadcasted_iota(jnp.int32, (nb, tb, tn), 1).reshape(tm, tn)
    am1 = jnp.where(trow == 0, p1, pltpu.roll(a, 1, 0))
    am2 = jnp.where(trow == 0, p0, jnp.where(trow == 1, p1, pltpu.roll(a, 2, 0)))
    cw = cw_ref[...]
    c = cb_ref[...] + am2 * cw[0:1] + am1 * cw[1:2] + a * cw[2:3]
    gelu = 0.5 * c * (1.0 + lax.erf(c * (2.0 ** -0.5)))
    g_ref[...] = (gelu * bgate).astype(g_ref.dtype)


def ffn_up(h, w_a, w_b, conv_w, conv_b, state, b, t, *, tm=1024, tn=512):
    m, d = h.shape
    f = w_a.shape[1]
    tm = _row_tile(m, tm)
    assert f % tn == 0
    if tm >= t:
        assert tm % t == 0
        nb, tb, blocks_per_seq = tm // t, t, 1
    else:
        assert t % tm == 0
        nb, tb, blocks_per_seq = 1, tm, t // tm
    if nb > 1:
        seq_map = lambda i, j: (i, 0, j)
    else:
        seq_map = lambda i, j: (i // blocks_per_seq, 0, j)
    tail_map = lambda i, j: (i, 0, j)
    w_spec = pl.BlockSpec((d, tn), lambda i, j: (0, j))
    vmem = 2 * (tm * d * 2 + 2 * d * tn * 2 + tm * tn * 2 + 2 * nb * 8 * tn * 4) + 8 * tm * tn * 4
    g, conv_new = pl.pallas_call(
        functools.partial(_ffn_up_kernel, nb=nb, tb=tb, blocks_per_seq=blocks_per_seq),
        grid=(m // tm, f // tn),
        in_specs=[pl.BlockSpec((tm, d), lambda i, j: (i, 0)), w_spec, w_spec,
                  pl.BlockSpec((CONV_W, tn), lambda i, j: (0, j)),
                  pl.BlockSpec((1, tn), lambda i, j: (0, j)),
                  pl.BlockSpec((nb, CONV_W - 1, tn), seq_map)],
        out_specs=[pl.BlockSpec((tm, tn), lambda i, j: (i, j)),
                   pl.BlockSpec((nb, CONV_W - 1, tn), tail_map)],
        out_shape=[jax.ShapeDtypeStruct((m, f), BF16),
                   jax.ShapeDtypeStruct((b * blocks_per_seq, CONV_W - 1, f), F32)],
        scratch_shapes=[pltpu.VMEM((f // tn, CONV_W - 1, tn), F32)],
        compiler_params=_compiler_params(("arbitrary", "arbitrary"), vmem),
        name="ffn_up",
    )(h, w_a, w_b, conv_w, conv_b, state)
    return g, conv_new.reshape(b, blocks_per_seq, CONV_W - 1, f)[:, -1]


def _pad_cols(a, f_pad):
    return jnp.pad(a, [(0, 0)] * (a.ndim - 1) + [(0, f_pad - a.shape[-1])])


def _layer_weights(l, P):
    d_ff = P['w_ffn_down'].shape[1]
    f_pad = -(-d_ff // 512) * 512
    w_up = P['w_ffn_up'][l]
    return {
        'w_in': P['w_in'][l].astype(BF16),
        'w_mem_kv': P['w_mem_kv'][l].astype(BF16),
        'w_branch': P['w_branch'][l].astype(BF16),
        'w_out': P['w_out'][l].astype(BF16),
        'w_up_a': _pad_cols(w_up[:, :d_ff].astype(BF16), f_pad),
        'w_up_b': _pad_cols(w_up[:, d_ff:].astype(BF16), f_pad),
        'w_down': jnp.pad(P['w_ffn_down'][l].astype(BF16), [(0, f_pad - d_ff), (0, 0)]),
        'conv_w': _pad_cols(P['ffn_conv_w'][l], f_pad),
        'conv_b': _pad_cols(P['ffn_conv_b'][l].reshape(1, d_ff), f_pad),
        'f_pad': f_pad,
        'd_ff': d_ff,
    }


def _mixer_inputs(h, W, P, l, rope_tab, tm):
    m, d = h.shape
    bw = d // 2
    w_in = W['w_in']
    tile = lambda g, reps: jnp.tile(g.reshape(1, -1), (1, reps))
    g_spec = lambda width: pl.BlockSpec((1, width), lambda i, j: (0, 0))
    n_tab = rope_tab.shape[0] // tm
    tab_spec = pl.BlockSpec((tm, 3 * V7X_LANES), lambda i, j: (i % n_tab, 0))

    def headnorm(col, hd, gain, dtype, name):
        return _proj_call(functools.partial(_proj_headnorm_kernel, hd=hd), h, w_in, col, bw, dtype,
                          [(gain.reshape(1, hd), g_spec(hd))], tm=tm, name=name)

    def norm_rope(col, gain, dtype, name):
        return _proj_call(_proj_norm_rope_kernel, h, w_in, col, bw, dtype,
                          [(tile(gain, 2), g_spec(V7X_LANES)), (rope_tab, tab_spec)], tm=tm, name=name)

    def plain(col, name):
        return _proj_call(_proj_plain_kernel, h, w_in, col, bw, F32, [], tm=tm, name=name)

    qa = headnorm(0, HD_A, P['a_q_norm_g'][l], BF16, "proj_qa")
    ka = headnorm(bw, HD_A, P['a_k_norm_g'][l], F32, "proj_ka")
    va = plain(2 * bw, "proj_va")
    qb = norm_rope(3 * bw, P['b_q_norm_g'][l], BF16, "proj_qb")
    kb = norm_rope(4 * bw, P['b_k_norm_g'][l], F32, "proj_kb")
    vb = plain(5 * bw, "proj_vb")
    qm = headnorm(6 * bw, bw // H_M, P['m_q_norm_g'][l], BF16, "proj_qm")
    gate_b = P['gate_b'][l].reshape(1, N_BRANCH * d)
    n_gate_blocks = 7 * bw // 512
    gates = _proj_call(_proj_gate_kernel, h, w_in, 7 * bw, N_BRANCH * d, BF16,
                       [(gate_b, pl.BlockSpec((1, 512), lambda i, j: (0, j)))], tm=tm, name="proj_gates")
    del n_gate_blocks
    return qa, ka, va, qb, kb, vb, qm, gates


def _finish_layer(x, outs, gates, W, P, l, state, b, t, tm):
    merged = merge_branches(*outs, gates, W['w_branch'], tm=tm)
    d = x.shape[1]
    res_spec = pl.BlockSpec((tm, 512), lambda i, j: (i, j))
    x = _proj_call(_proj_residual_kernel, merged, W['w_out'], 0, d, F32, [(x, res_spec)], tm=tm, name="proj_out")
    h = rmsnorm_cast(x, P['norm_ffn_g'][l])
    g, conv_new = ffn_up(h, W['w_up_a'], W['w_up_b'], W['conv_w'], W['conv_b'], state, b, t, tm=tm)
    res_spec = pl.BlockSpec((tm, 256), lambda i, j: (i, j))
    x = _proj_call(_proj_residual_kernel, g, W['w_down'], 0, d, F32, [(x, res_spec)], tm=tm, tn=256, name="ffn_down")
    return x, conv_new[:, :, :W['d_ff']]


def _band_bias_prompt(tab):
    i = np.arange(A_QBLK)[:, None]
    r = np.arange(A_KBLK)[None, :]
    idx = np.clip(r - BAND_PAST - i, -REL_CLIP, REL_CLIP) + REL_CLIP
    qc, kc = i // CHUNK, r // CHUNK
    inband = (kc >= qc) & (kc <= qc + BAND_CHUNKS)
    return jnp.where(inband[None], tab[:, idx], NEG_INF)


def kernel(x_prompt, x_sample, cache_a_k, cache_a_v, cache_b_k, cache_b_v, cache_mem_k, cache_mem_v, state_ffn_conv, mem_prompt, norm_mix_g, w_in, a_q_norm_g, a_k_norm_g, a_rel_bias, b_q_norm_g, b_k_norm_g, b_lam_q1, b_lam_k1, b_lam_q2, b_lam_k2, b_subln_g, m_q_norm_g, m_k_norm_g, mem_norm_g, w_mem_kv, gate_b, w_branch, w_out, norm_ffn_g, w_ffn_up, ffn_conv_w, ffn_conv_b, w_ffn_down):
    P = {'w_in': w_in, 'a_q_norm_g': a_q_norm_g, 'a_k_norm_g': a_k_norm_g, 'b_q_norm_g': b_q_norm_g,
         'b_k_norm_g': b_k_norm_g, 'm_q_norm_g': m_q_norm_g, 'm_k_norm_g': m_k_norm_g, 'w_mem_kv': w_mem_kv,
         'gate_b': gate_b, 'w_branch': w_branch, 'w_out': w_out, 'norm_ffn_g': norm_ffn_g,
         'w_ffn_up': w_ffn_up, 'ffn_conv_w': ffn_conv_w, 'ffn_conv_b': ffn_conv_b, 'w_ffn_down': w_ffn_down}
    bp, tp, d = x_prompt.shape
    bs, ts, _ = x_sample.shape
    depth = w_in.shape[0]
    bw = d // 2
    past = cache_b_k.shape[2]
    a_len = cache_a_k.shape[2]
    n_mem = mem_prompt.shape[1]
    a_keep = min(BAND_PAST, tp)
    h_a = bw // HD_A
    h_b = bw // (2 * DIFF_HD)
    hd_m = bw // H_M
    mp, ms = bp * tp, bs * ts
    tm_p = _row_tile(mp, 1024)
    tm_s = _row_tile(ms, 1024)
    assert tm_p <= tp and tp % tm_p == 0 or tm_p % tp == 0

    pos_s = past + np.arange(ts)
    key_pos_a = np.concatenate([past - a_len + np.arange(a_len), pos_s])
    q_chunk_s = pos_s // CHUNK
    k_chunk_a = key_pos_a // CHUNK
    valid_a_s = (k_chunk_a[None, :] <= q_chunk_s[:, None]) & (k_chunk_a[None, :] >= q_chunk_s[:, None] - BAND_CHUNKS)
    idx_a_s = np.clip(key_pos_a[None, :] - pos_s[:, None], -REL_CLIP, REL_CLIP) + REL_CLIP
    key_pos_b = np.concatenate([np.arange(past), pos_s])
    valid_b_s = (key_pos_b // CHUNK)[None, :] <= q_chunk_s[:, None]
    assert valid_b_s.all(), "sample queries are expected to see every cached and new differential key"

    rope_p = _rope_table(jnp.arange(max(tp, tm_p), dtype=jnp.int32) % tp)
    rope_s = _rope_table(past + (jnp.arange(max(ts, tm_s), dtype=jnp.int32) % ts))

    xp = x_prompt.reshape(mp, d)
    xs = x_sample.reshape(ms, d)
    mem2d = mem_prompt.reshape(bp * n_mem, d)
    outs = {k: [] for k in ('ak_p', 'av_p', 'bk_p', 'bv_p', 'mk_p', 'mv_p', 'cv_p',
                            'ak_s', 'av_s', 'bk_s', 'bv_s', 'cv_s')}
    for l in range(depth):
        W = _layer_weights(l, P)
        lam_init = 0.8 - 0.6 * math.exp(-0.3 * l)
        lam_params = jnp.stack([b_lam_q1[l], b_lam_k1[l], b_lam_q2[l], b_lam_k2[l]]).astype(F32)
        bias_p = _band_bias_prompt(a_rel_bias[l])
        bias_s = jnp.where(valid_a_s[None], a_rel_bias[l][:, idx_a_s], NEG_INF)

        h = rmsnorm_cast(xp, norm_mix_g[l])
        qa, ka, va, qb, kb, vb, qm, gates = _mixer_inputs(h, W, P, l, rope_p, tm_p)
        oa = attn_a_prompt(qa, ka, va, bias_p, bp, tp)
        ob = attn_b_prompt(qb, kb, vb, lam_params, b_subln_g[l], lam_init, bp, tp)
        hm = rmsnorm_cast(mem2d, mem_norm_g[l])
        tm_m = _row_tile(bp * n_mem, 1024)
        mk = _proj_call(functools.partial(_proj_headnorm_kernel, hd=hd_m), hm, W['w_mem_kv'], 0, bw, F32,
                        [(m_k_norm_g[l].reshape(1, hd_m), pl.BlockSpec((1, hd_m), lambda i, j: (0, 0)))],
                        tm=tm_m, name="proj_mk")
        mv = _proj_call(_proj_plain_kernel, hm, W['w_mem_kv'], bw, bw, F32, [], tm=tm_m, name="proj_mv")
        om = attn_m(qm, mk, mv, bp, tp)
        zeros_state = jnp.zeros((bp, CONV_W - 1, W['f_pad']), F32)
        xp, conv_new = _finish_layer(xp, (oa, ob, om), gates, W, P, l, zeros_state, bp, tp, tm_p)
        outs['ak_p'].append(ka.reshape(bp, tp, h_a, HD_A)[:, tp - a_keep:])
        outs['av_p'].append(va.reshape(bp, tp, h_a, HD_A)[:, tp - a_keep:])
        outs['bk_p'].append(kb.reshape(bp, tp, h_b, 2 * DIFF_HD))
        outs['bv_p'].append(vb.reshape(bp, tp, h_b, 2 * DIFF_HD))
        outs['mk_p'].append(mk.reshape(bp, n_mem, H_M, hd_m))
        outs['mv_p'].append(mv.reshape(bp, n_mem, H_M, hd_m))
        outs['cv_p'].append(conv_new)

        h = rmsnorm_cast(xs, norm_mix_g[l])
        qa, ka, va, qb, kb, vb, qm, gates = _mixer_inputs(h, W, P, l, rope_s, tm_s)
        oa = attn_a_sample(qa, ka, va, cache_a_k, cache_a_v, l, bias_s[:, :, :a_len], bias_s[:, :, a_len:], bs, ts)
        ob = attn_b_sample(qb, kb, vb, cache_b_k, cache_b_v, l, lam_params, b_subln_g[l], lam_init, bs, ts)
        om = attn_m(qm, cache_mem_k, cache_mem_v, bs, ts, layer=l)
        state = _pad_cols(state_ffn_conv[l], W['f_pad'])
        xs, conv_new = _finish_layer(xs, (oa, ob, om), gates, W, P, l, state, bs, ts, tm_s)
        outs['ak_s'].append(ka.reshape(bs, ts, h_a, HD_A))
        outs['av_s'].append(va.reshape(bs, ts, h_a, HD_A))
        outs['bk_s'].append(kb.reshape(bs, ts, h_b, 2 * DIFF_HD))
        outs['bv_s'].append(vb.reshape(bs, ts, h_b, 2 * DIFF_HD))
        outs['cv_s'].append(conv_new)

    stack = lambda k: jnp.stack(outs[k])
    return (xp.reshape(bp, tp, d), xs.reshape(bs, ts, d),
            stack('ak_p'), stack('av_p'), stack('bk_p'), stack('bv_p'), stack('mk_p'), stack('mv_p'), stack('cv_p'),
            stack('ak_s'), stack('av_s'), stack('bk_s'), stack('bv_s'), stack('cv_s'))
```

```python
import functools
import math

import numpy as np
import jax
import jax.numpy as jnp
from jax import lax
from jax.experimental import pallas as pl
from jax.experimental.pallas import tpu as pltpu

F32 = jnp.float32
BF16 = jnp.bfloat16

CHUNK = 64
BAND_CHUNKS = 8
BAND_PAST = BAND_CHUNKS * CHUNK
REL_CLIP = 128
HD_A = 128
DIFF_HD = 64
ROT_DIM = DIFF_HD // 4
ROPE_THETA = 500000.0
H_M = 4
N_BRANCH = 3
CONV_W = 3
EPS = 1e-6
NEG_INF = -1e30

V7X_LANES = 128
V7X_VMEM_BYTES = 64 * 1024 * 1024
MIB = 1024 * 1024


def _compiler_params(semantics, vmem_estimate_bytes):
    limit = min(int(vmem_estimate_bytes * 1.25) + 8 * MIB, V7X_VMEM_BYTES - 4 * MIB)
    return pltpu.CompilerParams(dimension_semantics=semantics, vmem_limit_bytes=limit)


def _row_tile(m, target):
    t = min(m, target)
    assert m % t == 0, (m, t)
    return t


def _rmsnorm_kernel(x_ref, g_ref, o_ref):
    x = x_ref[...]
    ms = jnp.mean(x * x, axis=-1, keepdims=True)
    o_ref[...] = (x * lax.rsqrt(ms + EPS) * g_ref[...]).astype(o_ref.dtype)


def rmsnorm_cast(x, g):
    m, d = x.shape
    tm = _row_tile(m, 512)
    return pl.pallas_call(
        _rmsnorm_kernel,
        grid=(m // tm,),
        in_specs=[pl.BlockSpec((tm, d), lambda i: (i, 0)), pl.BlockSpec((1, d), lambda i: (0, 0))],
        out_specs=pl.BlockSpec((tm, d), lambda i: (i, 0)),
        out_shape=jax.ShapeDtypeStruct((m, d), BF16),
        compiler_params=_compiler_params(("parallel",), 2 * tm * d * 6),
        name="rmsnorm_cast",
    )(x, g.reshape(1, d))


def _dot(a, b):
    return jnp.dot(a, b, preferred_element_type=F32)


def _dot_nt(a, b):
    return lax.dot_general(a, b, (((1,), (1,)), ((), ())), preferred_element_type=F32)


def _proj_plain_kernel(h_ref, w_ref, o_ref):
    o_ref[...] = _dot(h_ref[...], w_ref[...]).astype(o_ref.dtype)


def _headnorm_store(acc, g, o_ref, hd):
    for k in range(acc.shape[1] // hd):
        s = acc[:, k * hd:(k + 1) * hd]
        ms = jnp.mean(s * s, axis=-1, keepdims=True)
        o_ref[:, k * hd:(k + 1) * hd] = (s * lax.rsqrt(ms + EPS) * g).astype(o_ref.dtype)


def _norm_rope_store(acc, g, tab_ref, o_ref):
    cos = tab_ref[:, 0:V7X_LANES]
    sin_up = tab_ref[:, V7X_LANES:2 * V7X_LANES]
    sin_dn = tab_ref[:, 2 * V7X_LANES:3 * V7X_LANES]
    half = ROT_DIM // 2
    left = lax.broadcasted_iota(jnp.int32, (acc.shape[0], V7X_LANES), 1) < DIFF_HD
    for k in range(acc.shape[1] // V7X_LANES):
        s = acc[:, k * V7X_LANES:(k + 1) * V7X_LANES]
        sq = s * s
        tot = jnp.sum(sq, axis=-1, keepdims=True)
        lsum = jnp.sum(jnp.where(left, sq, 0.0), axis=-1, keepdims=True)
        ms = jnp.where(left, lsum, tot - lsum) * (1.0 / DIFF_HD)
        y = s * lax.rsqrt(ms + EPS) * g
        out = (y * cos + pltpu.roll(y, half, 1) * sin_up
               + pltpu.roll(y, V7X_LANES - half, 1) * sin_dn)
        o_ref[:, k * V7X_LANES:(k + 1) * V7X_LANES] = out.astype(o_ref.dtype)


def _proj_headnorm_kernel(h_ref, w_ref, g_ref, o_ref, *, hd):
    _headnorm_store(_dot(h_ref[...], w_ref[...]), g_ref[...], o_ref, hd)


def _proj_in_kernel(h_ref, w_ref, gqa_ref, gka_ref, gqb_ref, gkb_ref, gqm_ref, tab_ref,
                    qa_ref, ka_ref, va_ref, qb_ref, kb_ref, vb_ref, qm_ref, *, blocks_per_group, hd_m):
    acc = _dot(h_ref[...], w_ref[...])
    group = pl.program_id(1) // blocks_per_group
    epilogues = (
        lambda: _headnorm_store(acc, gqa_ref[...], qa_ref, HD_A),
        lambda: _headnorm_store(acc, gka_ref[...], ka_ref, HD_A),
        lambda: va_ref.__setitem__(Ellipsis, acc),
        lambda: _norm_rope_store(acc, gqb_ref[...], tab_ref, qb_ref),
        lambda: _norm_rope_store(acc, gkb_ref[...], tab_ref, kb_ref),
        lambda: vb_ref.__setitem__(Ellipsis, acc),
        lambda: _headnorm_store(acc, gqm_ref[...], qm_ref, hd_m),
    )
    for n, fn in enumerate(epilogues):
        pl.when(group == n)(fn)


def proj_in(h, w_in, gains, rope_tab, *, tm, tn=512):
    m, d = h.shape
    bw = d // 2
    hd_m = bw // H_M
    tm = _row_tile(m, tm)
    assert bw % tn == 0 and rope_tab.shape[0] % tm == 0
    bpg = bw // tn
    n_tab = rope_tab.shape[0] // tm
    g_a_q, g_a_k, g_b_q, g_b_k, g_m_q = gains
    tile2 = lambda g: jnp.tile(g.reshape(1, -1), (1, 2))
    gain_args = [g_a_q.reshape(1, HD_A), g_a_k.reshape(1, HD_A), tile2(g_b_q), tile2(g_b_k), g_m_q.reshape(1, hd_m)]
    const = lambda a: pl.BlockSpec(a.shape, lambda i, j: (0, 0))

    def out_spec(n):
        return pl.BlockSpec((tm, tn), lambda i, j: (i, jnp.clip(j - n * bpg, 0, bpg - 1)))

    dtypes = (BF16, F32, F32, BF16, F32, F32, BF16)
    out_bytes = sum(tm * tn * jnp.dtype(t).itemsize for t in dtypes)
    vmem = 2 * (tm * d * 2 + d * tn * 2 + tm * 3 * V7X_LANES * 4 + out_bytes) + 2 * tm * tn * 4
    return pl.pallas_call(
        functools.partial(_proj_in_kernel, blocks_per_group=bpg, hd_m=hd_m),
        grid=(m // tm, 7 * bpg),
        in_specs=[pl.BlockSpec((tm, d), lambda i, j: (i, 0)), pl.BlockSpec((d, tn), lambda i, j: (0, j))]
                 + [const(g) for g in gain_args]
                 + [pl.BlockSpec((tm, 3 * V7X_LANES), lambda i, j: (i % n_tab, 0))],
        out_specs=[out_spec(n) for n in range(7)],
        out_shape=[jax.ShapeDtypeStruct((m, bw), t) for t in dtypes],
        compiler_params=_compiler_params(("parallel", "arbitrary"), vmem),
        name="proj_in",
    )(h, w_in, *gain_args, rope_tab)


def _proj_residual_kernel(h_ref, w_ref, x_ref, o_ref):
    o_ref[...] = x_ref[...] + _dot(h_ref[...], w_ref[...])


def _proj_call(kernel_fn, h, w, col0, ncols, out_dtype, extras, *, tm=1024, tn=512, name):
    m, k = h.shape
    tm = _row_tile(m, tm)
    tn = min(tn, ncols)
    assert ncols % tn == 0 and col0 % tn == 0, (ncols, col0, tn)
    cb = col0 // tn
    in_specs = [pl.BlockSpec((tm, k), lambda i, j: (i, 0)),
                pl.BlockSpec((k, tn), lambda i, j: (0, j + cb))]
    in_specs += [spec for _, spec in extras]
    extra_bytes = sum(int(np.prod(spec.block_shape)) * a.dtype.itemsize for a, spec in extras)
    vmem = 2 * (tm * k * 2 + k * tn * 2 + tm * tn * 4 + extra_bytes) + tm * tn * 8
    return pl.pallas_call(
        kernel_fn,
        grid=(m // tm, ncols // tn),
        in_specs=in_specs,
        out_specs=pl.BlockSpec((tm, tn), lambda i, j: (i, j)),
        out_shape=jax.ShapeDtypeStruct((m, ncols), out_dtype),
        compiler_params=_compiler_params(("parallel", "arbitrary"), vmem),
        name=name,
    )(h, w, *[a for a, _ in extras])


def _rope_table(pos):
    half = ROT_DIM // 2
    inv_freq = jnp.exp(jnp.arange(half, dtype=F32) * (-2.0 * math.log(ROPE_THETA) / ROT_DIM))
    ang = pos.astype(F32)[:, None] * inv_freq[None, :]
    cos = jnp.cos(ang)
    sin = jnp.sin(ang)
    p = pos.shape[0]
    rest = DIFF_HD - ROT_DIM
    c64 = jnp.concatenate([cos, cos, jnp.ones((p, rest), F32)], axis=1)
    up64 = jnp.concatenate([jnp.zeros((p, half), F32), sin, jnp.zeros((p, rest), F32)], axis=1)
    dn64 = jnp.concatenate([-sin, jnp.zeros((p, half + rest), F32)], axis=1)
    return jnp.concatenate([c64, c64, up64, up64, dn64, dn64], axis=1)


A_QBLK = 4 * CHUNK
A_KBLK = BAND_PAST + A_QBLK
A_BIAS_W = 1024


def _rel_bias_row(tab):
    assert A_KBLK + A_QBLK - 1 <= A_BIAS_W
    lo = BAND_PAST - REL_CLIP
    hi = BAND_PAST + REL_CLIP + 1
    rep = lambda col, n: jnp.repeat(tab[:, col:col + 1], n, axis=1)
    row = jnp.concatenate([rep(0, lo), tab, rep(2 * REL_CLIP, A_KBLK - hi), rep(0, A_BIAS_W - A_KBLK)], axis=1)
    return row[:, None, :]


def _toeplitz_bias(row, rows, width):
    full = pltpu.roll(jnp.broadcast_to(row, (rows, A_BIAS_W)), 0, 1, stride=1, stride_axis=0)
    return full[:, :width]


def _softmax_pv(s, v):
    m = jnp.max(s, axis=-1, keepdims=True)
    p = jnp.exp(s - m)
    l = jnp.sum(p, axis=-1, keepdims=True)
    return _dot(p.astype(BF16), v), l


def _attn_a_prompt_kernel(q_ref, k_ref, v_ref, row_ref, o_ref, kb, vb, *, t):
    kb[...] = k_ref[0].astype(BF16)
    vb[...] = v_ref[0].astype(BF16)
    scale = HD_A ** -0.5
    qc = lax.broadcasted_iota(jnp.int32, (A_QBLK, A_KBLK), 0) // CHUNK
    kc = lax.broadcasted_iota(jnp.int32, (A_QBLK, A_KBLK), 1) // CHUNK
    inband = (kc >= qc) & (kc <= qc + BAND_CHUNKS)
    bias = jnp.where(inband, _toeplitz_bias(row_ref[0], A_QBLK, A_KBLK), NEG_INF)
    for i in range(t // A_QBLK):
        r0 = i * A_QBLK
        k0 = max(r0 - BAND_PAST, 0)
        k1 = r0 + A_QBLK
        q = q_ref[0, r0:k1, :]
        s = _dot_nt(q, kb[k0:k1, :]) * scale + bias[:, A_KBLK - (k1 - k0):]
        o, l = _softmax_pv(s, vb[k0:k1, :])
        o_ref[0, r0:k1, :] = (o / l).astype(o_ref.dtype)


def attn_a_prompt(qa, ka, va, bias_row, b, t):
    h = qa.shape[1] // HD_A
    assert t % A_QBLK == 0 and A_QBLK % V7X_LANES == 0
    q3, k3, v3 = (a.reshape(b, t, h * HD_A) for a in (qa, ka, va))
    spec = pl.BlockSpec((1, t, HD_A), lambda bi, hi: (bi, 0, hi))
    vmem = 2 * t * HD_A * (2 + 4 + 4 + 2) + 2 * t * HD_A * 2 + 24 * MIB
    out = pl.pallas_call(
        functools.partial(_attn_a_prompt_kernel, t=t),
        grid=(b, h),
        in_specs=[spec, spec, spec, pl.BlockSpec((1, 1, A_BIAS_W), lambda bi, hi: (hi, 0, 0))],
        out_specs=spec,
        out_shape=jax.ShapeDtypeStruct((b, t, h * HD_A), BF16),
        scratch_shapes=[pltpu.VMEM((t, HD_A), BF16), pltpu.VMEM((t, HD_A), BF16)],
        compiler_params=_compiler_params(("parallel", "parallel"), vmem),
        name="attn_a_prompt",
    )(q3, k3, v3, bias_row)
    return out.reshape(b * t, h * HD_A)


def _attn_a_sample_kernel(q_ref, kn_ref, vn_ref, row_ref, mask_ref, kc_hbm, vc_hbm, o_ref, kbuf, vbuf, sem,
                          *, layer, heads, a_len):
    b = pl.program_id(0)
    nb = pl.num_programs(0)
    scale = HD_A ** -0.5
    t = q_ref.shape[2]

    def copies(bi, slot):
        out = []
        for h in range(heads):
            out.append(pltpu.make_async_copy(kc_hbm.at[layer, bi, :, h, :], kbuf.at[slot, h], sem.at[0, slot, h]))
            out.append(pltpu.make_async_copy(vc_hbm.at[layer, bi, :, h, :], vbuf.at[slot, h], sem.at[1, slot, h]))
        return out

    @pl.when(b == 0)
    def _():
        for c in copies(b, 0):
            c.start()

    slot = b % 2

    @pl.when(b + 1 < nb)
    def _():
        for c in copies(b + 1, 1 - slot):
            c.start()

    for c in copies(b, slot):
        c.wait()
    for h in range(heads):
        q = q_ref[0, h]
        bias = _toeplitz_bias(row_ref[h], t, a_len + t) + mask_ref[...]
        sc = _dot_nt(q, kbuf[slot, h].astype(BF16)) * scale + bias[:, :a_len]
        sn = _dot_nt(q, kn_ref[0, h].astype(BF16)) * scale + bias[:, a_len:]
        m = jnp.maximum(jnp.max(sc, axis=-1, keepdims=True), jnp.max(sn, axis=-1, keepdims=True))
        pc = jnp.exp(sc - m)
        pn = jnp.exp(sn - m)
        l = jnp.sum(pc, axis=-1, keepdims=True) + jnp.sum(pn, axis=-1, keepdims=True)
        o = _dot(pc.astype(BF16), vbuf[slot, h].astype(BF16)) + _dot(pn.astype(BF16), vn_ref[0, h].astype(BF16))
        o_ref[0, h] = (o / l).astype(o_ref.dtype)


def _by_head(a, b, t, heads, hd):
    return a.reshape(b, t, heads, hd).transpose(0, 2, 1, 3)


def attn_a_sample(qa, ka, va, cache_k, cache_v, layer, bias_row, mask, b, t):
    heads = qa.shape[1] // HD_A
    a_len = cache_k.shape[2]
    assert a_len == BAND_PAST and a_len + t <= A_KBLK
    head_spec = pl.BlockSpec((1, heads, t, HD_A), lambda bi: (bi, 0, 0, 0))
    vmem = 2 * 2 * heads * a_len * HD_A * 4 + 2 * 4 * heads * t * HD_A * 4 + 16 * MIB
    out = pl.pallas_call(
        functools.partial(_attn_a_sample_kernel, layer=layer, heads=heads, a_len=a_len),
        grid=(b,),
        in_specs=[head_spec, head_spec, head_spec,
                  pl.BlockSpec(bias_row.shape, lambda bi: (0, 0, 0)),
                  pl.BlockSpec(mask.shape, lambda bi: (0, 0)),
                  pl.BlockSpec(memory_space=pl.ANY), pl.BlockSpec(memory_space=pl.ANY)],
        out_specs=head_spec,
        out_shape=jax.ShapeDtypeStruct((b, heads, t, HD_A), BF16),
        scratch_shapes=[pltpu.VMEM((2, heads, a_len, HD_A), F32), pltpu.VMEM((2, heads, a_len, HD_A), F32),
                        pltpu.SemaphoreType.DMA((2, 2, heads))],
        compiler_params=_compiler_params(("arbitrary",), vmem),
        name="attn_a_sample",
    )(*(_by_head(a, b, t, heads, HD_A) for a in (qa, ka, va)), bias_row, mask, cache_k, cache_v)
    return out.transpose(0, 2, 1, 3).reshape(b * t, heads * HD_A)


def _diff_lambda(lam_ref, lam_init):
    v = lam_ref[...]
    d1 = jnp.sum(v[0:1] * v[1:2], axis=-1, keepdims=True)
    d2 = jnp.sum(v[2:3] * v[3:4], axis=-1, keepdims=True)
    return jnp.exp(d1) - jnp.exp(d2) + lam_init


def _split_diff_queries(q):
    lane = lax.broadcasted_iota(jnp.int32, q.shape, 1)
    qs = q * jnp.asarray(DIFF_HD ** -0.5, q.dtype)
    zero = jnp.zeros_like(qs)
    return jnp.where(lane < DIFF_HD, qs, zero), jnp.where(lane >= DIFF_HD, qs, zero)


def _stack_diff_queries(q):
    return jnp.concatenate(_split_diff_queries(q), axis=0)


def _diff_post(o, g, post_scale):
    ms = jnp.mean(o * o, axis=-1, keepdims=True)
    return (o * lax.rsqrt(ms + EPS) * g) * post_scale


def _diff_finish(l, acc, lam, g, post_scale, tq):
    o = acc[:tq] / l[:tq] - lam * (acc[tq:] / l[tq:])
    return _diff_post(o, g, post_scale)


B_TQ = 8 * CHUNK


def _online_step(carry, s, v):
    m, l, acc = carry
    m_new = jnp.maximum(m, jnp.max(s, axis=-1, keepdims=True))
    alpha = jnp.exp(m - m_new)
    p = jnp.exp(s - m_new)
    l = alpha * l + jnp.sum(p, axis=-1, keepdims=True)
    acc = alpha * acc + _dot(p.astype(BF16), v)
    return m_new, l, acc


def _attn_b_prompt_kernel(lam_ref, q_ref, k_ref, v_ref, g_ref, o_ref, kb, vb, *, t, lam_init):
    tq = B_TQ
    kb[...] = k_ref[0].astype(BF16)
    vb[...] = v_ref[0].astype(BF16)
    lam = _diff_lambda(lam_ref, lam_init)
    row = lax.broadcasted_iota(jnp.int32, (tq, tq), 0)
    col = lax.broadcasted_iota(jnp.int32, (tq, tq), 1)
    diag_ok = (col // CHUNK) <= (row // CHUNK)
    for qi in range(t // tq):
        q0 = qi * tq
        qs = _split_diff_queries(q_ref[0, q0:q0 + tq, :])
        spans = ([(0, q0, False)] if q0 else []) + [(q0, tq, True)]
        outs = []
        for c in range(2):
            carry = (jnp.full((tq, 1), NEG_INF, F32), jnp.zeros((tq, 1), F32), jnp.zeros((tq, 2 * DIFF_HD), F32))
            for k0, width, masked in spans:
                s = _dot_nt(qs[c], kb[k0:k0 + width, :])
                if masked:
                    s = jnp.where(diag_ok, s, NEG_INF)
                carry = _online_step(carry, s, vb[k0:k0 + width, :])
            outs.append(carry[2] / carry[1])
        o = outs[0] - lam * outs[1]
        o_ref[0, q0:q0 + tq, :] = _diff_post(o, g_ref[...], 1.0 - lam_init).astype(o_ref.dtype)


def attn_b_prompt(qb, kb, vb, lam_params, subln_g, lam_init, b, t):
    hd = 2 * DIFF_HD
    heads = qb.shape[1] // hd
    assert t % B_TQ == 0
    q3, k3, v3 = (a.reshape(b, t, heads * hd) for a in (qb, kb, vb))
    spec = pl.BlockSpec((1, t, hd), lambda bi, hi: (bi, 0, hi))
    vmem = 2 * t * hd * (2 + 4 + 4 + 2) + 2 * t * hd * 2 + 32 * MIB
    out = pl.pallas_call(
        functools.partial(_attn_b_prompt_kernel, t=t, lam_init=lam_init),
        grid=(b, heads),
        in_specs=[pl.BlockSpec(lam_params.shape, lambda bi, hi: (0, 0)), spec, spec, spec,
                  pl.BlockSpec((1, hd), lambda bi, hi: (0, 0))],
        out_specs=spec,
        out_shape=jax.ShapeDtypeStruct((b, t, heads * hd), BF16),
        scratch_shapes=[pltpu.VMEM((t, hd), BF16), pltpu.VMEM((t, hd), BF16)],
        compiler_params=_compiler_params(("parallel", "parallel"), vmem),
        name="attn_b_prompt",
    )(lam_params, q3, k3, v3, subln_g.reshape(1, hd))
    return out.reshape(b * t, heads * hd)


def _attn_b_sample_kernel(lam_ref, q_ref, kn_ref, vn_ref, g_ref, kc_hbm, vc_hbm, o_ref, kbuf, vbuf, sem,
                          *, layer, heads, t, lam_init):
    b = pl.program_id(0)
    nb = pl.num_programs(0)

    def copies(bi, h, slot):
        return (pltpu.make_async_copy(kc_hbm.at[layer, bi, :, h, :], kbuf.at[slot], sem.at[0, slot]),
                pltpu.make_async_copy(vc_hbm.at[layer, bi, :, h, :], vbuf.at[slot], sem.at[1, slot]))

    def start(bi, h, slot):
        for c in copies(bi, h, slot):
            c.start()

    @pl.when(b == 0)
    def _():
        start(b, 0, 0)

    lam = _diff_lambda(lam_ref, lam_init)
    for h in range(heads):
        slot = h % 2
        if h + 1 < heads:
            start(b, h + 1, 1 - slot)
        else:
            @pl.when(b + 1 < nb)
            def _():
                start(b + 1, 0, 1 - slot)
        for c in copies(b, h, slot):
            c.wait()
        q2 = _stack_diff_queries(q_ref[0, h])
        sc = _dot_nt(q2, kbuf[slot].astype(BF16))
        sn = _dot_nt(q2, kn_ref[0, h].astype(BF16))
        m = jnp.maximum(jnp.max(sc, axis=-1, keepdims=True), jnp.max(sn, axis=-1, keepdims=True))
        pc = jnp.exp(sc - m)
        pn = jnp.exp(sn - m)
        l = jnp.sum(pc, axis=-1, keepdims=True) + jnp.sum(pn, axis=-1, keepdims=True)
        acc = _dot(pc.astype(BF16), vbuf[slot].astype(BF16)) + _dot(pn.astype(BF16), vn_ref[0, h].astype(BF16))
        o_ref[0, h] = _diff_finish(l, acc, lam, g_ref[...], 1.0 - lam_init, t).astype(o_ref.dtype)


def attn_b_sample(qb, kb, vb, cache_k, cache_v, layer, lam_params, subln_g, lam_init, b, t):
    hd = 2 * DIFF_HD
    heads = qb.shape[1] // hd
    assert heads % 2 == 0
    past = cache_k.shape[2]
    head_spec = pl.BlockSpec((1, heads, t, hd), lambda bi: (bi, 0, 0, 0))
    vmem = 2 * 2 * past * hd * 4 + 2 * 3 * heads * t * hd * 4 + 12 * 2 * t * past * 4
    out = pl.pallas_call(
        functools.partial(_attn_b_sample_kernel, layer=layer, heads=heads, t=t, lam_init=lam_init),
        grid=(b,),
        in_specs=[pl.BlockSpec(lam_params.shape, lambda bi: (0, 0)), head_spec, head_spec, head_spec,
                  pl.BlockSpec((1, hd), lambda bi: (0, 0)),
                  pl.BlockSpec(memory_space=pl.ANY), pl.BlockSpec(memory_space=pl.ANY)],
        out_specs=head_spec,
        out_shape=jax.ShapeDtypeStruct((b, heads, t, hd), BF16),
        scratch_shapes=[pltpu.VMEM((2, past, hd), F32), pltpu.VMEM((2, past, hd), F32),
                        pltpu.SemaphoreType.DMA((2, 2))],
        compiler_params=_compiler_params(("arbitrary",), vmem),
        name="attn_b_sample",
    )(lam_params, *(_by_head(a, b, t, heads, hd) for a in (qb, kb, vb)), subln_g.reshape(1, hd), cache_k, cache_v)
    return out.transpose(0, 2, 1, 3).reshape(b * t, heads * hd)


def _attn_m_kernel(q_ref, k_ref, v_ref, o_ref, *, heads, hd, native):
    scale = hd ** -0.5
    for h in range(heads):
        sl = slice(h * hd, (h + 1) * hd)
        if native:
            k = k_ref[0, 0, :, h, :]
            v = v_ref[0, 0, :, h, :]
        else:
            k = k_ref[0, :, sl]
            v = v_ref[0, :, sl]
        s = _dot_nt(q_ref[0, :, sl], k.astype(BF16)) * scale
        o, l = _softmax_pv(s, v.astype(BF16))
        o_ref[0, :, sl] = (o / l).astype(o_ref.dtype)


def attn_m(qm, mem_k, mem_v, b, t, *, layer=None, tq=512):
    width = qm.shape[1]
    hd = width // H_M
    tq = min(tq, t)
    assert t % tq == 0
    q3 = qm.reshape(b, t, width)
    q_spec = pl.BlockSpec((1, tq, width), lambda bi, qi: (bi, qi, 0))
    if layer is None:
        n = mem_k.shape[0] // b
        mem_k = mem_k.reshape(b, n, width)
        mem_v = mem_v.reshape(b, n, width)
        kv_spec = pl.BlockSpec((1, n, width), lambda bi, qi: (bi, 0, 0))
        kv_bytes = n * width * 4
    else:
        n = mem_k.shape[2]
        kv_spec = pl.BlockSpec((1, 1, n, H_M, hd), lambda bi, qi: (layer, bi, 0, 0, 0))
        kv_bytes = n * 8 * hd * 4
    vmem = 2 * (2 * kv_bytes + 2 * tq * width * 2) + 8 * tq * n * 4
    out = pl.pallas_call(
        functools.partial(_attn_m_kernel, heads=H_M, hd=hd, native=layer is not None),
        grid=(b, t // tq),
        in_specs=[q_spec, kv_spec, kv_spec],
        out_specs=q_spec,
        out_shape=jax.ShapeDtypeStruct((b, t, width), BF16),
        compiler_params=_compiler_params(("parallel", "arbitrary"), vmem),
        name="attn_m",
    )(q3, mem_k, mem_v)
    return out.reshape(b * t, width)


def _merge_kernel(h_ref, oa_ref, ob_ref, om_ref, wga_ref, wgb_ref, wgm_ref, gb_ref, wbr_ref, o_ref):
    h = h_ref[...]
    acc = None
    for n, (o_n, wg_n) in enumerate(((oa_ref, wga_ref), (ob_ref, wgb_ref), (om_ref, wgm_ref))):
        gate = jax.nn.sigmoid(_dot(h, wg_n[...]) + gb_ref[n])
        term = gate * _dot(o_n[...], wbr_ref[n])
        acc = term if acc is None else acc + term
    o_ref[...] = acc.astype(o_ref.dtype)


def merge_branches(h, oa, ob, om, w_in, gate_col0, gate_b, w_br, *, tm, tn=256):
    m, bw = oa.shape
    d = w_br.shape[2]
    tm = _row_tile(m, tm)
    assert d % tn == 0 and gate_col0 % tn == 0
    nj = d // tn
    g0 = gate_col0 // tn
    row_spec = lambda width: pl.BlockSpec((tm, width), lambda i, j: (i, 0))
    gate_w_specs = [pl.BlockSpec((d, tn), functools.partial(lambda i, j, n: (0, g0 + n * nj + j), n=n))
                    for n in range(N_BRANCH)]
    vmem = 2 * (tm * d * 2 + 3 * tm * bw * 2 + 3 * d * tn * 2 + 3 * bw * tn * 2 + tm * tn * 2) + 6 * tm * tn * 4
    return pl.pallas_call(
        _merge_kernel,
        grid=(m // tm, nj),
        in_specs=[row_spec(d), row_spec(bw), row_spec(bw), row_spec(bw)] + gate_w_specs
                 + [pl.BlockSpec((N_BRANCH, 1, tn), lambda i, j: (0, 0, j)),
                    pl.BlockSpec((N_BRANCH, bw, tn), lambda i, j: (0, 0, j))],
        out_specs=pl.BlockSpec((tm, tn), lambda i, j: (i, j)),
        out_shape=jax.ShapeDtypeStruct((m, d), BF16),
        compiler_params=_compiler_params(("parallel", "arbitrary"), vmem),
        name="merge_branches",
    )(h, oa, ob, om, w_in, w_in, w_in, gate_b, w_br)


def _ffn_up_kernel(h_ref, wa_ref, wb_ref, cw_ref, cb_ref, st_ref, g_ref, cn_ref, carry, *, nb, tb, blocks_per_seq):
    i = pl.program_id(0)
    j = pl.program_id(1)
    h = h_ref[...]
    a = _dot(h, wa_ref[...])
    bgate = _dot(h, wb_ref[...])
    tm, tn = a.shape
    if nb == 1:
        @pl.when((i % blocks_per_seq) == 0)
        def _():
            carry[j] = st_ref[0]

        prev = carry[j]
        p0 = jnp.broadcast_to(prev[0:1], (tm, tn))
        p1 = jnp.broadcast_to(prev[1:2], (tm, tn))
        carry[j] = a[tm - 2:tm]
        cn_ref[0] = a[tm - 2:tm]
        trow = lax.broadcasted_iota(jnp.int32, (tm, tn), 0)
    else:
        st = st_ref[...]
        p0 = jnp.broadcast_to(st[:, 0:1, :], (nb, tb, tn)).reshape(tm, tn)
        p1 = jnp.broadcast_to(st[:, 1:2, :], (nb, tb, tn)).reshape(tm, tn)
        cn_ref[...] = a.reshape(nb, tb, tn)[:, tb - 2:tb, :]
        trow = lax.broadcasted_iota(jnp.int32, (nb, tb, tn), 1).reshape(tm, tn)
    am1 = jnp.where(trow == 0, p1, pltpu.roll(a, 1, 0))
    am2 = jnp.where(trow == 0, p0, jnp.where(trow == 1, p1, pltpu.roll(a, 2, 0)))
    cw = cw_ref[...]
    c = cb_ref[...] + am2 * cw[0:1] + am1 * cw[1:2] + a * cw[2:3]
    gelu = 0.5 * c * (1.0 + lax.erf(c * (2.0 ** -0.5)))
    g_ref[...] = (gelu * bgate).astype(g_ref.dtype)


def ffn_up(h, w_a, w_b, conv_w, conv_b, state, b, t, *, tm=1024, tn=512):
    m, d = h.shape
    f = w_a.shape[1]
    tm = _row_tile(m, tm)
    assert f % tn == 0
    if tm >= t:
        assert tm % t == 0
        nb, tb, blocks_per_seq = tm // t, t, 1
    else:
        assert t % tm == 0
        nb, tb, blocks_per_seq = 1, tm, t // tm
    if nb > 1:
        seq_map = lambda i, j: (i, 0, j)
    else:
        seq_map = lambda i, j: (i // blocks_per_seq, 0, j)
    tail_map = lambda i, j: (i, 0, j)
    w_spec = pl.BlockSpec((d, tn), lambda i, j: (0, j))
    vmem = 2 * (tm * d * 2 + 2 * d * tn * 2 + tm * tn * 2 + 2 * nb * 8 * tn * 4) + 8 * tm * tn * 4
    g, conv_new = pl.pallas_call(
        functools.partial(_ffn_up_kernel, nb=nb, tb=tb, blocks_per_seq=blocks_per_seq),
        grid=(m // tm, f // tn),
        in_specs=[pl.BlockSpec((tm, d), lambda i, j: (i, 0)), w_spec, w_spec,
                  pl.BlockSpec((CONV_W, tn), lambda i, j: (0, j)),
                  pl.BlockSpec((1, tn), lambda i, j: (0, j)),
                  pl.BlockSpec((nb, CONV_W - 1, tn), seq_map)],
        out_specs=[pl.BlockSpec((tm, tn), lambda i, j: (i, j)),
                   pl.BlockSpec((nb, CONV_W - 1, tn), tail_map)],
        out_shape=[jax.ShapeDtypeStruct((m, f), BF16),
                   jax.ShapeDtypeStruct((b * blocks_per_seq, CONV_W - 1, f), F32)],
        scratch_shapes=[pltpu.VMEM((f // tn, CONV_W - 1, tn), F32)],
        compiler_params=_compiler_params(("arbitrary", "arbitrary"), vmem),
        name="ffn_up",
    )(h, w_a, w_b, conv_w, conv_b, state)
    return g, conv_new.reshape(b, blocks_per_seq, CONV_W - 1, f)[:, -1]


def _pad_cols(a, f_pad):
    return jnp.pad(a, [(0, 0)] * (a.ndim - 1) + [(0, f_pad - a.shape[-1])])


def _layer_weights(l, P):
    d_ff = P['w_ffn_down'].shape[1]
    f_pad = -(-d_ff // 512) * 512
    w_up = P['w_ffn_up'][l]
    return {
        'w_in': P['w_in'][l].astype(BF16),
        'w_mem_kv': P['w_mem_kv'][l].astype(BF16),
        'w_branch': P['w_branch'][l].astype(BF16),
        'w_out': P['w_out'][l].astype(BF16),
        'w_up_a': _pad_cols(w_up[:, :d_ff].astype(BF16), f_pad),
        'w_up_b': _pad_cols(w_up[:, d_ff:].astype(BF16), f_pad),
        'w_down': jnp.pad(P['w_ffn_down'][l].astype(BF16), [(0, f_pad - d_ff), (0, 0)]),
        'conv_w': _pad_cols(P['ffn_conv_w'][l], f_pad),
        'conv_b': _pad_cols(P['ffn_conv_b'][l].reshape(1, d_ff), f_pad),
        'f_pad': f_pad,
        'd_ff': d_ff,
    }


def _mixer_inputs(h, W, P, l, rope_tab, tm):
    gains = (P['a_q_norm_g'][l], P['a_k_norm_g'][l], P['b_q_norm_g'][l], P['b_k_norm_g'][l], P['m_q_norm_g'][l])
    return proj_in(h, W['w_in'], gains, rope_tab, tm=tm)


def _finish_layer(x, h, outs, W, P, l, state, b, t, tm):
    d = x.shape[1]
    gate_b = P['gate_b'][l].reshape(N_BRANCH, 1, d)
    merged = merge_branches(h, *outs, W['w_in'], 7 * (d // 2), gate_b, W['w_branch'], tm=tm)
    res_spec = lambda tn: pl.BlockSpec((tm, tn), lambda i, j: (i, j))
    x = _proj_call(_proj_residual_kernel, merged, W['w_out'], 0, d, F32, [(x, res_spec(512))], tm=tm, name="proj_out")
    h = rmsnorm_cast(x, P['norm_ffn_g'][l])
    g, conv_new = ffn_up(h, W['w_up_a'], W['w_up_b'], W['conv_w'], W['conv_b'], state, b, t, tm=tm)
    x = _proj_call(_proj_residual_kernel, g, W['w_down'], 0, d, F32, [(x, res_spec(256))], tm=tm, tn=256,
                   name="ffn_down")
    return x, conv_new[:, :, :W['d_ff']]


def kernel(x_prompt, x_sample, cache_a_k, cache_a_v, cache_b_k, cache_b_v, cache_mem_k, cache_mem_v, state_ffn_conv, mem_prompt, norm_mix_g, w_in, a_q_norm_g, a_k_norm_g, a_rel_bias, b_q_norm_g, b_k_norm_g, b_lam_q1, b_lam_k1, b_lam_q2, b_lam_k2, b_subln_g, m_q_norm_g, m_k_norm_g, mem_norm_g, w_mem_kv, gate_b, w_branch, w_out, norm_ffn_g, w_ffn_up, ffn_conv_w, ffn_conv_b, w_ffn_down):
    P = {'w_in': w_in, 'a_q_norm_g': a_q_norm_g, 'a_k_norm_g': a_k_norm_g, 'b_q_norm_g': b_q_norm_g,
         'b_k_norm_g': b_k_norm_g, 'm_q_norm_g': m_q_norm_g, 'm_k_norm_g': m_k_norm_g, 'w_mem_kv': w_mem_kv,
         'gate_b': gate_b, 'w_branch': w_branch, 'w_out': w_out, 'norm_ffn_g': norm_ffn_g,
         'w_ffn_up': w_ffn_up, 'ffn_conv_w': ffn_conv_w, 'ffn_conv_b': ffn_conv_b, 'w_ffn_down': w_ffn_down}
    bp, tp, d = x_prompt.shape
    bs, ts, _ = x_sample.shape
    depth = w_in.shape[0]
    bw = d // 2
    past = cache_b_k.shape[2]
    a_len = cache_a_k.shape[2]
    n_mem = mem_prompt.shape[1]
    a_keep = min(BAND_PAST, tp)
    h_a = bw // HD_A
    h_b = bw // (2 * DIFF_HD)
    hd_m = bw // H_M
    mp, ms = bp * tp, bs * ts
    tm_p = _row_tile(mp, 1024)
    tm_s = _row_tile(ms, 1024)
    assert tm_p <= tp and tp % tm_p == 0 or tm_p % tp == 0

    pos_s = past + np.arange(ts)
    key_pos_a = np.concatenate([past - a_len + np.arange(a_len), pos_s])
    q_chunk_s = pos_s // CHUNK
    k_chunk_a = key_pos_a // CHUNK
    valid_a_s = (k_chunk_a[None, :] <= q_chunk_s[:, None]) & (k_chunk_a[None, :] >= q_chunk_s[:, None] - BAND_CHUNKS)
    mask_a_s = jnp.asarray(np.where(valid_a_s, 0.0, NEG_INF), F32)
    key_pos_b = np.concatenate([np.arange(past), pos_s])
    valid_b_s = (key_pos_b // CHUNK)[None, :] <= q_chunk_s[:, None]
    assert valid_b_s.all(), "sample queries are expected to see every cached and new differential key"

    rope_p = _rope_table(jnp.arange(max(tp, tm_p), dtype=jnp.int32) % tp)
    rope_s = _rope_table(past + (jnp.arange(max(ts, tm_s), dtype=jnp.int32) % ts))

    xp = x_prompt.reshape(mp, d)
    xs = x_sample.reshape(ms, d)
    mem2d = mem_prompt.reshape(bp * n_mem, d)
    outs = {k: [] for k in ('ak_p', 'av_p', 'bk_p', 'bv_p', 'mk_p', 'mv_p', 'cv_p',
                            'ak_s', 'av_s', 'bk_s', 'bv_s', 'cv_s')}
    for l in range(depth):
        W = _layer_weights(l, P)
        lam_init = 0.8 - 0.6 * math.exp(-0.3 * l)
        lam_params = jnp.stack([b_lam_q1[l], b_lam_k1[l], b_lam_q2[l], b_lam_k2[l]]).astype(F32)
        bias_row = _rel_bias_row(a_rel_bias[l])

        h = rmsnorm_cast(xp, norm_mix_g[l])
        qa, ka, va, qb, kb, vb, qm = _mixer_inputs(h, W, P, l, rope_p, tm_p)
        oa = attn_a_prompt(qa, ka, va, bias_row, bp, tp)
        ob = attn_b_prompt(qb, kb, vb, lam_params, b_subln_g[l], lam_init, bp, tp)
        hm = rmsnorm_cast(mem2d, mem_norm_g[l])
        tm_m = _row_tile(bp * n_mem, 1024)
        mk = _proj_call(functools.partial(_proj_headnorm_kernel, hd=hd_m), hm, W['w_mem_kv'], 0, bw, F32,
                        [(m_k_norm_g[l].reshape(1, hd_m), pl.BlockSpec((1, hd_m), lambda i, j: (0, 0)))],
                        tm=tm_m, name="proj_mk")
        mv = _proj_call(_proj_plain_kernel, hm, W['w_mem_kv'], bw, bw, F32, [], tm=tm_m, name="proj_mv")
        om = attn_m(qm, mk, mv, bp, tp)
        zeros_state = jnp.zeros((bp, CONV_W - 1, W['f_pad']), F32)
        xp, conv_new = _finish_layer(xp, h, (oa, ob, om), W, P, l, zeros_state, bp, tp, tm_p)
        outs['ak_p'].append(ka.reshape(bp, tp, h_a, HD_A)[:, tp - a_keep:])
        outs['av_p'].append(va.reshape(bp, tp, h_a, HD_A)[:, tp - a_keep:])
        outs['bk_p'].append(kb.reshape(bp, tp, h_b, 2 * DIFF_HD))
        outs['bv_p'].append(vb.reshape(bp, tp, h_b, 2 * DIFF_HD))
        outs['mk_p'].append(mk.reshape(bp, n_mem, H_M, hd_m))
        outs['mv_p'].append(mv.reshape(bp, n_mem, H_M, hd_m))
        outs['cv_p'].append(conv_new)

        h = rmsnorm_cast(xs, norm_mix_g[l])
        qa, ka, va, qb, kb, vb, qm = _mixer_inputs(h, W, P, l, rope_s, tm_s)
        oa = attn_a_sample(qa, ka, va, cache_a_k, cache_a_v, l, bias_row, mask_a_s, bs, ts)
        ob = attn_b_sample(qb, kb, vb, cache_b_k, cache_b_v, l, lam_params, b_subln_g[l], lam_init, bs, ts)
        om = attn_m(qm, cache_mem_k, cache_mem_v, bs, ts, layer=l)
        state = _pad_cols(state_ffn_conv[l], W['f_pad'])
        xs, conv_new = _finish_layer(xs, h, (oa, ob, om), W, P, l, state, bs, ts, tm_s)
        outs['ak_s'].append(ka.reshape(bs, ts, h_a, HD_A))
        outs['av_s'].append(va.reshape(bs, ts, h_a, HD_A))
        outs['bk_s'].append(kb.reshape(bs, ts, h_b, 2 * DIFF_HD))
        outs['bv_s'].append(vb.reshape(bs, ts, h_b, 2 * DIFF_HD))
        outs['cv_s'].append(conv_new)

    stack = lambda k: jnp.stack(outs[k])
    return (xp.reshape(bp, tp, d), xs.reshape(bs, ts, d),
            stack('ak_p'), stack('av_p'), stack('bk_p'), stack('bv_p'), stack('mk_p'), stack('mv_p'), stack('cv_p'),
            stack('ak_s'), stack('av_s'), stack('bk_s'), stack('bv_s'), stack('cv_s'))
```

```python
import functools
import math

import numpy as np
import jax
import jax.numpy as jnp
from jax import lax
from jax.experimental import pallas as pl
from jax.experimental.pallas import tpu as pltpu

F32 = jnp.float32
BF16 = jnp.bfloat16

CHUNK = 64
BAND_CHUNKS = 8
BAND_PAST = BAND_CHUNKS * CHUNK
REL_CLIP = 128
HD_A = 128
DIFF_HD = 64
ROT_DIM = DIFF_HD // 4
ROPE_THETA = 500000.0
H_M = 4
N_BRANCH = 3
CONV_W = 3
EPS = 1e-6
NEG_INF = -1e30

V7X_LANES = 128
V7X_VMEM_BYTES = 64 * 1024 * 1024
MIB = 1024 * 1024


def _compiler_params(semantics, vmem_estimate_bytes):
    limit = min(int(vmem_estimate_bytes * 1.25) + 8 * MIB, V7X_VMEM_BYTES - 4 * MIB)
    return pltpu.CompilerParams(dimension_semantics=semantics, vmem_limit_bytes=limit)


def _row_tile(m, target):
    t = min(m, target)
    assert m % t == 0, (m, t)
    return t


def _rmsnorm_kernel(x_ref, g_ref, o_ref):
    x = x_ref[...]
    ms = jnp.mean(x * x, axis=-1, keepdims=True)
    o_ref[...] = (x * lax.rsqrt(ms + EPS) * g_ref[...]).astype(o_ref.dtype)


def rmsnorm_cast(x, g):
    m, d = x.shape
    tm = _row_tile(m, 512)
    return pl.pallas_call(
        _rmsnorm_kernel,
        grid=(m // tm,),
        in_specs=[pl.BlockSpec((tm, d), lambda i: (i, 0)), pl.BlockSpec((1, d), lambda i: (0, 0))],
        out_specs=pl.BlockSpec((tm, d), lambda i: (i, 0)),
        out_shape=jax.ShapeDtypeStruct((m, d), BF16),
        compiler_params=_compiler_params(("parallel",), 2 * tm * d * 6),
        name="rmsnorm_cast",
    )(x, g.reshape(1, d))


def _dot(a, b):
    return jnp.dot(a, b, preferred_element_type=F32)


def _dot_nt(a, b):
    return lax.dot_general(a, b, (((1,), (1,)), ((), ())), preferred_element_type=F32)


def _proj_plain_kernel(h_ref, w_ref, o_ref):
    o_ref[...] = _dot(h_ref[...], w_ref[...]).astype(o_ref.dtype)


def _headnorm_store(acc, g, o_ref, hd):
    for k in range(acc.shape[1] // hd):
        s = acc[:, k * hd:(k + 1) * hd]
        ms = jnp.mean(s * s, axis=-1, keepdims=True)
        o_ref[:, k * hd:(k + 1) * hd] = (s * lax.rsqrt(ms + EPS) * g).astype(o_ref.dtype)


def _norm_rope_store(acc, g, tab_ref, o_ref):
    rows, width = acc.shape
    grp_r = lax.broadcasted_iota(jnp.int32, (width, width), 0) // DIFF_HD
    grp_c = lax.broadcasted_iota(jnp.int32, (width, width), 1) // DIFF_HD
    ones_bd = jnp.where(grp_r == grp_c, 1.0, 0.0).astype(BF16)
    ms = _dot((acc * acc).astype(BF16), ones_bd) * (1.0 / DIFF_HD)
    y = acc * lax.rsqrt(ms + EPS)
    cos = tab_ref[:, 0:V7X_LANES]
    sin_up = tab_ref[:, V7X_LANES:2 * V7X_LANES]
    sin_dn = tab_ref[:, 2 * V7X_LANES:3 * V7X_LANES]
    half = ROT_DIM // 2
    for k in range(width // V7X_LANES):
        yk = y[:, k * V7X_LANES:(k + 1) * V7X_LANES] * g
        out = (yk * cos + pltpu.roll(yk, half, 1) * sin_up
               + pltpu.roll(yk, V7X_LANES - half, 1) * sin_dn)
        o_ref[:, k * V7X_LANES:(k + 1) * V7X_LANES] = out.astype(o_ref.dtype)


def _proj_headnorm_kernel(h_ref, w_ref, g_ref, o_ref, *, hd):
    _headnorm_store(_dot(h_ref[...], w_ref[...]), g_ref[...], o_ref, hd)


PROJ_SLAB = 256


def _proj_in_kernel(h_ref, w_ref, gqa_ref, gka_ref, gqb_ref, gkb_ref, gqm_ref, tab_ref,
                    qa_ref, ka_ref, va_ref, qb_ref, kb_ref, vb_ref, qm_ref, *, blocks_per_group, hd_m):
    group = pl.program_id(1) // blocks_per_group
    tn = w_ref.shape[1]

    def run(o_ref, epilogue):
        def body():
            slabs = [slice(c0, c0 + PROJ_SLAB) for c0 in range(0, tn, PROJ_SLAB)]
            accs = [_dot(h_ref[...], w_ref[:, cols]) for cols in slabs]
            for acc, cols in zip(accs, slabs):
                epilogue(acc, o_ref.at[:, cols])
        return body

    headnorm = lambda g_ref, hd: (lambda acc, o: _headnorm_store(acc, g_ref[...], o, hd))
    norm_rope = lambda g_ref: (lambda acc, o: _norm_rope_store(acc, g_ref[...], tab_ref, o))
    plain = lambda acc, o: o.__setitem__(Ellipsis, acc)
    bodies = (
        run(qa_ref, headnorm(gqa_ref, HD_A)), run(ka_ref, headnorm(gka_ref, HD_A)), run(va_ref, plain),
        run(qb_ref, norm_rope(gqb_ref)), run(kb_ref, norm_rope(gkb_ref)), run(vb_ref, plain),
        run(qm_ref, headnorm(gqm_ref, hd_m)),
    )
    for n, body in enumerate(bodies):
        pl.when(group == n)(body)


def proj_in(h, w_in, gains, rope_tab, *, tm, tn=512):
    m, d = h.shape
    bw = d // 2
    hd_m = bw // H_M
    tm = _row_tile(m, tm)
    assert bw % tn == 0 and tn % PROJ_SLAB == 0 and PROJ_SLAB % hd_m == 0 and rope_tab.shape[0] % tm == 0
    bpg = bw // tn
    n_tab = rope_tab.shape[0] // tm
    g_a_q, g_a_k, g_b_q, g_b_k, g_m_q = gains
    tile2 = lambda g: jnp.tile(g.reshape(1, -1), (1, 2))
    gain_args = [g_a_q.reshape(1, HD_A), g_a_k.reshape(1, HD_A), tile2(g_b_q), tile2(g_b_k), g_m_q.reshape(1, hd_m)]
    const = lambda a: pl.BlockSpec(a.shape, lambda i, j: (0, 0))

    def out_spec(n):
        return pl.BlockSpec((tm, tn), lambda i, j: (i, jnp.clip(j - n * bpg, 0, bpg - 1)))

    dtypes = (BF16, F32, F32, BF16, F32, F32, BF16)
    out_bytes = sum(tm * tn * jnp.dtype(t).itemsize for t in dtypes)
    vmem = 2 * (tm * d * 2 + d * tn * 2 + tm * 3 * V7X_LANES * 4 + out_bytes) + 2 * tm * tn * 4
    return pl.pallas_call(
        functools.partial(_proj_in_kernel, blocks_per_group=bpg, hd_m=hd_m),
        grid=(m // tm, 7 * bpg),
        in_specs=[pl.BlockSpec((tm, d), lambda i, j: (i, 0)), pl.BlockSpec((d, tn), lambda i, j: (0, j))]
                 + [const(g) for g in gain_args]
                 + [pl.BlockSpec((tm, 3 * V7X_LANES), lambda i, j: (i % n_tab, 0))],
        out_specs=[out_spec(n) for n in range(7)],
        out_shape=[jax.ShapeDtypeStruct((m, bw), t) for t in dtypes],
        compiler_params=_compiler_params(("parallel", "arbitrary"), vmem),
        name="proj_in",
    )(h, w_in, *gain_args, rope_tab)


def _proj_residual_kernel(h_ref, w_ref, x_ref, o_ref):
    o_ref[...] = x_ref[...] + _dot(h_ref[...], w_ref[...])


def _proj_call(kernel_fn, h, w, col0, ncols, out_dtype, extras, *, tm=1024, tn=512, name):
    m, k = h.shape
    tm = _row_tile(m, tm)
    tn = min(tn, ncols)
    assert ncols % tn == 0 and col0 % tn == 0, (ncols, col0, tn)
    cb = col0 // tn
    in_specs = [pl.BlockSpec((tm, k), lambda i, j: (i, 0)),
                pl.BlockSpec((k, tn), lambda i, j: (0, j + cb))]
    in_specs += [spec for _, spec in extras]
    extra_bytes = sum(int(np.prod(spec.block_shape)) * a.dtype.itemsize for a, spec in extras)
    vmem = 2 * (tm * k * 2 + k * tn * 2 + tm * tn * 4 + extra_bytes) + tm * tn * 8
    return pl.pallas_call(
        kernel_fn,
        grid=(m // tm, ncols // tn),
        in_specs=in_specs,
        out_specs=pl.BlockSpec((tm, tn), lambda i, j: (i, j)),
        out_shape=jax.ShapeDtypeStruct((m, ncols), out_dtype),
        compiler_params=_compiler_params(("parallel", "arbitrary"), vmem),
        name=name,
    )(h, w, *[a for a, _ in extras])


def _rope_table(pos):
    half = ROT_DIM // 2
    inv_freq = jnp.exp(jnp.arange(half, dtype=F32) * (-2.0 * math.log(ROPE_THETA) / ROT_DIM))
    ang = pos.astype(F32)[:, None] * inv_freq[None, :]
    cos = jnp.cos(ang)
    sin = jnp.sin(ang)
    p = pos.shape[0]
    rest = DIFF_HD - ROT_DIM
    c64 = jnp.concatenate([cos, cos, jnp.ones((p, rest), F32)], axis=1)
    up64 = jnp.concatenate([jnp.zeros((p, half), F32), sin, jnp.zeros((p, rest), F32)], axis=1)
    dn64 = jnp.concatenate([-sin, jnp.zeros((p, half + rest), F32)], axis=1)
    return jnp.concatenate([c64, c64, up64, up64, dn64, dn64], axis=1)


A_QBLK = 4 * CHUNK
A_KBLK = BAND_PAST + A_QBLK
A_BIAS_W = 1024


def _rel_bias_row(tab):
    assert A_KBLK + A_QBLK - 1 <= A_BIAS_W
    lo = BAND_PAST - REL_CLIP
    hi = BAND_PAST + REL_CLIP + 1
    rep = lambda col, n: jnp.repeat(tab[:, col:col + 1], n, axis=1)
    row = jnp.concatenate([rep(0, lo), tab, rep(2 * REL_CLIP, A_KBLK - hi), rep(0, A_BIAS_W - A_KBLK)], axis=1)
    return row[:, None, :]


def _toeplitz_bias(row, rows, width):
    full = pltpu.roll(jnp.broadcast_to(row, (rows, A_BIAS_W)), 0, 1, stride=1, stride_axis=0)
    return full[:, :width]


def _softmax_pv(s, v):
    m = jnp.max(s, axis=-1, keepdims=True)
    p = jnp.exp(s - m)
    l = jnp.sum(p, axis=-1, keepdims=True)
    return _dot(p.astype(BF16), v), l


def _attn_a_prompt_kernel(q_ref, k_ref, v_ref, row_ref, o_ref, kb, vb, *, t):
    kb[...] = k_ref[0].astype(BF16)
    vb[...] = v_ref[0].astype(BF16)
    scale = HD_A ** -0.5
    qc = lax.broadcasted_iota(jnp.int32, (A_QBLK, A_KBLK), 0) // CHUNK
    kc = lax.broadcasted_iota(jnp.int32, (A_QBLK, A_KBLK), 1) // CHUNK
    inband = (kc >= qc) & (kc <= qc + BAND_CHUNKS)
    bias = jnp.where(inband, _toeplitz_bias(row_ref[0], A_QBLK, A_KBLK), NEG_INF)
    for i in range(t // A_QBLK):
        r0 = i * A_QBLK
        k0 = max(r0 - BAND_PAST, 0)
        k1 = r0 + A_QBLK
        q = q_ref[0, r0:k1, :]
        s = _dot_nt(q, kb[k0:k1, :]) * scale + bias[:, A_KBLK - (k1 - k0):]
        o, l = _softmax_pv(s, vb[k0:k1, :])
        o_ref[0, r0:k1, :] = (o / l).astype(o_ref.dtype)


def attn_a_prompt(qa, ka, va, bias_row, b, t):
    h = qa.shape[1] // HD_A
    assert t % A_QBLK == 0 and A_QBLK % V7X_LANES == 0
    q3, k3, v3 = (a.reshape(b, t, h * HD_A) for a in (qa, ka, va))
    spec = pl.BlockSpec((1, t, HD_A), lambda bi, hi: (bi, 0, hi))
    vmem = 2 * t * HD_A * (2 + 4 + 4 + 2) + 2 * t * HD_A * 2 + 24 * MIB
    out = pl.pallas_call(
        functools.partial(_attn_a_prompt_kernel, t=t),
        grid=(b, h),
        in_specs=[spec, spec, spec, pl.BlockSpec((1, 1, A_BIAS_W), lambda bi, hi: (hi, 0, 0))],
        out_specs=spec,
        out_shape=jax.ShapeDtypeStruct((b, t, h * HD_A), BF16),
        scratch_shapes=[pltpu.VMEM((t, HD_A), BF16), pltpu.VMEM((t, HD_A), BF16)],
        compiler_params=_compiler_params(("parallel", "parallel"), vmem),
        name="attn_a_prompt",
    )(q3, k3, v3, bias_row)
    return out.reshape(b * t, h * HD_A)


def _attn_a_sample_kernel(q_ref, kn_ref, vn_ref, row_ref, mask_ref, kc_hbm, vc_hbm, o_ref, kbuf, vbuf, sem,
                          *, layer, heads, a_len):
    b = pl.program_id(0)
    nb = pl.num_programs(0)
    scale = HD_A ** -0.5
    t = q_ref.shape[2]

    def copies(bi, slot):
        out = []
        for h in range(heads):
            out.append(pltpu.make_async_copy(kc_hbm.at[layer, bi, :, h, :], kbuf.at[slot, h], sem.at[0, slot, h]))
            out.append(pltpu.make_async_copy(vc_hbm.at[layer, bi, :, h, :], vbuf.at[slot, h], sem.at[1, slot, h]))
        return out

    @pl.when(b == 0)
    def _():
        for c in copies(b, 0):
            c.start()

    slot = b % 2

    @pl.when(b + 1 < nb)
    def _():
        for c in copies(b + 1, 1 - slot):
            c.start()

    for c in copies(b, slot):
        c.wait()
    for h in range(heads):
        q = q_ref[0, h]
        bias = _toeplitz_bias(row_ref[h], t, a_len + t) + mask_ref[...]
        sc = _dot_nt(q, kbuf[slot, h].astype(BF16)) * scale + bias[:, :a_len]
        sn = _dot_nt(q, kn_ref[0, h].astype(BF16)) * scale + bias[:, a_len:]
        m = jnp.maximum(jnp.max(sc, axis=-1, keepdims=True), jnp.max(sn, axis=-1, keepdims=True))
        pc = jnp.exp(sc - m)
        pn = jnp.exp(sn - m)
        l = jnp.sum(pc, axis=-1, keepdims=True) + jnp.sum(pn, axis=-1, keepdims=True)
        o = _dot(pc.astype(BF16), vbuf[slot, h].astype(BF16)) + _dot(pn.astype(BF16), vn_ref[0, h].astype(BF16))
        o_ref[0, h] = (o / l).astype(o_ref.dtype)


def _by_head(a, b, t, heads, hd):
    return a.reshape(b, t, heads, hd).transpose(0, 2, 1, 3)


def attn_a_sample(qa, ka, va, cache_k, cache_v, layer, bias_row, mask, b, t):
    heads = qa.shape[1] // HD_A
    a_len = cache_k.shape[2]
    assert a_len == BAND_PAST and a_len + t <= A_KBLK
    head_spec = pl.BlockSpec((1, heads, t, HD_A), lambda bi: (bi, 0, 0, 0))
    vmem = 2 * 2 * heads * a_len * HD_A * 4 + 2 * 4 * heads * t * HD_A * 4 + 16 * MIB
    out = pl.pallas_call(
        functools.partial(_attn_a_sample_kernel, layer=layer, heads=heads, a_len=a_len),
        grid=(b,),
        in_specs=[head_spec, head_spec, head_spec,
                  pl.BlockSpec(bias_row.shape, lambda bi: (0, 0, 0)),
                  pl.BlockSpec(mask.shape, lambda bi: (0, 0)),
                  pl.BlockSpec(memory_space=pl.ANY), pl.BlockSpec(memory_space=pl.ANY)],
        out_specs=head_spec,
        out_shape=jax.ShapeDtypeStruct((b, heads, t, HD_A), BF16),
        scratch_shapes=[pltpu.VMEM((2, heads, a_len, HD_A), F32), pltpu.VMEM((2, heads, a_len, HD_A), F32),
                        pltpu.SemaphoreType.DMA((2, 2, heads))],
        compiler_params=_compiler_params(("arbitrary",), vmem),
        name="attn_a_sample",
    )(*(_by_head(a, b, t, heads, HD_A) for a in (qa, ka, va)), bias_row, mask, cache_k, cache_v)
    return out.transpose(0, 2, 1, 3).reshape(b * t, heads * HD_A)


def _diff_lambda(lam_ref, lam_init):
    v = lam_ref[...]
    d1 = jnp.sum(v[0:1] * v[1:2], axis=-1, keepdims=True)
    d2 = jnp.sum(v[2:3] * v[3:4], axis=-1, keepdims=True)
    return jnp.exp(d1) - jnp.exp(d2) + lam_init


def _split_diff_queries(q):
    lane = lax.broadcasted_iota(jnp.int32, q.shape, 1)
    qs = q * jnp.asarray(DIFF_HD ** -0.5, q.dtype)
    zero = jnp.zeros_like(qs)
    return jnp.where(lane < DIFF_HD, qs, zero), jnp.where(lane >= DIFF_HD, qs, zero)


def _stack_diff_queries(q):
    return jnp.concatenate(_split_diff_queries(q), axis=0)


def _diff_post(o, g, post_scale):
    ms = jnp.mean(o * o, axis=-1, keepdims=True)
    return (o * lax.rsqrt(ms + EPS) * g) * post_scale


def _diff_finish(l, acc, lam, g, post_scale, tq):
    o = acc[:tq] / l[:tq] - lam * (acc[tq:] / l[tq:])
    return _diff_post(o, g, post_scale)


B_TQ = 8 * CHUNK


def _online_step(carry, s, v):
    m, l, acc = carry
    m_new = jnp.maximum(m, jnp.max(s, axis=-1, keepdims=True))
    alpha = jnp.exp(m - m_new)
    p = jnp.exp(s - m_new)
    l = alpha * l + jnp.sum(p, axis=-1, keepdims=True)
    acc = alpha * acc + _dot(p.astype(BF16), v)
    return m_new, l, acc


def _attn_b_prompt_kernel(lam_ref, q_ref, k_ref, v_ref, g_ref, o_ref, kb, vb, *, t, lam_init):
    tq = B_TQ
    kb[...] = k_ref[0].astype(BF16)
    vb[...] = v_ref[0].astype(BF16)
    lam = _diff_lambda(lam_ref, lam_init)
    row = lax.broadcasted_iota(jnp.int32, (tq, tq), 0)
    col = lax.broadcasted_iota(jnp.int32, (tq, tq), 1)
    diag_ok = (col // CHUNK) <= (row // CHUNK)
    for qi in range(t // tq):
        q0 = qi * tq
        qs = _split_diff_queries(q_ref[0, q0:q0 + tq, :])
        spans = ([(0, q0, False)] if q0 else []) + [(q0, tq, True)]
        outs = []
        for c in range(2):
            carry = (jnp.full((tq, 1), NEG_INF, F32), jnp.zeros((tq, 1), F32), jnp.zeros((tq, 2 * DIFF_HD), F32))
            for k0, width, masked in spans:
                s = _dot_nt(qs[c], kb[k0:k0 + width, :])
                if masked:
                    s = jnp.where(diag_ok, s, NEG_INF)
                carry = _online_step(carry, s, vb[k0:k0 + width, :])
            outs.append(carry[2] / carry[1])
        o = outs[0] - lam * outs[1]
        o_ref[0, q0:q0 + tq, :] = _diff_post(o, g_ref[...], 1.0 - lam_init).astype(o_ref.dtype)


def attn_b_prompt(qb, kb, vb, lam_params, subln_g, lam_init, b, t):
    hd = 2 * DIFF_HD
    heads = qb.shape[1] // hd
    assert t % B_TQ == 0
    q3, k3, v3 = (a.reshape(b, t, heads * hd) for a in (qb, kb, vb))
    spec = pl.BlockSpec((1, t, hd), lambda bi, hi: (bi, 0, hi))
    vmem = 2 * t * hd * (2 + 4 + 4 + 2) + 2 * t * hd * 2 + 32 * MIB
    out = pl.pallas_call(
        functools.partial(_attn_b_prompt_kernel, t=t, lam_init=lam_init),
        grid=(b, heads),
        in_specs=[pl.BlockSpec(lam_params.shape, lambda bi, hi: (0, 0)), spec, spec, spec,
                  pl.BlockSpec((1, hd), lambda bi, hi: (0, 0))],
        out_specs=spec,
        out_shape=jax.ShapeDtypeStruct((b, t, heads * hd), BF16),
        scratch_shapes=[pltpu.VMEM((t, hd), BF16), pltpu.VMEM((t, hd), BF16)],
        compiler_params=_compiler_params(("parallel", "parallel"), vmem),
        name="attn_b_prompt",
    )(lam_params, q3, k3, v3, subln_g.reshape(1, hd))
    return out.reshape(b * t, heads * hd)


def _attn_b_sample_kernel(lam_ref, q_ref, kn_ref, vn_ref, g_ref, kc_hbm, vc_hbm, o_ref, kbuf, vbuf, sem,
                          *, layer, heads, t, lam_init):
    b = pl.program_id(0)
    nb = pl.num_programs(0)

    def copies(bi, h, slot):
        return (pltpu.make_async_copy(kc_hbm.at[layer, bi, :, h, :], kbuf.at[slot], sem.at[0, slot]),
                pltpu.make_async_copy(vc_hbm.at[layer, bi, :, h, :], vbuf.at[slot], sem.at[1, slot]))

    def start(bi, h, slot):
        for c in copies(bi, h, slot):
            c.start()

    @pl.when(b == 0)
    def _():
        start(b, 0, 0)

    lam = _diff_lambda(lam_ref, lam_init)
    for h in range(heads):
        slot = h % 2
        if h + 1 < heads:
            start(b, h + 1, 1 - slot)
        else:
            @pl.when(b + 1 < nb)
            def _():
                start(b + 1, 0, 1 - slot)
        for c in copies(b, h, slot):
            c.wait()
        q2 = _stack_diff_queries(q_ref[0, h])
        sc = _dot_nt(q2, kbuf[slot].astype(BF16))
        sn = _dot_nt(q2, kn_ref[0, h].astype(BF16))
        m = jnp.maximum(jnp.max(sc, axis=-1, keepdims=True), jnp.max(sn, axis=-1, keepdims=True))
        pc = jnp.exp(sc - m)
        pn = jnp.exp(sn - m)
        l = jnp.sum(pc, axis=-1, keepdims=True) + jnp.sum(pn, axis=-1, keepdims=True)
        acc = _dot(pc.astype(BF16), vbuf[slot].astype(BF16)) + _dot(pn.astype(BF16), vn_ref[0, h].astype(BF16))
        o_ref[0, h] = _diff_finish(l, acc, lam, g_ref[...], 1.0 - lam_init, t).astype(o_ref.dtype)


def attn_b_sample(qb, kb, vb, cache_k, cache_v, layer, lam_params, subln_g, lam_init, b, t):
    hd = 2 * DIFF_HD
    heads = qb.shape[1] // hd
    assert heads % 2 == 0
    past = cache_k.shape[2]
    head_spec = pl.BlockSpec((1, heads, t, hd), lambda bi: (bi, 0, 0, 0))
    vmem = 2 * 2 * past * hd * 4 + 2 * 3 * heads * t * hd * 4 + 12 * 2 * t * past * 4
    out = pl.pallas_call(
        functools.partial(_attn_b_sample_kernel, layer=layer, heads=heads, t=t, lam_init=lam_init),
        grid=(b,),
        in_specs=[pl.BlockSpec(lam_params.shape, lambda bi: (0, 0)), head_spec, head_spec, head_spec,
                  pl.BlockSpec((1, hd), lambda bi: (0, 0)),
                  pl.BlockSpec(memory_space=pl.ANY), pl.BlockSpec(memory_space=pl.ANY)],
        out_specs=head_spec,
        out_shape=jax.ShapeDtypeStruct((b, heads, t, hd), BF16),
        scratch_shapes=[pltpu.VMEM((2, past, hd), F32), pltpu.VMEM((2, past, hd), F32),
                        pltpu.SemaphoreType.DMA((2, 2))],
        compiler_params=_compiler_params(("arbitrary",), vmem),
        name="attn_b_sample",
    )(lam_params, *(_by_head(a, b, t, heads, hd) for a in (qb, kb, vb)), subln_g.reshape(1, hd), cache_k, cache_v)
    return out.transpose(0, 2, 1, 3).reshape(b * t, heads * hd)


def _attn_m_kernel(q_ref, k_ref, v_ref, o_ref, *, heads, hd):
    scale = hd ** -0.5
    for h in range(heads):
        sl = slice(h * hd, (h + 1) * hd)
        s = _dot_nt(q_ref[0, :, sl], k_ref[0, :, sl].astype(BF16)) * scale
        o, l = _softmax_pv(s, v_ref[0, :, sl].astype(BF16))
        o_ref[0, :, sl] = (o / l).astype(o_ref.dtype)


def attn_m_prompt(qm, mem_k, mem_v, b, t, *, tq=512):
    width = qm.shape[1]
    hd = width // H_M
    tq = min(tq, t)
    assert t % tq == 0
    n = mem_k.shape[0] // b
    q_spec = pl.BlockSpec((1, tq, width), lambda bi, qi: (bi, qi, 0))
    kv_spec = pl.BlockSpec((1, n, width), lambda bi, qi: (bi, 0, 0))
    vmem = 2 * (2 * n * width * 4 + 2 * tq * width * 2) + 8 * tq * n * 4
    out = pl.pallas_call(
        functools.partial(_attn_m_kernel, heads=H_M, hd=hd),
        grid=(b, t // tq),
        in_specs=[q_spec, kv_spec, kv_spec],
        out_specs=q_spec,
        out_shape=jax.ShapeDtypeStruct((b, t, width), BF16),
        compiler_params=_compiler_params(("parallel", "arbitrary"), vmem),
        name="attn_m_prompt",
    )(qm.reshape(b, t, width), mem_k.reshape(b, n, width), mem_v.reshape(b, n, width))
    return out.reshape(b * t, width)


def _attn_m_sample_kernel(q_ref, kc_hbm, vc_hbm, o_ref, kbuf, vbuf, sem, *, layer, heads, hd):
    b = pl.program_id(0)
    nb = pl.num_programs(0)
    scale = hd ** -0.5

    def copies(bi, slot):
        out = []
        for h in range(heads):
            out.append(pltpu.make_async_copy(kc_hbm.at[layer, bi, :, h, :], kbuf.at[slot, h], sem.at[0, slot, h]))
            out.append(pltpu.make_async_copy(vc_hbm.at[layer, bi, :, h, :], vbuf.at[slot, h], sem.at[1, slot, h]))
        return out

    @pl.when(b == 0)
    def _():
        for c in copies(b, 0):
            c.start()

    slot = b % 2

    @pl.when(b + 1 < nb)
    def _():
        for c in copies(b + 1, 1 - slot):
            c.start()

    for c in copies(b, slot):
        c.wait()
    for h in range(heads):
        s = _dot_nt(q_ref[0, h], kbuf[slot, h].astype(BF16)) * scale
        o, l = _softmax_pv(s, vbuf[slot, h].astype(BF16))
        o_ref[0, h] = (o / l).astype(o_ref.dtype)


def attn_m_sample(qm, cache_k, cache_v, layer, b, t):
    n, heads, hd = cache_k.shape[2:]
    head_spec = pl.BlockSpec((1, heads, t, hd), lambda bi: (bi, 0, 0, 0))
    vmem = 2 * 2 * heads * n * hd * 4 + 2 * 2 * heads * t * hd * 2 + 8 * MIB
    out = pl.pallas_call(
        functools.partial(_attn_m_sample_kernel, layer=layer, heads=heads, hd=hd),
        grid=(b,),
        in_specs=[head_spec, pl.BlockSpec(memory_space=pl.ANY), pl.BlockSpec(memory_space=pl.ANY)],
        out_specs=head_spec,
        out_shape=jax.ShapeDtypeStruct((b, heads, t, hd), BF16),
        scratch_shapes=[pltpu.VMEM((2, heads, n, hd), F32), pltpu.VMEM((2, heads, n, hd), F32),
                        pltpu.SemaphoreType.DMA((2, 2, heads))],
        compiler_params=_compiler_params(("arbitrary",), vmem),
        name="attn_m_sample",
    )(_by_head(qm, b, t, heads, hd), cache_k, cache_v)
    return out.transpose(0, 2, 1, 3).reshape(b * t, heads * hd)


def _merge_kernel(h_ref, oa_ref, ob_ref, om_ref, wga_ref, wgb_ref, wgm_ref, gb_ref, wbr_ref, o_ref):
    h = h_ref[...]
    acc = None
    for n, (o_n, wg_n) in enumerate(((oa_ref, wga_ref), (ob_ref, wgb_ref), (om_ref, wgm_ref))):
        gate = jax.nn.sigmoid(_dot(h, wg_n[...]) + gb_ref[n])
        term = gate * _dot(o_n[...], wbr_ref[n])
        acc = term if acc is None else acc + term
    o_ref[...] = acc.astype(o_ref.dtype)


def merge_branches(h, oa, ob, om, w_in, gate_col0, gate_b, w_br, *, tm, tn=256):
    m, bw = oa.shape
    d = w_br.shape[2]
    tm = _row_tile(m, tm)
    assert d % tn == 0 and gate_col0 % tn == 0
    nj = d // tn
    g0 = gate_col0 // tn
    row_spec = lambda width: pl.BlockSpec((tm, width), lambda i, j: (i, 0))
    gate_w_specs = [pl.BlockSpec((d, tn), functools.partial(lambda i, j, n: (0, g0 + n * nj + j), n=n))
                    for n in range(N_BRANCH)]
    vmem = 2 * (tm * d * 2 + 3 * tm * bw * 2 + 3 * d * tn * 2 + 3 * bw * tn * 2 + tm * tn * 2) + 6 * tm * tn * 4
    return pl.pallas_call(
        _merge_kernel,
        grid=(m // tm, nj),
        in_specs=[row_spec(d), row_spec(bw), row_spec(bw), row_spec(bw)] + gate_w_specs
                 + [pl.BlockSpec((N_BRANCH, 1, tn), lambda i, j: (0, 0, j)),
                    pl.BlockSpec((N_BRANCH, bw, tn), lambda i, j: (0, 0, j))],
        out_specs=pl.BlockSpec((tm, tn), lambda i, j: (i, j)),
        out_shape=jax.ShapeDtypeStruct((m, d), BF16),
        compiler_params=_compiler_params(("parallel", "arbitrary"), vmem),
        name="merge_branches",
    )(h, oa, ob, om, w_in, w_in, w_in, gate_b, w_br)


def _ffn_up_kernel(h_ref, wa_ref, wb_ref, cw_ref, cb_ref, st_ref, g_ref, cn_ref, carry, *, nb, tb, blocks_per_seq):
    i = pl.program_id(0)
    j = pl.program_id(1)
    tm, tn = g_ref.shape
    if nb == 1:
        @pl.when((i % blocks_per_seq) == 0)
        def _():
            carry[j] = st_ref[0]

        trow = lax.broadcasted_iota(jnp.int32, (tm, PROJ_SLAB), 0)
    else:
        trow = lax.broadcasted_iota(jnp.int32, (nb, tb, PROJ_SLAB), 1).reshape(tm, PROJ_SLAB)
    h = h_ref[...]
    slabs = [slice(c0, c0 + PROJ_SLAB) for c0 in range(0, tn, PROJ_SLAB)]
    dots = [(_dot(h, wa_ref[:, cols]), _dot(h, wb_ref[:, cols])) for cols in slabs]
    for cols, (a, bgate) in zip(slabs, dots):
        if nb == 1:
            prev = carry[j, :, cols]
            p0 = jnp.broadcast_to(prev[0:1], (tm, PROJ_SLAB))
            p1 = jnp.broadcast_to(prev[1:2], (tm, PROJ_SLAB))
            carry[j, :, cols] = a[tm - 2:tm]
            cn_ref[0, :, cols] = a[tm - 2:tm]
        else:
            st = st_ref[:, :, cols]
            p0 = jnp.broadcast_to(st[:, 0:1, :], (nb, tb, PROJ_SLAB)).reshape(tm, PROJ_SLAB)
            p1 = jnp.broadcast_to(st[:, 1:2, :], (nb, tb, PROJ_SLAB)).reshape(tm, PROJ_SLAB)
            cn_ref[:, :, cols] = a.reshape(nb, tb, PROJ_SLAB)[:, tb - 2:tb, :]
        am1 = jnp.where(trow == 0, p1, pltpu.roll(a, 1, 0))
        am2 = jnp.where(trow == 0, p0, jnp.where(trow == 1, p1, pltpu.roll(a, 2, 0)))
        cw = cw_ref[:, cols]
        c = cb_ref[:, cols] + am2 * cw[0:1] + am1 * cw[1:2] + a * cw[2:3]
        gelu = 0.5 * c * (1.0 + lax.erf(c * (2.0 ** -0.5)))
        g_ref[:, cols] = (gelu * bgate).astype(g_ref.dtype)


def ffn_up(h, w_a, w_b, conv_w, conv_b, state, b, t, *, tm=1024, tn=512):
    m, d = h.shape
    f = w_a.shape[1]
    tm = _row_tile(m, tm)
    assert f % tn == 0 and tn % PROJ_SLAB == 0
    if tm >= t:
        assert tm % t == 0
        nb, tb, blocks_per_seq = tm // t, t, 1
    else:
        assert t % tm == 0
        nb, tb, blocks_per_seq = 1, tm, t // tm
    if nb > 1:
        seq_map = lambda i, j: (i, 0, j)
    else:
        seq_map = lambda i, j: (i // blocks_per_seq, 0, j)
    tail_map = lambda i, j: (i, 0, j)
    w_spec = pl.BlockSpec((d, tn), lambda i, j: (0, j))
    vmem = 2 * (tm * d * 2 + 2 * d * tn * 2 + tm * tn * 2 + 2 * nb * 8 * tn * 4) + 8 * tm * tn * 4
    g, conv_new = pl.pallas_call(
        functools.partial(_ffn_up_kernel, nb=nb, tb=tb, blocks_per_seq=blocks_per_seq),
        grid=(m // tm, f // tn),
        in_specs=[pl.BlockSpec((tm, d), lambda i, j: (i, 0)), w_spec, w_spec,
                  pl.BlockSpec((CONV_W, tn), lambda i, j: (0, j)),
                  pl.BlockSpec((1, tn), lambda i, j: (0, j)),
                  pl.BlockSpec((nb, CONV_W - 1, tn), seq_map)],
        out_specs=[pl.BlockSpec((tm, tn), lambda i, j: (i, j)),
                   pl.BlockSpec((nb, CONV_W - 1, tn), tail_map)],
        out_shape=[jax.ShapeDtypeStruct((m, f), BF16),
                   jax.ShapeDtypeStruct((b * blocks_per_seq, CONV_W - 1, f), F32)],
        scratch_shapes=[pltpu.VMEM((f // tn, CONV_W - 1, tn), F32)],
        compiler_params=_compiler_params(("arbitrary", "arbitrary"), vmem),
        name="ffn_up",
    )(h, w_a, w_b, conv_w, conv_b, state)
    return g, conv_new.reshape(b, blocks_per_seq, CONV_W - 1, f)[:, -1]


def _pad_cols(a, f_pad):
    return jnp.pad(a, [(0, 0)] * (a.ndim - 1) + [(0, f_pad - a.shape[-1])])


def _layer_weights(l, P):
    d_ff = P['w_ffn_down'].shape[1]
    f_pad = -(-d_ff // 512) * 512
    w_up = P['w_ffn_up'][l]
    return {
        'w_in': P['w_in'][l].astype(BF16),
        'w_mem_kv': P['w_mem_kv'][l].astype(BF16),
        'w_branch': P['w_branch'][l].astype(BF16),
        'w_out': P['w_out'][l].astype(BF16),
        'w_up_a': _pad_cols(w_up[:, :d_ff].astype(BF16), f_pad),
        'w_up_b': _pad_cols(w_up[:, d_ff:].astype(BF16), f_pad),
        'w_down': jnp.pad(P['w_ffn_down'][l].astype(BF16), [(0, f_pad - d_ff), (0, 0)]),
        'conv_w': _pad_cols(P['ffn_conv_w'][l], f_pad),
        'conv_b': _pad_cols(P['ffn_conv_b'][l].reshape(1, d_ff), f_pad),
        'f_pad': f_pad,
        'd_ff': d_ff,
    }


def _mixer_inputs(h, W, P, l, rope_tab, tm):
    gains = (P['a_q_norm_g'][l], P['a_k_norm_g'][l], P['b_q_norm_g'][l], P['b_k_norm_g'][l], P['m_q_norm_g'][l])
    return proj_in(h, W['w_in'], gains, rope_tab, tm=tm)


def _finish_layer(x, h, outs, W, P, l, state, b, t, tm):
    d = x.shape[1]
    gate_b = P['gate_b'][l].reshape(N_BRANCH, 1, d)
    merged = merge_branches(h, *outs, W['w_in'], 7 * (d // 2), gate_b, W['w_branch'], tm=tm)
    res_spec = lambda tn: pl.BlockSpec((tm, tn), lambda i, j: (i, j))
    x = _proj_call(_proj_residual_kernel, merged, W['w_out'], 0, d, F32, [(x, res_spec(512))], tm=tm, name="proj_out")
    h = rmsnorm_cast(x, P['norm_ffn_g'][l])
    g, conv_new = ffn_up(h, W['w_up_a'], W['w_up_b'], W['conv_w'], W['conv_b'], state, b, t, tm=tm)
    x = _proj_call(_proj_residual_kernel, g, W['w_down'], 0, d, F32, [(x, res_spec(512))], tm=tm, name="ffn_down")
    return x, conv_new[:, :, :W['d_ff']]


def kernel(x_prompt, x_sample, cache_a_k, cache_a_v, cache_b_k, cache_b_v, cache_mem_k, cache_mem_v, state_ffn_conv, mem_prompt, norm_mix_g, w_in, a_q_norm_g, a_k_norm_g, a_rel_bias, b_q_norm_g, b_k_norm_g, b_lam_q1, b_lam_k1, b_lam_q2, b_lam_k2, b_subln_g, m_q_norm_g, m_k_norm_g, mem_norm_g, w_mem_kv, gate_b, w_branch, w_out, norm_ffn_g, w_ffn_up, ffn_conv_w, ffn_conv_b, w_ffn_down):
    P = {'w_in': w_in, 'a_q_norm_g': a_q_norm_g, 'a_k_norm_g': a_k_norm_g, 'b_q_norm_g': b_q_norm_g,
         'b_k_norm_g': b_k_norm_g, 'm_q_norm_g': m_q_norm_g, 'm_k_norm_g': m_k_norm_g, 'w_mem_kv': w_mem_kv,
         'gate_b': gate_b, 'w_branch': w_branch, 'w_out': w_out, 'norm_ffn_g': norm_ffn_g,
         'w_ffn_up': w_ffn_up, 'ffn_conv_w': ffn_conv_w, 'ffn_conv_b': ffn_conv_b, 'w_ffn_down': w_ffn_down}
    bp, tp, d = x_prompt.shape
    bs, ts, _ = x_sample.shape
    depth = w_in.shape[0]
    bw = d // 2
    past = cache_b_k.shape[2]
    a_len = cache_a_k.shape[2]
    n_mem = mem_prompt.shape[1]
    a_keep = min(BAND_PAST, tp)
    h_a = bw // HD_A
    h_b = bw // (2 * DIFF_HD)
    hd_m = bw // H_M
    mp, ms = bp * tp, bs * ts
    tm_p = _row_tile(mp, 1024)
    tm_s = _row_tile(ms, 1024)
    assert tm_p <= tp and tp % tm_p == 0 or tm_p % tp == 0

    pos_s = past + np.arange(ts)
    key_pos_a = np.concatenate([past - a_len + np.arange(a_len), pos_s])
    q_chunk_s = pos_s // CHUNK
    k_chunk_a = key_pos_a // CHUNK
    valid_a_s = (k_chunk_a[None, :] <= q_chunk_s[:, None]) & (k_chunk_a[None, :] >= q_chunk_s[:, None] - BAND_CHUNKS)
    mask_a_s = jnp.asarray(np.where(valid_a_s, 0.0, NEG_INF), F32)
    key_pos_b = np.concatenate([np.arange(past), pos_s])
    valid_b_s = (key_pos_b // CHUNK)[None, :] <= q_chunk_s[:, None]
    assert valid_b_s.all(), "sample queries are expected to see every cached and new differential key"

    rope_p = _rope_table(jnp.arange(max(tp, tm_p), dtype=jnp.int32) % tp)
    rope_s = _rope_table(past + (jnp.arange(max(ts, tm_s), dtype=jnp.int32) % ts))

    xp = x_prompt.reshape(mp, d)
    xs = x_sample.reshape(ms, d)
    mem2d = mem_prompt.reshape(bp * n_mem, d)
    outs = {k: [] for k in ('ak_p', 'av_p', 'bk_p', 'bv_p', 'mk_p', 'mv_p', 'cv_p',
                            'ak_s', 'av_s', 'bk_s', 'bv_s', 'cv_s')}
    for l in range(depth):
        W = _layer_weights(l, P)
        lam_init = 0.8 - 0.6 * math.exp(-0.3 * l)
        lam_params = jnp.stack([b_lam_q1[l], b_lam_k1[l], b_lam_q2[l], b_lam_k2[l]]).astype(F32)
        bias_row = _rel_bias_row(a_rel_bias[l])

        h = rmsnorm_cast(xp, norm_mix_g[l])
        qa, ka, va, qb, kb, vb, qm = _mixer_inputs(h, W, P, l, rope_p, tm_p)
        oa = attn_a_prompt(qa, ka, va, bias_row, bp, tp)
        ob = attn_b_prompt(qb, kb, vb, lam_params, b_subln_g[l], lam_init, bp, tp)
        hm = rmsnorm_cast(mem2d, mem_norm_g[l])
        tm_m = _row_tile(bp * n_mem, 1024)
        mk = _proj_call(functools.partial(_proj_headnorm_kernel, hd=hd_m), hm, W['w_mem_kv'], 0, bw, F32,
                        [(m_k_norm_g[l].reshape(1, hd_m), pl.BlockSpec((1, hd_m), lambda i, j: (0, 0)))],
                        tm=tm_m, name="proj_mk")
        mv = _proj_call(_proj_plain_kernel, hm, W['w_mem_kv'], bw, bw, F32, [], tm=tm_m, name="proj_mv")
        om = attn_m_prompt(qm, mk, mv, bp, tp)
        zeros_state = jnp.zeros((bp, CONV_W - 1, W['f_pad']), F32)
        xp, conv_new = _finish_layer(xp, h, (oa, ob, om), W, P, l, zeros_state, bp, tp, tm_p)
        outs['ak_p'].append(ka.reshape(bp, tp, h_a, HD_A)[:, tp - a_keep:])
        outs['av_p'].append(va.reshape(bp, tp, h_a, HD_A)[:, tp - a_keep:])
        outs['bk_p'].append(kb.reshape(bp, tp, h_b, 2 * DIFF_HD))
        outs['bv_p'].append(vb.reshape(bp, tp, h_b, 2 * DIFF_HD))
        outs['mk_p'].append(mk.reshape(bp, n_mem, H_M, hd_m))
        outs['mv_p'].append(mv.reshape(bp, n_mem, H_M, hd_m))
        outs['cv_p'].append(conv_new)

        h = rmsnorm_cast(xs, norm_mix_g[l])
        qa, ka, va, qb, kb, vb, qm = _mixer_inputs(h, W, P, l, rope_s, tm_s)
        oa = attn_a_sample(qa, ka, va, cache_a_k, cache_a_v, l, bias_row, mask_a_s, bs, ts)
        ob = attn_b_sample(qb, kb, vb, cache_b_k, cache_b_v, l, lam_params, b_subln_g[l], lam_init, bs, ts)
        om = attn_m_sample(qm, cache_mem_k, cache_mem_v, l, bs, ts)
        state = _pad_cols(state_ffn_conv[l], W['f_pad'])
        xs, conv_new = _finish_layer(xs, h, (oa, ob, om), W, P, l, state, bs, ts, tm_s)
        outs['ak_s'].append(ka.reshape(bs, ts, h_a, HD_A))
        outs['av_s'].append(va.reshape(bs, ts, h_a, HD_A))
        outs['bk_s'].append(kb.reshape(bs, ts, h_b, 2 * DIFF_HD))
        outs['bv_s'].append(vb.reshape(bs, ts, h_b, 2 * DIFF_HD))
        outs['cv_s'].append(conv_new)

    stack = lambda k: jnp.stack(outs[k])
    return (xp.reshape(bp, tp, d), xs.reshape(bs, ts, d),
            stack('ak_p'), stack('av_p'), stack('bk_p'), stack('bv_p'), stack('mk_p'), stack('mv_p'), stack('cv_p'),
            stack('ak_s'), stack('av_s'), stack('bk_s'), stack('bv_s'), stack('cv_s'))
```

```python
import functools
import math

import numpy as np
import jax
import jax.numpy as jnp
from jax import lax
from jax.experimental import pallas as pl
from jax.experimental.pallas import tpu as pltpu

F32 = jnp.float32
BF16 = jnp.bfloat16

CHUNK = 64
BAND_CHUNKS = 8
BAND_PAST = BAND_CHUNKS * CHUNK
REL_CLIP = 128
HD_A = 128
DIFF_HD = 64
ROT_DIM = DIFF_HD // 4
ROPE_THETA = 500000.0
H_M = 4
N_BRANCH = 3
CONV_W = 3
EPS = 1e-6
NEG_INF = -1e30

V7X_LANES = 128
V7X_VMEM_BYTES = 64 * 1024 * 1024
MIB = 1024 * 1024


def _compiler_params(semantics, vmem_estimate_bytes):
    limit = min(int(vmem_estimate_bytes * 1.25) + 8 * MIB, V7X_VMEM_BYTES - 4 * MIB)
    return pltpu.CompilerParams(dimension_semantics=semantics, vmem_limit_bytes=limit)


def _row_tile(m, target):
    t = min(m, target)
    assert m % t == 0, (m, t)
    return t


def _rmsnorm_kernel(x_ref, g_ref, o_ref):
    x = x_ref[...]
    ms = jnp.mean(x * x, axis=-1, keepdims=True)
    o_ref[...] = (x * lax.rsqrt(ms + EPS) * g_ref[...]).astype(o_ref.dtype)


def rmsnorm_cast(x, g):
    m, d = x.shape
    tm = _row_tile(m, 512)
    return pl.pallas_call(
        _rmsnorm_kernel,
        grid=(m // tm,),
        in_specs=[pl.BlockSpec((tm, d), lambda i: (i, 0)), pl.BlockSpec((1, d), lambda i: (0, 0))],
        out_specs=pl.BlockSpec((tm, d), lambda i: (i, 0)),
        out_shape=jax.ShapeDtypeStruct((m, d), BF16),
        compiler_params=_compiler_params(("parallel",), 2 * tm * d * 6),
        name="rmsnorm_cast",
    )(x, g.reshape(1, d))


def _dot(a, b):
    return jnp.dot(a, b, preferred_element_type=F32)


def _dot_nt(a, b):
    return lax.dot_general(a, b, (((1,), (1,)), ((), ())), preferred_element_type=F32)


def _proj_plain_kernel(h_ref, w_ref, o_ref):
    o_ref[...] = _dot(h_ref[...], w_ref[...]).astype(o_ref.dtype)


def _headnorm_store(acc, g, o_ref, hd):
    for k in range(acc.shape[1] // hd):
        s = acc[:, k * hd:(k + 1) * hd]
        ms = jnp.mean(s * s, axis=-1, keepdims=True)
        o_ref[:, k * hd:(k + 1) * hd] = (s * lax.rsqrt(ms + EPS) * g).astype(o_ref.dtype)


def _norm_rope_store(acc, g, tab_ref, o_ref):
    rows, width = acc.shape
    grp_r = lax.broadcasted_iota(jnp.int32, (width, width), 0) // DIFF_HD
    grp_c = lax.broadcasted_iota(jnp.int32, (width, width), 1) // DIFF_HD
    ones_bd = jnp.where(grp_r == grp_c, 1.0, 0.0).astype(BF16)
    ms = _dot((acc * acc).astype(BF16), ones_bd) * (1.0 / DIFF_HD)
    y = acc * lax.rsqrt(ms + EPS)
    cos = tab_ref[:, 0:V7X_LANES]
    sin_up = tab_ref[:, V7X_LANES:2 * V7X_LANES]
    sin_dn = tab_ref[:, 2 * V7X_LANES:3 * V7X_LANES]
    half = ROT_DIM // 2
    for k in range(width // V7X_LANES):
        yk = y[:, k * V7X_LANES:(k + 1) * V7X_LANES] * g
        out = (yk * cos + pltpu.roll(yk, half, 1) * sin_up
               + pltpu.roll(yk, V7X_LANES - half, 1) * sin_dn)
        o_ref[:, k * V7X_LANES:(k + 1) * V7X_LANES] = out.astype(o_ref.dtype)


def _proj_headnorm_kernel(h_ref, w_ref, g_ref, o_ref, *, hd):
    _headnorm_store(_dot(h_ref[...], w_ref[...]), g_ref[...], o_ref, hd)


PROJ_SLAB = 256


def _proj_in_kernel(h_ref, w_ref, gqa_ref, gka_ref, gqb_ref, gkb_ref, gqm_ref, tab_ref,
                    qa_ref, ka_ref, va_ref, qb_ref, kb_ref, vb_ref, qm_ref, *, blocks_per_group, hd_m):
    group = pl.program_id(1) // blocks_per_group
    tn = w_ref.shape[1]

    def run(o_ref, epilogue):
        def body():
            slabs = [slice(c0, c0 + PROJ_SLAB) for c0 in range(0, tn, PROJ_SLAB)]
            accs = [_dot(h_ref[...], w_ref[:, cols]) for cols in slabs]
            for acc, cols in zip(accs, slabs):
                epilogue(acc, o_ref.at[:, cols])
        return body

    headnorm = lambda g_ref, hd: (lambda acc, o: _headnorm_store(acc, g_ref[...], o, hd))
    norm_rope = lambda g_ref: (lambda acc, o: _norm_rope_store(acc, g_ref[...], tab_ref, o))
    plain = lambda acc, o: o.__setitem__(Ellipsis, acc)
    bodies = (
        run(qa_ref, headnorm(gqa_ref, HD_A)), run(ka_ref, headnorm(gka_ref, HD_A)), run(va_ref, plain),
        run(qb_ref, norm_rope(gqb_ref)), run(kb_ref, norm_rope(gkb_ref)), run(vb_ref, plain),
        run(qm_ref, headnorm(gqm_ref, hd_m)),
    )
    for n, body in enumerate(bodies):
        pl.when(group == n)(body)


def proj_in(h, w_in, gains, rope_tab, *, tm, tn=512):
    m, d = h.shape
    bw = d // 2
    hd_m = bw // H_M
    tm = _row_tile(m, tm)
    assert bw % tn == 0 and tn % PROJ_SLAB == 0 and PROJ_SLAB % hd_m == 0 and rope_tab.shape[0] % tm == 0
    bpg = bw // tn
    n_tab = rope_tab.shape[0] // tm
    g_a_q, g_a_k, g_b_q, g_b_k, g_m_q = gains
    tile2 = lambda g: jnp.tile(g.reshape(1, -1), (1, 2))
    gain_args = [g_a_q.reshape(1, HD_A), g_a_k.reshape(1, HD_A), tile2(g_b_q), tile2(g_b_k), g_m_q.reshape(1, hd_m)]
    const = lambda a: pl.BlockSpec(a.shape, lambda i, j: (0, 0))

    def out_spec(n):
        return pl.BlockSpec((tm, tn), lambda i, j: (i, jnp.clip(j - n * bpg, 0, bpg - 1)))

    dtypes = (BF16, F32, F32, BF16, F32, F32, BF16)
    out_bytes = sum(tm * tn * jnp.dtype(t).itemsize for t in dtypes)
    vmem = 2 * (tm * d * 2 + d * tn * 2 + tm * 3 * V7X_LANES * 4 + out_bytes) + 2 * tm * tn * 4
    return pl.pallas_call(
        functools.partial(_proj_in_kernel, blocks_per_group=bpg, hd_m=hd_m),
        grid=(m // tm, 7 * bpg),
        in_specs=[pl.BlockSpec((tm, d), lambda i, j: (i, 0)), pl.BlockSpec((d, tn), lambda i, j: (0, j))]
                 + [const(g) for g in gain_args]
                 + [pl.BlockSpec((tm, 3 * V7X_LANES), lambda i, j: (i % n_tab, 0))],
        out_specs=[out_spec(n) for n in range(7)],
        out_shape=[jax.ShapeDtypeStruct((m, bw), t) for t in dtypes],
        compiler_params=_compiler_params(("parallel", "arbitrary"), vmem),
        name="proj_in",
    )(h, w_in, *gain_args, rope_tab)


def _proj_residual_kernel(h_ref, w_ref, x_ref, o_ref):
    o_ref[...] = x_ref[...] + _dot(h_ref[...], w_ref[...])


def _proj_out_norm_kernel(m_ref, w_ref, x_ref, g_ref, o_ref, h_ref, xrow):
    j = pl.program_id(1)
    y = x_ref[...] + _dot(m_ref[...], w_ref[...])
    o_ref[...] = y
    xrow[j] = y

    @pl.when(j == pl.num_programs(1) - 1)
    def _():
        nj, _, tn = xrow.shape
        ssq = None
        for jj in range(nj):
            xb = xrow[jj]
            part = jnp.sum(xb * xb, axis=-1, keepdims=True)
            ssq = part if ssq is None else ssq + part
        scale = lax.rsqrt(ssq * (1.0 / (nj * tn)) + EPS)
        for jj in range(nj):
            cols = slice(jj * tn, (jj + 1) * tn)
            h_ref[:, cols] = (xrow[jj] * scale * g_ref[:, cols]).astype(h_ref.dtype)


def proj_out_norm(merged, w_out, x, g, *, tm, tn=512):
    m, d = x.shape
    tm = _row_tile(m, tm)
    assert d % tn == 0
    row_spec = pl.BlockSpec((tm, d), lambda i, j: (i, 0))
    blk_spec = pl.BlockSpec((tm, tn), lambda i, j: (i, j))
    vmem = 2 * (tm * d * 2 + d * tn * 2 + 2 * tm * tn * 4 + tm * d * 2) + tm * d * 4 + 2 * tm * tn * 4
    return pl.pallas_call(
        _proj_out_norm_kernel,
        grid=(m // tm, d // tn),
        in_specs=[row_spec, pl.BlockSpec((d, tn), lambda i, j: (0, j)), blk_spec,
                  pl.BlockSpec((1, d), lambda i, j: (0, 0))],
        out_specs=[blk_spec, row_spec],
        out_shape=[jax.ShapeDtypeStruct((m, d), F32), jax.ShapeDtypeStruct((m, d), BF16)],
        scratch_shapes=[pltpu.VMEM((d // tn, tm, tn), F32)],
        compiler_params=_compiler_params(("parallel", "arbitrary"), vmem),
        name="proj_out_norm",
    )(merged, w_out, x, g.reshape(1, d))


def _proj_call(kernel_fn, h, w, col0, ncols, out_dtype, extras, *, tm=1024, tn=512, name):
    m, k = h.shape
    tm = _row_tile(m, tm)
    tn = min(tn, ncols)
    assert ncols % tn == 0 and col0 % tn == 0, (ncols, col0, tn)
    cb = col0 // tn
    in_specs = [pl.BlockSpec((tm, k), lambda i, j: (i, 0)),
                pl.BlockSpec((k, tn), lambda i, j: (0, j + cb))]
    in_specs += [spec for _, spec in extras]
    extra_bytes = sum(int(np.prod(spec.block_shape)) * a.dtype.itemsize for a, spec in extras)
    vmem = 2 * (tm * k * 2 + k * tn * 2 + tm * tn * 4 + extra_bytes) + tm * tn * 8
    return pl.pallas_call(
        kernel_fn,
        grid=(m // tm, ncols // tn),
        in_specs=in_specs,
        out_specs=pl.BlockSpec((tm, tn), lambda i, j: (i, j)),
        out_shape=jax.ShapeDtypeStruct((m, ncols), out_dtype),
        compiler_params=_compiler_params(("parallel", "arbitrary"), vmem),
        name=name,
    )(h, w, *[a for a, _ in extras])


def _rope_table(pos):
    half = ROT_DIM // 2
    inv_freq = jnp.exp(jnp.arange(half, dtype=F32) * (-2.0 * math.log(ROPE_THETA) / ROT_DIM))
    ang = pos.astype(F32)[:, None] * inv_freq[None, :]
    cos = jnp.cos(ang)
    sin = jnp.sin(ang)
    p = pos.shape[0]
    rest = DIFF_HD - ROT_DIM
    c64 = jnp.concatenate([cos, cos, jnp.ones((p, rest), F32)], axis=1)
    up64 = jnp.concatenate([jnp.zeros((p, half), F32), sin, jnp.zeros((p, rest), F32)], axis=1)
    dn64 = jnp.concatenate([-sin, jnp.zeros((p, half + rest), F32)], axis=1)
    return jnp.concatenate([c64, c64, up64, up64, dn64, dn64], axis=1)


A_QBLK = 4 * CHUNK
A_KBLK = BAND_PAST + A_QBLK
A_BIAS_W = 1024


def _rel_bias_row(tab):
    assert A_KBLK + A_QBLK - 1 <= A_BIAS_W
    lo = BAND_PAST - REL_CLIP
    hi = BAND_PAST + REL_CLIP + 1
    rep = lambda col, n: jnp.repeat(tab[:, col:col + 1], n, axis=1)
    row = jnp.concatenate([rep(0, lo), tab, rep(2 * REL_CLIP, A_KBLK - hi), rep(0, A_BIAS_W - A_KBLK)], axis=1)
    return row[:, None, :]


def _toeplitz_bias(row, rows, width):
    full = pltpu.roll(jnp.broadcast_to(row, (rows, A_BIAS_W)), 0, 1, stride=1, stride_axis=0)
    return full[:, :width]


def _softmax_pv(s, v):
    m = jnp.max(s, axis=-1, keepdims=True)
    p = jnp.exp(s - m)
    l = jnp.sum(p, axis=-1, keepdims=True)
    return _dot(p.astype(BF16), v), l


def _attn_a_prompt_kernel(q_ref, k_ref, v_ref, row_ref, o_ref, kb, vb, *, t):
    kb[...] = k_ref[0].astype(BF16)
    vb[...] = v_ref[0].astype(BF16)
    scale = HD_A ** -0.5
    qc = lax.broadcasted_iota(jnp.int32, (A_QBLK, A_KBLK), 0) // CHUNK
    kc = lax.broadcasted_iota(jnp.int32, (A_QBLK, A_KBLK), 1) // CHUNK
    inband = (kc >= qc) & (kc <= qc + BAND_CHUNKS)
    bias = jnp.where(inband, _toeplitz_bias(row_ref[0], A_QBLK, A_KBLK), NEG_INF)
    for i in range(t // A_QBLK):
        r0 = i * A_QBLK
        k0 = max(r0 - BAND_PAST, 0)
        k1 = r0 + A_QBLK
        q = q_ref[0, r0:k1, :]
        s = _dot_nt(q, kb[k0:k1, :]) * scale + bias[:, A_KBLK - (k1 - k0):]
        o, l = _softmax_pv(s, vb[k0:k1, :])
        o_ref[0, r0:k1, :] = (o / l).astype(o_ref.dtype)


def attn_a_prompt(qa, ka, va, bias_row, b, t):
    h = qa.shape[1] // HD_A
    assert t % A_QBLK == 0 and A_QBLK % V7X_LANES == 0
    q3, k3, v3 = (a.reshape(b, t, h * HD_A) for a in (qa, ka, va))
    spec = pl.BlockSpec((1, t, HD_A), lambda bi, hi: (bi, 0, hi))
    vmem = 2 * t * HD_A * (2 + 4 + 4 + 2) + 2 * t * HD_A * 2 + 24 * MIB
    out = pl.pallas_call(
        functools.partial(_attn_a_prompt_kernel, t=t),
        grid=(b, h),
        in_specs=[spec, spec, spec, pl.BlockSpec((1, 1, A_BIAS_W), lambda bi, hi: (hi, 0, 0))],
        out_specs=spec,
        out_shape=jax.ShapeDtypeStruct((b, t, h * HD_A), BF16),
        scratch_shapes=[pltpu.VMEM((t, HD_A), BF16), pltpu.VMEM((t, HD_A), BF16)],
        compiler_params=_compiler_params(("parallel", "parallel"), vmem),
        name="attn_a_prompt",
    )(q3, k3, v3, bias_row)
    return out.reshape(b * t, h * HD_A)


def _attn_a_sample_kernel(q_ref, kn_ref, vn_ref, row_ref, mask_ref, kc_hbm, vc_hbm, o_ref, kbuf, vbuf, sem,
                          *, layer, heads, a_len):
    b = pl.program_id(0)
    nb = pl.num_programs(0)
    scale = HD_A ** -0.5
    t = q_ref.shape[2]

    def copies(bi, slot):
        out = []
        for h in range(heads):
            out.append(pltpu.make_async_copy(kc_hbm.at[layer, bi, :, h, :], kbuf.at[slot, h], sem.at[0, slot, h]))
            out.append(pltpu.make_async_copy(vc_hbm.at[layer, bi, :, h, :], vbuf.at[slot, h], sem.at[1, slot, h]))
        return out

    @pl.when(b == 0)
    def _():
        for c in copies(b, 0):
            c.start()

    slot = b % 2

    @pl.when(b + 1 < nb)
    def _():
        for c in copies(b + 1, 1 - slot):
            c.start()

    for c in copies(b, slot):
        c.wait()
    for h in range(heads):
        q = q_ref[0, h]
        bias = _toeplitz_bias(row_ref[h], t, a_len + t) + mask_ref[...]
        sc = _dot_nt(q, kbuf[slot, h].astype(BF16)) * scale + bias[:, :a_len]
        sn = _dot_nt(q, kn_ref[0, h].astype(BF16)) * scale + bias[:, a_len:]
        m = jnp.maximum(jnp.max(sc, axis=-1, keepdims=True), jnp.max(sn, axis=-1, keepdims=True))
        pc = jnp.exp(sc - m)
        pn = jnp.exp(sn - m)
        l = jnp.sum(pc, axis=-1, keepdims=True) + jnp.sum(pn, axis=-1, keepdims=True)
        o = _dot(pc.astype(BF16), vbuf[slot, h].astype(BF16)) + _dot(pn.astype(BF16), vn_ref[0, h].astype(BF16))
        o_ref[0, h] = (o / l).astype(o_ref.dtype)


def _by_head(a, b, t, heads, hd):
    return a.reshape(b, t, heads, hd).transpose(0, 2, 1, 3)


def attn_a_sample(qa, ka, va, cache_k, cache_v, layer, bias_row, mask, b, t):
    heads = qa.shape[1] // HD_A
    a_len = cache_k.shape[2]
    assert a_len == BAND_PAST and a_len + t <= A_KBLK
    head_spec = pl.BlockSpec((1, heads, t, HD_A), lambda bi: (bi, 0, 0, 0))
    vmem = 2 * 2 * heads * a_len * HD_A * 4 + 2 * 4 * heads * t * HD_A * 4 + 16 * MIB
    out = pl.pallas_call(
        functools.partial(_attn_a_sample_kernel, layer=layer, heads=heads, a_len=a_len),
        grid=(b,),
        in_specs=[head_spec, head_spec, head_spec,
                  pl.BlockSpec(bias_row.shape, lambda bi: (0, 0, 0)),
                  pl.BlockSpec(mask.shape, lambda bi: (0, 0)),
                  pl.BlockSpec(memory_space=pl.ANY), pl.BlockSpec(memory_space=pl.ANY)],
        out_specs=head_spec,
        out_shape=jax.ShapeDtypeStruct((b, heads, t, HD_A), BF16),
        scratch_shapes=[pltpu.VMEM((2, heads, a_len, HD_A), F32), pltpu.VMEM((2, heads, a_len, HD_A), F32),
                        pltpu.SemaphoreType.DMA((2, 2, heads))],
        compiler_params=_compiler_params(("arbitrary",), vmem),
        name="attn_a_sample",
    )(*(_by_head(a, b, t, heads, HD_A) for a in (qa, ka, va)), bias_row, mask, cache_k, cache_v)
    return out.transpose(0, 2, 1, 3).reshape(b * t, heads * HD_A)


def _diff_lambda(lam_ref, lam_init):
    v = lam_ref[...]
    d1 = jnp.sum(v[0:1] * v[1:2], axis=-1, keepdims=True)
    d2 = jnp.sum(v[2:3] * v[3:4], axis=-1, keepdims=True)
    return jnp.exp(d1) - jnp.exp(d2) + lam_init


def _split_diff_queries(q):
    lane = lax.broadcasted_iota(jnp.int32, q.shape, 1)
    qs = q * jnp.asarray(DIFF_HD ** -0.5, q.dtype)
    zero = jnp.zeros_like(qs)
    return jnp.where(lane < DIFF_HD, qs, zero), jnp.where(lane >= DIFF_HD, qs, zero)


def _stack_diff_queries(q):
    return jnp.concatenate(_split_diff_queries(q), axis=0)


def _diff_post(o, g, post_scale):
    ms = jnp.mean(o * o, axis=-1, keepdims=True)
    return (o * lax.rsqrt(ms + EPS) * g) * post_scale


def _diff_finish(l, acc, lam, g, post_scale, tq):
    o = acc[:tq] / l[:tq] - lam * (acc[tq:] / l[tq:])
    return _diff_post(o, g, post_scale)


B_TQ = 8 * CHUNK


def _online_step(carry, s, v):
    m, l, acc = carry
    m_new = jnp.maximum(m, jnp.max(s, axis=-1, keepdims=True))
    alpha = jnp.exp(m - m_new)
    p = jnp.exp(s - m_new)
    l = alpha * l + jnp.sum(p, axis=-1, keepdims=True)
    acc = alpha * acc + _dot(p.astype(BF16), v)
    return m_new, l, acc


def _attn_b_prompt_kernel(lam_ref, q_ref, k_ref, v_ref, g_ref, o_ref, kb, vb, *, t, lam_init):
    tq = B_TQ
    kb[...] = k_ref[0].astype(BF16)
    vb[...] = v_ref[0].astype(BF16)
    lam = _diff_lambda(lam_ref, lam_init)
    row = lax.broadcasted_iota(jnp.int32, (tq, tq), 0)
    col = lax.broadcasted_iota(jnp.int32, (tq, tq), 1)
    diag_ok = (col // CHUNK) <= (row // CHUNK)
    for qi in range(t // tq):
        q0 = qi * tq
        qs = _split_diff_queries(q_ref[0, q0:q0 + tq, :])
        spans = ([(0, q0, False)] if q0 else []) + [(q0, tq, True)]
        outs = []
        for c in range(2):
            carry = (jnp.full((tq, 1), NEG_INF, F32), jnp.zeros((tq, 1), F32), jnp.zeros((tq, 2 * DIFF_HD), F32))
            for k0, width, masked in spans:
                s = _dot_nt(qs[c], kb[k0:k0 + width, :])
                if masked:
                    s = jnp.where(diag_ok, s, NEG_INF)
                carry = _online_step(carry, s, vb[k0:k0 + width, :])
            outs.append(carry[2] / carry[1])
        o = outs[0] - lam * outs[1]
        o_ref[0, q0:q0 + tq, :] = _diff_post(o, g_ref[...], 1.0 - lam_init).astype(o_ref.dtype)


def attn_b_prompt(qb, kb, vb, lam_params, subln_g, lam_init, b, t):
    hd = 2 * DIFF_HD
    heads = qb.shape[1] // hd
    assert t % B_TQ == 0
    q3, k3, v3 = (a.reshape(b, t, heads * hd) for a in (qb, kb, vb))
    spec = pl.BlockSpec((1, t, hd), lambda bi, hi: (bi, 0, hi))
    vmem = 2 * t * hd * (2 + 4 + 4 + 2) + 2 * t * hd * 2 + 32 * MIB
    out = pl.pallas_call(
        functools.partial(_attn_b_prompt_kernel, t=t, lam_init=lam_init),
        grid=(b, heads),
        in_specs=[pl.BlockSpec(lam_params.shape, lambda bi, hi: (0, 0)), spec, spec, spec,
                  pl.BlockSpec((1, hd), lambda bi, hi: (0, 0))],
        out_specs=spec,
        out_shape=jax.ShapeDtypeStruct((b, t, heads * hd), BF16),
        scratch_shapes=[pltpu.VMEM((t, hd), BF16), pltpu.VMEM((t, hd), BF16)],
        compiler_params=_compiler_params(("parallel", "parallel"), vmem),
        name="attn_b_prompt",
    )(lam_params, q3, k3, v3, subln_g.reshape(1, hd))
    return out.reshape(b * t, heads * hd)


def _attn_b_sample_kernel(lam_ref, q_ref, kn_ref, vn_ref, g_ref, kc_hbm, vc_hbm, o_ref, kbuf, vbuf, sem,
                          *, layer, heads, t, lam_init):
    b = pl.program_id(0)
    nb = pl.num_programs(0)

    def copies(bi, h, slot):
        return (pltpu.make_async_copy(kc_hbm.at[layer, bi, :, h, :], kbuf.at[slot], sem.at[0, slot]),
                pltpu.make_async_copy(vc_hbm.at[layer, bi, :, h, :], vbuf.at[slot], sem.at[1, slot]))

    def start(bi, h, slot):
        for c in copies(bi, h, slot):
            c.start()

    @pl.when(b == 0)
    def _():
        start(b, 0, 0)

    lam = _diff_lambda(lam_ref, lam_init)
    for h in range(heads):
        slot = h % 2
        if h + 1 < heads:
            start(b, h + 1, 1 - slot)
        else:
            @pl.when(b + 1 < nb)
            def _():
                start(b + 1, 0, 1 - slot)
        for c in copies(b, h, slot):
            c.wait()
        q2 = _stack_diff_queries(q_ref[0, h])
        sc = _dot_nt(q2, kbuf[slot].astype(BF16))
        sn = _dot_nt(q2, kn_ref[0, h].astype(BF16))
        m = jnp.maximum(jnp.max(sc, axis=-1, keepdims=True), jnp.max(sn, axis=-1, keepdims=True))
        pc = jnp.exp(sc - m)
        pn = jnp.exp(sn - m)
        l = jnp.sum(pc, axis=-1, keepdims=True) + jnp.sum(pn, axis=-1, keepdims=True)
        acc = _dot(pc.astype(BF16), vbuf[slot].astype(BF16)) + _dot(pn.astype(BF16), vn_ref[0, h].astype(BF16))
        o_ref[0, h] = _diff_finish(l, acc, lam, g_ref[...], 1.0 - lam_init, t).astype(o_ref.dtype)


def attn_b_sample(qb, kb, vb, cache_k, cache_v, layer, lam_params, subln_g, lam_init, b, t):
    hd = 2 * DIFF_HD
    heads = qb.shape[1] // hd
    assert heads % 2 == 0
    past = cache_k.shape[2]
    head_spec = pl.BlockSpec((1, heads, t, hd), lambda bi: (bi, 0, 0, 0))
    vmem = 2 * 2 * past * hd * 4 + 2 * 3 * heads * t * hd * 4 + 12 * 2 * t * past * 4
    out = pl.pallas_call(
        functools.partial(_attn_b_sample_kernel, layer=layer, heads=heads, t=t, lam_init=lam_init),
        grid=(b,),
        in_specs=[pl.BlockSpec(lam_params.shape, lambda bi: (0, 0)), head_spec, head_spec, head_spec,
                  pl.BlockSpec((1, hd), lambda bi: (0, 0)),
                  pl.BlockSpec(memory_space=pl.ANY), pl.BlockSpec(memory_space=pl.ANY)],
        out_specs=head_spec,
        out_shape=jax.ShapeDtypeStruct((b, heads, t, hd), BF16),
        scratch_shapes=[pltpu.VMEM((2, past, hd), F32), pltpu.VMEM((2, past, hd), F32),
                        pltpu.SemaphoreType.DMA((2, 2))],
        compiler_params=_compiler_params(("arbitrary",), vmem),
        name="attn_b_sample",
    )(lam_params, *(_by_head(a, b, t, heads, hd) for a in (qb, kb, vb)), subln_g.reshape(1, hd), cache_k, cache_v)
    return out.transpose(0, 2, 1, 3).reshape(b * t, heads * hd)


def _attn_m_kernel(q_ref, k_ref, v_ref, o_ref, *, heads, hd):
    scale = hd ** -0.5
    for h in range(heads):
        sl = slice(h * hd, (h + 1) * hd)
        s = _dot_nt(q_ref[0, :, sl], k_ref[0, :, sl].astype(BF16)) * scale
        o, l = _softmax_pv(s, v_ref[0, :, sl].astype(BF16))
        o_ref[0, :, sl] = (o / l).astype(o_ref.dtype)


def attn_m_prompt(qm, mem_k, mem_v, b, t, *, tq=512):
    width = qm.shape[1]
    hd = width // H_M
    tq = min(tq, t)
    assert t % tq == 0
    n = mem_k.shape[0] // b
    q_spec = pl.BlockSpec((1, tq, width), lambda bi, qi: (bi, qi, 0))
    kv_spec = pl.BlockSpec((1, n, width), lambda bi, qi: (bi, 0, 0))
    vmem = 2 * (2 * n * width * 4 + 2 * tq * width * 2) + 8 * tq * n * 4
    out = pl.pallas_call(
        functools.partial(_attn_m_kernel, heads=H_M, hd=hd),
        grid=(b, t // tq),
        in_specs=[q_spec, kv_spec, kv_spec],
        out_specs=q_spec,
        out_shape=jax.ShapeDtypeStruct((b, t, width), BF16),
        compiler_params=_compiler_params(("parallel", "arbitrary"), vmem),
        name="attn_m_prompt",
    )(qm.reshape(b, t, width), mem_k.reshape(b, n, width), mem_v.reshape(b, n, width))
    return out.reshape(b * t, width)


def _attn_m_sample_kernel(q_ref, kc_hbm, vc_hbm, o_ref, kbuf, vbuf, sem, *, layer, heads, hd):
    b = pl.program_id(0)
    nb = pl.num_programs(0)
    scale = hd ** -0.5

    def copies(bi, slot):
        out = []
        for h in range(heads):
            out.append(pltpu.make_async_copy(kc_hbm.at[layer, bi, :, h, :], kbuf.at[slot, h], sem.at[0, slot, h]))
            out.append(pltpu.make_async_copy(vc_hbm.at[layer, bi, :, h, :], vbuf.at[slot, h], sem.at[1, slot, h]))
        return out

    @pl.when(b == 0)
    def _():
        for c in copies(b, 0):
            c.start()

    slot = b % 2

    @pl.when(b + 1 < nb)
    def _():
        for c in copies(b + 1, 1 - slot):
            c.start()

    for c in copies(b, slot):
        c.wait()
    for h in range(heads):
        s = _dot_nt(q_ref[0, h], kbuf[slot, h].astype(BF16)) * scale
        o, l = _softmax_pv(s, vbuf[slot, h].astype(BF16))
        o_ref[0, h] = (o / l).astype(o_ref.dtype)


def attn_m_sample(qm, cache_k, cache_v, layer, b, t):
    n, heads, hd = cache_k.shape[2:]
    head_spec = pl.BlockSpec((1, heads, t, hd), lambda bi: (bi, 0, 0, 0))
    vmem = 2 * 2 * heads * n * hd * 4 + 2 * 2 * heads * t * hd * 2 + 8 * MIB
    out = pl.pallas_call(
        functools.partial(_attn_m_sample_kernel, layer=layer, heads=heads, hd=hd),
        grid=(b,),
        in_specs=[head_spec, pl.BlockSpec(memory_space=pl.ANY), pl.BlockSpec(memory_space=pl.ANY)],
        out_specs=head_spec,
        out_shape=jax.ShapeDtypeStruct((b, heads, t, hd), BF16),
        scratch_shapes=[pltpu.VMEM((2, heads, n, hd), F32), pltpu.VMEM((2, heads, n, hd), F32),
                        pltpu.SemaphoreType.DMA((2, 2, heads))],
        compiler_params=_compiler_params(("arbitrary",), vmem),
        name="attn_m_sample",
    )(_by_head(qm, b, t, heads, hd), cache_k, cache_v)
    return out.transpose(0, 2, 1, 3).reshape(b * t, heads * hd)


def _merge_kernel(h_ref, oa_ref, ob_ref, om_ref, wga_ref, wgb_ref, wgm_ref, gb_ref, wbr_ref, o_ref):
    h = h_ref[...]
    acc = None
    for n, (o_n, wg_n) in enumerate(((oa_ref, wga_ref), (ob_ref, wgb_ref), (om_ref, wgm_ref))):
        gate = jax.nn.sigmoid(_dot(h, wg_n[...]) + gb_ref[n])
        term = gate * _dot(o_n[...], wbr_ref[n])
        acc = term if acc is None else acc + term
    o_ref[...] = acc.astype(o_ref.dtype)


def merge_branches(h, oa, ob, om, w_in, gate_col0, gate_b, w_br, *, tm, tn=256):
    m, bw = oa.shape
    d = w_br.shape[2]
    tm = _row_tile(m, tm)
    assert d % tn == 0 and gate_col0 % tn == 0
    nj = d // tn
    g0 = gate_col0 // tn
    row_spec = lambda width: pl.BlockSpec((tm, width), lambda i, j: (i, 0))
    gate_w_specs = [pl.BlockSpec((d, tn), functools.partial(lambda i, j, n: (0, g0 + n * nj + j), n=n))
                    for n in range(N_BRANCH)]
    vmem = 2 * (tm * d * 2 + 3 * tm * bw * 2 + 3 * d * tn * 2 + 3 * bw * tn * 2 + tm * tn * 2) + 6 * tm * tn * 4
    return pl.pallas_call(
        _merge_kernel,
        grid=(m // tm, nj),
        in_specs=[row_spec(d), row_spec(bw), row_spec(bw), row_spec(bw)] + gate_w_specs
                 + [pl.BlockSpec((N_BRANCH, 1, tn), lambda i, j: (0, 0, j)),
                    pl.BlockSpec((N_BRANCH, bw, tn), lambda i, j: (0, 0, j))],
        out_specs=pl.BlockSpec((tm, tn), lambda i, j: (i, j)),
        out_shape=jax.ShapeDtypeStruct((m, d), BF16),
        compiler_params=_compiler_params(("parallel", "arbitrary"), vmem),
        name="merge_branches",
    )(h, oa, ob, om, w_in, w_in, w_in, gate_b, w_br)


def _ffn_up_kernel(h_ref, wa_ref, wb_ref, cw_ref, cb_ref, st_ref, g_ref, cn_ref, carry, *, nb, tb, blocks_per_seq):
    i = pl.program_id(0)
    j = pl.program_id(1)
    tm, tn = g_ref.shape
    if nb == 1:
        @pl.when((i % blocks_per_seq) == 0)
        def _():
            carry[j] = st_ref[0]

        trow = lax.broadcasted_iota(jnp.int32, (tm, PROJ_SLAB), 0)
    else:
        trow = lax.broadcasted_iota(jnp.int32, (nb, tb, PROJ_SLAB), 1).reshape(tm, PROJ_SLAB)
    h = h_ref[...]
    slabs = [slice(c0, c0 + PROJ_SLAB) for c0 in range(0, tn, PROJ_SLAB)]
    dots = [(_dot(h, wa_ref[:, cols]), _dot(h, wb_ref[:, cols])) for cols in slabs]
    for cols, (a, bgate) in zip(slabs, dots):
        if nb == 1:
            prev = carry[j, :, cols]
            p0 = jnp.broadcast_to(prev[0:1], (tm, PROJ_SLAB))
            p1 = jnp.broadcast_to(prev[1:2], (tm, PROJ_SLAB))
            carry[j, :, cols] = a[tm - 2:tm]
            cn_ref[0, :, cols] = a[tm - 2:tm]
        else:
            st = st_ref[:, :, cols]
            p0 = jnp.broadcast_to(st[:, 0:1, :], (nb, tb, PROJ_SLAB)).reshape(tm, PROJ_SLAB)
            p1 = jnp.broadcast_to(st[:, 1:2, :], (nb, tb, PROJ_SLAB)).reshape(tm, PROJ_SLAB)
            cn_ref[:, :, cols] = a.reshape(nb, tb, PROJ_SLAB)[:, tb - 2:tb, :]
        am1 = jnp.where(trow == 0, p1, pltpu.roll(a, 1, 0))
        am2 = jnp.where(trow == 0, p0, jnp.where(trow == 1, p1, pltpu.roll(a, 2, 0)))
        cw = cw_ref[:, cols]
        c = cb_ref[:, cols] + am2 * cw[0:1] + am1 * cw[1:2] + a * cw[2:3]
        gelu = 0.5 * c * (1.0 + lax.erf(c * (2.0 ** -0.5)))
        g_ref[:, cols] = (gelu * bgate).astype(g_ref.dtype)


def ffn_up(h, w_a, w_b, conv_w, conv_b, state, b, t, *, tm=1024, tn=512):
    m, d = h.shape
    f = w_a.shape[1]
    tm = _row_tile(m, tm)
    assert f % tn == 0 and tn % PROJ_SLAB == 0
    if tm >= t:
        assert tm % t == 0
        nb, tb, blocks_per_seq = tm // t, t, 1
    else:
        assert t % tm == 0
        nb, tb, blocks_per_seq = 1, tm, t // tm
    if nb > 1:
        seq_map = lambda i, j: (i, 0, j)
    else:
        seq_map = lambda i, j: (i // blocks_per_seq, 0, j)
    tail_map = lambda i, j: (i, 0, j)
    w_spec = pl.BlockSpec((d, tn), lambda i, j: (0, j))
    vmem = 2 * (tm * d * 2 + 2 * d * tn * 2 + tm * tn * 2 + 2 * nb * 8 * tn * 4) + 8 * tm * tn * 4
    g, conv_new = pl.pallas_call(
        functools.partial(_ffn_up_kernel, nb=nb, tb=tb, blocks_per_seq=blocks_per_seq),
        grid=(m // tm, f // tn),
        in_specs=[pl.BlockSpec((tm, d), lambda i, j: (i, 0)), w_spec, w_spec,
                  pl.BlockSpec((CONV_W, tn), lambda i, j: (0, j)),
                  pl.BlockSpec((1, tn), lambda i, j: (0, j)),
                  pl.BlockSpec((nb, CONV_W - 1, tn), seq_map)],
        out_specs=[pl.BlockSpec((tm, tn), lambda i, j: (i, j)),
                   pl.BlockSpec((nb, CONV_W - 1, tn), tail_map)],
        out_shape=[jax.ShapeDtypeStruct((m, f), BF16),
                   jax.ShapeDtypeStruct((b * blocks_per_seq, CONV_W - 1, f), F32)],
        scratch_shapes=[pltpu.VMEM((f // tn, CONV_W - 1, tn), F32)],
        compiler_params=_compiler_params(("arbitrary", "arbitrary"), vmem),
        name="ffn_up",
    )(h, w_a, w_b, conv_w, conv_b, state)
    return g, conv_new.reshape(b, blocks_per_seq, CONV_W - 1, f)[:, -1]


def _pad_cols(a, f_pad):
    return jnp.pad(a, [(0, 0)] * (a.ndim - 1) + [(0, f_pad - a.shape[-1])])


def _cast_kernel(w_ref, o_ref, *, valid_rows, valid_cols):
    rb, cb = o_ref.shape
    rows = pl.program_id(0) * rb + lax.broadcasted_iota(jnp.int32, (rb, cb), 0)
    cols = pl.program_id(1) * cb + lax.broadcasted_iota(jnp.int32, (rb, cb), 1)
    ok = (rows < valid_rows) & (cols < valid_cols)
    o_ref[...] = jnp.where(ok, w_ref[...], 0.0).astype(o_ref.dtype)


def cast_weight(w, layer, *, rb, cb, col0=0, ncols=None, out_rows=None, out_cols=None):
    _, r, c = w.shape
    ncols = c - col0 if ncols is None else ncols
    out_rows = r if out_rows is None else out_rows
    out_cols = ncols if out_cols is None else out_cols
    assert col0 % cb == 0 and out_rows % rb == 0 and out_cols % cb == 0
    c0 = col0 // cb
    last_r = (r - 1) // rb
    last_c = (col0 + ncols - 1) // cb
    in_map = lambda i, j: (layer, jnp.minimum(i, last_r), jnp.minimum(j + c0, last_c))
    return pl.pallas_call(
        functools.partial(_cast_kernel, valid_rows=r, valid_cols=ncols),
        grid=(out_rows // rb, out_cols // cb),
        in_specs=[pl.BlockSpec((None, rb, cb), in_map)],
        out_specs=pl.BlockSpec((rb, cb), lambda i, j: (i, j)),
        out_shape=jax.ShapeDtypeStruct((out_rows, out_cols), BF16),
        compiler_params=_compiler_params(("parallel", "parallel"), 2 * rb * cb * 6),
        name="cast_weight",
    )(w)


def _layer_weights(l, P):
    depth, d, _ = P['w_in'].shape
    d_ff = P['w_ffn_down'].shape[1]
    f_pad = -(-d_ff // 512) * 512
    bw = d // 2
    assert d_ff % V7X_LANES == 0
    w_br = P['w_branch'].reshape(depth, N_BRANCH * bw, d)
    return {
        'w_in': cast_weight(P['w_in'], l, rb=d, cb=512),
        'w_mem_kv': cast_weight(P['w_mem_kv'], l, rb=d, cb=512),
        'w_branch': cast_weight(w_br, l, rb=bw, cb=d).reshape(N_BRANCH, bw, d),
        'w_out': cast_weight(P['w_out'], l, rb=d, cb=512),
        'w_up_a': cast_weight(P['w_ffn_up'], l, rb=d, cb=V7X_LANES, col0=0, ncols=d_ff, out_cols=f_pad),
        'w_up_b': cast_weight(P['w_ffn_up'], l, rb=d, cb=V7X_LANES, col0=d_ff, ncols=d_ff, out_cols=f_pad),
        'w_down': cast_weight(P['w_ffn_down'], l, rb=512, cb=d, out_rows=f_pad),
        'conv_w': _pad_cols(P['ffn_conv_w'][l], f_pad),
        'conv_b': _pad_cols(P['ffn_conv_b'][l].reshape(1, d_ff), f_pad),
        'f_pad': f_pad,
        'd_ff': d_ff,
    }


def _mixer_inputs(h, W, P, l, rope_tab, tm):
    gains = (P['a_q_norm_g'][l], P['a_k_norm_g'][l], P['b_q_norm_g'][l], P['b_k_norm_g'][l], P['m_q_norm_g'][l])
    return proj_in(h, W['w_in'], gains, rope_tab, tm=tm)


def _finish_layer(x, h, outs, W, P, l, state, b, t, tm):
    d = x.shape[1]
    gate_b = P['gate_b'][l].reshape(N_BRANCH, 1, d)
    merged = merge_branches(h, *outs, W['w_in'], 7 * (d // 2), gate_b, W['w_branch'], tm=tm)
    res_spec = lambda tn: pl.BlockSpec((tm, tn), lambda i, j: (i, j))
    x, h = proj_out_norm(merged, W['w_out'], x, P['norm_ffn_g'][l], tm=tm)
    g, conv_new = ffn_up(h, W['w_up_a'], W['w_up_b'], W['conv_w'], W['conv_b'], state, b, t, tm=tm)
    x = _proj_call(_proj_residual_kernel, g, W['w_down'], 0, d, F32, [(x, res_spec(512))], tm=tm, name="ffn_down")
    return x, conv_new[:, :, :W['d_ff']]


def kernel(x_prompt, x_sample, cache_a_k, cache_a_v, cache_b_k, cache_b_v, cache_mem_k, cache_mem_v, state_ffn_conv, mem_prompt, norm_mix_g, w_in, a_q_norm_g, a_k_norm_g, a_rel_bias, b_q_norm_g, b_k_norm_g, b_lam_q1, b_lam_k1, b_lam_q2, b_lam_k2, b_subln_g, m_q_norm_g, m_k_norm_g, mem_norm_g, w_mem_kv, gate_b, w_branch, w_out, norm_ffn_g, w_ffn_up, ffn_conv_w, ffn_conv_b, w_ffn_down):
    P = {'w_in': w_in, 'a_q_norm_g': a_q_norm_g, 'a_k_norm_g': a_k_norm_g, 'b_q_norm_g': b_q_norm_g,
         'b_k_norm_g': b_k_norm_g, 'm_q_norm_g': m_q_norm_g, 'm_k_norm_g': m_k_norm_g, 'w_mem_kv': w_mem_kv,
         'gate_b': gate_b, 'w_branch': w_branch, 'w_out': w_out, 'norm_ffn_g': norm_ffn_g,
         'w_ffn_up': w_ffn_up, 'ffn_conv_w': ffn_conv_w, 'ffn_conv_b': ffn_conv_b, 'w_ffn_down': w_ffn_down}
    bp, tp, d = x_prompt.shape
    bs, ts, _ = x_sample.shape
    depth = w_in.shape[0]
    bw = d // 2
    past = cache_b_k.shape[2]
    a_len = cache_a_k.shape[2]
    n_mem = mem_prompt.shape[1]
    a_keep = min(BAND_PAST, tp)
    h_a = bw // HD_A
    h_b = bw // (2 * DIFF_HD)
    hd_m = bw // H_M
    mp, ms = bp * tp, bs * ts
    tm_p = _row_tile(mp, 1024)
    tm_s = _row_tile(ms, 1024)
    assert tm_p <= tp and tp % tm_p == 0 or tm_p % tp == 0

    pos_s = past + np.arange(ts)
    key_pos_a = np.concatenate([past - a_len + np.arange(a_len), pos_s])
    q_chunk_s = pos_s // CHUNK
    k_chunk_a = key_pos_a // CHUNK
    valid_a_s = (k_chunk_a[None, :] <= q_chunk_s[:, None]) & (k_chunk_a[None, :] >= q_chunk_s[:, None] - BAND_CHUNKS)
    mask_a_s = jnp.asarray(np.where(valid_a_s, 0.0, NEG_INF), F32)
    key_pos_b = np.concatenate([np.arange(past), pos_s])
    valid_b_s = (key_pos_b // CHUNK)[None, :] <= q_chunk_s[:, None]
    assert valid_b_s.all(), "sample queries are expected to see every cached and new differential key"

    rope_p = _rope_table(jnp.arange(max(tp, tm_p), dtype=jnp.int32) % tp)
    rope_s = _rope_table(past + (jnp.arange(max(ts, tm_s), dtype=jnp.int32) % ts))

    xp = x_prompt.reshape(mp, d)
    xs = x_sample.reshape(ms, d)
    mem2d = mem_prompt.reshape(bp * n_mem, d)
    outs = {k: [] for k in ('ak_p', 'av_p', 'bk_p', 'bv_p', 'mk_p', 'mv_p', 'cv_p',
                            'ak_s', 'av_s', 'bk_s', 'bv_s', 'cv_s')}
    for l in range(depth):
        W = _layer_weights(l, P)
        lam_init = 0.8 - 0.6 * math.exp(-0.3 * l)
        lam_params = jnp.stack([b_lam_q1[l], b_lam_k1[l], b_lam_q2[l], b_lam_k2[l]]).astype(F32)
        bias_row = _rel_bias_row(a_rel_bias[l])

        h = rmsnorm_cast(xp, norm_mix_g[l])
        qa, ka, va, qb, kb, vb, qm = _mixer_inputs(h, W, P, l, rope_p, tm_p)
        oa = attn_a_prompt(qa, ka, va, bias_row, bp, tp)
        ob = attn_b_prompt(qb, kb, vb, lam_params, b_subln_g[l], lam_init, bp, tp)
        hm = rmsnorm_cast(mem2d, mem_norm_g[l])
        tm_m = _row_tile(bp * n_mem, 1024)
        mk = _proj_call(functools.partial(_proj_headnorm_kernel, hd=hd_m), hm, W['w_mem_kv'], 0, bw, F32,
                        [(m_k_norm_g[l].reshape(1, hd_m), pl.BlockSpec((1, hd_m), lambda i, j: (0, 0)))],
                        tm=tm_m, name="proj_mk")
        mv = _proj_call(_proj_plain_kernel, hm, W['w_mem_kv'], bw, bw, F32, [], tm=tm_m, name="proj_mv")
        om = attn_m_prompt(qm, mk, mv, bp, tp)
        zeros_state = jnp.zeros((bp, CONV_W - 1, W['f_pad']), F32)
        xp, conv_new = _finish_layer(xp, h, (oa, ob, om), W, P, l, zeros_state, bp, tp, tm_p)
        outs['ak_p'].append(ka.reshape(bp, tp, h_a, HD_A)[:, tp - a_keep:])
        outs['av_p'].append(va.reshape(bp, tp, h_a, HD_A)[:, tp - a_keep:])
        outs['bk_p'].append(kb.reshape(bp, tp, h_b, 2 * DIFF_HD))
        outs['bv_p'].append(vb.reshape(bp, tp, h_b, 2 * DIFF_HD))
        outs['mk_p'].append(mk.reshape(bp, n_mem, H_M, hd_m))
        outs['mv_p'].append(mv.reshape(bp, n_mem, H_M, hd_m))
        outs['cv_p'].append(conv_new)

        h = rmsnorm_cast(xs, norm_mix_g[l])
        qa, ka, va, qb, kb, vb, qm = _mixer_inputs(h, W, P, l, rope_s, tm_s)
        oa = attn_a_sample(qa, ka, va, cache_a_k, cache_a_v, l, bias_row, mask_a_s, bs, ts)
        ob = attn_b_sample(qb, kb, vb, cache_b_k, cache_b_v, l, lam_params, b_subln_g[l], lam_init, bs, ts)
        om = attn_m_sample(qm, cache_mem_k, cache_mem_v, l, bs, ts)
        state = _pad_cols(state_ffn_conv[l], W['f_pad'])
        xs, conv_new = _finish_layer(xs, h, (oa, ob, om), W, P, l, state, bs, ts, tm_s)
        outs['ak_s'].append(ka.reshape(bs, ts, h_a, HD_A))
        outs['av_s'].append(va.reshape(bs, ts, h_a, HD_A))
        outs['bk_s'].append(kb.reshape(bs, ts, h_b, 2 * DIFF_HD))
        outs['bv_s'].append(vb.reshape(bs, ts, h_b, 2 * DIFF_HD))
        outs['cv_s'].append(conv_new)

    stack = lambda k: jnp.stack(outs[k])
    return (xp.reshape(bp, tp, d), xs.reshape(bs, ts, d),
            stack('ak_p'), stack('av_p'), stack('bk_p'), stack('bv_p'), stack('mk_p'), stack('mv_p'), stack('cv_p'),
            stack('ak_s'), stack('av_s'), stack('bk_s'), stack('bv_s'), stack('cv_s'))
```

```python
import functools
import math

import numpy as np
import jax
import jax.numpy as jnp
from jax import lax
from jax.experimental import pallas as pl
from jax.experimental.pallas import tpu as pltpu

F32 = jnp.float32
BF16 = jnp.bfloat16

CHUNK = 64
BAND_CHUNKS = 8
BAND_PAST = BAND_CHUNKS * CHUNK
REL_CLIP = 128
HD_A = 128
DIFF_HD = 64
ROT_DIM = DIFF_HD // 4
ROPE_THETA = 500000.0
H_M = 4
N_BRANCH = 3
CONV_W = 3
EPS = 1e-6
NEG_INF = -1e30
LOG2E = math.log2(math.e)

V7X_LANES = 128
V7X_VMEM_BYTES = 64 * 1024 * 1024
MIB = 1024 * 1024


def _compiler_params(semantics, vmem_estimate_bytes):
    limit = min(int(vmem_estimate_bytes * 1.25) + 8 * MIB, V7X_VMEM_BYTES - 4 * MIB)
    return pltpu.CompilerParams(dimension_semantics=semantics, vmem_limit_bytes=limit)


def _row_tile(m, target):
    t = min(m, target)
    assert m % t == 0, (m, t)
    return t


def _rmsnorm_kernel(x_ref, g_ref, o_ref):
    x = x_ref[...]
    ms = jnp.mean(x * x, axis=-1, keepdims=True)
    o_ref[...] = (x * lax.rsqrt(ms + EPS) * g_ref[...]).astype(o_ref.dtype)


def rmsnorm_cast(x, g):
    m, d = x.shape
    tm = _row_tile(m, 512)
    return pl.pallas_call(
        _rmsnorm_kernel,
        grid=(m // tm,),
        in_specs=[pl.BlockSpec((tm, d), lambda i: (i, 0)), pl.BlockSpec((1, d), lambda i: (0, 0))],
        out_specs=pl.BlockSpec((tm, d), lambda i: (i, 0)),
        out_shape=jax.ShapeDtypeStruct((m, d), BF16),
        compiler_params=_compiler_params(("parallel",), 2 * tm * d * 6),
        name="rmsnorm_cast",
    )(x, g.reshape(1, d))


def _dot(a, b):
    return jnp.dot(a, b, preferred_element_type=F32)


def _dot_nt(a, b):
    return lax.dot_general(a, b, (((1,), (1,)), ((), ())), preferred_element_type=F32)


def _proj_plain_kernel(h_ref, w_ref, o_ref):
    o_ref[...] = _dot(h_ref[...], w_ref[...]).astype(o_ref.dtype)


def _headnorm_store(acc, g, o_ref, hd):
    for k in range(acc.shape[1] // hd):
        s = acc[:, k * hd:(k + 1) * hd]
        ms = jnp.mean(s * s, axis=-1, keepdims=True)
        o_ref[:, k * hd:(k + 1) * hd] = (s * lax.rsqrt(ms + EPS) * g).astype(o_ref.dtype)


def _norm_rope_store(acc, g, tab_ref, o_ref):
    rows, width = acc.shape
    grp_r = lax.broadcasted_iota(jnp.int32, (width, width), 0) // DIFF_HD
    grp_c = lax.broadcasted_iota(jnp.int32, (width, width), 1) // DIFF_HD
    ones_bd = jnp.where(grp_r == grp_c, 1.0, 0.0).astype(BF16)
    ms = _dot((acc * acc).astype(BF16), ones_bd) * (1.0 / DIFF_HD)
    y = acc * lax.rsqrt(ms + EPS)
    cos = tab_ref[:, 0:V7X_LANES]
    sin_up = tab_ref[:, V7X_LANES:2 * V7X_LANES]
    sin_dn = tab_ref[:, 2 * V7X_LANES:3 * V7X_LANES]
    half = ROT_DIM // 2
    for k in range(width // V7X_LANES):
        yk = y[:, k * V7X_LANES:(k + 1) * V7X_LANES] * g
        out = (yk * cos + pltpu.roll(yk, half, 1) * sin_up
               + pltpu.roll(yk, V7X_LANES - half, 1) * sin_dn)
        o_ref[:, k * V7X_LANES:(k + 1) * V7X_LANES] = out.astype(o_ref.dtype)


def _proj_headnorm_kernel(h_ref, w_ref, g_ref, o_ref, *, hd):
    _headnorm_store(_dot(h_ref[...], w_ref[...]), g_ref[...], o_ref, hd)


PROJ_SLAB = 256


def _proj_in_kernel(h_ref, w_ref, gqa_ref, gka_ref, gqb_ref, gkb_ref, gqm_ref, tab_ref,
                    qa_ref, ka_ref, va_ref, qb_ref, kb_ref, vb_ref, qm_ref, *, blocks_per_group, hd_m):
    group = pl.program_id(1) // blocks_per_group
    tn = w_ref.shape[1]

    def run(o_ref, epilogue):
        def body():
            slabs = [slice(c0, c0 + PROJ_SLAB) for c0 in range(0, tn, PROJ_SLAB)]
            accs = [_dot(h_ref[...], w_ref[:, cols]) for cols in slabs]
            for acc, cols in zip(accs, slabs):
                epilogue(acc, o_ref.at[:, cols])
        return body

    headnorm = lambda g_ref, hd: (lambda acc, o: _headnorm_store(acc, g_ref[...], o, hd))
    norm_rope = lambda g_ref: (lambda acc, o: _norm_rope_store(acc, g_ref[...], tab_ref, o))
    plain = lambda acc, o: o.__setitem__(Ellipsis, acc)
    bodies = (
        run(qa_ref, headnorm(gqa_ref, HD_A)), run(ka_ref, headnorm(gka_ref, HD_A)), run(va_ref, plain),
        run(qb_ref, norm_rope(gqb_ref)), run(kb_ref, norm_rope(gkb_ref)), run(vb_ref, plain),
        run(qm_ref, headnorm(gqm_ref, hd_m)),
    )
    for n, body in enumerate(bodies):
        pl.when(group == n)(body)


def proj_in(h, w_in, gains, rope_tab, *, tm, tn=512):
    m, d = h.shape
    bw = d // 2
    hd_m = bw // H_M
    tm = _row_tile(m, tm)
    assert bw % tn == 0 and tn % PROJ_SLAB == 0 and PROJ_SLAB % hd_m == 0 and rope_tab.shape[0] % tm == 0
    bpg = bw // tn
    n_tab = rope_tab.shape[0] // tm
    g_a_q, g_a_k, g_b_q, g_b_k, g_m_q = gains
    tile2 = lambda g: jnp.tile(g.reshape(1, -1), (1, 2))
    gain_args = [g_a_q.reshape(1, HD_A), g_a_k.reshape(1, HD_A), tile2(g_b_q), tile2(g_b_k), g_m_q.reshape(1, hd_m)]
    const = lambda a: pl.BlockSpec(a.shape, lambda i, j: (0, 0))

    def out_spec(n):
        return pl.BlockSpec((tm, tn), lambda i, j: (i, jnp.clip(j - n * bpg, 0, bpg - 1)))

    dtypes = (BF16, F32, F32, BF16, F32, F32, BF16)
    out_bytes = sum(tm * tn * jnp.dtype(t).itemsize for t in dtypes)
    vmem = 2 * (tm * d * 2 + d * tn * 2 + tm * 3 * V7X_LANES * 4 + out_bytes) + 2 * tm * tn * 4
    return pl.pallas_call(
        functools.partial(_proj_in_kernel, blocks_per_group=bpg, hd_m=hd_m),
        grid=(m // tm, 7 * bpg),
        in_specs=[pl.BlockSpec((tm, d), lambda i, j: (i, 0)), pl.BlockSpec((d, tn), lambda i, j: (0, j))]
                 + [const(g) for g in gain_args]
                 + [pl.BlockSpec((tm, 3 * V7X_LANES), lambda i, j: (i % n_tab, 0))],
        out_specs=[out_spec(n) for n in range(7)],
        out_shape=[jax.ShapeDtypeStruct((m, bw), t) for t in dtypes],
        compiler_params=_compiler_params(("parallel", "arbitrary"), vmem),
        name="proj_in",
    )(h, w_in, *gain_args, rope_tab)


def _proj_residual_kernel(h_ref, w_ref, x_ref, o_ref):
    o_ref[...] = x_ref[...] + _dot(h_ref[...], w_ref[...])


def _proj_out_norm_kernel(m_ref, w_ref, x_ref, g_ref, o_ref, h_ref, xrow):
    j = pl.program_id(1)
    y = x_ref[...] + _dot(m_ref[...], w_ref[...])
    o_ref[...] = y
    xrow[j] = y

    @pl.when(j == pl.num_programs(1) - 1)
    def _():
        nj, _, tn = xrow.shape
        ssq = None
        for jj in range(nj):
            xb = xrow[jj]
            part = jnp.sum(xb * xb, axis=-1, keepdims=True)
            ssq = part if ssq is None else ssq + part
        scale = lax.rsqrt(ssq * (1.0 / (nj * tn)) + EPS)
        for jj in range(nj):
            cols = slice(jj * tn, (jj + 1) * tn)
            h_ref[:, cols] = (xrow[jj] * scale * g_ref[:, cols]).astype(h_ref.dtype)


def proj_out_norm(merged, w_out, x, g, *, tm, tn=512):
    m, d = x.shape
    tm = _row_tile(m, tm)
    assert d % tn == 0
    row_spec = pl.BlockSpec((tm, d), lambda i, j: (i, 0))
    blk_spec = pl.BlockSpec((tm, tn), lambda i, j: (i, j))
    vmem = 2 * (tm * d * 2 + d * tn * 2 + 2 * tm * tn * 4 + tm * d * 2) + tm * d * 4 + 2 * tm * tn * 4
    return pl.pallas_call(
        _proj_out_norm_kernel,
        grid=(m // tm, d // tn),
        in_specs=[row_spec, pl.BlockSpec((d, tn), lambda i, j: (0, j)), blk_spec,
                  pl.BlockSpec((1, d), lambda i, j: (0, 0))],
        out_specs=[blk_spec, row_spec],
        out_shape=[jax.ShapeDtypeStruct((m, d), F32), jax.ShapeDtypeStruct((m, d), BF16)],
        scratch_shapes=[pltpu.VMEM((d // tn, tm, tn), F32)],
        compiler_params=_compiler_params(("parallel", "arbitrary"), vmem),
        name="proj_out_norm",
    )(merged, w_out, x, g.reshape(1, d))


def _proj_call(kernel_fn, h, w, col0, ncols, out_dtype, extras, *, tm=1024, tn=512, name):
    m, k = h.shape
    tm = _row_tile(m, tm)
    tn = min(tn, ncols)
    assert ncols % tn == 0 and col0 % tn == 0, (ncols, col0, tn)
    cb = col0 // tn
    in_specs = [pl.BlockSpec((tm, k), lambda i, j: (i, 0)),
                pl.BlockSpec((k, tn), lambda i, j: (0, j + cb))]
    in_specs += [spec for _, spec in extras]
    extra_bytes = sum(int(np.prod(spec.block_shape)) * a.dtype.itemsize for a, spec in extras)
    vmem = 2 * (tm * k * 2 + k * tn * 2 + tm * tn * 4 + extra_bytes) + tm * tn * 8
    return pl.pallas_call(
        kernel_fn,
        grid=(m // tm, ncols // tn),
        in_specs=in_specs,
        out_specs=pl.BlockSpec((tm, tn), lambda i, j: (i, j)),
        out_shape=jax.ShapeDtypeStruct((m, ncols), out_dtype),
        compiler_params=_compiler_params(("parallel", "arbitrary"), vmem),
        name=name,
    )(h, w, *[a for a, _ in extras])


def _rope_table(pos):
    half = ROT_DIM // 2
    inv_freq = jnp.exp(jnp.arange(half, dtype=F32) * (-2.0 * math.log(ROPE_THETA) / ROT_DIM))
    ang = pos.astype(F32)[:, None] * inv_freq[None, :]
    cos = jnp.cos(ang)
    sin = jnp.sin(ang)
    p = pos.shape[0]
    rest = DIFF_HD - ROT_DIM
    c64 = jnp.concatenate([cos, cos, jnp.ones((p, rest), F32)], axis=1)
    up64 = jnp.concatenate([jnp.zeros((p, half), F32), sin, jnp.zeros((p, rest), F32)], axis=1)
    dn64 = jnp.concatenate([-sin, jnp.zeros((p, half + rest), F32)], axis=1)
    return jnp.concatenate([c64, c64, up64, up64, dn64, dn64], axis=1)


A_QBLK = 4 * CHUNK
A_KBLK = BAND_PAST + A_QBLK
A_BIAS_W = 1024


def _rel_bias_row(tab):
    assert A_KBLK + A_QBLK - 1 <= A_BIAS_W
    lo = BAND_PAST - REL_CLIP
    hi = BAND_PAST + REL_CLIP + 1
    rep = lambda col, n: jnp.repeat(tab[:, col:col + 1], n, axis=1)
    row = jnp.concatenate([rep(0, lo), tab, rep(2 * REL_CLIP, A_KBLK - hi), rep(0, A_BIAS_W - A_KBLK)], axis=1)
    return row[:, None, :]


def _toeplitz_bias(row, rows, width):
    full = pltpu.roll(jnp.broadcast_to(row, (rows, A_BIAS_W)), 0, 1, stride=1, stride_axis=0)
    return full[:, :width]


def _softmax_pv(s, v):
    m = jnp.max(s, axis=-1, keepdims=True)
    p = jnp.exp(s - m)
    l = jnp.sum(p, axis=-1, keepdims=True)
    return _dot(p.astype(BF16), v), l


def _attn_a_prompt_kernel(q_ref, k_ref, v_ref, row_ref, o_ref, kb, vb, *, t):
    kscale = (HD_A ** -0.5) * LOG2E
    kb[...] = (k_ref[0] * kscale).astype(BF16)
    vb[:, :HD_A] = v_ref[0].astype(BF16)
    vb[:, HD_A:] = jnp.ones((t, HD_A), BF16)
    qc = lax.broadcasted_iota(jnp.int32, (A_QBLK, A_KBLK), 0) // CHUNK
    kc = lax.broadcasted_iota(jnp.int32, (A_QBLK, A_KBLK), 1) // CHUNK
    inband = (kc >= qc) & (kc <= qc + BAND_CHUNKS)
    bias = jnp.where(inband, _toeplitz_bias(row_ref[0], A_QBLK, A_KBLK) * LOG2E, NEG_INF)
    for i in range(t // A_QBLK):
        r0 = i * A_QBLK
        k0 = max(r0 - BAND_PAST, 0)
        k1 = r0 + A_QBLK
        q = q_ref[0, r0:k1, :]
        s = _dot_nt(q, kb[k0:k1, :]) + bias[:, A_KBLK - (k1 - k0):]
        p = jnp.exp2(s - jnp.max(s, axis=-1, keepdims=True))
        o = _dot(p.astype(BF16), vb[k0:k1, :])
        o_ref[0, r0:k1, :] = (o[:, :HD_A] / o[:, HD_A:]).astype(o_ref.dtype)


def attn_a_prompt(qa, ka, va, bias_row, b, t):
    h = qa.shape[1] // HD_A
    assert t % A_QBLK == 0 and A_QBLK % V7X_LANES == 0
    q3, k3, v3 = (a.reshape(b, t, h * HD_A) for a in (qa, ka, va))
    spec = pl.BlockSpec((1, t, HD_A), lambda bi, hi: (bi, 0, hi))
    vmem = 2 * t * HD_A * (2 + 4 + 4 + 2) + 2 * t * HD_A * 2 + 24 * MIB
    out = pl.pallas_call(
        functools.partial(_attn_a_prompt_kernel, t=t),
        grid=(b, h),
        in_specs=[spec, spec, spec, pl.BlockSpec((1, 1, A_BIAS_W), lambda bi, hi: (hi, 0, 0))],
        out_specs=spec,
        out_shape=jax.ShapeDtypeStruct((b, t, h * HD_A), BF16),
        scratch_shapes=[pltpu.VMEM((t, HD_A), BF16), pltpu.VMEM((t, 2 * HD_A), BF16)],
        compiler_params=_compiler_params(("parallel", "parallel"), vmem),
        name="attn_a_prompt",
    )(q3, k3, v3, bias_row)
    return out.reshape(b * t, h * HD_A)


def _attn_a_sample_kernel(q_ref, kn_ref, vn_ref, row_ref, mask_ref, kc_hbm, vc_hbm, o_ref, kbuf, vbuf, sem,
                          *, layer, heads, a_len):
    b = pl.program_id(0)
    nb = pl.num_programs(0)
    scale = HD_A ** -0.5
    t = q_ref.shape[2]

    def copies(bi, slot):
        out = []
        for h in range(heads):
            out.append(pltpu.make_async_copy(kc_hbm.at[layer, bi, :, h, :], kbuf.at[slot, h], sem.at[0, slot, h]))
            out.append(pltpu.make_async_copy(vc_hbm.at[layer, bi, :, h, :], vbuf.at[slot, h], sem.at[1, slot, h]))
        return out

    @pl.when(b == 0)
    def _():
        for c in copies(b, 0):
            c.start()

    slot = b % 2

    @pl.when(b + 1 < nb)
    def _():
        for c in copies(b + 1, 1 - slot):
            c.start()

    for c in copies(b, slot):
        c.wait()
    for h in range(heads):
        q = q_ref[0, h]
        bias = _toeplitz_bias(row_ref[h], t, a_len + t) + mask_ref[...]
        sc = _dot_nt(q, kbuf[slot, h].astype(BF16)) * scale + bias[:, :a_len]
        sn = _dot_nt(q, kn_ref[0, h].astype(BF16)) * scale + bias[:, a_len:]
        m = jnp.maximum(jnp.max(sc, axis=-1, keepdims=True), jnp.max(sn, axis=-1, keepdims=True))
        pc = jnp.exp(sc - m)
        pn = jnp.exp(sn - m)
        l = jnp.sum(pc, axis=-1, keepdims=True) + jnp.sum(pn, axis=-1, keepdims=True)
        o = _dot(pc.astype(BF16), vbuf[slot, h].astype(BF16)) + _dot(pn.astype(BF16), vn_ref[0, h].astype(BF16))
        o_ref[0, h] = (o / l).astype(o_ref.dtype)


def _by_head(a, b, t, heads, hd):
    return a.reshape(b, t, heads, hd).transpose(0, 2, 1, 3)


def attn_a_sample(qa, ka, va, cache_k, cache_v, layer, bias_row, mask, b, t):
    heads = qa.shape[1] // HD_A
    a_len = cache_k.shape[2]
    assert a_len == BAND_PAST and a_len + t <= A_KBLK
    head_spec = pl.BlockSpec((1, heads, t, HD_A), lambda bi: (bi, 0, 0, 0))
    vmem = 2 * 2 * heads * a_len * HD_A * 4 + 2 * 4 * heads * t * HD_A * 4 + 16 * MIB
    out = pl.pallas_call(
        functools.partial(_attn_a_sample_kernel, layer=layer, heads=heads, a_len=a_len),
        grid=(b,),
        in_specs=[head_spec, head_spec, head_spec,
                  pl.BlockSpec(bias_row.shape, lambda bi: (0, 0, 0)),
                  pl.BlockSpec(mask.shape, lambda bi: (0, 0)),
                  pl.BlockSpec(memory_space=pl.ANY), pl.BlockSpec(memory_space=pl.ANY)],
        out_specs=head_spec,
        out_shape=jax.ShapeDtypeStruct((b, heads, t, HD_A), BF16),
        scratch_shapes=[pltpu.VMEM((2, heads, a_len, HD_A), F32), pltpu.VMEM((2, heads, a_len, HD_A), F32),
                        pltpu.SemaphoreType.DMA((2, 2, heads))],
        compiler_params=_compiler_params(("arbitrary",), vmem),
        name="attn_a_sample",
    )(*(_by_head(a, b, t, heads, HD_A) for a in (qa, ka, va)), bias_row, mask, cache_k, cache_v)
    return out.transpose(0, 2, 1, 3).reshape(b * t, heads * HD_A)


def _diff_lambda(lam_ref, lam_init):
    v = lam_ref[...]
    d1 = jnp.sum(v[0:1] * v[1:2], axis=-1, keepdims=True)
    d2 = jnp.sum(v[2:3] * v[3:4], axis=-1, keepdims=True)
    return jnp.exp(d1) - jnp.exp(d2) + lam_init


def _split_diff_queries(q):
    lane = lax.broadcasted_iota(jnp.int32, q.shape, 1)
    qs = q * jnp.asarray(DIFF_HD ** -0.5, q.dtype)
    zero = jnp.zeros_like(qs)
    return jnp.where(lane < DIFF_HD, qs, zero), jnp.where(lane >= DIFF_HD, qs, zero)


def _stack_diff_queries(q):
    return jnp.concatenate(_split_diff_queries(q), axis=0)


def _diff_post(o, g, post_scale):
    ms = jnp.mean(o * o, axis=-1, keepdims=True)
    return (o * lax.rsqrt(ms + EPS) * g) * post_scale


def _diff_finish(l, acc, lam, g, post_scale, tq):
    o = acc[:tq] / l[:tq] - lam * (acc[tq:] / l[tq:])
    return _diff_post(o, g, post_scale)


B_TQ = 8 * CHUNK


def _online_step(carry, s, v_ext):
    m, acc = carry
    m_new = jnp.maximum(m, jnp.max(s, axis=-1, keepdims=True))
    alpha = jnp.exp2(m - m_new)
    p = jnp.exp2(s - m_new)
    acc = alpha * acc + _dot(p.astype(BF16), v_ext)
    return m_new, acc


def _attn_b_prompt_kernel(lam_ref, q_ref, k_ref, v_ref, g_ref, o_ref, kb, vb, *, t, lam_init):
    tq = B_TQ
    hd = 2 * DIFF_HD
    kb[...] = (k_ref[0] * LOG2E).astype(BF16)
    vb[:, :hd] = v_ref[0].astype(BF16)
    vb[:, hd:] = jnp.ones((t, hd), BF16)
    lam = _diff_lambda(lam_ref, lam_init)
    row = lax.broadcasted_iota(jnp.int32, (tq, tq), 0)
    col = lax.broadcasted_iota(jnp.int32, (tq, tq), 1)
    diag_ok = (col // CHUNK) <= (row // CHUNK)
    for qi in range(t // tq):
        q0 = qi * tq
        qs = _split_diff_queries(q_ref[0, q0:q0 + tq, :])
        spans = ([(0, q0, False)] if q0 else []) + [(q0, tq, True)]
        outs = []
        for c in range(2):
            carry = (jnp.full((tq, 1), NEG_INF, F32), jnp.zeros((tq, 2 * hd), F32))
            for k0, width, masked in spans:
                s = _dot_nt(qs[c], kb[k0:k0 + width, :])
                if masked:
                    s = jnp.where(diag_ok, s, NEG_INF)
                carry = _online_step(carry, s, vb[k0:k0 + width, :])
            outs.append(carry[1][:, :hd] / carry[1][:, hd:])
        o = outs[0] - lam * outs[1]
        o_ref[0, q0:q0 + tq, :] = _diff_post(o, g_ref[...], 1.0 - lam_init).astype(o_ref.dtype)


def attn_b_prompt(qb, kb, vb, lam_params, subln_g, lam_init, b, t):
    hd = 2 * DIFF_HD
    heads = qb.shape[1] // hd
    assert t % B_TQ == 0
    q3, k3, v3 = (a.reshape(b, t, heads * hd) for a in (qb, kb, vb))
    spec = pl.BlockSpec((1, t, hd), lambda bi, hi: (bi, 0, hi))
    vmem = 2 * t * hd * (2 + 4 + 4 + 2) + 2 * t * hd * 2 + 32 * MIB
    out = pl.pallas_call(
        functools.partial(_attn_b_prompt_kernel, t=t, lam_init=lam_init),
        grid=(b, heads),
        in_specs=[pl.BlockSpec(lam_params.shape, lambda bi, hi: (0, 0)), spec, spec, spec,
                  pl.BlockSpec((1, hd), lambda bi, hi: (0, 0))],
        out_specs=spec,
        out_shape=jax.ShapeDtypeStruct((b, t, heads * hd), BF16),
        scratch_shapes=[pltpu.VMEM((t, hd), BF16), pltpu.VMEM((t, 2 * hd), BF16)],
        compiler_params=_compiler_params(("parallel", "parallel"), vmem),
        name="attn_b_prompt",
    )(lam_params, q3, k3, v3, subln_g.reshape(1, hd))
    return out.reshape(b * t, heads * hd)


def _attn_b_sample_kernel(lam_ref, q_ref, kn_ref, vn_ref, g_ref, kc_hbm, vc_hbm, o_ref, kbuf, vbuf, sem,
                          *, layer, heads, t, lam_init):
    b = pl.program_id(0)
    nb = pl.num_programs(0)

    def copies(bi, h, slot):
        return (pltpu.make_async_copy(kc_hbm.at[layer, bi, :, h, :], kbuf.at[slot], sem.at[0, slot]),
                pltpu.make_async_copy(vc_hbm.at[layer, bi, :, h, :], vbuf.at[slot], sem.at[1, slot]))

    def start(bi, h, slot):
        for c in copies(bi, h, slot):
            c.start()

    @pl.when(b == 0)
    def _():
        start(b, 0, 0)

    lam = _diff_lambda(lam_ref, lam_init)
    for h in range(heads):
        slot = h % 2
        if h + 1 < heads:
            start(b, h + 1, 1 - slot)
        else:
            @pl.when(b + 1 < nb)
            def _():
                start(b + 1, 0, 1 - slot)
        for c in copies(b, h, slot):
            c.wait()
        q2 = _stack_diff_queries(q_ref[0, h])
        sc = _dot_nt(q2, kbuf[slot].astype(BF16))
        sn = _dot_nt(q2, kn_ref[0, h].astype(BF16))
        m = jnp.maximum(jnp.max(sc, axis=-1, keepdims=True), jnp.max(sn, axis=-1, keepdims=True))
        pc = jnp.exp(sc - m)
        pn = jnp.exp(sn - m)
        l = jnp.sum(pc, axis=-1, keepdims=True) + jnp.sum(pn, axis=-1, keepdims=True)
        acc = _dot(pc.astype(BF16), vbuf[slot].astype(BF16)) + _dot(pn.astype(BF16), vn_ref[0, h].astype(BF16))
        o_ref[0, h] = _diff_finish(l, acc, lam, g_ref[...], 1.0 - lam_init, t).astype(o_ref.dtype)


def attn_b_sample(qb, kb, vb, cache_k, cache_v, layer, lam_params, subln_g, lam_init, b, t):
    hd = 2 * DIFF_HD
    heads = qb.shape[1] // hd
    assert heads % 2 == 0
    past = cache_k.shape[2]
    head_spec = pl.BlockSpec((1, heads, t, hd), lambda bi: (bi, 0, 0, 0))
    vmem = 2 * 2 * past * hd * 4 + 2 * 3 * heads * t * hd * 4 + 12 * 2 * t * past * 4
    out = pl.pallas_call(
        functools.partial(_attn_b_sample_kernel, layer=layer, heads=heads, t=t, lam_init=lam_init),
        grid=(b,),
        in_specs=[pl.BlockSpec(lam_params.shape, lambda bi: (0, 0)), head_spec, head_spec, head_spec,
                  pl.BlockSpec((1, hd), lambda bi: (0, 0)),
                  pl.BlockSpec(memory_space=pl.ANY), pl.BlockSpec(memory_space=pl.ANY)],
        out_specs=head_spec,
        out_shape=jax.ShapeDtypeStruct((b, heads, t, hd), BF16),
        scratch_shapes=[pltpu.VMEM((2, past, hd), F32), pltpu.VMEM((2, past, hd), F32),
                        pltpu.SemaphoreType.DMA((2, 2))],
        compiler_params=_compiler_params(("arbitrary",), vmem),
        name="attn_b_sample",
    )(lam_params, *(_by_head(a, b, t, heads, hd) for a in (qb, kb, vb)), subln_g.reshape(1, hd), cache_k, cache_v)
    return out.transpose(0, 2, 1, 3).reshape(b * t, heads * hd)


def _attn_m_kernel(q_ref, k_ref, v_ref, o_ref, *, heads, hd):
    scale = hd ** -0.5
    for h in range(heads):
        sl = slice(h * hd, (h + 1) * hd)
        s = _dot_nt(q_ref[0, :, sl], k_ref[0, :, sl].astype(BF16)) * scale
        o, l = _softmax_pv(s, v_ref[0, :, sl].astype(BF16))
        o_ref[0, :, sl] = (o / l).astype(o_ref.dtype)


def attn_m_prompt(qm, mem_k, mem_v, b, t, *, tq=512):
    width = qm.shape[1]
    hd = width // H_M
    tq = min(tq, t)
    assert t % tq == 0
    n = mem_k.shape[0] // b
    q_spec = pl.BlockSpec((1, tq, width), lambda bi, qi: (bi, qi, 0))
    kv_spec = pl.BlockSpec((1, n, width), lambda bi, qi: (bi, 0, 0))
    vmem = 2 * (2 * n * width * 4 + 2 * tq * width * 2) + 8 * tq * n * 4
    out = pl.pallas_call(
        functools.partial(_attn_m_kernel, heads=H_M, hd=hd),
        grid=(b, t // tq),
        in_specs=[q_spec, kv_spec, kv_spec],
        out_specs=q_spec,
        out_shape=jax.ShapeDtypeStruct((b, t, width), BF16),
        compiler_params=_compiler_params(("parallel", "arbitrary"), vmem),
        name="attn_m_prompt",
    )(qm.reshape(b, t, width), mem_k.reshape(b, n, width), mem_v.reshape(b, n, width))
    return out.reshape(b * t, width)


def _attn_m_sample_kernel(q_ref, kc_hbm, vc_hbm, o_ref, kbuf, vbuf, sem, *, layer, heads, hd):
    b = pl.program_id(0)
    nb = pl.num_programs(0)
    scale = hd ** -0.5

    def copies(bi, slot):
        out = []
        for h in range(heads):
            out.append(pltpu.make_async_copy(kc_hbm.at[layer, bi, :, h, :], kbuf.at[slot, h], sem.at[0, slot, h]))
            out.append(pltpu.make_async_copy(vc_hbm.at[layer, bi, :, h, :], vbuf.at[slot, h], sem.at[1, slot, h]))
        return out

    @pl.when(b == 0)
    def _():
        for c in copies(b, 0):
            c.start()

    slot = b % 2

    @pl.when(b + 1 < nb)
    def _():
        for c in copies(b + 1, 1 - slot):
            c.start()

    for c in copies(b, slot):
        c.wait()
    for h in range(heads):
        s = _dot_nt(q_ref[0, h], kbuf[slot, h].astype(BF16)) * scale
        o, l = _softmax_pv(s, vbuf[slot, h].astype(BF16))
        o_ref[0, h] = (o / l).astype(o_ref.dtype)


def attn_m_sample(qm, cache_k, cache_v, layer, b, t):
    n, heads, hd = cache_k.shape[2:]
    head_spec = pl.BlockSpec((1, heads, t, hd), lambda bi: (bi, 0, 0, 0))
    vmem = 2 * 2 * heads * n * hd * 4 + 2 * 2 * heads * t * hd * 2 + 8 * MIB
    out = pl.pallas_call(
        functools.partial(_attn_m_sample_kernel, layer=layer, heads=heads, hd=hd),
        grid=(b,),
        in_specs=[head_spec, pl.BlockSpec(memory_space=pl.ANY), pl.BlockSpec(memory_space=pl.ANY)],
        out_specs=head_spec,
        out_shape=jax.ShapeDtypeStruct((b, heads, t, hd), BF16),
        scratch_shapes=[pltpu.VMEM((2, heads, n, hd), F32), pltpu.VMEM((2, heads, n, hd), F32),
                        pltpu.SemaphoreType.DMA((2, 2, heads))],
        compiler_params=_compiler_params(("arbitrary",), vmem),
        name="attn_m_sample",
    )(_by_head(qm, b, t, heads, hd), cache_k, cache_v)
    return out.transpose(0, 2, 1, 3).reshape(b * t, heads * hd)


def _merge_kernel(h_ref, oa_ref, ob_ref, om_ref, wga_ref, wgb_ref, wgm_ref, gb_ref, wbr_ref, o_ref):
    h = h_ref[...]
    acc = None
    for n, (o_n, wg_n) in enumerate(((oa_ref, wga_ref), (ob_ref, wgb_ref), (om_ref, wgm_ref))):
        gate = jax.nn.sigmoid(_dot(h, wg_n[...]) + gb_ref[n])
        term = gate * _dot(o_n[...], wbr_ref[n])
        acc = term if acc is None else acc + term
    o_ref[...] = acc.astype(o_ref.dtype)


def merge_branches(h, oa, ob, om, w_in, gate_col0, gate_b, w_br, *, tm, tn=256):
    m, bw = oa.shape
    d = w_br.shape[2]
    tm = _row_tile(m, tm)
    assert d % tn == 0 and gate_col0 % tn == 0
    nj = d // tn
    g0 = gate_col0 // tn
    row_spec = lambda width: pl.BlockSpec((tm, width), lambda i, j: (i, 0))
    gate_w_specs = [pl.BlockSpec((d, tn), functools.partial(lambda i, j, n: (0, g0 + n * nj + j), n=n))
                    for n in range(N_BRANCH)]
    vmem = 2 * (tm * d * 2 + 3 * tm * bw * 2 + 3 * d * tn * 2 + 3 * bw * tn * 2 + tm * tn * 2) + 6 * tm * tn * 4
    return pl.pallas_call(
        _merge_kernel,
        grid=(m // tm, nj),
        in_specs=[row_spec(d), row_spec(bw), row_spec(bw), row_spec(bw)] + gate_w_specs
                 + [pl.BlockSpec((N_BRANCH, 1, tn), lambda i, j: (0, 0, j)),
                    pl.BlockSpec((N_BRANCH, bw, tn), lambda i, j: (0, 0, j))],
        out_specs=pl.BlockSpec((tm, tn), lambda i, j: (i, j)),
        out_shape=jax.ShapeDtypeStruct((m, d), BF16),
        compiler_params=_compiler_params(("parallel", "arbitrary"), vmem),
        name="merge_branches",
    )(h, oa, ob, om, w_in, w_in, w_in, gate_b, w_br)


def _ffn_up_kernel(h_ref, wa_ref, wb_ref, cw_ref, cb_ref, st_ref, g_ref, cn_ref, carry, *, nb, tb, blocks_per_seq):
    i = pl.program_id(0)
    j = pl.program_id(1)
    tm, tn = g_ref.shape
    if nb == 1:
        @pl.when((i % blocks_per_seq) == 0)
        def _():
            carry[j] = st_ref[0]

        trow = lax.broadcasted_iota(jnp.int32, (tm, PROJ_SLAB), 0)
    else:
        trow = lax.broadcasted_iota(jnp.int32, (nb, tb, PROJ_SLAB), 1).reshape(tm, PROJ_SLAB)
    h = h_ref[...]
    slabs = [slice(c0, c0 + PROJ_SLAB) for c0 in range(0, tn, PROJ_SLAB)]
    dots = [(_dot(h, wa_ref[:, cols]), _dot(h, wb_ref[:, cols])) for cols in slabs]
    for cols, (a, bgate) in zip(slabs, dots):
        if nb == 1:
            prev = carry[j, :, cols]
            p0 = jnp.broadcast_to(prev[0:1], (tm, PROJ_SLAB))
            p1 = jnp.broadcast_to(prev[1:2], (tm, PROJ_SLAB))
            carry[j, :, cols] = a[tm - 2:tm]
            cn_ref[0, :, cols] = a[tm - 2:tm]
        else:
            st = st_ref[:, :, cols]
            p0 = jnp.broadcast_to(st[:, 0:1, :], (nb, tb, PROJ_SLAB)).reshape(tm, PROJ_SLAB)
            p1 = jnp.broadcast_to(st[:, 1:2, :], (nb, tb, PROJ_SLAB)).reshape(tm, PROJ_SLAB)
            cn_ref[:, :, cols] = a.reshape(nb, tb, PROJ_SLAB)[:, tb - 2:tb, :]
        am1 = jnp.where(trow == 0, p1, pltpu.roll(a, 1, 0))
        am2 = jnp.where(trow == 0, p0, jnp.where(trow == 1, p1, pltpu.roll(a, 2, 0)))
        cw = cw_ref[:, cols]
        c = cb_ref[:, cols] + am2 * cw[0:1] + am1 * cw[1:2] + a * cw[2:3]
        gelu = 0.5 * c * (1.0 + lax.erf(c * (2.0 ** -0.5)))
        g_ref[:, cols] = (gelu * bgate).astype(g_ref.dtype)


def ffn_up(h, w_a, w_b, conv_w, conv_b, state, b, t, *, tm=1024, tn=512):
    m, d = h.shape
    f = w_a.shape[1]
    tm = _row_tile(m, tm)
    assert f % tn == 0 and tn % PROJ_SLAB == 0
    if tm >= t:
        assert tm % t == 0
        nb, tb, blocks_per_seq = tm // t, t, 1
    else:
        assert t % tm == 0
        nb, tb, blocks_per_seq = 1, tm, t // tm
    if nb > 1:
        seq_map = lambda i, j: (i, 0, j)
    else:
        seq_map = lambda i, j: (i // blocks_per_seq, 0, j)
    tail_map = lambda i, j: (i, 0, j)
    w_spec = pl.BlockSpec((d, tn), lambda i, j: (0, j))
    vmem = 2 * (tm * d * 2 + 2 * d * tn * 2 + tm * tn * 2 + 2 * nb * 8 * tn * 4) + 8 * tm * tn * 4
    g, conv_new = pl.pallas_call(
        functools.partial(_ffn_up_kernel, nb=nb, tb=tb, blocks_per_seq=blocks_per_seq),
        grid=(m // tm, f // tn),
        in_specs=[pl.BlockSpec((tm, d), lambda i, j: (i, 0)), w_spec, w_spec,
                  pl.BlockSpec((CONV_W, tn), lambda i, j: (0, j)),
                  pl.BlockSpec((1, tn), lambda i, j: (0, j)),
                  pl.BlockSpec((nb, CONV_W - 1, tn), seq_map)],
        out_specs=[pl.BlockSpec((tm, tn), lambda i, j: (i, j)),
                   pl.BlockSpec((nb, CONV_W - 1, tn), tail_map)],
        out_shape=[jax.ShapeDtypeStruct((m, f), BF16),
                   jax.ShapeDtypeStruct((b * blocks_per_seq, CONV_W - 1, f), F32)],
        scratch_shapes=[pltpu.VMEM((f // tn, CONV_W - 1, tn), F32)],
        compiler_params=_compiler_params(("arbitrary", "arbitrary"), vmem),
        name="ffn_up",
    )(h, w_a, w_b, conv_w, conv_b, state)
    return g, conv_new.reshape(b, blocks_per_seq, CONV_W - 1, f)[:, -1]


def _pad_cols(a, f_pad):
    return jnp.pad(a, [(0, 0)] * (a.ndim - 1) + [(0, f_pad - a.shape[-1])])


def _cast_kernel(w_ref, o_ref, *, valid_rows, valid_cols):
    rb, cb = o_ref.shape
    rows = pl.program_id(0) * rb + lax.broadcasted_iota(jnp.int32, (rb, cb), 0)
    cols = pl.program_id(1) * cb + lax.broadcasted_iota(jnp.int32, (rb, cb), 1)
    ok = (rows < valid_rows) & (cols < valid_cols)
    o_ref[...] = jnp.where(ok, w_ref[...], 0.0).astype(o_ref.dtype)


def cast_weight(w, layer, *, rb, cb, col0=0, ncols=None, out_rows=None, out_cols=None):
    _, r, c = w.shape
    ncols = c - col0 if ncols is None else ncols
    out_rows = r if out_rows is None else out_rows
    out_cols = ncols if out_cols is None else out_cols
    assert col0 % cb == 0 and out_rows % rb == 0 and out_cols % cb == 0
    c0 = col0 // cb
    last_r = (r - 1) // rb
    last_c = (col0 + ncols - 1) // cb
    in_map = lambda i, j: (layer, jnp.minimum(i, last_r), jnp.minimum(j + c0, last_c))
    return pl.pallas_call(
        functools.partial(_cast_kernel, valid_rows=r, valid_cols=ncols),
        grid=(out_rows // rb, out_cols // cb),
        in_specs=[pl.BlockSpec((None, rb, cb), in_map)],
        out_specs=pl.BlockSpec((rb, cb), lambda i, j: (i, j)),
        out_shape=jax.ShapeDtypeStruct((out_rows, out_cols), BF16),
        compiler_params=_compiler_params(("parallel", "parallel"), 2 * rb * cb * 6),
        name="cast_weight",
    )(w)


def _layer_weights(l, P):
    depth, d, _ = P['w_in'].shape
    d_ff = P['w_ffn_down'].shape[1]
    f_pad = -(-d_ff // 512) * 512
    bw = d // 2
    assert d_ff % V7X_LANES == 0
    w_br = P['w_branch'].reshape(depth, N_BRANCH * bw, d)
    return {
        'w_in': cast_weight(P['w_in'], l, rb=d, cb=512),
        'w_mem_kv': cast_weight(P['w_mem_kv'], l, rb=d, cb=512),
        'w_branch': cast_weight(w_br, l, rb=bw, cb=d).reshape(N_BRANCH, bw, d),
        'w_out': cast_weight(P['w_out'], l, rb=d, cb=512),
        'w_up_a': cast_weight(P['w_ffn_up'], l, rb=d, cb=V7X_LANES, col0=0, ncols=d_ff, out_cols=f_pad),
        'w_up_b': cast_weight(P['w_ffn_up'], l, rb=d, cb=V7X_LANES, col0=d_ff, ncols=d_ff, out_cols=f_pad),
        'w_down': cast_weight(P['w_ffn_down'], l, rb=512, cb=d, out_rows=f_pad),
        'conv_w': _pad_cols(P['ffn_conv_w'][l], f_pad),
        'conv_b': _pad_cols(P['ffn_conv_b'][l].reshape(1, d_ff), f_pad),
        'f_pad': f_pad,
        'd_ff': d_ff,
    }


def _mixer_inputs(h, W, P, l, rope_tab, tm):
    gains = (P['a_q_norm_g'][l], P['a_k_norm_g'][l], P['b_q_norm_g'][l], P['b_k_norm_g'][l], P['m_q_norm_g'][l])
    return proj_in(h, W['w_in'], gains, rope_tab, tm=tm)


def _finish_layer(x, h, outs, W, P, l, state, b, t, tm):
    d = x.shape[1]
    gate_b = P['gate_b'][l].reshape(N_BRANCH, 1, d)
    merged = merge_branches(h, *outs, W['w_in'], 7 * (d // 2), gate_b, W['w_branch'], tm=tm)
    res_spec = lambda tn: pl.BlockSpec((tm, tn), lambda i, j: (i, j))
    x, h = proj_out_norm(merged, W['w_out'], x, P['norm_ffn_g'][l], tm=tm)
    g, conv_new = ffn_up(h, W['w_up_a'], W['w_up_b'], W['conv_w'], W['conv_b'], state, b, t, tm=tm)
    x = _proj_call(_proj_residual_kernel, g, W['w_down'], 0, d, F32, [(x, res_spec(512))], tm=tm, name="ffn_down")
    return x, conv_new[:, :, :W['d_ff']]


def kernel(x_prompt, x_sample, cache_a_k, cache_a_v, cache_b_k, cache_b_v, cache_mem_k, cache_mem_v, state_ffn_conv, mem_prompt, norm_mix_g, w_in, a_q_norm_g, a_k_norm_g, a_rel_bias, b_q_norm_g, b_k_norm_g, b_lam_q1, b_lam_k1, b_lam_q2, b_lam_k2, b_subln_g, m_q_norm_g, m_k_norm_g, mem_norm_g, w_mem_kv, gate_b, w_branch, w_out, norm_ffn_g, w_ffn_up, ffn_conv_w, ffn_conv_b, w_ffn_down):
    P = {'w_in': w_in, 'a_q_norm_g': a_q_norm_g, 'a_k_norm_g': a_k_norm_g, 'b_q_norm_g': b_q_norm_g,
         'b_k_norm_g': b_k_norm_g, 'm_q_norm_g': m_q_norm_g, 'm_k_norm_g': m_k_norm_g, 'w_mem_kv': w_mem_kv,
         'gate_b': gate_b, 'w_branch': w_branch, 'w_out': w_out, 'norm_ffn_g': norm_ffn_g,
         'w_ffn_up': w_ffn_up, 'ffn_conv_w': ffn_conv_w, 'ffn_conv_b': ffn_conv_b, 'w_ffn_down': w_ffn_down}
    bp, tp, d = x_prompt.shape
    bs, ts, _ = x_sample.shape
    depth = w_in.shape[0]
    bw = d // 2
    past = cache_b_k.shape[2]
    a_len = cache_a_k.shape[2]
    n_mem = mem_prompt.shape[1]
    a_keep = min(BAND_PAST, tp)
    h_a = bw // HD_A
    h_b = bw // (2 * DIFF_HD)
    hd_m = bw // H_M
    mp, ms = bp * tp, bs * ts
    tm_p = _row_tile(mp, 1024)
    tm_s = _row_tile(ms, 1024)
    assert tm_p <= tp and tp % tm_p == 0 or tm_p % tp == 0

    pos_s = past + np.arange(ts)
    key_pos_a = np.concatenate([past - a_len + np.arange(a_len), pos_s])
    q_chunk_s = pos_s // CHUNK
    k_chunk_a = key_pos_a // CHUNK
    valid_a_s = (k_chunk_a[None, :] <= q_chunk_s[:, None]) & (k_chunk_a[None, :] >= q_chunk_s[:, None] - BAND_CHUNKS)
    mask_a_s = jnp.asarray(np.where(valid_a_s, 0.0, NEG_INF), F32)
    key_pos_b = np.concatenate([np.arange(past), pos_s])
    valid_b_s = (key_pos_b // CHUNK)[None, :] <= q_chunk_s[:, None]
    assert valid_b_s.all(), "sample queries are expected to see every cached and new differential key"

    rope_p = _rope_table(jnp.arange(max(tp, tm_p), dtype=jnp.int32) % tp)
    rope_s = _rope_table(past + (jnp.arange(max(ts, tm_s), dtype=jnp.int32) % ts))

    xp = x_prompt.reshape(mp, d)
    xs = x_sample.reshape(ms, d)
    mem2d = mem_prompt.reshape(bp * n_mem, d)
    outs = {k: [] for k in ('ak_p', 'av_p', 'bk_p', 'bv_p', 'mk_p', 'mv_p', 'cv_p',
                            'ak_s', 'av_s', 'bk_s', 'bv_s', 'cv_s')}
    for l in range(depth):
        W = _layer_weights(l, P)
        lam_init = 0.8 - 0.6 * math.exp(-0.3 * l)
        lam_params = jnp.stack([b_lam_q1[l], b_lam_k1[l], b_lam_q2[l], b_lam_k2[l]]).astype(F32)
        bias_row = _rel_bias_row(a_rel_bias[l])

        h = rmsnorm_cast(xp, norm_mix_g[l])
        qa, ka, va, qb, kb, vb, qm = _mixer_inputs(h, W, P, l, rope_p, tm_p)
        oa = attn_a_prompt(qa, ka, va, bias_row, bp, tp)
        ob = attn_b_prompt(qb, kb, vb, lam_params, b_subln_g[l], lam_init, bp, tp)
        hm = rmsnorm_cast(mem2d, mem_norm_g[l])
        tm_m = _row_tile(bp * n_mem, 1024)
        mk = _proj_call(functools.partial(_proj_headnorm_kernel, hd=hd_m), hm, W['w_mem_kv'], 0, bw, F32,
                        [(m_k_norm_g[l].reshape(1, hd_m), pl.BlockSpec((1, hd_m), lambda i, j: (0, 0)))],
                        tm=tm_m, name="proj_mk")
        mv = _proj_call(_proj_plain_kernel, hm, W['w_mem_kv'], bw, bw, F32, [], tm=tm_m, name="proj_mv")
        om = attn_m_prompt(qm, mk, mv, bp, tp)
        zeros_state = jnp.zeros((bp, CONV_W - 1, W['f_pad']), F32)
        xp, conv_new = _finish_layer(xp, h, (oa, ob, om), W, P, l, zeros_state, bp, tp, tm_p)
        outs['ak_p'].append(ka.reshape(bp, tp, h_a, HD_A)[:, tp - a_keep:])
        outs['av_p'].append(va.reshape(bp, tp, h_a, HD_A)[:, tp - a_keep:])
        outs['bk_p'].append(kb.reshape(bp, tp, h_b, 2 * DIFF_HD))
        outs['bv_p'].append(vb.reshape(bp, tp, h_b, 2 * DIFF_HD))
        outs['mk_p'].append(mk.reshape(bp, n_mem, H_M, hd_m))
        outs['mv_p'].append(mv.reshape(bp, n_mem, H_M, hd_m))
        outs['cv_p'].append(conv_new)

        h = rmsnorm_cast(xs, norm_mix_g[l])
        qa, ka, va, qb, kb, vb, qm = _mixer_inputs(h, W, P, l, rope_s, tm_s)
        oa = attn_a_sample(qa, ka, va, cache_a_k, cache_a_v, l, bias_row, mask_a_s, bs, ts)
        ob = attn_b_sample(qb, kb, vb, cache_b_k, cache_b_v, l, lam_params, b_subln_g[l], lam_init, bs, ts)
        om = attn_m_sample(qm, cache_mem_k, cache_mem_v, l, bs, ts)
        state = _pad_cols(state_ffn_conv[l], W['f_pad'])
        xs, conv_new = _finish_layer(xs, h, (oa, ob, om), W, P, l, state, bs, ts, tm_s)
        outs['ak_s'].append(ka.reshape(bs, ts, h_a, HD_A))
        outs['av_s'].append(va.reshape(bs, ts, h_a, HD_A))
        outs['bk_s'].append(kb.reshape(bs, ts, h_b, 2 * DIFF_HD))
        outs['bv_s'].append(vb.reshape(bs, ts, h_b, 2 * DIFF_HD))
        outs['cv_s'].append(conv_new)

    stack = lambda k: jnp.stack(outs[k])
    return (xp.reshape(bp, tp, d), xs.reshape(bs, ts, d),
            stack('ak_p'), stack('av_p'), stack('bk_p'), stack('bv_p'), stack('mk_p'), stack('mv_p'), stack('cv_p'),
            stack('ak_s'), stack('av_s'), stack('bk_s'), stack('bv_s'), stack('cv_s'))
```

```python
import functools
import math

import numpy as np
import jax
import jax.numpy as jnp
from jax import lax
from jax.experimental import pallas as pl
from jax.experimental.pallas import tpu as pltpu

F32 = jnp.float32
BF16 = jnp.bfloat16

CHUNK = 64
BAND_CHUNKS = 8
BAND_PAST = BAND_CHUNKS * CHUNK
REL_CLIP = 128
HD_A = 128
DIFF_HD = 64
ROT_DIM = DIFF_HD // 4
ROPE_THETA = 500000.0
H_M = 4
N_BRANCH = 3
CONV_W = 3
EPS = 1e-6
NEG_INF = -1e30
LOG2E = math.log2(math.e)

V7X_LANES = 128
V7X_VMEM_BYTES = 64 * 1024 * 1024
MIB = 1024 * 1024


def _compiler_params(semantics, vmem_estimate_bytes):
    limit = min(int(vmem_estimate_bytes * 1.25) + 8 * MIB, V7X_VMEM_BYTES - 4 * MIB)
    return pltpu.CompilerParams(dimension_semantics=semantics, vmem_limit_bytes=limit)


def _row_tile(m, target):
    t = min(m, target)
    assert m % t == 0, (m, t)
    return t


def _rmsnorm_kernel(x_ref, g_ref, o_ref):
    x = x_ref[...]
    ms = jnp.mean(x * x, axis=-1, keepdims=True)
    o_ref[...] = (x * lax.rsqrt(ms + EPS) * g_ref[...]).astype(o_ref.dtype)


def rmsnorm_cast(x, g):
    m, d = x.shape
    tm = _row_tile(m, 512)
    return pl.pallas_call(
        _rmsnorm_kernel,
        grid=(m // tm,),
        in_specs=[pl.BlockSpec((tm, d), lambda i: (i, 0)), pl.BlockSpec((1, d), lambda i: (0, 0))],
        out_specs=pl.BlockSpec((tm, d), lambda i: (i, 0)),
        out_shape=jax.ShapeDtypeStruct((m, d), BF16),
        compiler_params=_compiler_params(("parallel",), 2 * tm * d * 6),
        name="rmsnorm_cast",
    )(x, g.reshape(1, d))


def _dot(a, b):
    return jnp.dot(a, b, preferred_element_type=F32)


def _dot_nt(a, b):
    return lax.dot_general(a, b, (((1,), (1,)), ((), ())), preferred_element_type=F32)


def _proj_plain_kernel(h_ref, w_ref, o_ref):
    o_ref[...] = _dot(h_ref[...], w_ref[...]).astype(o_ref.dtype)


def _headnorm_store(acc, g, o_ref, hd):
    for k in range(acc.shape[1] // hd):
        s = acc[:, k * hd:(k + 1) * hd]
        ms = jnp.mean(s * s, axis=-1, keepdims=True)
        o_ref[:, k * hd:(k + 1) * hd] = (s * lax.rsqrt(ms + EPS) * g).astype(o_ref.dtype)


def _norm_rope_store(acc, g, tab_ref, o_ref):
    rows, width = acc.shape
    grp_r = lax.broadcasted_iota(jnp.int32, (width, width), 0) // DIFF_HD
    grp_c = lax.broadcasted_iota(jnp.int32, (width, width), 1) // DIFF_HD
    ones_bd = jnp.where(grp_r == grp_c, 1.0, 0.0).astype(BF16)
    ms = _dot((acc * acc).astype(BF16), ones_bd) * (1.0 / DIFF_HD)
    y = acc * lax.rsqrt(ms + EPS)
    cos = tab_ref[:, 0:V7X_LANES]
    sin_up = tab_ref[:, V7X_LANES:2 * V7X_LANES]
    sin_dn = tab_ref[:, 2 * V7X_LANES:3 * V7X_LANES]
    half = ROT_DIM // 2
    for k in range(width // V7X_LANES):
        yk = y[:, k * V7X_LANES:(k + 1) * V7X_LANES] * g
        out = (yk * cos + pltpu.roll(yk, half, 1) * sin_up
               + pltpu.roll(yk, V7X_LANES - half, 1) * sin_dn)
        o_ref[:, k * V7X_LANES:(k + 1) * V7X_LANES] = out.astype(o_ref.dtype)


def _proj_headnorm_kernel(h_ref, w_ref, g_ref, o_ref, *, hd):
    _headnorm_store(_dot(h_ref[...], w_ref[...]), g_ref[...], o_ref, hd)


PROJ_SLAB = 256


FINAL_GROUPS = (1, 2, 4, 5)


def _proj_in_kernel(h_ref, w_ref, gqa_ref, gka_ref, gqb_ref, gkb_ref, gqm_ref, tab_ref, *rest,
                    blocks_per_group, hd_m, final):
    i = pl.program_id(0)
    j = pl.program_id(1)
    group = j // blocks_per_group
    tm = h_ref.shape[0]
    tn = w_ref.shape[1]
    n_prev = 0 if final is None else 4 * final['layer']
    prev_refs = rest[:n_prev]
    qa_ref, ka_ref, va_ref, qb_ref, kb_ref, vb_ref, qm_ref = rest[n_prev:n_prev + 7]
    out_refs = {0: qa_ref, 1: ka_ref, 2: va_ref, 3: qb_ref, 4: kb_ref, 5: vb_ref, 6: qm_ref}
    if final is not None:
        final_refs = dict(zip(FINAL_GROUPS, rest[n_prev + 7:n_prev + 11]))
        stage, sem, copy_sem = rest[n_prev + 11:]
        heads_per_block = tn // V7X_LANES
        bps = final['seq_len'] // tm
        stacked = final['layer'] == final['depth'] - 1

        def head_copies(g, c, slot):
            keep = final['a_keep'] if g in (1, 2) else final['seq_len']
            if keep >= tm:
                rows, r_lo, t0 = tm, 0, (i % bps) * tm - (final['seq_len'] - keep)
            else:
                rows, r_lo, t0 = keep, tm - keep, 0
            dst = final_refs[g].at[final['layer']] if stacked else final_refs[g]
            out = []
            for hh in range(heads_per_block):
                src = stage.at[slot, pl.ds(r_lo, rows), pl.ds(hh * V7X_LANES, V7X_LANES)]
                out.append(pltpu.make_async_copy(
                    src, dst.at[i // bps, pl.ds(t0, rows), c * heads_per_block + hh, :], sem.at[hh]))
            return out

        def kept(g):
            keep = final['a_keep'] if g in (1, 2) else final['seq_len']
            return (i % bps) >= bps - max(keep // tm, 1)

        def wait_block(g, c, slot):
            @pl.when(kept(g))
            def _():
                for cp in head_copies(g, c, slot):
                    cp.wait()

        def layer_copies():
            out = []
            for p in range(final['layer']):
                for k, g in enumerate(FINAL_GROUPS):
                    out.append(pltpu.make_async_copy(prev_refs[4 * p + k], final_refs[g].at[p], copy_sem.at[p, k]))
            return out

        if stacked:
            @pl.when((i == 0) & (j == 0))
            def _():
                for cp in layer_copies():
                    cp.start()

    def run(g, epilogue):
        o_ref = out_refs[g]

        def body():
            slabs = [slice(c0, c0 + PROJ_SLAB) for c0 in range(0, tn, PROJ_SLAB)]
            accs = [_dot(h_ref[...], w_ref[:, cols]) for cols in slabs]
            for acc, cols in zip(accs, slabs):
                epilogue(acc, o_ref.at[:, cols])
            if final is None:
                return
            c = j - g * blocks_per_group
            slot = j % 2
            if g in FINAL_GROUPS:
                @pl.when(c > 0)
                def _():
                    wait_block(g, c - 1, 1 - slot)
            if g - 1 in FINAL_GROUPS:
                @pl.when(c == 0)
                def _():
                    wait_block(g - 1, blocks_per_group - 1, 1 - slot)
            if g in FINAL_GROUPS:
                @pl.when(kept(g))
                def _():
                    stage[slot] = o_ref[...]
                    for cp in head_copies(g, c, slot):
                        cp.start()
        return body

    headnorm = lambda g_ref, hd: (lambda acc, o: _headnorm_store(acc, g_ref[...], o, hd))
    norm_rope = lambda g_ref: (lambda acc, o: _norm_rope_store(acc, g_ref[...], tab_ref, o))
    plain = lambda acc, o: o.__setitem__(Ellipsis, acc)
    bodies = (
        run(0, headnorm(gqa_ref, HD_A)), run(1, headnorm(gka_ref, HD_A)), run(2, plain),
        run(3, norm_rope(gqb_ref)), run(4, norm_rope(gkb_ref)), run(5, plain),
        run(6, headnorm(gqm_ref, hd_m)),
    )
    for n, body in enumerate(bodies):
        pl.when(group == n)(body)

    if final is not None and stacked:
        @pl.when((i == pl.num_programs(0) - 1) & (j == pl.num_programs(1) - 1))
        def _():
            for cp in layer_copies():
                cp.wait()


def proj_in(h, w_in, gains, rope_tab, *, tm, tn=512, final=None):
    m, d = h.shape
    bw = d // 2
    hd_m = bw // H_M
    tm = _row_tile(m, tm)
    assert bw % tn == 0 and tn % PROJ_SLAB == 0 and PROJ_SLAB % hd_m == 0 and rope_tab.shape[0] % tm == 0
    bpg = bw // tn
    n_tab = rope_tab.shape[0] // tm
    g_a_q, g_a_k, g_b_q, g_b_k, g_m_q = gains
    tile2 = lambda g: jnp.tile(g.reshape(1, -1), (1, 2))
    gain_args = [g_a_q.reshape(1, HD_A), g_a_k.reshape(1, HD_A), tile2(g_b_q), tile2(g_b_k), g_m_q.reshape(1, hd_m)]
    const = lambda a: pl.BlockSpec(a.shape, lambda i, j: (0, 0))

    def out_spec(n):
        return pl.BlockSpec((tm, tn), lambda i, j: (i, jnp.clip(j - n * bpg, 0, bpg - 1)))

    dtypes = (BF16, F32, F32, BF16, F32, F32, BF16)
    out_specs = [out_spec(n) for n in range(7)]
    out_shape = [jax.ShapeDtypeStruct((m, bw), t) for t in dtypes]
    prev, scratch, semantics = [], [], ("parallel", "arbitrary")
    if final is not None:
        t, keep, batch = final['seq_len'], final['a_keep'], final['batch']
        heads = bw // V7X_LANES
        assert HD_A == V7X_LANES and 2 * DIFF_HD == V7X_LANES and m == batch * t and t % tm == 0
        assert keep % tm == 0 or (keep < tm and keep % 8 == 0)
        prev = [a for layer_arrays in final['prev'] for a in layer_arrays]
        assert len(prev) == 4 * final['layer']
        lead = (final['depth'],) if final['layer'] == final['depth'] - 1 else ()
        out_shape += [jax.ShapeDtypeStruct(lead + (batch, rows, heads, V7X_LANES), F32) for rows in (keep, keep, t, t)]
        out_specs += [pl.BlockSpec(memory_space=pl.ANY)] * 4
        scratch = [pltpu.VMEM((2, tm, tn), F32), pltpu.SemaphoreType.DMA((tn // V7X_LANES,)),
                   pltpu.SemaphoreType.DMA((max(final['layer'], 1), 4))]
        semantics = ("arbitrary", "arbitrary")
        final = {k: v for k, v in final.items() if k != 'prev'}
    out_bytes = sum(tm * tn * jnp.dtype(t).itemsize for t in dtypes)
    vmem = 2 * (tm * d * 2 + d * tn * 2 + tm * 3 * V7X_LANES * 4 + out_bytes) + 4 * tm * tn * 4
    return pl.pallas_call(
        functools.partial(_proj_in_kernel, blocks_per_group=bpg, hd_m=hd_m, final=final),
        grid=(m // tm, 7 * bpg),
        in_specs=[pl.BlockSpec((tm, d), lambda i, j: (i, 0)), pl.BlockSpec((d, tn), lambda i, j: (0, j))]
                 + [const(g) for g in gain_args]
                 + [pl.BlockSpec((tm, 3 * V7X_LANES), lambda i, j: (i % n_tab, 0))]
                 + [pl.BlockSpec(memory_space=pl.ANY)] * len(prev),
        out_specs=out_specs,
        out_shape=out_shape,
        scratch_shapes=scratch,
        compiler_params=_compiler_params(semantics, vmem),
        name="proj_in",
    )(h, w_in, *gain_args, rope_tab, *prev)


def _proj_residual_kernel(h_ref, w_ref, x_ref, o_ref):
    o_ref[...] = x_ref[...] + _dot(h_ref[...], w_ref[...])


def _proj_out_norm_kernel(m_ref, w_ref, x_ref, g_ref, o_ref, h_ref, xrow):
    j = pl.program_id(1)
    y = x_ref[...] + _dot(m_ref[...], w_ref[...])
    o_ref[...] = y
    xrow[j] = y

    @pl.when(j == pl.num_programs(1) - 1)
    def _():
        nj, _, tn = xrow.shape
        ssq = None
        for jj in range(nj):
            xb = xrow[jj]
            part = jnp.sum(xb * xb, axis=-1, keepdims=True)
            ssq = part if ssq is None else ssq + part
        scale = lax.rsqrt(ssq * (1.0 / (nj * tn)) + EPS)
        for jj in range(nj):
            cols = slice(jj * tn, (jj + 1) * tn)
            h_ref[:, cols] = (xrow[jj] * scale * g_ref[:, cols]).astype(h_ref.dtype)


def proj_out_norm(merged, w_out, x, g, *, tm, tn=512):
    m, d = x.shape
    tm = _row_tile(m, tm)
    assert d % tn == 0
    row_spec = pl.BlockSpec((tm, d), lambda i, j: (i, 0))
    blk_spec = pl.BlockSpec((tm, tn), lambda i, j: (i, j))
    vmem = 2 * (tm * d * 2 + d * tn * 2 + 2 * tm * tn * 4 + tm * d * 2) + tm * d * 4 + 2 * tm * tn * 4
    return pl.pallas_call(
        _proj_out_norm_kernel,
        grid=(m // tm, d // tn),
        in_specs=[row_spec, pl.BlockSpec((d, tn), lambda i, j: (0, j)), blk_spec,
                  pl.BlockSpec((1, d), lambda i, j: (0, 0))],
        out_specs=[blk_spec, row_spec],
        out_shape=[jax.ShapeDtypeStruct((m, d), F32), jax.ShapeDtypeStruct((m, d), BF16)],
        scratch_shapes=[pltpu.VMEM((d // tn, tm, tn), F32)],
        compiler_params=_compiler_params(("parallel", "arbitrary"), vmem),
        name="proj_out_norm",
    )(merged, w_out, x, g.reshape(1, d))


def _proj_call(kernel_fn, h, w, col0, ncols, out_dtype, extras, *, tm=1024, tn=512, name):
    m, k = h.shape
    tm = _row_tile(m, tm)
    tn = min(tn, ncols)
    assert ncols % tn == 0 and col0 % tn == 0, (ncols, col0, tn)
    cb = col0 // tn
    in_specs = [pl.BlockSpec((tm, k), lambda i, j: (i, 0)),
                pl.BlockSpec((k, tn), lambda i, j: (0, j + cb))]
    in_specs += [spec for _, spec in extras]
    extra_bytes = sum(int(np.prod(spec.block_shape)) * a.dtype.itemsize for a, spec in extras)
    vmem = 2 * (tm * k * 2 + k * tn * 2 + tm * tn * 4 + extra_bytes) + tm * tn * 8
    return pl.pallas_call(
        kernel_fn,
        grid=(m // tm, ncols // tn),
        in_specs=in_specs,
        out_specs=pl.BlockSpec((tm, tn), lambda i, j: (i, j)),
        out_shape=jax.ShapeDtypeStruct((m, ncols), out_dtype),
        compiler_params=_compiler_params(("parallel", "arbitrary"), vmem),
        name=name,
    )(h, w, *[a for a, _ in extras])


def _rope_table(pos):
    half = ROT_DIM // 2
    inv_freq = jnp.exp(jnp.arange(half, dtype=F32) * (-2.0 * math.log(ROPE_THETA) / ROT_DIM))
    ang = pos.astype(F32)[:, None] * inv_freq[None, :]
    cos = jnp.cos(ang)
    sin = jnp.sin(ang)
    p = pos.shape[0]
    rest = DIFF_HD - ROT_DIM
    c64 = jnp.concatenate([cos, cos, jnp.ones((p, rest), F32)], axis=1)
    up64 = jnp.concatenate([jnp.zeros((p, half), F32), sin, jnp.zeros((p, rest), F32)], axis=1)
    dn64 = jnp.concatenate([-sin, jnp.zeros((p, half + rest), F32)], axis=1)
    return jnp.concatenate([c64, c64, up64, up64, dn64, dn64], axis=1)


A_QBLK = 4 * CHUNK
A_KBLK = BAND_PAST + A_QBLK
A_BIAS_W = 1024


def _rel_bias_row(tab):
    assert A_KBLK + A_QBLK - 1 <= A_BIAS_W
    lo = BAND_PAST - REL_CLIP
    hi = BAND_PAST + REL_CLIP + 1
    rep = lambda col, n: jnp.repeat(tab[:, col:col + 1], n, axis=1)
    row = jnp.concatenate([rep(0, lo), tab, rep(2 * REL_CLIP, A_KBLK - hi), rep(0, A_BIAS_W - A_KBLK)], axis=1)
    return row[:, None, :]


def _toeplitz_bias(row, rows, width):
    full = pltpu.roll(jnp.broadcast_to(row, (rows, A_BIAS_W)), 0, 1, stride=1, stride_axis=0)
    return full[:, :width]


def _softmax_pv(s, v):
    m = jnp.max(s, axis=-1, keepdims=True)
    p = jnp.exp(s - m)
    l = jnp.sum(p, axis=-1, keepdims=True)
    return _dot(p.astype(BF16), v), l


def _attn_a_prompt_kernel(q_ref, k_ref, v_ref, row_ref, o_ref, kb, vb, *, t):
    kscale = (HD_A ** -0.5) * LOG2E
    kb[...] = (k_ref[0] * kscale).astype(BF16)
    vb[:, :HD_A] = v_ref[0].astype(BF16)
    vb[:, HD_A:] = jnp.ones((t, HD_A), BF16)
    qc = lax.broadcasted_iota(jnp.int32, (A_QBLK, A_KBLK), 0) // CHUNK
    kc = lax.broadcasted_iota(jnp.int32, (A_QBLK, A_KBLK), 1) // CHUNK
    inband = (kc >= qc) & (kc <= qc + BAND_CHUNKS)
    bias = jnp.where(inband, _toeplitz_bias(row_ref[0], A_QBLK, A_KBLK) * LOG2E, NEG_INF)
    for i in range(t // A_QBLK):
        r0 = i * A_QBLK
        k0 = max(r0 - BAND_PAST, 0)
        k1 = r0 + A_QBLK
        q = q_ref[0, r0:k1, :]
        s = _dot_nt(q, kb[k0:k1, :]) + bias[:, A_KBLK - (k1 - k0):]
        p = jnp.exp2(s - jnp.max(s, axis=-1, keepdims=True))
        o = _dot(p.astype(BF16), vb[k0:k1, :])
        o_ref[0, r0:k1, :] = (o[:, :HD_A] / o[:, HD_A:]).astype(o_ref.dtype)


def attn_a_prompt(qa, ka, va, bias_row, b, t):
    h = qa.shape[1] // HD_A
    assert t % A_QBLK == 0 and A_QBLK % V7X_LANES == 0
    q3, k3, v3 = (a.reshape(b, t, h * HD_A) for a in (qa, ka, va))
    spec = pl.BlockSpec((1, t, HD_A), lambda bi, hi: (bi, 0, hi))
    vmem = 2 * t * HD_A * (2 + 4 + 4 + 2) + 2 * t * HD_A * 2 + 24 * MIB
    out = pl.pallas_call(
        functools.partial(_attn_a_prompt_kernel, t=t),
        grid=(b, h),
        in_specs=[spec, spec, spec, pl.BlockSpec((1, 1, A_BIAS_W), lambda bi, hi: (hi, 0, 0))],
        out_specs=spec,
        out_shape=jax.ShapeDtypeStruct((b, t, h * HD_A), BF16),
        scratch_shapes=[pltpu.VMEM((t, HD_A), BF16), pltpu.VMEM((t, 2 * HD_A), BF16)],
        compiler_params=_compiler_params(("parallel", "parallel"), vmem),
        name="attn_a_prompt",
    )(q3, k3, v3, bias_row)
    return out.reshape(b * t, h * HD_A)


def _attn_a_sample_kernel(q_ref, kn_ref, vn_ref, row_ref, mask_ref, kc_hbm, vc_hbm, o_ref, kbuf, vbuf, sem,
                          *, layer, heads, a_len):
    b = pl.program_id(0)
    nb = pl.num_programs(0)
    scale = HD_A ** -0.5
    t = q_ref.shape[2]

    def copies(bi, slot):
        out = []
        for h in range(heads):
            out.append(pltpu.make_async_copy(kc_hbm.at[layer, bi, :, h, :], kbuf.at[slot, h], sem.at[0, slot, h]))
            out.append(pltpu.make_async_copy(vc_hbm.at[layer, bi, :, h, :], vbuf.at[slot, h], sem.at[1, slot, h]))
        return out

    @pl.when(b == 0)
    def _():
        for c in copies(b, 0):
            c.start()

    slot = b % 2

    @pl.when(b + 1 < nb)
    def _():
        for c in copies(b + 1, 1 - slot):
            c.start()

    for c in copies(b, slot):
        c.wait()
    for h in range(heads):
        q = q_ref[0, h]
        bias = _toeplitz_bias(row_ref[h], t, a_len + t) + mask_ref[...]
        sc = _dot_nt(q, kbuf[slot, h].astype(BF16)) * scale + bias[:, :a_len]
        sn = _dot_nt(q, kn_ref[0, h].astype(BF16)) * scale + bias[:, a_len:]
        m = jnp.maximum(jnp.max(sc, axis=-1, keepdims=True), jnp.max(sn, axis=-1, keepdims=True))
        pc = jnp.exp(sc - m)
        pn = jnp.exp(sn - m)
        l = jnp.sum(pc, axis=-1, keepdims=True) + jnp.sum(pn, axis=-1, keepdims=True)
        o = _dot(pc.astype(BF16), vbuf[slot, h].astype(BF16)) + _dot(pn.astype(BF16), vn_ref[0, h].astype(BF16))
        o_ref[0, h] = (o / l).astype(o_ref.dtype)


def _by_head(a, b, t, heads, hd):
    return a.reshape(b, t, heads, hd).transpose(0, 2, 1, 3)


def attn_a_sample(qa, ka, va, cache_k, cache_v, layer, bias_row, mask, b, t):
    heads = qa.shape[1] // HD_A
    a_len = cache_k.shape[2]
    assert a_len == BAND_PAST and a_len + t <= A_KBLK
    head_spec = pl.BlockSpec((1, heads, t, HD_A), lambda bi: (bi, 0, 0, 0))
    vmem = 2 * 2 * heads * a_len * HD_A * 4 + 2 * 4 * heads * t * HD_A * 4 + 16 * MIB
    out = pl.pallas_call(
        functools.partial(_attn_a_sample_kernel, layer=layer, heads=heads, a_len=a_len),
        grid=(b,),
        in_specs=[head_spec, head_spec, head_spec,
                  pl.BlockSpec(bias_row.shape, lambda bi: (0, 0, 0)),
                  pl.BlockSpec(mask.shape, lambda bi: (0, 0)),
                  pl.BlockSpec(memory_space=pl.ANY), pl.BlockSpec(memory_space=pl.ANY)],
        out_specs=head_spec,
        out_shape=jax.ShapeDtypeStruct((b, heads, t, HD_A), BF16),
        scratch_shapes=[pltpu.VMEM((2, heads, a_len, HD_A), F32), pltpu.VMEM((2, heads, a_len, HD_A), F32),
                        pltpu.SemaphoreType.DMA((2, 2, heads))],
        compiler_params=_compiler_params(("arbitrary",), vmem),
        name="attn_a_sample",
    )(*(_by_head(a, b, t, heads, HD_A) for a in (qa, ka, va)), bias_row, mask, cache_k, cache_v)
    return out.transpose(0, 2, 1, 3).reshape(b * t, heads * HD_A)


def _diff_lambda(lam_ref, lam_init):
    v = lam_ref[...]
    d1 = jnp.sum(v[0:1] * v[1:2], axis=-1, keepdims=True)
    d2 = jnp.sum(v[2:3] * v[3:4], axis=-1, keepdims=True)
    return jnp.exp(d1) - jnp.exp(d2) + lam_init


def _split_diff_queries(q):
    lane = lax.broadcasted_iota(jnp.int32, q.shape, 1)
    qs = q * jnp.asarray(DIFF_HD ** -0.5, q.dtype)
    zero = jnp.zeros_like(qs)
    return jnp.where(lane < DIFF_HD, qs, zero), jnp.where(lane >= DIFF_HD, qs, zero)


def _stack_diff_queries(q):
    return jnp.concatenate(_split_diff_queries(q), axis=0)


def _diff_post(o, g, post_scale):
    ms = jnp.mean(o * o, axis=-1, keepdims=True)
    return (o * lax.rsqrt(ms + EPS) * g) * post_scale


def _diff_finish(l, acc, lam, g, post_scale, tq):
    o = acc[:tq] / l[:tq] - lam * (acc[tq:] / l[tq:])
    return _diff_post(o, g, post_scale)


B_TQ = 8 * CHUNK


def _online_step(carry, s, v_ext):
    m, acc = carry
    m_new = jnp.maximum(m, jnp.max(s, axis=-1, keepdims=True))
    alpha = jnp.exp2(m - m_new)
    p = jnp.exp2(s - m_new)
    acc = alpha * acc + _dot(p.astype(BF16), v_ext)
    return m_new, acc


def _attn_b_prompt_kernel(lam_ref, q_ref, k_ref, v_ref, g_ref, o_ref, kb, vb, *, t, lam_init):
    tq = B_TQ
    hd = 2 * DIFF_HD
    kb[...] = (k_ref[0] * LOG2E).astype(BF16)
    vb[:, :hd] = v_ref[0].astype(BF16)
    vb[:, hd:] = jnp.ones((t, hd), BF16)
    lam = _diff_lambda(lam_ref, lam_init)
    row = lax.broadcasted_iota(jnp.int32, (tq, tq), 0)
    col = lax.broadcasted_iota(jnp.int32, (tq, tq), 1)
    diag_ok = (col // CHUNK) <= (row // CHUNK)
    for qi in range(t // tq):
        q0 = qi * tq
        qs = _split_diff_queries(q_ref[0, q0:q0 + tq, :])
        spans = ([(0, q0, False)] if q0 else []) + [(q0, tq, True)]
        outs = []
        for c in range(2):
            carry = (jnp.full((tq, 1), NEG_INF, F32), jnp.zeros((tq, 2 * hd), F32))
            for k0, width, masked in spans:
                s = _dot_nt(qs[c], kb[k0:k0 + width, :])
                if masked:
                    s = jnp.where(diag_ok, s, NEG_INF)
                carry = _online_step(carry, s, vb[k0:k0 + width, :])
            outs.append(carry[1][:, :hd] / carry[1][:, hd:])
        o = outs[0] - lam * outs[1]
        o_ref[0, q0:q0 + tq, :] = _diff_post(o, g_ref[...], 1.0 - lam_init).astype(o_ref.dtype)


def attn_b_prompt(qb, kb, vb, lam_params, subln_g, lam_init, b, t):
    hd = 2 * DIFF_HD
    heads = qb.shape[1] // hd
    assert t % B_TQ == 0
    q3, k3, v3 = (a.reshape(b, t, heads * hd) for a in (qb, kb, vb))
    spec = pl.BlockSpec((1, t, hd), lambda bi, hi: (bi, 0, hi))
    vmem = 2 * t * hd * (2 + 4 + 4 + 2) + 2 * t * hd * 2 + 32 * MIB
    out = pl.pallas_call(
        functools.partial(_attn_b_prompt_kernel, t=t, lam_init=lam_init),
        grid=(b, heads),
        in_specs=[pl.BlockSpec(lam_params.shape, lambda bi, hi: (0, 0)), spec, spec, spec,
                  pl.BlockSpec((1, hd), lambda bi, hi: (0, 0))],
        out_specs=spec,
        out_shape=jax.ShapeDtypeStruct((b, t, heads * hd), BF16),
        scratch_shapes=[pltpu.VMEM((t, hd), BF16), pltpu.VMEM((t, 2 * hd), BF16)],
        compiler_params=_compiler_params(("parallel", "parallel"), vmem),
        name="attn_b_prompt",
    )(lam_params, q3, k3, v3, subln_g.reshape(1, hd))
    return out.reshape(b * t, heads * hd)


def _attn_b_sample_kernel(lam_ref, q_ref, kn_ref, vn_ref, g_ref, kc_hbm, vc_hbm, o_ref, kbuf, vbuf, sem,
                          *, layer, heads, t, lam_init):
    b = pl.program_id(0)
    nb = pl.num_programs(0)

    def copies(bi, h, slot):
        return (pltpu.make_async_copy(kc_hbm.at[layer, bi, :, h, :], kbuf.at[slot], sem.at[0, slot]),
                pltpu.make_async_copy(vc_hbm.at[layer, bi, :, h, :], vbuf.at[slot], sem.at[1, slot]))

    def start(bi, h, slot):
        for c in copies(bi, h, slot):
            c.start()

    @pl.when(b == 0)
    def _():
        start(b, 0, 0)

    lam = _diff_lambda(lam_ref, lam_init)
    for h in range(heads):
        slot = h % 2
        if h + 1 < heads:
            start(b, h + 1, 1 - slot)
        else:
            @pl.when(b + 1 < nb)
            def _():
                start(b + 1, 0, 1 - slot)
        for c in copies(b, h, slot):
            c.wait()
        q2 = _stack_diff_queries(q_ref[0, h])
        sc = _dot_nt(q2, kbuf[slot].astype(BF16))
        sn = _dot_nt(q2, kn_ref[0, h].astype(BF16))
        m = jnp.maximum(jnp.max(sc, axis=-1, keepdims=True), jnp.max(sn, axis=-1, keepdims=True))
        pc = jnp.exp(sc - m)
        pn = jnp.exp(sn - m)
        l = jnp.sum(pc, axis=-1, keepdims=True) + jnp.sum(pn, axis=-1, keepdims=True)
        acc = _dot(pc.astype(BF16), vbuf[slot].astype(BF16)) + _dot(pn.astype(BF16), vn_ref[0, h].astype(BF16))
        o_ref[0, h] = _diff_finish(l, acc, lam, g_ref[...], 1.0 - lam_init, t).astype(o_ref.dtype)


def attn_b_sample(qb, kb, vb, cache_k, cache_v, layer, lam_params, subln_g, lam_init, b, t):
    hd = 2 * DIFF_HD
    heads = qb.shape[1] // hd
    assert heads % 2 == 0
    past = cache_k.shape[2]
    head_spec = pl.BlockSpec((1, heads, t, hd), lambda bi: (bi, 0, 0, 0))
    vmem = 2 * 2 * past * hd * 4 + 2 * 3 * heads * t * hd * 4 + 12 * 2 * t * past * 4
    out = pl.pallas_call(
        functools.partial(_attn_b_sample_kernel, layer=layer, heads=heads, t=t, lam_init=lam_init),
        grid=(b,),
        in_specs=[pl.BlockSpec(lam_params.shape, lambda bi: (0, 0)), head_spec, head_spec, head_spec,
                  pl.BlockSpec((1, hd), lambda bi: (0, 0)),
                  pl.BlockSpec(memory_space=pl.ANY), pl.BlockSpec(memory_space=pl.ANY)],
        out_specs=head_spec,
        out_shape=jax.ShapeDtypeStruct((b, heads, t, hd), BF16),
        scratch_shapes=[pltpu.VMEM((2, past, hd), F32), pltpu.VMEM((2, past, hd), F32),
                        pltpu.SemaphoreType.DMA((2, 2))],
        compiler_params=_compiler_params(("arbitrary",), vmem),
        name="attn_b_sample",
    )(lam_params, *(_by_head(a, b, t, heads, hd) for a in (qb, kb, vb)), subln_g.reshape(1, hd), cache_k, cache_v)
    return out.transpose(0, 2, 1, 3).reshape(b * t, heads * hd)


def _attn_m_kernel(q_ref, k_ref, v_ref, o_ref, *, heads, hd):
    scale = hd ** -0.5
    for h in range(heads):
        sl = slice(h * hd, (h + 1) * hd)
        s = _dot_nt(q_ref[0, :, sl], k_ref[0, :, sl].astype(BF16)) * scale
        o, l = _softmax_pv(s, v_ref[0, :, sl].astype(BF16))
        o_ref[0, :, sl] = (o / l).astype(o_ref.dtype)


def attn_m_prompt(qm, mem_k, mem_v, b, t, *, tq=512):
    width = qm.shape[1]
    hd = width // H_M
    tq = min(tq, t)
    assert t % tq == 0
    n = mem_k.shape[0] // b
    q_spec = pl.BlockSpec((1, tq, width), lambda bi, qi: (bi, qi, 0))
    kv_spec = pl.BlockSpec((1, n, width), lambda bi, qi: (bi, 0, 0))
    vmem = 2 * (2 * n * width * 4 + 2 * tq * width * 2) + 8 * tq * n * 4
    out = pl.pallas_call(
        functools.partial(_attn_m_kernel, heads=H_M, hd=hd),
        grid=(b, t // tq),
        in_specs=[q_spec, kv_spec, kv_spec],
        out_specs=q_spec,
        out_shape=jax.ShapeDtypeStruct((b, t, width), BF16),
        compiler_params=_compiler_params(("parallel", "arbitrary"), vmem),
        name="attn_m_prompt",
    )(qm.reshape(b, t, width), mem_k.reshape(b, n, width), mem_v.reshape(b, n, width))
    return out.reshape(b * t, width)


def _attn_m_sample_kernel(q_ref, kc_hbm, vc_hbm, o_ref, kbuf, vbuf, sem, *, layer, heads, hd):
    b = pl.program_id(0)
    nb = pl.num_programs(0)
    scale = hd ** -0.5

    def copies(bi, slot):
        out = []
        for h in range(heads):
            out.append(pltpu.make_async_copy(kc_hbm.at[layer, bi, :, h, :], kbuf.at[slot, h], sem.at[0, slot, h]))
            out.append(pltpu.make_async_copy(vc_hbm.at[layer, bi, :, h, :], vbuf.at[slot, h], sem.at[1, slot, h]))
        return out

    @pl.when(b == 0)
    def _():
        for c in copies(b, 0):
            c.start()

    slot = b % 2

    @pl.when(b + 1 < nb)
    def _():
        for c in copies(b + 1, 1 - slot):
            c.start()

    for c in copies(b, slot):
        c.wait()
    for h in range(heads):
        s = _dot_nt(q_ref[0, h], kbuf[slot, h].astype(BF16)) * scale
        o, l = _softmax_pv(s, vbuf[slot, h].astype(BF16))
        o_ref[0, h] = (o / l).astype(o_ref.dtype)


def attn_m_sample(qm, cache_k, cache_v, layer, b, t):
    n, heads, hd = cache_k.shape[2:]
    head_spec = pl.BlockSpec((1, heads, t, hd), lambda bi: (bi, 0, 0, 0))
    vmem = 2 * 2 * heads * n * hd * 4 + 2 * 2 * heads * t * hd * 2 + 8 * MIB
    out = pl.pallas_call(
        functools.partial(_attn_m_sample_kernel, layer=layer, heads=heads, hd=hd),
        grid=(b,),
        in_specs=[head_spec, pl.BlockSpec(memory_space=pl.ANY), pl.BlockSpec(memory_space=pl.ANY)],
        out_specs=head_spec,
        out_shape=jax.ShapeDtypeStruct((b, heads, t, hd), BF16),
        scratch_shapes=[pltpu.VMEM((2, heads, n, hd), F32), pltpu.VMEM((2, heads, n, hd), F32),
                        pltpu.SemaphoreType.DMA((2, 2, heads))],
        compiler_params=_compiler_params(("arbitrary",), vmem),
        name="attn_m_sample",
    )(_by_head(qm, b, t, heads, hd), cache_k, cache_v)
    return out.transpose(0, 2, 1, 3).reshape(b * t, heads * hd)


def _merge_kernel(h_ref, oa_ref, ob_ref, om_ref, wga_ref, wgb_ref, wgm_ref, gb_ref, wbr_ref, o_ref):
    h = h_ref[...]
    acc = None
    for n, (o_n, wg_n) in enumerate(((oa_ref, wga_ref), (ob_ref, wgb_ref), (om_ref, wgm_ref))):
        gate = jax.nn.sigmoid(_dot(h, wg_n[...]) + gb_ref[n])
        term = gate * _dot(o_n[...], wbr_ref[n])
        acc = term if acc is None else acc + term
    o_ref[...] = acc.astype(o_ref.dtype)


def merge_branches(h, oa, ob, om, w_in, gate_col0, gate_b, w_br, *, tm, tn=256):
    m, bw = oa.shape
    d = w_br.shape[2]
    tm = _row_tile(m, tm)
    assert d % tn == 0 and gate_col0 % tn == 0
    nj = d // tn
    g0 = gate_col0 // tn
    row_spec = lambda width: pl.BlockSpec((tm, width), lambda i, j: (i, 0))
    gate_w_specs = [pl.BlockSpec((d, tn), functools.partial(lambda i, j, n: (0, g0 + n * nj + j), n=n))
                    for n in range(N_BRANCH)]
    vmem = 2 * (tm * d * 2 + 3 * tm * bw * 2 + 3 * d * tn * 2 + 3 * bw * tn * 2 + tm * tn * 2) + 6 * tm * tn * 4
    return pl.pallas_call(
        _merge_kernel,
        grid=(m // tm, nj),
        in_specs=[row_spec(d), row_spec(bw), row_spec(bw), row_spec(bw)] + gate_w_specs
                 + [pl.BlockSpec((N_BRANCH, 1, tn), lambda i, j: (0, 0, j)),
                    pl.BlockSpec((N_BRANCH, bw, tn), lambda i, j: (0, 0, j))],
        out_specs=pl.BlockSpec((tm, tn), lambda i, j: (i, j)),
        out_shape=jax.ShapeDtypeStruct((m, d), BF16),
        compiler_params=_compiler_params(("parallel", "arbitrary"), vmem),
        name="merge_branches",
    )(h, oa, ob, om, w_in, w_in, w_in, gate_b, w_br)


def _ffn_up_kernel(h_ref, wa_ref, wb_ref, cw_ref, cb_ref, st_ref, g_ref, cn_ref, carry, *, nb, tb, blocks_per_seq):
    i = pl.program_id(0)
    j = pl.program_id(1)
    tm, tn = g_ref.shape
    if nb == 1:
        @pl.when((i % blocks_per_seq) == 0)
        def _():
            carry[j] = st_ref[0]

        trow = lax.broadcasted_iota(jnp.int32, (tm, PROJ_SLAB), 0)
    else:
        trow = lax.broadcasted_iota(jnp.int32, (nb, tb, PROJ_SLAB), 1).reshape(tm, PROJ_SLAB)
    h = h_ref[...]
    slabs = [slice(c0, c0 + PROJ_SLAB) for c0 in range(0, tn, PROJ_SLAB)]
    dots = [(_dot(h, wa_ref[:, cols]), _dot(h, wb_ref[:, cols])) for cols in slabs]
    for cols, (a, bgate) in zip(slabs, dots):
        if nb == 1:
            prev = carry[j, :, cols]
            p0 = jnp.broadcast_to(prev[0:1], (tm, PROJ_SLAB))
            p1 = jnp.broadcast_to(prev[1:2], (tm, PROJ_SLAB))
            carry[j, :, cols] = a[tm - 2:tm]
            cn_ref[0, :, cols] = a[tm - 2:tm]
        else:
            st = st_ref[:, :, cols]
            p0 = jnp.broadcast_to(st[:, 0:1, :], (nb, tb, PROJ_SLAB)).reshape(tm, PROJ_SLAB)
            p1 = jnp.broadcast_to(st[:, 1:2, :], (nb, tb, PROJ_SLAB)).reshape(tm, PROJ_SLAB)
            cn_ref[:, :, cols] = a.reshape(nb, tb, PROJ_SLAB)[:, tb - 2:tb, :]
        am1 = jnp.where(trow == 0, p1, pltpu.roll(a, 1, 0))
        am2 = jnp.where(trow == 0, p0, jnp.where(trow == 1, p1, pltpu.roll(a, 2, 0)))
        cw = cw_ref[:, cols]
        c = cb_ref[:, cols] + am2 * cw[0:1] + am1 * cw[1:2] + a * cw[2:3]
        gelu = 0.5 * c * (1.0 + lax.erf(c * (2.0 ** -0.5)))
        g_ref[:, cols] = (gelu * bgate).astype(g_ref.dtype)


def ffn_up(h, w_a, w_b, conv_w, conv_b, state, b, t, *, tm=1024, tn=512):
    m, d = h.shape
    f = w_a.shape[1]
    tm = _row_tile(m, tm)
    assert f % tn == 0 and tn % PROJ_SLAB == 0
    if tm >= t:
        assert tm % t == 0
        nb, tb, blocks_per_seq = tm // t, t, 1
    else:
        assert t % tm == 0
        nb, tb, blocks_per_seq = 1, tm, t // tm
    if nb > 1:
        seq_map = lambda i, j: (i, 0, j)
    else:
        seq_map = lambda i, j: (i // blocks_per_seq, 0, j)
    tail_map = lambda i, j: (i, 0, j)
    w_spec = pl.BlockSpec((d, tn), lambda i, j: (0, j))
    vmem = 2 * (tm * d * 2 + 2 * d * tn * 2 + tm * tn * 2 + 2 * nb * 8 * tn * 4) + 8 * tm * tn * 4
    g, conv_new = pl.pallas_call(
        functools.partial(_ffn_up_kernel, nb=nb, tb=tb, blocks_per_seq=blocks_per_seq),
        grid=(m // tm, f // tn),
        in_specs=[pl.BlockSpec((tm, d), lambda i, j: (i, 0)), w_spec, w_spec,
                  pl.BlockSpec((CONV_W, tn), lambda i, j: (0, j)),
                  pl.BlockSpec((1, tn), lambda i, j: (0, j)),
                  pl.BlockSpec((nb, CONV_W - 1, tn), seq_map)],
        out_specs=[pl.BlockSpec((tm, tn), lambda i, j: (i, j)),
                   pl.BlockSpec((nb, CONV_W - 1, tn), tail_map)],
        out_shape=[jax.ShapeDtypeStruct((m, f), BF16),
                   jax.ShapeDtypeStruct((b * blocks_per_seq, CONV_W - 1, f), F32)],
        scratch_shapes=[pltpu.VMEM((f // tn, CONV_W - 1, tn), F32)],
        compiler_params=_compiler_params(("arbitrary", "arbitrary"), vmem),
        name="ffn_up",
    )(h, w_a, w_b, conv_w, conv_b, state)
    return g, conv_new.reshape(b, blocks_per_seq, CONV_W - 1, f)[:, -1]


def _pad_cols(a, f_pad):
    return jnp.pad(a, [(0, 0)] * (a.ndim - 1) + [(0, f_pad - a.shape[-1])])


def _cast_kernel(w_ref, o_ref, *, valid_rows, valid_cols):
    rb, cb = o_ref.shape
    rows = pl.program_id(0) * rb + lax.broadcasted_iota(jnp.int32, (rb, cb), 0)
    cols = pl.program_id(1) * cb + lax.broadcasted_iota(jnp.int32, (rb, cb), 1)
    ok = (rows < valid_rows) & (cols < valid_cols)
    o_ref[...] = jnp.where(ok, w_ref[...], 0.0).astype(o_ref.dtype)


def cast_weight(w, layer, *, rb, cb, col0=0, ncols=None, out_rows=None, out_cols=None):
    _, r, c = w.shape
    ncols = c - col0 if ncols is None else ncols
    out_rows = r if out_rows is None else out_rows
    out_cols = ncols if out_cols is None else out_cols
    assert col0 % cb == 0 and out_rows % rb == 0 and out_cols % cb == 0
    c0 = col0 // cb
    last_r = (r - 1) // rb
    last_c = (col0 + ncols - 1) // cb
    in_map = lambda i, j: (layer, jnp.minimum(i, last_r), jnp.minimum(j + c0, last_c))
    return pl.pallas_call(
        functools.partial(_cast_kernel, valid_rows=r, valid_cols=ncols),
        grid=(out_rows // rb, out_cols // cb),
        in_specs=[pl.BlockSpec((None, rb, cb), in_map)],
        out_specs=pl.BlockSpec((rb, cb), lambda i, j: (i, j)),
        out_shape=jax.ShapeDtypeStruct((out_rows, out_cols), BF16),
        compiler_params=_compiler_params(("parallel", "parallel"), 2 * rb * cb * 6),
        name="cast_weight",
    )(w)


def _cast_halves_kernel(wa_ref, wb_ref, oa_ref, ob_ref, *, valid_cols):
    rb, cb = oa_ref.shape
    cols = pl.program_id(0) * cb + lax.broadcasted_iota(jnp.int32, (rb, cb), 1)
    ok = cols < valid_cols
    oa_ref[...] = jnp.where(ok, wa_ref[...], 0.0).astype(oa_ref.dtype)
    ob_ref[...] = jnp.where(ok, wb_ref[...], 0.0).astype(ob_ref.dtype)


def cast_weight_halves(w, layer, out_cols, *, cb=V7X_LANES):
    _, r, c = w.shape
    half = c // 2
    assert half % cb == 0 and out_cols % cb == 0
    nb = half // cb
    in_spec = lambda first: pl.BlockSpec((None, r, cb), lambda j: (layer, 0, first + jnp.minimum(j, nb - 1)))
    out_spec = pl.BlockSpec((r, cb), lambda j: (0, j))
    out = jax.ShapeDtypeStruct((r, out_cols), BF16)
    return pl.pallas_call(
        functools.partial(_cast_halves_kernel, valid_cols=half),
        grid=(out_cols // cb,),
        in_specs=[in_spec(0), in_spec(nb)],
        out_specs=[out_spec, out_spec],
        out_shape=[out, out],
        compiler_params=_compiler_params(("parallel",), 2 * 2 * r * cb * 6),
        name="cast_weight_halves",
    )(w, w)


def _layer_weights(l, P):
    depth, d, _ = P['w_in'].shape
    d_ff = P['w_ffn_down'].shape[1]
    f_pad = -(-d_ff // 512) * 512
    bw = d // 2
    assert d_ff % V7X_LANES == 0
    w_br = P['w_branch'].reshape(depth, N_BRANCH * bw, d)
    w_up_a, w_up_b = cast_weight_halves(P['w_ffn_up'], l, f_pad)
    return {
        'w_up_a': w_up_a,
        'w_up_b': w_up_b,
        'w_in': cast_weight(P['w_in'], l, rb=d, cb=512),
        'w_mem_kv': cast_weight(P['w_mem_kv'], l, rb=d, cb=512),
        'w_branch': cast_weight(w_br, l, rb=bw, cb=d).reshape(N_BRANCH, bw, d),
        'w_out': cast_weight(P['w_out'], l, rb=d, cb=512),
        'w_down': cast_weight(P['w_ffn_down'], l, rb=512, cb=d, out_rows=f_pad),
        'conv_w': _pad_cols(P['ffn_conv_w'][l], f_pad),
        'conv_b': _pad_cols(P['ffn_conv_b'][l].reshape(1, d_ff), f_pad),
        'f_pad': f_pad,
        'd_ff': d_ff,
    }


def _mixer_inputs(h, W, P, l, rope_tab, tm, final=None):
    gains = (P['a_q_norm_g'][l], P['a_k_norm_g'][l], P['b_q_norm_g'][l], P['b_k_norm_g'][l], P['m_q_norm_g'][l])
    return proj_in(h, W['w_in'], gains, rope_tab, tm=tm, final=final)


def _finish_layer(x, h, outs, W, P, l, state, b, t, tm):
    d = x.shape[1]
    gate_b = P['gate_b'][l].reshape(N_BRANCH, 1, d)
    merged = merge_branches(h, *outs, W['w_in'], 7 * (d // 2), gate_b, W['w_branch'], tm=tm)
    res_spec = lambda tn: pl.BlockSpec((tm, tn), lambda i, j: (i, j))
    x, h = proj_out_norm(merged, W['w_out'], x, P['norm_ffn_g'][l], tm=tm)
    g, conv_new = ffn_up(h, W['w_up_a'], W['w_up_b'], W['conv_w'], W['conv_b'], state, b, t, tm=tm)
    x = _proj_call(_proj_residual_kernel, g, W['w_down'], 0, d, F32, [(x, res_spec(512))], tm=tm, name="ffn_down")
    return x, conv_new[:, :, :W['d_ff']]


def kernel(x_prompt, x_sample, cache_a_k, cache_a_v, cache_b_k, cache_b_v, cache_mem_k, cache_mem_v, state_ffn_conv, mem_prompt, norm_mix_g, w_in, a_q_norm_g, a_k_norm_g, a_rel_bias, b_q_norm_g, b_k_norm_g, b_lam_q1, b_lam_k1, b_lam_q2, b_lam_k2, b_subln_g, m_q_norm_g, m_k_norm_g, mem_norm_g, w_mem_kv, gate_b, w_branch, w_out, norm_ffn_g, w_ffn_up, ffn_conv_w, ffn_conv_b, w_ffn_down):
    P = {'w_in': w_in, 'a_q_norm_g': a_q_norm_g, 'a_k_norm_g': a_k_norm_g, 'b_q_norm_g': b_q_norm_g,
         'b_k_norm_g': b_k_norm_g, 'm_q_norm_g': m_q_norm_g, 'm_k_norm_g': m_k_norm_g, 'w_mem_kv': w_mem_kv,
         'gate_b': gate_b, 'w_branch': w_branch, 'w_out': w_out, 'norm_ffn_g': norm_ffn_g,
         'w_ffn_up': w_ffn_up, 'ffn_conv_w': ffn_conv_w, 'ffn_conv_b': ffn_conv_b, 'w_ffn_down': w_ffn_down}
    bp, tp, d = x_prompt.shape
    bs, ts, _ = x_sample.shape
    depth = w_in.shape[0]
    bw = d // 2
    past = cache_b_k.shape[2]
    a_len = cache_a_k.shape[2]
    n_mem = mem_prompt.shape[1]
    a_keep = min(BAND_PAST, tp)
    h_a = bw // HD_A
    h_b = bw // (2 * DIFF_HD)
    hd_m = bw // H_M
    mp, ms = bp * tp, bs * ts
    tm_p = _row_tile(mp, 1024)
    tm_s = _row_tile(ms, 1024)
    assert tm_p <= tp and tp % tm_p == 0 or tm_p % tp == 0

    pos_s = past + np.arange(ts)
    key_pos_a = np.concatenate([past - a_len + np.arange(a_len), pos_s])
    q_chunk_s = pos_s // CHUNK
    k_chunk_a = key_pos_a // CHUNK
    valid_a_s = (k_chunk_a[None, :] <= q_chunk_s[:, None]) & (k_chunk_a[None, :] >= q_chunk_s[:, None] - BAND_CHUNKS)
    mask_a_s = jnp.asarray(np.where(valid_a_s, 0.0, NEG_INF), F32)
    key_pos_b = np.concatenate([np.arange(past), pos_s])
    valid_b_s = (key_pos_b // CHUNK)[None, :] <= q_chunk_s[:, None]
    assert valid_b_s.all(), "sample queries are expected to see every cached and new differential key"

    rope_p = _rope_table(jnp.arange(max(tp, tm_p), dtype=jnp.int32) % tp)
    rope_s = _rope_table(past + (jnp.arange(max(ts, tm_s), dtype=jnp.int32) % ts))

    xp = x_prompt.reshape(mp, d)
    xs = x_sample.reshape(ms, d)
    mem2d = mem_prompt.reshape(bp * n_mem, d)
    outs = {k: [] for k in ('mk_p', 'mv_p', 'cv_p', 'ak_s', 'av_s', 'bk_s', 'bv_s', 'cv_s')}
    kv_prompt = []
    for l in range(depth):
        W = _layer_weights(l, P)
        lam_init = 0.8 - 0.6 * math.exp(-0.3 * l)
        lam_params = jnp.stack([b_lam_q1[l], b_lam_k1[l], b_lam_q2[l], b_lam_k2[l]]).astype(F32)
        bias_row = _rel_bias_row(a_rel_bias[l])

        h = rmsnorm_cast(xp, norm_mix_g[l])
        final = dict(batch=bp, seq_len=tp, a_keep=a_keep, layer=l, depth=depth, prev=kv_prompt)
        qa, ka, va, qb, kb, vb, qm, *kv_final = _mixer_inputs(h, W, P, l, rope_p, tm_p, final)
        kv_prompt.append(kv_final)
        oa = attn_a_prompt(qa, ka, va, bias_row, bp, tp)
        ob = attn_b_prompt(qb, kb, vb, lam_params, b_subln_g[l], lam_init, bp, tp)
        hm = rmsnorm_cast(mem2d, mem_norm_g[l])
        tm_m = _row_tile(bp * n_mem, 1024)
        mk = _proj_call(functools.partial(_proj_headnorm_kernel, hd=hd_m), hm, W['w_mem_kv'], 0, bw, F32,
                        [(m_k_norm_g[l].reshape(1, hd_m), pl.BlockSpec((1, hd_m), lambda i, j: (0, 0)))],
                        tm=tm_m, name="proj_mk")
        mv = _proj_call(_proj_plain_kernel, hm, W['w_mem_kv'], bw, bw, F32, [], tm=tm_m, name="proj_mv")
        om = attn_m_prompt(qm, mk, mv, bp, tp)
        zeros_state = jnp.zeros((bp, CONV_W - 1, W['f_pad']), F32)
        xp, conv_new = _finish_layer(xp, h, (oa, ob, om), W, P, l, zeros_state, bp, tp, tm_p)
        outs['mk_p'].append(mk.reshape(bp, n_mem, H_M, hd_m))
        outs['mv_p'].append(mv.reshape(bp, n_mem, H_M, hd_m))
        outs['cv_p'].append(conv_new)

        h = rmsnorm_cast(xs, norm_mix_g[l])
        qa, ka, va, qb, kb, vb, qm = _mixer_inputs(h, W, P, l, rope_s, tm_s)
        oa = attn_a_sample(qa, ka, va, cache_a_k, cache_a_v, l, bias_row, mask_a_s, bs, ts)
        ob = attn_b_sample(qb, kb, vb, cache_b_k, cache_b_v, l, lam_params, b_subln_g[l], lam_init, bs, ts)
        om = attn_m_sample(qm, cache_mem_k, cache_mem_v, l, bs, ts)
        state = _pad_cols(state_ffn_conv[l], W['f_pad'])
        xs, conv_new = _finish_layer(xs, h, (oa, ob, om), W, P, l, state, bs, ts, tm_s)
        outs['ak_s'].append(ka.reshape(bs, ts, h_a, HD_A))
        outs['av_s'].append(va.reshape(bs, ts, h_a, HD_A))
        outs['bk_s'].append(kb.reshape(bs, ts, h_b, 2 * DIFF_HD))
        outs['bv_s'].append(vb.reshape(bs, ts, h_b, 2 * DIFF_HD))
        outs['cv_s'].append(conv_new)

    stack = lambda k: jnp.stack(outs[k])
    ak_p, av_p, bk_p, bv_p = kv_prompt[-1]
    return (xp.reshape(bp, tp, d), xs.reshape(bs, ts, d),
            ak_p, av_p, bk_p, bv_p, stack('mk_p'), stack('mv_p'), stack('cv_p'),
            stack('ak_s'), stack('av_s'), stack('bk_s'), stack('bv_s'), stack('cv_s'))
```

```python
import functools
import math

import numpy as np
import jax
import jax.numpy as jnp
from jax import lax
from jax.experimental import pallas as pl
from jax.experimental.pallas import tpu as pltpu

F32 = jnp.float32
BF16 = jnp.bfloat16

CHUNK = 64
BAND_CHUNKS = 8
BAND_PAST = BAND_CHUNKS * CHUNK
REL_CLIP = 128
HD_A = 128
DIFF_HD = 64
ROT_DIM = DIFF_HD // 4
ROPE_THETA = 500000.0
H_M = 4
N_BRANCH = 3
CONV_W = 3
EPS = 1e-6
NEG_INF = -1e30
LOG2E = math.log2(math.e)

V7X_LANES = 128
V7X_VMEM_BYTES = 64 * 1024 * 1024
MIB = 1024 * 1024


def _compiler_params(semantics, vmem_estimate_bytes):
    limit = min(int(vmem_estimate_bytes * 1.25) + 8 * MIB, V7X_VMEM_BYTES - 4 * MIB)
    return pltpu.CompilerParams(dimension_semantics=semantics, vmem_limit_bytes=limit)


def _row_tile(m, target):
    t = min(m, target)
    assert m % t == 0, (m, t)
    return t


def _rmsnorm_kernel(x_ref, g_ref, o_ref):
    x = x_ref[...]
    ms = jnp.mean(x * x, axis=-1, keepdims=True)
    o_ref[...] = (x * lax.rsqrt(ms + EPS) * g_ref[...]).astype(o_ref.dtype)


def rmsnorm_cast(x, g):
    m, d = x.shape
    tm = _row_tile(m, 512)
    return pl.pallas_call(
        _rmsnorm_kernel,
        grid=(m // tm,),
        in_specs=[pl.BlockSpec((tm, d), lambda i: (i, 0)), pl.BlockSpec((1, d), lambda i: (0, 0))],
        out_specs=pl.BlockSpec((tm, d), lambda i: (i, 0)),
        out_shape=jax.ShapeDtypeStruct((m, d), BF16),
        compiler_params=_compiler_params(("parallel",), 2 * tm * d * 6),
        name="rmsnorm_cast",
    )(x, g.reshape(1, d))


def _dot(a, b):
    return jnp.dot(a, b, preferred_element_type=F32)


def _dot_nt(a, b):
    return lax.dot_general(a, b, (((1,), (1,)), ((), ())), preferred_element_type=F32)


def _proj_plain_kernel(h_ref, w_ref, o_ref):
    o_ref[...] = _dot(h_ref[...], w_ref[...]).astype(o_ref.dtype)


def _headnorm_store(acc, g, o_ref, hd):
    for k in range(acc.shape[1] // hd):
        s = acc[:, k * hd:(k + 1) * hd]
        ms = jnp.mean(s * s, axis=-1, keepdims=True)
        o_ref[:, k * hd:(k + 1) * hd] = (s * lax.rsqrt(ms + EPS) * g).astype(o_ref.dtype)


def _norm_rope_store(acc, g, tab_ref, o_ref):
    rows, width = acc.shape
    grp_r = lax.broadcasted_iota(jnp.int32, (width, width), 0) // DIFF_HD
    grp_c = lax.broadcasted_iota(jnp.int32, (width, width), 1) // DIFF_HD
    ones_bd = jnp.where(grp_r == grp_c, 1.0, 0.0).astype(BF16)
    ms = _dot((acc * acc).astype(BF16), ones_bd) * (1.0 / DIFF_HD)
    y = acc * lax.rsqrt(ms + EPS)
    cos = tab_ref[:, 0:V7X_LANES]
    sin_up = tab_ref[:, V7X_LANES:2 * V7X_LANES]
    sin_dn = tab_ref[:, 2 * V7X_LANES:3 * V7X_LANES]
    half = ROT_DIM // 2
    for k in range(width // V7X_LANES):
        yk = y[:, k * V7X_LANES:(k + 1) * V7X_LANES] * g
        out = (yk * cos + pltpu.roll(yk, half, 1) * sin_up
               + pltpu.roll(yk, V7X_LANES - half, 1) * sin_dn)
        o_ref[:, k * V7X_LANES:(k + 1) * V7X_LANES] = out.astype(o_ref.dtype)


def _proj_headnorm_kernel(h_ref, w_ref, g_ref, o_ref, *, hd):
    _headnorm_store(_dot(h_ref[...], w_ref[...]), g_ref[...], o_ref, hd)


PROJ_SLAB = 256


FINAL_GROUPS = (1, 2, 4, 5)


def _proj_in_kernel(h_ref, w_ref, gqa_ref, gka_ref, gqb_ref, gkb_ref, gqm_ref, tab_ref, *rest,
                    blocks_per_group, hd_m, final):
    i = pl.program_id(0)
    j = pl.program_id(1)
    group = j // blocks_per_group
    tm = h_ref.shape[0]
    tn = w_ref.shape[1]
    n_prev = 0 if final is None or final['layer'] == 0 else 4
    qa_ref, ka_ref, va_ref, qb_ref, kb_ref, vb_ref, qm_ref = rest[n_prev:n_prev + 7]
    out_refs = {0: qa_ref, 1: ka_ref, 2: va_ref, 3: qb_ref, 4: kb_ref, 5: vb_ref, 6: qm_ref}
    if final is not None:
        final_refs = dict(zip(FINAL_GROUPS, rest[n_prev + 7:n_prev + 11]))
        stage, zeros, sem, zero_sem = rest[n_prev + 11:]
        heads_per_block = tn // V7X_LANES
        bps = final['seq_len'] // tm
        clear_slots = range(1, final['depth']) if final['layer'] == 0 else ()

        @pl.when((i == 0) & (j == 0))
        def _():
            zeros[...] = jnp.zeros(zeros.shape, zeros.dtype)

        def head_copies(g, c, slot):
            keep = final['a_keep'] if g in (1, 2) else final['seq_len']
            if keep >= tm:
                rows, r_lo, t0 = tm, 0, (i % bps) * tm - (final['seq_len'] - keep)
            else:
                rows, r_lo, t0 = keep, tm - keep, 0
            out = []
            for hh in range(heads_per_block):
                where = (i // bps, pl.ds(t0, rows), c * heads_per_block + hh, slice(None))
                src = stage.at[slot, pl.ds(r_lo, rows), pl.ds(hh * V7X_LANES, V7X_LANES)]
                out.append(pltpu.make_async_copy(src, final_refs[g].at[(final['layer'],) + where], sem.at[hh]))
                for p in clear_slots:
                    out.append(pltpu.make_async_copy(zeros.at[pl.ds(0, rows), :], final_refs[g].at[(p,) + where],
                                                     zero_sem.at[p - 1, hh]))
            return out

        def kept(g):
            keep = final['a_keep'] if g in (1, 2) else final['seq_len']
            return (i % bps) >= bps - max(keep // tm, 1)

        def wait_block(g, c, slot):
            @pl.when(kept(g))
            def _():
                for cp in head_copies(g, c, slot):
                    cp.wait()

    def run(g, epilogue):
        o_ref = out_refs[g]

        def body():
            slabs = [slice(c0, c0 + PROJ_SLAB) for c0 in range(0, tn, PROJ_SLAB)]
            accs = [_dot(h_ref[...], w_ref[:, cols]) for cols in slabs]
            for acc, cols in zip(accs, slabs):
                epilogue(acc, o_ref.at[:, cols])
            if final is None:
                return
            c = j - g * blocks_per_group
            slot = j % 2
            if g in FINAL_GROUPS:
                @pl.when(c > 0)
                def _():
                    wait_block(g, c - 1, 1 - slot)
            if g - 1 in FINAL_GROUPS:
                @pl.when(c == 0)
                def _():
                    wait_block(g - 1, blocks_per_group - 1, 1 - slot)
            if g in FINAL_GROUPS:
                @pl.when(kept(g))
                def _():
                    stage[slot] = o_ref[...]
                    for cp in head_copies(g, c, slot):
                        cp.start()
        return body

    headnorm = lambda g_ref, hd: (lambda acc, o: _headnorm_store(acc, g_ref[...], o, hd))
    norm_rope = lambda g_ref: (lambda acc, o: _norm_rope_store(acc, g_ref[...], tab_ref, o))
    plain = lambda acc, o: o.__setitem__(Ellipsis, acc)
    bodies = (
        run(0, headnorm(gqa_ref, HD_A)), run(1, headnorm(gka_ref, HD_A)), run(2, plain),
        run(3, norm_rope(gqb_ref)), run(4, norm_rope(gkb_ref)), run(5, plain),
        run(6, headnorm(gqm_ref, hd_m)),
    )
    for n, body in enumerate(bodies):
        pl.when(group == n)(body)


def proj_in(h, w_in, gains, rope_tab, *, tm, tn=512, final=None):
    m, d = h.shape
    bw = d // 2
    hd_m = bw // H_M
    tm = _row_tile(m, tm)
    assert bw % tn == 0 and tn % PROJ_SLAB == 0 and PROJ_SLAB % hd_m == 0 and rope_tab.shape[0] % tm == 0
    bpg = bw // tn
    n_tab = rope_tab.shape[0] // tm
    g_a_q, g_a_k, g_b_q, g_b_k, g_m_q = gains
    tile2 = lambda g: jnp.tile(g.reshape(1, -1), (1, 2))
    gain_args = [g_a_q.reshape(1, HD_A), g_a_k.reshape(1, HD_A), tile2(g_b_q), tile2(g_b_k), g_m_q.reshape(1, hd_m)]
    const = lambda a: pl.BlockSpec(a.shape, lambda i, j: (0, 0))

    def out_spec(n):
        return pl.BlockSpec((tm, tn), lambda i, j: (i, jnp.clip(j - n * bpg, 0, bpg - 1)))

    dtypes = (BF16, F32, F32, BF16, F32, F32, BF16)
    out_specs = [out_spec(n) for n in range(7)]
    out_shape = [jax.ShapeDtypeStruct((m, bw), t) for t in dtypes]
    prev, scratch, aliases, semantics = [], [], {}, ("parallel", "arbitrary")
    n_fixed_inputs = 8
    if final is not None:
        t, keep, batch = final['seq_len'], final['a_keep'], final['batch']
        heads = bw // V7X_LANES
        assert HD_A == V7X_LANES and 2 * DIFF_HD == V7X_LANES and m == batch * t and t % tm == 0
        assert keep % tm == 0 or (keep < tm and keep % 8 == 0)
        prev = list(final['prev'] or ())
        assert len(prev) == (4 if final['layer'] else 0)
        depth = final['depth']
        out_shape += [jax.ShapeDtypeStruct((depth, batch, rows, heads, V7X_LANES), F32) for rows in (keep, keep, t, t)]
        out_specs += [pl.BlockSpec(memory_space=pl.ANY)] * 4
        hpb = tn // V7X_LANES
        scratch = [pltpu.VMEM((2, tm, tn), F32), pltpu.VMEM((tm, V7X_LANES), F32), pltpu.SemaphoreType.DMA((hpb,)),
                   pltpu.SemaphoreType.DMA((max(depth - 1, 1), hpb))]
        aliases = {n_fixed_inputs + k: 7 + k for k in range(len(prev))}
        semantics = ("arbitrary", "arbitrary")
        final = {k: v for k, v in final.items() if k != 'prev'}
    out_bytes = sum(tm * tn * jnp.dtype(t).itemsize for t in dtypes)
    vmem = 2 * (tm * d * 2 + d * tn * 2 + tm * 3 * V7X_LANES * 4 + out_bytes) + 4 * tm * tn * 4
    return pl.pallas_call(
        functools.partial(_proj_in_kernel, blocks_per_group=bpg, hd_m=hd_m, final=final),
        grid=(m // tm, 7 * bpg),
        in_specs=[pl.BlockSpec((tm, d), lambda i, j: (i, 0)), pl.BlockSpec((d, tn), lambda i, j: (0, j))]
                 + [const(g) for g in gain_args]
                 + [pl.BlockSpec((tm, 3 * V7X_LANES), lambda i, j: (i % n_tab, 0))]
                 + [pl.BlockSpec(memory_space=pl.ANY)] * len(prev),
        out_specs=out_specs,
        out_shape=out_shape,
        input_output_aliases=aliases,
        scratch_shapes=scratch,
        compiler_params=_compiler_params(semantics, vmem),
        name="proj_in",
    )(h, w_in, *gain_args, rope_tab, *prev)


def _proj_residual_kernel(h_ref, w_ref, x_ref, o_ref):
    o_ref[...] = x_ref[...] + _dot(h_ref[...], w_ref[...])


def _proj_out_norm_kernel(m_ref, w_ref, x_ref, g_ref, o_ref, h_ref, xrow):
    j = pl.program_id(1)
    y = x_ref[...] + _dot(m_ref[...], w_ref[...])
    o_ref[...] = y
    xrow[j] = y

    @pl.when(j == pl.num_programs(1) - 1)
    def _():
        nj, _, tn = xrow.shape
        ssq = None
        for jj in range(nj):
            xb = xrow[jj]
            part = jnp.sum(xb * xb, axis=-1, keepdims=True)
            ssq = part if ssq is None else ssq + part
        scale = lax.rsqrt(ssq * (1.0 / (nj * tn)) + EPS)
        for jj in range(nj):
            cols = slice(jj * tn, (jj + 1) * tn)
            h_ref[:, cols] = (xrow[jj] * scale * g_ref[:, cols]).astype(h_ref.dtype)


def proj_out_norm(merged, w_out, x, g, *, tm, tn=512):
    m, d = x.shape
    tm = _row_tile(m, tm)
    assert d % tn == 0
    row_spec = pl.BlockSpec((tm, d), lambda i, j: (i, 0))
    blk_spec = pl.BlockSpec((tm, tn), lambda i, j: (i, j))
    vmem = 2 * (tm * d * 2 + d * tn * 2 + 2 * tm * tn * 4 + tm * d * 2) + tm * d * 4 + 2 * tm * tn * 4
    return pl.pallas_call(
        _proj_out_norm_kernel,
        grid=(m // tm, d // tn),
        in_specs=[row_spec, pl.BlockSpec((d, tn), lambda i, j: (0, j)), blk_spec,
                  pl.BlockSpec((1, d), lambda i, j: (0, 0))],
        out_specs=[blk_spec, row_spec],
        out_shape=[jax.ShapeDtypeStruct((m, d), F32), jax.ShapeDtypeStruct((m, d), BF16)],
        scratch_shapes=[pltpu.VMEM((d // tn, tm, tn), F32)],
        compiler_params=_compiler_params(("parallel", "arbitrary"), vmem),
        name="proj_out_norm",
    )(merged, w_out, x, g.reshape(1, d))


def _proj_call(kernel_fn, h, w, col0, ncols, out_dtype, extras, *, tm=1024, tn=512, name):
    m, k = h.shape
    tm = _row_tile(m, tm)
    tn = min(tn, ncols)
    assert ncols % tn == 0 and col0 % tn == 0, (ncols, col0, tn)
    cb = col0 // tn
    in_specs = [pl.BlockSpec((tm, k), lambda i, j: (i, 0)),
                pl.BlockSpec((k, tn), lambda i, j: (0, j + cb))]
    in_specs += [spec for _, spec in extras]
    extra_bytes = sum(int(np.prod(spec.block_shape)) * a.dtype.itemsize for a, spec in extras)
    vmem = 2 * (tm * k * 2 + k * tn * 2 + tm * tn * 4 + extra_bytes) + tm * tn * 8
    return pl.pallas_call(
        kernel_fn,
        grid=(m // tm, ncols // tn),
        in_specs=in_specs,
        out_specs=pl.BlockSpec((tm, tn), lambda i, j: (i, j)),
        out_shape=jax.ShapeDtypeStruct((m, ncols), out_dtype),
        compiler_params=_compiler_params(("parallel", "arbitrary"), vmem),
        name=name,
    )(h, w, *[a for a, _ in extras])


def _rope_table(pos):
    half = ROT_DIM // 2
    inv_freq = jnp.exp(jnp.arange(half, dtype=F32) * (-2.0 * math.log(ROPE_THETA) / ROT_DIM))
    ang = pos.astype(F32)[:, None] * inv_freq[None, :]
    cos = jnp.cos(ang)
    sin = jnp.sin(ang)
    p = pos.shape[0]
    rest = DIFF_HD - ROT_DIM
    c64 = jnp.concatenate([cos, cos, jnp.ones((p, rest), F32)], axis=1)
    up64 = jnp.concatenate([jnp.zeros((p, half), F32), sin, jnp.zeros((p, rest), F32)], axis=1)
    dn64 = jnp.concatenate([-sin, jnp.zeros((p, half + rest), F32)], axis=1)
    return jnp.concatenate([c64, c64, up64, up64, dn64, dn64], axis=1)


A_QBLK = 4 * CHUNK
A_KBLK = BAND_PAST + A_QBLK
A_BIAS_W = 1024


def _rel_bias_row(tab):
    assert A_KBLK + A_QBLK - 1 <= A_BIAS_W
    lo = BAND_PAST - REL_CLIP
    hi = BAND_PAST + REL_CLIP + 1
    rep = lambda col, n: jnp.repeat(tab[:, col:col + 1], n, axis=1)
    row = jnp.concatenate([rep(0, lo), tab, rep(2 * REL_CLIP, A_KBLK - hi), rep(0, A_BIAS_W - A_KBLK)], axis=1)
    return row[:, None, :]


def _toeplitz_bias(row, rows, width):
    full = pltpu.roll(jnp.broadcast_to(row, (rows, A_BIAS_W)), 0, 1, stride=1, stride_axis=0)
    return full[:, :width]


def _softmax_pv(s, v):
    m = jnp.max(s, axis=-1, keepdims=True)
    p = jnp.exp(s - m)
    l = jnp.sum(p, axis=-1, keepdims=True)
    return _dot(p.astype(BF16), v), l


def _attn_a_prompt_kernel(q_ref, k_ref, v_ref, row_ref, o_ref, kb, vb, *, t):
    kscale = (HD_A ** -0.5) * LOG2E
    kb[...] = (k_ref[0] * kscale).astype(BF16)
    vb[:, :HD_A] = v_ref[0].astype(BF16)
    vb[:, HD_A:] = jnp.ones((t, HD_A), BF16)
    qc = lax.broadcasted_iota(jnp.int32, (A_QBLK, A_KBLK), 0) // CHUNK
    kc = lax.broadcasted_iota(jnp.int32, (A_QBLK, A_KBLK), 1) // CHUNK
    inband = (kc >= qc) & (kc <= qc + BAND_CHUNKS)
    bias = jnp.where(inband, _toeplitz_bias(row_ref[0], A_QBLK, A_KBLK) * LOG2E, NEG_INF)
    for i in range(t // A_QBLK):
        r0 = i * A_QBLK
        k0 = max(r0 - BAND_PAST, 0)
        k1 = r0 + A_QBLK
        q = q_ref[0, r0:k1, :]
        s = _dot_nt(q, kb[k0:k1, :]) + bias[:, A_KBLK - (k1 - k0):]
        p = jnp.exp2(s - jnp.max(s, axis=-1, keepdims=True))
        o = _dot(p.astype(BF16), vb[k0:k1, :])
        o_ref[0, r0:k1, :] = (o[:, :HD_A] / o[:, HD_A:]).astype(o_ref.dtype)


def attn_a_prompt(qa, ka, va, bias_row, b, t):
    h = qa.shape[1] // HD_A
    assert t % A_QBLK == 0 and A_QBLK % V7X_LANES == 0
    q3, k3, v3 = (a.reshape(b, t, h * HD_A) for a in (qa, ka, va))
    spec = pl.BlockSpec((1, t, HD_A), lambda bi, hi: (bi, 0, hi))
    vmem = 2 * t * HD_A * (2 + 4 + 4 + 2) + 2 * t * HD_A * 2 + 24 * MIB
    out = pl.pallas_call(
        functools.partial(_attn_a_prompt_kernel, t=t),
        grid=(b, h),
        in_specs=[spec, spec, spec, pl.BlockSpec((1, 1, A_BIAS_W), lambda bi, hi: (hi, 0, 0))],
        out_specs=spec,
        out_shape=jax.ShapeDtypeStruct((b, t, h * HD_A), BF16),
        scratch_shapes=[pltpu.VMEM((t, HD_A), BF16), pltpu.VMEM((t, 2 * HD_A), BF16)],
        compiler_params=_compiler_params(("parallel", "parallel"), vmem),
        name="attn_a_prompt",
    )(q3, k3, v3, bias_row)
    return out.reshape(b * t, h * HD_A)


def _attn_a_sample_kernel(q_ref, kn_ref, vn_ref, row_ref, mask_ref, kc_hbm, vc_hbm, o_ref, kbuf, vbuf, sem,
                          *, layer, heads, a_len):
    b = pl.program_id(0)
    nb = pl.num_programs(0)
    scale = HD_A ** -0.5
    t = q_ref.shape[2]

    def copies(bi, slot):
        out = []
        for h in range(heads):
            out.append(pltpu.make_async_copy(kc_hbm.at[layer, bi, :, h, :], kbuf.at[slot, h], sem.at[0, slot, h]))
            out.append(pltpu.make_async_copy(vc_hbm.at[layer, bi, :, h, :], vbuf.at[slot, h], sem.at[1, slot, h]))
        return out

    @pl.when(b == 0)
    def _():
        for c in copies(b, 0):
            c.start()

    slot = b % 2

    @pl.when(b + 1 < nb)
    def _():
        for c in copies(b + 1, 1 - slot):
            c.start()

    for c in copies(b, slot):
        c.wait()
    for h in range(heads):
        q = q_ref[0, h]
        bias = _toeplitz_bias(row_ref[h], t, a_len + t) + mask_ref[...]
        sc = _dot_nt(q, kbuf[slot, h].astype(BF16)) * scale + bias[:, :a_len]
        sn = _dot_nt(q, kn_ref[0, h].astype(BF16)) * scale + bias[:, a_len:]
        m = jnp.maximum(jnp.max(sc, axis=-1, keepdims=True), jnp.max(sn, axis=-1, keepdims=True))
        pc = jnp.exp(sc - m)
        pn = jnp.exp(sn - m)
        l = jnp.sum(pc, axis=-1, keepdims=True) + jnp.sum(pn, axis=-1, keepdims=True)
        o = _dot(pc.astype(BF16), vbuf[slot, h].astype(BF16)) + _dot(pn.astype(BF16), vn_ref[0, h].astype(BF16))
        o_ref[0, h] = (o / l).astype(o_ref.dtype)


def _by_head(a, b, t, heads, hd):
    return a.reshape(b, t, heads, hd).transpose(0, 2, 1, 3)


def attn_a_sample(qa, ka, va, cache_k, cache_v, layer, bias_row, mask, b, t):
    heads = qa.shape[1] // HD_A
    a_len = cache_k.shape[2]
    assert a_len == BAND_PAST and a_len + t <= A_KBLK
    head_spec = pl.BlockSpec((1, heads, t, HD_A), lambda bi: (bi, 0, 0, 0))
    vmem = 2 * 2 * heads * a_len * HD_A * 4 + 2 * 4 * heads * t * HD_A * 4 + 16 * MIB
    out = pl.pallas_call(
        functools.partial(_attn_a_sample_kernel, layer=layer, heads=heads, a_len=a_len),
        grid=(b,),
        in_specs=[head_spec, head_spec, head_spec,
                  pl.BlockSpec(bias_row.shape, lambda bi: (0, 0, 0)),
                  pl.BlockSpec(mask.shape, lambda bi: (0, 0)),
                  pl.BlockSpec(memory_space=pl.ANY), pl.BlockSpec(memory_space=pl.ANY)],
        out_specs=head_spec,
        out_shape=jax.ShapeDtypeStruct((b, heads, t, HD_A), BF16),
        scratch_shapes=[pltpu.VMEM((2, heads, a_len, HD_A), F32), pltpu.VMEM((2, heads, a_len, HD_A), F32),
                        pltpu.SemaphoreType.DMA((2, 2, heads))],
        compiler_params=_compiler_params(("arbitrary",), vmem),
        name="attn_a_sample",
    )(*(_by_head(a, b, t, heads, HD_A) for a in (qa, ka, va)), bias_row, mask, cache_k, cache_v)
    return out.transpose(0, 2, 1, 3).reshape(b * t, heads * HD_A)


def _diff_lambda(lam_ref, lam_init):
    v = lam_ref[...]
    d1 = jnp.sum(v[0:1] * v[1:2], axis=-1, keepdims=True)
    d2 = jnp.sum(v[2:3] * v[3:4], axis=-1, keepdims=True)
    return jnp.exp(d1) - jnp.exp(d2) + lam_init


def _split_diff_queries(q):
    lane = lax.broadcasted_iota(jnp.int32, q.shape, 1)
    qs = q * jnp.asarray(DIFF_HD ** -0.5, q.dtype)
    zero = jnp.zeros_like(qs)
    return jnp.where(lane < DIFF_HD, qs, zero), jnp.where(lane >= DIFF_HD, qs, zero)


def _stack_diff_queries(q):
    return jnp.concatenate(_split_diff_queries(q), axis=0)


def _diff_post(o, g, post_scale):
    ms = jnp.mean(o * o, axis=-1, keepdims=True)
    return (o * lax.rsqrt(ms + EPS) * g) * post_scale


def _diff_finish(l, acc, lam, g, post_scale, tq):
    o = acc[:tq] / l[:tq] - lam * (acc[tq:] / l[tq:])
    return _diff_post(o, g, post_scale)


B_TQ = 8 * CHUNK


def _online_step(carry, s, v_ext):
    m, acc = carry
    m_new = jnp.maximum(m, jnp.max(s, axis=-1, keepdims=True))
    alpha = jnp.exp2(m - m_new)
    p = jnp.exp2(s - m_new)
    acc = alpha * acc + _dot(p.astype(BF16), v_ext)
    return m_new, acc


def _attn_b_prompt_kernel(lam_ref, q_ref, k_ref, v_ref, g_ref, o_ref, kb, vb, *, t, lam_init):
    tq = B_TQ
    hd = 2 * DIFF_HD
    kb[...] = (k_ref[0] * LOG2E).astype(BF16)
    vb[:, :hd] = v_ref[0].astype(BF16)
    vb[:, hd:] = jnp.ones((t, hd), BF16)
    lam = _diff_lambda(lam_ref, lam_init)
    row = lax.broadcasted_iota(jnp.int32, (tq, tq), 0)
    col = lax.broadcasted_iota(jnp.int32, (tq, tq), 1)
    diag_ok = (col // CHUNK) <= (row // CHUNK)
    for qi in range(t // tq):
        q0 = qi * tq
        qs = _split_diff_queries(q_ref[0, q0:q0 + tq, :])
        spans = ([(0, q0, False)] if q0 else []) + [(q0, tq, True)]
        outs = []
        for c in range(2):
            carry = (jnp.full((tq, 1), NEG_INF, F32), jnp.zeros((tq, 2 * hd), F32))
            for k0, width, masked in spans:
                s = _dot_nt(qs[c], kb[k0:k0 + width, :])
                if masked:
                    s = jnp.where(diag_ok, s, NEG_INF)
                carry = _online_step(carry, s, vb[k0:k0 + width, :])
            outs.append(carry[1][:, :hd] / carry[1][:, hd:])
        o = outs[0] - lam * outs[1]
        o_ref[0, q0:q0 + tq, :] = _diff_post(o, g_ref[...], 1.0 - lam_init).astype(o_ref.dtype)


def attn_b_prompt(qb, kb, vb, lam_params, subln_g, lam_init, b, t):
    hd = 2 * DIFF_HD
    heads = qb.shape[1] // hd
    assert t % B_TQ == 0
    q3, k3, v3 = (a.reshape(b, t, heads * hd) for a in (qb, kb, vb))
    spec = pl.BlockSpec((1, t, hd), lambda bi, hi: (bi, 0, hi))
    vmem = 2 * t * hd * (2 + 4 + 4 + 2) + 2 * t * hd * 2 + 32 * MIB
    out = pl.pallas_call(
        functools.partial(_attn_b_prompt_kernel, t=t, lam_init=lam_init),
        grid=(b, heads),
        in_specs=[pl.BlockSpec(lam_params.shape, lambda bi, hi: (0, 0)), spec, spec, spec,
                  pl.BlockSpec((1, hd), lambda bi, hi: (0, 0))],
        out_specs=spec,
        out_shape=jax.ShapeDtypeStruct((b, t, heads * hd), BF16),
        scratch_shapes=[pltpu.VMEM((t, hd), BF16), pltpu.VMEM((t, 2 * hd), BF16)],
        compiler_params=_compiler_params(("parallel", "parallel"), vmem),
        name="attn_b_prompt",
    )(lam_params, q3, k3, v3, subln_g.reshape(1, hd))
    return out.reshape(b * t, heads * hd)


def _attn_b_sample_kernel(lam_ref, q_ref, kn_ref, vn_ref, g_ref, kc_hbm, vc_hbm, o_ref, kbuf, vbuf, sem,
                          *, layer, heads, t, lam_init):
    b = pl.program_id(0)
    nb = pl.num_programs(0)

    def copies(bi, h, slot):
        return (pltpu.make_async_copy(kc_hbm.at[layer, bi, :, h, :], kbuf.at[slot], sem.at[0, slot]),
                pltpu.make_async_copy(vc_hbm.at[layer, bi, :, h, :], vbuf.at[slot], sem.at[1, slot]))

    def start(bi, h, slot):
        for c in copies(bi, h, slot):
            c.start()

    @pl.when(b == 0)
    def _():
        start(b, 0, 0)

    lam = _diff_lambda(lam_ref, lam_init)
    for h in range(heads):
        slot = h % 2
        if h + 1 < heads:
            start(b, h + 1, 1 - slot)
        else:
            @pl.when(b + 1 < nb)
            def _():
                start(b + 1, 0, 1 - slot)
        for c in copies(b, h, slot):
            c.wait()
        q2 = _stack_diff_queries(q_ref[0, h])
        sc = _dot_nt(q2, kbuf[slot].astype(BF16))
        sn = _dot_nt(q2, kn_ref[0, h].astype(BF16))
        m = jnp.maximum(jnp.max(sc, axis=-1, keepdims=True), jnp.max(sn, axis=-1, keepdims=True))
        pc = jnp.exp(sc - m)
        pn = jnp.exp(sn - m)
        l = jnp.sum(pc, axis=-1, keepdims=True) + jnp.sum(pn, axis=-1, keepdims=True)
        acc = _dot(pc.astype(BF16), vbuf[slot].astype(BF16)) + _dot(pn.astype(BF16), vn_ref[0, h].astype(BF16))
        o_ref[0, h] = _diff_finish(l, acc, lam, g_ref[...], 1.0 - lam_init, t).astype(o_ref.dtype)


def attn_b_sample(qb, kb, vb, cache_k, cache_v, layer, lam_params, subln_g, lam_init, b, t):
    hd = 2 * DIFF_HD
    heads = qb.shape[1] // hd
    assert heads % 2 == 0
    past = cache_k.shape[2]
    head_spec = pl.BlockSpec((1, heads, t, hd), lambda bi: (bi, 0, 0, 0))
    vmem = 2 * 2 * past * hd * 4 + 2 * 3 * heads * t * hd * 4 + 12 * 2 * t * past * 4
    out = pl.pallas_call(
        functools.partial(_attn_b_sample_kernel, layer=layer, heads=heads, t=t, lam_init=lam_init),
        grid=(b,),
        in_specs=[pl.BlockSpec(lam_params.shape, lambda bi: (0, 0)), head_spec, head_spec, head_spec,
                  pl.BlockSpec((1, hd), lambda bi: (0, 0)),
                  pl.BlockSpec(memory_space=pl.ANY), pl.BlockSpec(memory_space=pl.ANY)],
        out_specs=head_spec,
        out_shape=jax.ShapeDtypeStruct((b, heads, t, hd), BF16),
        scratch_shapes=[pltpu.VMEM((2, past, hd), F32), pltpu.VMEM((2, past, hd), F32),
                        pltpu.SemaphoreType.DMA((2, 2))],
        compiler_params=_compiler_params(("arbitrary",), vmem),
        name="attn_b_sample",
    )(lam_params, *(_by_head(a, b, t, heads, hd) for a in (qb, kb, vb)), subln_g.reshape(1, hd), cache_k, cache_v)
    return out.transpose(0, 2, 1, 3).reshape(b * t, heads * hd)


def _attn_m_kernel(q_ref, k_ref, v_ref, o_ref, *, heads, hd):
    scale = hd ** -0.5
    for h in range(heads):
        sl = slice(h * hd, (h + 1) * hd)
        s = _dot_nt(q_ref[0, :, sl], k_ref[0, :, sl].astype(BF16)) * scale
        o, l = _softmax_pv(s, v_ref[0, :, sl].astype(BF16))
        o_ref[0, :, sl] = (o / l).astype(o_ref.dtype)


def attn_m_prompt(qm, mem_k, mem_v, b, t, *, tq=512):
    width = qm.shape[1]
    hd = width // H_M
    tq = min(tq, t)
    assert t % tq == 0
    n = mem_k.shape[0] // b
    q_spec = pl.BlockSpec((1, tq, width), lambda bi, qi: (bi, qi, 0))
    kv_spec = pl.BlockSpec((1, n, width), lambda bi, qi: (bi, 0, 0))
    vmem = 2 * (2 * n * width * 4 + 2 * tq * width * 2) + 8 * tq * n * 4
    out = pl.pallas_call(
        functools.partial(_attn_m_kernel, heads=H_M, hd=hd),
        grid=(b, t // tq),
        in_specs=[q_spec, kv_spec, kv_spec],
        out_specs=q_spec,
        out_shape=jax.ShapeDtypeStruct((b, t, width), BF16),
        compiler_params=_compiler_params(("parallel", "arbitrary"), vmem),
        name="attn_m_prompt",
    )(qm.reshape(b, t, width), mem_k.reshape(b, n, width), mem_v.reshape(b, n, width))
    return out.reshape(b * t, width)


def _attn_m_sample_kernel(q_ref, kc_hbm, vc_hbm, o_ref, kbuf, vbuf, sem, *, layer, heads, hd):
    b = pl.program_id(0)
    nb = pl.num_programs(0)
    scale = hd ** -0.5

    def copies(bi, slot):
        out = []
        for h in range(heads):
            out.append(pltpu.make_async_copy(kc_hbm.at[layer, bi, :, h, :], kbuf.at[slot, h], sem.at[0, slot, h]))
            out.append(pltpu.make_async_copy(vc_hbm.at[layer, bi, :, h, :], vbuf.at[slot, h], sem.at[1, slot, h]))
        return out

    @pl.when(b == 0)
    def _():
        for c in copies(b, 0):
            c.start()

    slot = b % 2

    @pl.when(b + 1 < nb)
    def _():
        for c in copies(b + 1, 1 - slot):
            c.start()

    for c in copies(b, slot):
        c.wait()
    for h in range(heads):
        s = _dot_nt(q_ref[0, h], kbuf[slot, h].astype(BF16)) * scale
        o, l = _softmax_pv(s, vbuf[slot, h].astype(BF16))
        o_ref[0, h] = (o / l).astype(o_ref.dtype)


def attn_m_sample(qm, cache_k, cache_v, layer, b, t):
    n, heads, hd = cache_k.shape[2:]
    head_spec = pl.BlockSpec((1, heads, t, hd), lambda bi: (bi, 0, 0, 0))
    vmem = 2 * 2 * heads * n * hd * 4 + 2 * 2 * heads * t * hd * 2 + 8 * MIB
    out = pl.pallas_call(
        functools.partial(_attn_m_sample_kernel, layer=layer, heads=heads, hd=hd),
        grid=(b,),
        in_specs=[head_spec, pl.BlockSpec(memory_space=pl.ANY), pl.BlockSpec(memory_space=pl.ANY)],
        out_specs=head_spec,
        out_shape=jax.ShapeDtypeStruct((b, heads, t, hd), BF16),
        scratch_shapes=[pltpu.VMEM((2, heads, n, hd), F32), pltpu.VMEM((2, heads, n, hd), F32),
                        pltpu.SemaphoreType.DMA((2, 2, heads))],
        compiler_params=_compiler_params(("arbitrary",), vmem),
        name="attn_m_sample",
    )(_by_head(qm, b, t, heads, hd), cache_k, cache_v)
    return out.transpose(0, 2, 1, 3).reshape(b * t, heads * hd)


def _merge_kernel(h_ref, oa_ref, ob_ref, om_ref, wga_ref, wgb_ref, wgm_ref, gb_ref, wbr_ref, o_ref):
    h = h_ref[...]
    acc = None
    for n, (o_n, wg_n) in enumerate(((oa_ref, wga_ref), (ob_ref, wgb_ref), (om_ref, wgm_ref))):
        gate = jax.nn.sigmoid(_dot(h, wg_n[...]) + gb_ref[n])
        term = gate * _dot(o_n[...], wbr_ref[n])
        acc = term if acc is None else acc + term
    o_ref[...] = acc.astype(o_ref.dtype)


def merge_branches(h, oa, ob, om, w_in, gate_col0, gate_b, w_br, *, tm, tn=256):
    m, bw = oa.shape
    d = w_br.shape[2]
    tm = _row_tile(m, tm)
    assert d % tn == 0 and gate_col0 % tn == 0
    nj = d // tn
    g0 = gate_col0 // tn
    row_spec = lambda width: pl.BlockSpec((tm, width), lambda i, j: (i, 0))
    gate_w_specs = [pl.BlockSpec((d, tn), functools.partial(lambda i, j, n: (0, g0 + n * nj + j), n=n))
                    for n in range(N_BRANCH)]
    vmem = 2 * (tm * d * 2 + 3 * tm * bw * 2 + 3 * d * tn * 2 + 3 * bw * tn * 2 + tm * tn * 2) + 6 * tm * tn * 4
    return pl.pallas_call(
        _merge_kernel,
        grid=(m // tm, nj),
        in_specs=[row_spec(d), row_spec(bw), row_spec(bw), row_spec(bw)] + gate_w_specs
                 + [pl.BlockSpec((N_BRANCH, 1, tn), lambda i, j: (0, 0, j)),
                    pl.BlockSpec((N_BRANCH, bw, tn), lambda i, j: (0, 0, j))],
        out_specs=pl.BlockSpec((tm, tn), lambda i, j: (i, j)),
        out_shape=jax.ShapeDtypeStruct((m, d), BF16),
        compiler_params=_compiler_params(("parallel", "arbitrary"), vmem),
        name="merge_branches",
    )(h, oa, ob, om, w_in, w_in, w_in, gate_b, w_br)


def _ffn_up_kernel(h_ref, wa_ref, wb_ref, cw_ref, cb_ref, st_ref, g_ref, cn_ref, carry, *, nb, tb, blocks_per_seq):
    i = pl.program_id(0)
    j = pl.program_id(1)
    tm, tn = g_ref.shape
    if nb == 1:
        @pl.when((i % blocks_per_seq) == 0)
        def _():
            carry[j] = st_ref[0]

        trow = lax.broadcasted_iota(jnp.int32, (tm, PROJ_SLAB), 0)
    else:
        trow = lax.broadcasted_iota(jnp.int32, (nb, tb, PROJ_SLAB), 1).reshape(tm, PROJ_SLAB)
    h = h_ref[...]
    slabs = [slice(c0, c0 + PROJ_SLAB) for c0 in range(0, tn, PROJ_SLAB)]
    dots = [(_dot(h, wa_ref[:, cols]), _dot(h, wb_ref[:, cols])) for cols in slabs]
    for cols, (a, bgate) in zip(slabs, dots):
        if nb == 1:
            prev = carry[j, :, cols]
            p0 = jnp.broadcast_to(prev[0:1], (tm, PROJ_SLAB))
            p1 = jnp.broadcast_to(prev[1:2], (tm, PROJ_SLAB))
            carry[j, :, cols] = a[tm - 2:tm]
            cn_ref[0, :, cols] = a[tm - 2:tm]
        else:
            st = st_ref[:, :, cols]
            p0 = jnp.broadcast_to(st[:, 0:1, :], (nb, tb, PROJ_SLAB)).reshape(tm, PROJ_SLAB)
            p1 = jnp.broadcast_to(st[:, 1:2, :], (nb, tb, PROJ_SLAB)).reshape(tm, PROJ_SLAB)
            cn_ref[:, :, cols] = a.reshape(nb, tb, PROJ_SLAB)[:, tb - 2:tb, :]
        am1 = jnp.where(trow == 0, p1, pltpu.roll(a, 1, 0))
        am2 = jnp.where(trow == 0, p0, jnp.where(trow == 1, p1, pltpu.roll(a, 2, 0)))
        cw = cw_ref[:, cols]
        c = cb_ref[:, cols] + am2 * cw[0:1] + am1 * cw[1:2] + a * cw[2:3]
        gelu = 0.5 * c * (1.0 + lax.erf(c * (2.0 ** -0.5)))
        g_ref[:, cols] = (gelu * bgate).astype(g_ref.dtype)


def ffn_up(h, w_a, w_b, conv_w, conv_b, state, b, t, *, tm=1024, tn=512):
    m, d = h.shape
    f = w_a.shape[1]
    tm = _row_tile(m, tm)
    assert f % tn == 0 and tn % PROJ_SLAB == 0
    if tm >= t:
        assert tm % t == 0
        nb, tb, blocks_per_seq = tm // t, t, 1
    else:
        assert t % tm == 0
        nb, tb, blocks_per_seq = 1, tm, t // tm
    if nb > 1:
        seq_map = lambda i, j: (i, 0, j)
    else:
        seq_map = lambda i, j: (i // blocks_per_seq, 0, j)
    tail_map = lambda i, j: (i, 0, j)
    w_spec = pl.BlockSpec((d, tn), lambda i, j: (0, j))
    vmem = 2 * (tm * d * 2 + 2 * d * tn * 2 + tm * tn * 2 + 2 * nb * 8 * tn * 4) + 8 * tm * tn * 4
    g, conv_new = pl.pallas_call(
        functools.partial(_ffn_up_kernel, nb=nb, tb=tb, blocks_per_seq=blocks_per_seq),
        grid=(m // tm, f // tn),
        in_specs=[pl.BlockSpec((tm, d), lambda i, j: (i, 0)), w_spec, w_spec,
                  pl.BlockSpec((CONV_W, tn), lambda i, j: (0, j)),
                  pl.BlockSpec((1, tn), lambda i, j: (0, j)),
                  pl.BlockSpec((nb, CONV_W - 1, tn), seq_map)],
        out_specs=[pl.BlockSpec((tm, tn), lambda i, j: (i, j)),
                   pl.BlockSpec((nb, CONV_W - 1, tn), tail_map)],
        out_shape=[jax.ShapeDtypeStruct((m, f), BF16),
                   jax.ShapeDtypeStruct((b * blocks_per_seq, CONV_W - 1, f), F32)],
        scratch_shapes=[pltpu.VMEM((f // tn, CONV_W - 1, tn), F32)],
        compiler_params=_compiler_params(("arbitrary", "arbitrary"), vmem),
        name="ffn_up",
    )(h, w_a, w_b, conv_w, conv_b, state)
    return g, conv_new.reshape(b, blocks_per_seq, CONV_W - 1, f)[:, -1]


def _pad_cols(a, f_pad):
    return jnp.pad(a, [(0, 0)] * (a.ndim - 1) + [(0, f_pad - a.shape[-1])])


def _cast_kernel(w_ref, o_ref, *, valid_rows, valid_cols):
    rb, cb = o_ref.shape
    rows = pl.program_id(0) * rb + lax.broadcasted_iota(jnp.int32, (rb, cb), 0)
    cols = pl.program_id(1) * cb + lax.broadcasted_iota(jnp.int32, (rb, cb), 1)
    ok = (rows < valid_rows) & (cols < valid_cols)
    o_ref[...] = jnp.where(ok, w_ref[...], 0.0).astype(o_ref.dtype)


def cast_weight(w, layer, *, rb, cb, col0=0, ncols=None, out_rows=None, out_cols=None):
    _, r, c = w.shape
    ncols = c - col0 if ncols is None else ncols
    out_rows = r if out_rows is None else out_rows
    out_cols = ncols if out_cols is None else out_cols
    assert col0 % cb == 0 and out_rows % rb == 0 and out_cols % cb == 0
    c0 = col0 // cb
    last_r = (r - 1) // rb
    last_c = (col0 + ncols - 1) // cb
    in_map = lambda i, j: (layer, jnp.minimum(i, last_r), jnp.minimum(j + c0, last_c))
    return pl.pallas_call(
        functools.partial(_cast_kernel, valid_rows=r, valid_cols=ncols),
        grid=(out_rows // rb, out_cols // cb),
        in_specs=[pl.BlockSpec((None, rb, cb), in_map)],
        out_specs=pl.BlockSpec((rb, cb), lambda i, j: (i, j)),
        out_shape=jax.ShapeDtypeStruct((out_rows, out_cols), BF16),
        compiler_params=_compiler_params(("parallel", "parallel"), 2 * rb * cb * 6),
        name="cast_weight",
    )(w)


def _cast_halves_kernel(wa_ref, wb_ref, oa_ref, ob_ref, *, valid_cols):
    rb, cb = oa_ref.shape
    cols = pl.program_id(0) * cb + lax.broadcasted_iota(jnp.int32, (rb, cb), 1)
    ok = cols < valid_cols
    oa_ref[...] = jnp.where(ok, wa_ref[...], 0.0).astype(oa_ref.dtype)
    ob_ref[...] = jnp.where(ok, wb_ref[...], 0.0).astype(ob_ref.dtype)


def cast_weight_halves(w, layer, out_cols, *, cb=V7X_LANES):
    _, r, c = w.shape
    half = c // 2
    assert half % cb == 0 and out_cols % cb == 0
    nb = half // cb
    in_spec = lambda first: pl.BlockSpec((None, r, cb), lambda j: (layer, 0, first + jnp.minimum(j, nb - 1)))
    out_spec = pl.BlockSpec((r, cb), lambda j: (0, j))
    out = jax.ShapeDtypeStruct((r, out_cols), BF16)
    return pl.pallas_call(
        functools.partial(_cast_halves_kernel, valid_cols=half),
        grid=(out_cols // cb,),
        in_specs=[in_spec(0), in_spec(nb)],
        out_specs=[out_spec, out_spec],
        out_shape=[out, out],
        compiler_params=_compiler_params(("parallel",), 2 * 2 * r * cb * 6),
        name="cast_weight_halves",
    )(w, w)


def _layer_weights(l, P):
    depth, d, _ = P['w_in'].shape
    d_ff = P['w_ffn_down'].shape[1]
    f_pad = -(-d_ff // 512) * 512
    bw = d // 2
    assert d_ff % V7X_LANES == 0
    w_br = P['w_branch'].reshape(depth, N_BRANCH * bw, d)
    w_up_a, w_up_b = cast_weight_halves(P['w_ffn_up'], l, f_pad)
    return {
        'w_up_a': w_up_a,
        'w_up_b': w_up_b,
        'w_in': cast_weight(P['w_in'], l, rb=d, cb=512),
        'w_mem_kv': cast_weight(P['w_mem_kv'], l, rb=d, cb=512),
        'w_branch': cast_weight(w_br, l, rb=bw, cb=d).reshape(N_BRANCH, bw, d),
        'w_out': cast_weight(P['w_out'], l, rb=d, cb=512),
        'w_down': cast_weight(P['w_ffn_down'], l, rb=512, cb=d, out_rows=f_pad),
        'conv_w': _pad_cols(P['ffn_conv_w'][l], f_pad),
        'conv_b': _pad_cols(P['ffn_conv_b'][l].reshape(1, d_ff), f_pad),
        'f_pad': f_pad,
        'd_ff': d_ff,
    }


def _mixer_inputs(h, W, P, l, rope_tab, tm, final=None):
    gains = (P['a_q_norm_g'][l], P['a_k_norm_g'][l], P['b_q_norm_g'][l], P['b_k_norm_g'][l], P['m_q_norm_g'][l])
    return proj_in(h, W['w_in'], gains, rope_tab, tm=tm, final=final)


def _finish_layer(x, h, outs, W, P, l, state, b, t, tm):
    d = x.shape[1]
    gate_b = P['gate_b'][l].reshape(N_BRANCH, 1, d)
    merged = merge_branches(h, *outs, W['w_in'], 7 * (d // 2), gate_b, W['w_branch'], tm=tm)
    res_spec = lambda tn: pl.BlockSpec((tm, tn), lambda i, j: (i, j))
    x, h = proj_out_norm(merged, W['w_out'], x, P['norm_ffn_g'][l], tm=tm)
    g, conv_new = ffn_up(h, W['w_up_a'], W['w_up_b'], W['conv_w'], W['conv_b'], state, b, t, tm=tm)
    x = _proj_call(_proj_residual_kernel, g, W['w_down'], 0, d, F32, [(x, res_spec(512))], tm=tm, name="ffn_down")
    return x, conv_new[:, :, :W['d_ff']]


def kernel(x_prompt, x_sample, cache_a_k, cache_a_v, cache_b_k, cache_b_v, cache_mem_k, cache_mem_v, state_ffn_conv, mem_prompt, norm_mix_g, w_in, a_q_norm_g, a_k_norm_g, a_rel_bias, b_q_norm_g, b_k_norm_g, b_lam_q1, b_lam_k1, b_lam_q2, b_lam_k2, b_subln_g, m_q_norm_g, m_k_norm_g, mem_norm_g, w_mem_kv, gate_b, w_branch, w_out, norm_ffn_g, w_ffn_up, ffn_conv_w, ffn_conv_b, w_ffn_down):
    P = {'w_in': w_in, 'a_q_norm_g': a_q_norm_g, 'a_k_norm_g': a_k_norm_g, 'b_q_norm_g': b_q_norm_g,
         'b_k_norm_g': b_k_norm_g, 'm_q_norm_g': m_q_norm_g, 'm_k_norm_g': m_k_norm_g, 'w_mem_kv': w_mem_kv,
         'gate_b': gate_b, 'w_branch': w_branch, 'w_out': w_out, 'norm_ffn_g': norm_ffn_g,
         'w_ffn_up': w_ffn_up, 'ffn_conv_w': ffn_conv_w, 'ffn_conv_b': ffn_conv_b, 'w_ffn_down': w_ffn_down}
    bp, tp, d = x_prompt.shape
    bs, ts, _ = x_sample.shape
    depth = w_in.shape[0]
    bw = d // 2
    past = cache_b_k.shape[2]
    a_len = cache_a_k.shape[2]
    n_mem = mem_prompt.shape[1]
    a_keep = min(BAND_PAST, tp)
    h_a = bw // HD_A
    h_b = bw // (2 * DIFF_HD)
    hd_m = bw // H_M
    mp, ms = bp * tp, bs * ts
    tm_p = _row_tile(mp, 1024)
    tm_s = _row_tile(ms, 1024)
    assert tm_p <= tp and tp % tm_p == 0 or tm_p % tp == 0

    pos_s = past + np.arange(ts)
    key_pos_a = np.concatenate([past - a_len + np.arange(a_len), pos_s])
    q_chunk_s = pos_s // CHUNK
    k_chunk_a = key_pos_a // CHUNK
    valid_a_s = (k_chunk_a[None, :] <= q_chunk_s[:, None]) & (k_chunk_a[None, :] >= q_chunk_s[:, None] - BAND_CHUNKS)
    mask_a_s = jnp.asarray(np.where(valid_a_s, 0.0, NEG_INF), F32)
    key_pos_b = np.concatenate([np.arange(past), pos_s])
    valid_b_s = (key_pos_b // CHUNK)[None, :] <= q_chunk_s[:, None]
    assert valid_b_s.all(), "sample queries are expected to see every cached and new differential key"

    rope_p = _rope_table(jnp.arange(max(tp, tm_p), dtype=jnp.int32) % tp)
    rope_s = _rope_table(past + (jnp.arange(max(ts, tm_s), dtype=jnp.int32) % ts))

    xp = x_prompt.reshape(mp, d)
    xs = x_sample.reshape(ms, d)
    mem2d = mem_prompt.reshape(bp * n_mem, d)
    outs = {k: [] for k in ('mk_p', 'mv_p', 'cv_p', 'ak_s', 'av_s', 'bk_s', 'bv_s', 'cv_s')}
    kv_prompt = None
    for l in range(depth):
        W = _layer_weights(l, P)
        lam_init = 0.8 - 0.6 * math.exp(-0.3 * l)
        lam_params = jnp.stack([b_lam_q1[l], b_lam_k1[l], b_lam_q2[l], b_lam_k2[l]]).astype(F32)
        bias_row = _rel_bias_row(a_rel_bias[l])

        h = rmsnorm_cast(xp, norm_mix_g[l])
        final = dict(batch=bp, seq_len=tp, a_keep=a_keep, layer=l, depth=depth, prev=kv_prompt)
        qa, ka, va, qb, kb, vb, qm, *kv_prompt = _mixer_inputs(h, W, P, l, rope_p, tm_p, final)
        oa = attn_a_prompt(qa, ka, va, bias_row, bp, tp)
        ob = attn_b_prompt(qb, kb, vb, lam_params, b_subln_g[l], lam_init, bp, tp)
        hm = rmsnorm_cast(mem2d, mem_norm_g[l])
        tm_m = _row_tile(bp * n_mem, 1024)
        mk = _proj_call(functools.partial(_proj_headnorm_kernel, hd=hd_m), hm, W['w_mem_kv'], 0, bw, F32,
                        [(m_k_norm_g[l].reshape(1, hd_m), pl.BlockSpec((1, hd_m), lambda i, j: (0, 0)))],
                        tm=tm_m, name="proj_mk")
        mv = _proj_call(_proj_plain_kernel, hm, W['w_mem_kv'], bw, bw, F32, [], tm=tm_m, name="proj_mv")
        om = attn_m_prompt(qm, mk, mv, bp, tp)
        zeros_state = jnp.zeros((bp, CONV_W - 1, W['f_pad']), F32)
        xp, conv_new = _finish_layer(xp, h, (oa, ob, om), W, P, l, zeros_state, bp, tp, tm_p)
        outs['mk_p'].append(mk.reshape(bp, n_mem, H_M, hd_m))
        outs['mv_p'].append(mv.reshape(bp, n_mem, H_M, hd_m))
        outs['cv_p'].append(conv_new)

        h = rmsnorm_cast(xs, norm_mix_g[l])
        qa, ka, va, qb, kb, vb, qm = _mixer_inputs(h, W, P, l, rope_s, tm_s)
        oa = attn_a_sample(qa, ka, va, cache_a_k, cache_a_v, l, bias_row, mask_a_s, bs, ts)
        ob = attn_b_sample(qb, kb, vb, cache_b_k, cache_b_v, l, lam_params, b_subln_g[l], lam_init, bs, ts)
        om = attn_m_sample(qm, cache_mem_k, cache_mem_v, l, bs, ts)
        state = _pad_cols(state_ffn_conv[l], W['f_pad'])
        xs, conv_new = _finish_layer(xs, h, (oa, ob, om), W, P, l, state, bs, ts, tm_s)
        outs['ak_s'].append(ka.reshape(bs, ts, h_a, HD_A))
        outs['av_s'].append(va.reshape(bs, ts, h_a, HD_A))
        outs['bk_s'].append(kb.reshape(bs, ts, h_b, 2 * DIFF_HD))
        outs['bv_s'].append(vb.reshape(bs, ts, h_b, 2 * DIFF_HD))
        outs['cv_s'].append(conv_new)

    stack = lambda k: jnp.stack(outs[k])
    ak_p, av_p, bk_p, bv_p = kv_prompt
    return (xp.reshape(bp, tp, d), xs.reshape(bs, ts, d),
            ak_p, av_p, bk_p, bv_p, stack('mk_p'), stack('mv_p'), stack('cv_p'),
            stack('ak_s'), stack('av_s'), stack('bk_s'), stack('bv_s'), stack('cv_s'))
```

```python
import functools
import math

import numpy as np
import jax
import jax.numpy as jnp
from jax import lax
from jax.experimental import pallas as pl
from jax.experimental.pallas import tpu as pltpu

F32 = jnp.float32
BF16 = jnp.bfloat16

CHUNK = 64
BAND_CHUNKS = 8
BAND_PAST = BAND_CHUNKS * CHUNK
REL_CLIP = 128
HD_A = 128
DIFF_HD = 64
ROT_DIM = DIFF_HD // 4
ROPE_THETA = 500000.0
H_M = 4
N_BRANCH = 3
CONV_W = 3
EPS = 1e-6
NEG_INF = -1e30
LOG2E = math.log2(math.e)

V7X_LANES = 128
ROW_TILE = 8
FFN_ROW_PARTS = 4
V7X_VMEM_BYTES = 64 * 1024 * 1024
MIB = 1024 * 1024


def _compiler_params(semantics, vmem_estimate_bytes):
    limit = min(int(vmem_estimate_bytes * 1.25) + 8 * MIB, V7X_VMEM_BYTES - 4 * MIB)
    return pltpu.CompilerParams(dimension_semantics=semantics, vmem_limit_bytes=limit)


def _row_tile(m, target):
    t = min(m, target)
    assert m % t == 0, (m, t)
    return t


def _rmsnorm_kernel(x_ref, g_ref, o_ref):
    x = x_ref[...]
    ms = jnp.mean(x * x, axis=-1, keepdims=True)
    o_ref[...] = (x * lax.rsqrt(ms + EPS) * g_ref[...]).astype(o_ref.dtype)


def rmsnorm_cast(x, g):
    m, d = x.shape
    tm = _row_tile(m, 512)
    return pl.pallas_call(
        _rmsnorm_kernel,
        grid=(m // tm,),
        in_specs=[pl.BlockSpec((tm, d), lambda i: (i, 0)), pl.BlockSpec((1, d), lambda i: (0, 0))],
        out_specs=pl.BlockSpec((tm, d), lambda i: (i, 0)),
        out_shape=jax.ShapeDtypeStruct((m, d), BF16),
        compiler_params=_compiler_params(("parallel",), 2 * tm * d * 6),
        name="rmsnorm_cast",
    )(x, g.reshape(1, d))


def _dot(a, b):
    return jnp.dot(a, b, preferred_element_type=F32)


def _dot_nt(a, b):
    return lax.dot_general(a, b, (((1,), (1,)), ((), ())), preferred_element_type=F32)


def _proj_plain_kernel(h_ref, w_ref, o_ref):
    o_ref[...] = _dot(h_ref[...], w_ref[...]).astype(o_ref.dtype)


def _headnorm_store(acc, g, o_ref, hd):
    for k in range(acc.shape[1] // hd):
        s = acc[:, k * hd:(k + 1) * hd]
        ms = jnp.mean(s * s, axis=-1, keepdims=True)
        o_ref[:, k * hd:(k + 1) * hd] = (s * lax.rsqrt(ms + EPS) * g).astype(o_ref.dtype)


def _norm_rope_store(acc, g, tab_ref, o_ref):
    rows, width = acc.shape
    grp_r = lax.broadcasted_iota(jnp.int32, (width, width), 0) // DIFF_HD
    grp_c = lax.broadcasted_iota(jnp.int32, (width, width), 1) // DIFF_HD
    ones_bd = jnp.where(grp_r == grp_c, 1.0, 0.0).astype(BF16)
    ms = _dot((acc * acc).astype(BF16), ones_bd) * (1.0 / DIFF_HD)
    y = acc * lax.rsqrt(ms + EPS)
    cos = tab_ref[:, 0:V7X_LANES]
    sin_up = tab_ref[:, V7X_LANES:2 * V7X_LANES]
    sin_dn = tab_ref[:, 2 * V7X_LANES:3 * V7X_LANES]
    half = ROT_DIM // 2
    for k in range(width // V7X_LANES):
        yk = y[:, k * V7X_LANES:(k + 1) * V7X_LANES] * g
        out = (yk * cos + pltpu.roll(yk, half, 1) * sin_up
               + pltpu.roll(yk, V7X_LANES - half, 1) * sin_dn)
        o_ref[:, k * V7X_LANES:(k + 1) * V7X_LANES] = out.astype(o_ref.dtype)


def _proj_headnorm_kernel(h_ref, w_ref, g_ref, o_ref, *, hd):
    _headnorm_store(_dot(h_ref[...], w_ref[...]), g_ref[...], o_ref, hd)


PROJ_SLAB = 256


FINAL_GROUPS = (1, 2, 4, 5)


def _proj_in_kernel(h_ref, w_ref, gqa_ref, gka_ref, gqb_ref, gkb_ref, gqm_ref, tab_ref, *rest,
                    blocks_per_group, hd_m, final):
    i = pl.program_id(0)
    j = pl.program_id(1)
    group = j // blocks_per_group
    tm = h_ref.shape[0]
    tn = w_ref.shape[1]
    n_prev = 0 if final is None or final['layer'] == 0 else 4
    qa_ref, ka_ref, va_ref, qb_ref, kb_ref, vb_ref, qm_ref = rest[n_prev:n_prev + 7]
    out_refs = {0: qa_ref, 1: ka_ref, 2: va_ref, 3: qb_ref, 4: kb_ref, 5: vb_ref, 6: qm_ref}
    if final is not None:
        final_refs = dict(zip(FINAL_GROUPS, rest[n_prev + 7:n_prev + 11]))
        stage, zeros, sem, zero_sem = rest[n_prev + 11:]
        heads_per_block = tn // V7X_LANES
        bps = final['seq_len'] // tm
        clear_slots = range(1, final['depth']) if final['layer'] == 0 else ()

        @pl.when((i == 0) & (j == 0))
        def _():
            zeros[...] = jnp.zeros(zeros.shape, zeros.dtype)

        def head_copies(g, c, slot):
            keep = final['a_keep'] if g in (1, 2) else final['seq_len']
            if keep >= tm:
                rows, r_lo, t0 = tm, 0, (i % bps) * tm - (final['seq_len'] - keep)
            else:
                rows, r_lo, t0 = keep, tm - keep, 0
            out = []
            for hh in range(heads_per_block):
                where = (i // bps, pl.ds(t0, rows), c * heads_per_block + hh, slice(None))
                src = stage.at[slot, pl.ds(r_lo, rows), pl.ds(hh * V7X_LANES, V7X_LANES)]
                out.append(pltpu.make_async_copy(src, final_refs[g].at[(final['layer'],) + where], sem.at[hh]))
                for p in clear_slots:
                    out.append(pltpu.make_async_copy(zeros.at[pl.ds(0, rows), :], final_refs[g].at[(p,) + where],
                                                     zero_sem.at[p - 1, hh]))
            return out

        def kept(g):
            keep = final['a_keep'] if g in (1, 2) else final['seq_len']
            return (i % bps) >= bps - max(keep // tm, 1)

        def wait_block(g, c, slot):
            @pl.when(kept(g))
            def _():
                for cp in head_copies(g, c, slot):
                    cp.wait()

    def run(g, epilogue):
        o_ref = out_refs[g]

        def body():
            slabs = [slice(c0, c0 + PROJ_SLAB) for c0 in range(0, tn, PROJ_SLAB)]
            accs = [_dot(h_ref[...], w_ref[:, cols]) for cols in slabs]
            for acc, cols in zip(accs, slabs):
                epilogue(acc, o_ref.at[:, cols])
            if final is None:
                return
            c = j - g * blocks_per_group
            slot = j % 2
            if g in FINAL_GROUPS:
                @pl.when(c > 0)
                def _():
                    wait_block(g, c - 1, 1 - slot)
            if g - 1 in FINAL_GROUPS:
                @pl.when(c == 0)
                def _():
                    wait_block(g - 1, blocks_per_group - 1, 1 - slot)
            if g in FINAL_GROUPS:
                @pl.when(kept(g))
                def _():
                    stage[slot] = o_ref[...]
                    for cp in head_copies(g, c, slot):
                        cp.start()
        return body

    headnorm = lambda g_ref, hd: (lambda acc, o: _headnorm_store(acc, g_ref[...], o, hd))
    norm_rope = lambda g_ref: (lambda acc, o: _norm_rope_store(acc, g_ref[...], tab_ref, o))
    plain = lambda acc, o: o.__setitem__(Ellipsis, acc)
    bodies = (
        run(0, headnorm(gqa_ref, HD_A)), run(1, headnorm(gka_ref, HD_A)), run(2, plain),
        run(3, norm_rope(gqb_ref)), run(4, norm_rope(gkb_ref)), run(5, plain),
        run(6, headnorm(gqm_ref, hd_m)),
    )
    for n, body in enumerate(bodies):
        pl.when(group == n)(body)


def proj_in(h, w_in, gains, rope_tab, *, tm, tn=512, final=None):
    m, d = h.shape
    bw = d // 2
    hd_m = bw // H_M
    tm = _row_tile(m, tm)
    assert bw % tn == 0 and tn % PROJ_SLAB == 0 and PROJ_SLAB % hd_m == 0 and rope_tab.shape[0] % tm == 0
    bpg = bw // tn
    n_tab = rope_tab.shape[0] // tm
    g_a_q, g_a_k, g_b_q, g_b_k, g_m_q = gains
    tile2 = lambda g: jnp.tile(g.reshape(1, -1), (1, 2))
    gain_args = [g_a_q.reshape(1, HD_A), g_a_k.reshape(1, HD_A), tile2(g_b_q), tile2(g_b_k), g_m_q.reshape(1, hd_m)]
    const = lambda a: pl.BlockSpec(a.shape, lambda i, j: (0, 0))

    def out_spec(n):
        return pl.BlockSpec((tm, tn), lambda i, j: (i, jnp.clip(j - n * bpg, 0, bpg - 1)))

    dtypes = (BF16, F32, F32, BF16, F32, F32, BF16)
    out_specs = [out_spec(n) for n in range(7)]
    out_shape = [jax.ShapeDtypeStruct((m, bw), t) for t in dtypes]
    prev, scratch, aliases, semantics = [], [], {}, ("parallel", "arbitrary")
    n_fixed_inputs = 8
    if final is not None:
        t, keep, batch = final['seq_len'], final['a_keep'], final['batch']
        heads = bw // V7X_LANES
        assert HD_A == V7X_LANES and 2 * DIFF_HD == V7X_LANES and m == batch * t and t % tm == 0
        assert keep % tm == 0 or (keep < tm and keep % 8 == 0)
        prev = list(final['prev'] or ())
        assert len(prev) == (4 if final['layer'] else 0)
        depth = final['depth']
        out_shape += [jax.ShapeDtypeStruct((depth, batch, rows, heads, V7X_LANES), F32) for rows in (keep, keep, t, t)]
        out_specs += [pl.BlockSpec(memory_space=pl.ANY)] * 4
        hpb = tn // V7X_LANES
        scratch = [pltpu.VMEM((2, tm, tn), F32), pltpu.VMEM((tm, V7X_LANES), F32), pltpu.SemaphoreType.DMA((hpb,)),
                   pltpu.SemaphoreType.DMA((max(depth - 1, 1), hpb))]
        aliases = {n_fixed_inputs + k: 7 + k for k in range(len(prev))}
        semantics = ("arbitrary", "arbitrary")
        final = {k: v for k, v in final.items() if k != 'prev'}
    out_bytes = sum(tm * tn * jnp.dtype(t).itemsize for t in dtypes)
    vmem = 2 * (tm * d * 2 + d * tn * 2 + tm * 3 * V7X_LANES * 4 + out_bytes) + 4 * tm * tn * 4
    return pl.pallas_call(
        functools.partial(_proj_in_kernel, blocks_per_group=bpg, hd_m=hd_m, final=final),
        grid=(m // tm, 7 * bpg),
        in_specs=[pl.BlockSpec((tm, d), lambda i, j: (i, 0)), pl.BlockSpec((d, tn), lambda i, j: (0, j))]
                 + [const(g) for g in gain_args]
                 + [pl.BlockSpec((tm, 3 * V7X_LANES), lambda i, j: (i % n_tab, 0))]
                 + [pl.BlockSpec(memory_space=pl.ANY)] * len(prev),
        out_specs=out_specs,
        out_shape=out_shape,
        input_output_aliases=aliases,
        scratch_shapes=scratch,
        compiler_params=_compiler_params(semantics, vmem),
        name="proj_in",
    )(h, w_in, *gain_args, rope_tab, *prev)


def _proj_residual_kernel(h_ref, w_ref, x_ref, o_ref):
    o_ref[...] = x_ref[...] + _dot(h_ref[...], w_ref[...])


def _proj_out_norm_kernel(m_ref, w_ref, x_ref, g_ref, o_ref, h_ref, xrow):
    j = pl.program_id(1)
    y = x_ref[...] + _dot(m_ref[...], w_ref[...])
    o_ref[...] = y
    xrow[j] = y

    @pl.when(j == pl.num_programs(1) - 1)
    def _():
        nj, _, tn = xrow.shape
        ssq = None
        for jj in range(nj):
            xb = xrow[jj]
            part = jnp.sum(xb * xb, axis=-1, keepdims=True)
            ssq = part if ssq is None else ssq + part
        scale = lax.rsqrt(ssq * (1.0 / (nj * tn)) + EPS)
        for jj in range(nj):
            cols = slice(jj * tn, (jj + 1) * tn)
            h_ref[:, cols] = (xrow[jj] * scale * g_ref[:, cols]).astype(h_ref.dtype)


def proj_out_norm(merged, w_out, x, g, *, tm, tn=512):
    m, d = x.shape
    tm = _row_tile(m, tm)
    assert d % tn == 0
    row_spec = pl.BlockSpec((tm, d), lambda i, j: (i, 0))
    blk_spec = pl.BlockSpec((tm, tn), lambda i, j: (i, j))
    vmem = 2 * (tm * d * 2 + d * tn * 2 + 2 * tm * tn * 4 + tm * d * 2) + tm * d * 4 + 2 * tm * tn * 4
    return pl.pallas_call(
        _proj_out_norm_kernel,
        grid=(m // tm, d // tn),
        in_specs=[row_spec, pl.BlockSpec((d, tn), lambda i, j: (0, j)), blk_spec,
                  pl.BlockSpec((1, d), lambda i, j: (0, 0))],
        out_specs=[blk_spec, row_spec],
        out_shape=[jax.ShapeDtypeStruct((m, d), F32), jax.ShapeDtypeStruct((m, d), BF16)],
        scratch_shapes=[pltpu.VMEM((d // tn, tm, tn), F32)],
        compiler_params=_compiler_params(("parallel", "arbitrary"), vmem),
        name="proj_out_norm",
    )(merged, w_out, x, g.reshape(1, d))


def _proj_call(kernel_fn, h, w, col0, ncols, out_dtype, extras, *, tm=1024, tn=512, name):
    m, k = h.shape
    tm = _row_tile(m, tm)
    tn = min(tn, ncols)
    assert ncols % tn == 0 and col0 % tn == 0, (ncols, col0, tn)
    cb = col0 // tn
    in_specs = [pl.BlockSpec((tm, k), lambda i, j: (i, 0)),
                pl.BlockSpec((k, tn), lambda i, j: (0, j + cb))]
    in_specs += [spec for _, spec in extras]
    extra_bytes = sum(int(np.prod(spec.block_shape)) * a.dtype.itemsize for a, spec in extras)
    vmem = 2 * (tm * k * 2 + k * tn * 2 + tm * tn * 4 + extra_bytes) + tm * tn * 8
    return pl.pallas_call(
        kernel_fn,
        grid=(m // tm, ncols // tn),
        in_specs=in_specs,
        out_specs=pl.BlockSpec((tm, tn), lambda i, j: (i, j)),
        out_shape=jax.ShapeDtypeStruct((m, ncols), out_dtype),
        compiler_params=_compiler_params(("parallel", "arbitrary"), vmem),
        name=name,
    )(h, w, *[a for a, _ in extras])


def _rope_table(pos):
    half = ROT_DIM // 2
    inv_freq = jnp.exp(jnp.arange(half, dtype=F32) * (-2.0 * math.log(ROPE_THETA) / ROT_DIM))
    ang = pos.astype(F32)[:, None] * inv_freq[None, :]
    cos = jnp.cos(ang)
    sin = jnp.sin(ang)
    p = pos.shape[0]
    rest = DIFF_HD - ROT_DIM
    c64 = jnp.concatenate([cos, cos, jnp.ones((p, rest), F32)], axis=1)
    up64 = jnp.concatenate([jnp.zeros((p, half), F32), sin, jnp.zeros((p, rest), F32)], axis=1)
    dn64 = jnp.concatenate([-sin, jnp.zeros((p, half + rest), F32)], axis=1)
    return jnp.concatenate([c64, c64, up64, up64, dn64, dn64], axis=1)


A_QBLK = 4 * CHUNK
A_KBLK = BAND_PAST + A_QBLK
A_BIAS_W = 1024


def _rel_bias_row(tab):
    assert A_KBLK + A_QBLK - 1 <= A_BIAS_W
    lo = BAND_PAST - REL_CLIP
    hi = BAND_PAST + REL_CLIP + 1
    rep = lambda col, n: jnp.repeat(tab[:, col:col + 1], n, axis=1)
    row = jnp.concatenate([rep(0, lo), tab, rep(2 * REL_CLIP, A_KBLK - hi), rep(0, A_BIAS_W - A_KBLK)], axis=1)
    return row[:, None, :]


def _toeplitz_bias(row, rows, width):
    full = pltpu.roll(jnp.broadcast_to(row, (rows, A_BIAS_W)), 0, 1, stride=1, stride_axis=0)
    return full[:, :width]


def _softmax_pv(s, v):
    m = jnp.max(s, axis=-1, keepdims=True)
    p = jnp.exp(s - m)
    l = jnp.sum(p, axis=-1, keepdims=True)
    return _dot(p.astype(BF16), v), l


def _attn_a_prompt_kernel(q_ref, k_ref, v_ref, row_ref, o_ref, kb, vb, *, t):
    kscale = (HD_A ** -0.5) * LOG2E
    kb[...] = (k_ref[0] * kscale).astype(BF16)
    vb[:, :HD_A] = v_ref[0].astype(BF16)
    vb[:, HD_A:] = jnp.ones((t, HD_A), BF16)
    qc = lax.broadcasted_iota(jnp.int32, (A_QBLK, A_KBLK), 0) // CHUNK
    kc = lax.broadcasted_iota(jnp.int32, (A_QBLK, A_KBLK), 1) // CHUNK
    inband = (kc >= qc) & (kc <= qc + BAND_CHUNKS)
    bias = jnp.where(inband, _toeplitz_bias(row_ref[0], A_QBLK, A_KBLK) * LOG2E, NEG_INF)
    for i in range(t // A_QBLK):
        r0 = i * A_QBLK
        k0 = max(r0 - BAND_PAST, 0)
        k1 = r0 + A_QBLK
        q = q_ref[0, r0:k1, :]
        s = _dot_nt(q, kb[k0:k1, :]) + bias[:, A_KBLK - (k1 - k0):]
        p = jnp.exp2(s - jnp.max(s, axis=-1, keepdims=True))
        o = _dot(p.astype(BF16), vb[k0:k1, :])
        o_ref[0, r0:k1, :] = (o[:, :HD_A] / o[:, HD_A:]).astype(o_ref.dtype)


def attn_a_prompt(qa, ka, va, bias_row, b, t):
    h = qa.shape[1] // HD_A
    assert t % A_QBLK == 0 and A_QBLK % V7X_LANES == 0
    q3, k3, v3 = (a.reshape(b, t, h * HD_A) for a in (qa, ka, va))
    spec = pl.BlockSpec((1, t, HD_A), lambda bi, hi: (bi, 0, hi))
    vmem = 2 * t * HD_A * (2 + 4 + 4 + 2) + 2 * t * HD_A * 2 + 24 * MIB
    out = pl.pallas_call(
        functools.partial(_attn_a_prompt_kernel, t=t),
        grid=(b, h),
        in_specs=[spec, spec, spec, pl.BlockSpec((1, 1, A_BIAS_W), lambda bi, hi: (hi, 0, 0))],
        out_specs=spec,
        out_shape=jax.ShapeDtypeStruct((b, t, h * HD_A), BF16),
        scratch_shapes=[pltpu.VMEM((t, HD_A), BF16), pltpu.VMEM((t, 2 * HD_A), BF16)],
        compiler_params=_compiler_params(("parallel", "parallel"), vmem),
        name="attn_a_prompt",
    )(q3, k3, v3, bias_row)
    return out.reshape(b * t, h * HD_A)


def _attn_a_sample_kernel(q_ref, kn_ref, vn_ref, row_ref, mask_ref, kc_hbm, vc_hbm, o_ref, kbuf, vbuf, sem,
                          *, layer, heads, a_len):
    b = pl.program_id(0)
    nb = pl.num_programs(0)
    scale = HD_A ** -0.5
    t = q_ref.shape[2]

    def copies(bi, slot):
        out = []
        for h in range(heads):
            out.append(pltpu.make_async_copy(kc_hbm.at[layer, bi, :, h, :], kbuf.at[slot, h], sem.at[0, slot, h]))
            out.append(pltpu.make_async_copy(vc_hbm.at[layer, bi, :, h, :], vbuf.at[slot, h], sem.at[1, slot, h]))
        return out

    @pl.when(b == 0)
    def _():
        for c in copies(b, 0):
            c.start()

    slot = b % 2

    @pl.when(b + 1 < nb)
    def _():
        for c in copies(b + 1, 1 - slot):
            c.start()

    for c in copies(b, slot):
        c.wait()
    for h in range(heads):
        q = q_ref[0, h]
        bias = _toeplitz_bias(row_ref[h], t, a_len + t) + mask_ref[...]
        sc = _dot_nt(q, kbuf[slot, h].astype(BF16)) * scale + bias[:, :a_len]
        sn = _dot_nt(q, kn_ref[0, h].astype(BF16)) * scale + bias[:, a_len:]
        m = jnp.maximum(jnp.max(sc, axis=-1, keepdims=True), jnp.max(sn, axis=-1, keepdims=True))
        pc = jnp.exp(sc - m)
        pn = jnp.exp(sn - m)
        l = jnp.sum(pc, axis=-1, keepdims=True) + jnp.sum(pn, axis=-1, keepdims=True)
        o = _dot(pc.astype(BF16), vbuf[slot, h].astype(BF16)) + _dot(pn.astype(BF16), vn_ref[0, h].astype(BF16))
        o_ref[0, h] = (o / l).astype(o_ref.dtype)


def _by_head(a, b, t, heads, hd):
    return a.reshape(b, t, heads, hd).transpose(0, 2, 1, 3)


def attn_a_sample(qa, ka, va, cache_k, cache_v, layer, bias_row, mask, b, t):
    heads = qa.shape[1] // HD_A
    a_len = cache_k.shape[2]
    assert a_len == BAND_PAST and a_len + t <= A_KBLK
    head_spec = pl.BlockSpec((1, heads, t, HD_A), lambda bi: (bi, 0, 0, 0))
    vmem = 2 * 2 * heads * a_len * HD_A * 4 + 2 * 4 * heads * t * HD_A * 4 + 16 * MIB
    out = pl.pallas_call(
        functools.partial(_attn_a_sample_kernel, layer=layer, heads=heads, a_len=a_len),
        grid=(b,),
        in_specs=[head_spec, head_spec, head_spec,
                  pl.BlockSpec(bias_row.shape, lambda bi: (0, 0, 0)),
                  pl.BlockSpec(mask.shape, lambda bi: (0, 0)),
                  pl.BlockSpec(memory_space=pl.ANY), pl.BlockSpec(memory_space=pl.ANY)],
        out_specs=head_spec,
        out_shape=jax.ShapeDtypeStruct((b, heads, t, HD_A), BF16),
        scratch_shapes=[pltpu.VMEM((2, heads, a_len, HD_A), F32), pltpu.VMEM((2, heads, a_len, HD_A), F32),
                        pltpu.SemaphoreType.DMA((2, 2, heads))],
        compiler_params=_compiler_params(("arbitrary",), vmem),
        name="attn_a_sample",
    )(*(_by_head(a, b, t, heads, HD_A) for a in (qa, ka, va)), bias_row, mask, cache_k, cache_v)
    return out.transpose(0, 2, 1, 3).reshape(b * t, heads * HD_A)


def _diff_lambda(lam_ref, lam_init):
    v = lam_ref[...]
    d1 = jnp.sum(v[0:1] * v[1:2], axis=-1, keepdims=True)
    d2 = jnp.sum(v[2:3] * v[3:4], axis=-1, keepdims=True)
    return jnp.exp(d1) - jnp.exp(d2) + lam_init


def _split_diff_queries(q):
    lane = lax.broadcasted_iota(jnp.int32, q.shape, 1)
    qs = q * jnp.asarray(DIFF_HD ** -0.5, q.dtype)
    zero = jnp.zeros_like(qs)
    return jnp.where(lane < DIFF_HD, qs, zero), jnp.where(lane >= DIFF_HD, qs, zero)


def _stack_diff_queries(q):
    return jnp.concatenate(_split_diff_queries(q), axis=0)


def _diff_post(o, g, post_scale):
    ms = jnp.mean(o * o, axis=-1, keepdims=True)
    return (o * lax.rsqrt(ms + EPS) * g) * post_scale


def _diff_finish(l, acc, lam, g, post_scale, tq):
    o = acc[:tq] / l[:tq] - lam * (acc[tq:] / l[tq:])
    return _diff_post(o, g, post_scale)


B_TQ = 8 * CHUNK


def _online_step(carry, s, v_ext):
    m, acc = carry
    m_new = jnp.maximum(m, jnp.max(s, axis=-1, keepdims=True))
    alpha = jnp.exp2(m - m_new)
    p = jnp.exp2(s - m_new)
    acc = alpha * acc + _dot(p.astype(BF16), v_ext)
    return m_new, acc


def _attn_b_prompt_kernel(lam_ref, q_ref, k_ref, v_ref, g_ref, o_ref, kb, vb, *, t, lam_init):
    tq = B_TQ
    hd = 2 * DIFF_HD
    kb[...] = (k_ref[0] * LOG2E).astype(BF16)
    vb[:, :hd] = v_ref[0].astype(BF16)
    vb[:, hd:] = jnp.ones((t, hd), BF16)
    lam = _diff_lambda(lam_ref, lam_init)
    row = lax.broadcasted_iota(jnp.int32, (tq, tq), 0)
    col = lax.broadcasted_iota(jnp.int32, (tq, tq), 1)
    diag_ok = (col // CHUNK) <= (row // CHUNK)
    for qi in range(t // tq):
        q0 = qi * tq
        qs = _split_diff_queries(q_ref[0, q0:q0 + tq, :])
        spans = ([(0, q0, False)] if q0 else []) + [(q0, tq, True)]
        outs = []
        for c in range(2):
            carry = (jnp.full((tq, 1), NEG_INF, F32), jnp.zeros((tq, 2 * hd), F32))
            for k0, width, masked in spans:
                s = _dot_nt(qs[c], kb[k0:k0 + width, :])
                if masked:
                    s = jnp.where(diag_ok, s, NEG_INF)
                carry = _online_step(carry, s, vb[k0:k0 + width, :])
            outs.append(carry[1][:, :hd] / carry[1][:, hd:])
        o = outs[0] - lam * outs[1]
        o_ref[0, q0:q0 + tq, :] = _diff_post(o, g_ref[...], 1.0 - lam_init).astype(o_ref.dtype)


def attn_b_prompt(qb, kb, vb, lam_params, subln_g, lam_init, b, t):
    hd = 2 * DIFF_HD
    heads = qb.shape[1] // hd
    assert t % B_TQ == 0
    q3, k3, v3 = (a.reshape(b, t, heads * hd) for a in (qb, kb, vb))
    spec = pl.BlockSpec((1, t, hd), lambda bi, hi: (bi, 0, hi))
    vmem = 2 * t * hd * (2 + 4 + 4 + 2) + 2 * t * hd * 2 + 32 * MIB
    out = pl.pallas_call(
        functools.partial(_attn_b_prompt_kernel, t=t, lam_init=lam_init),
        grid=(b, heads),
        in_specs=[pl.BlockSpec(lam_params.shape, lambda bi, hi: (0, 0)), spec, spec, spec,
                  pl.BlockSpec((1, hd), lambda bi, hi: (0, 0))],
        out_specs=spec,
        out_shape=jax.ShapeDtypeStruct((b, t, heads * hd), BF16),
        scratch_shapes=[pltpu.VMEM((t, hd), BF16), pltpu.VMEM((t, 2 * hd), BF16)],
        compiler_params=_compiler_params(("parallel", "parallel"), vmem),
        name="attn_b_prompt",
    )(lam_params, q3, k3, v3, subln_g.reshape(1, hd))
    return out.reshape(b * t, heads * hd)


def _attn_b_sample_kernel(lam_ref, q_ref, kn_ref, vn_ref, g_ref, kc_hbm, vc_hbm, o_ref, kbuf, vbuf, sem,
                          *, layer, heads, t, lam_init):
    b = pl.program_id(0)
    nb = pl.num_programs(0)

    def copies(bi, h, slot):
        return (pltpu.make_async_copy(kc_hbm.at[layer, bi, :, h, :], kbuf.at[slot], sem.at[0, slot]),
                pltpu.make_async_copy(vc_hbm.at[layer, bi, :, h, :], vbuf.at[slot], sem.at[1, slot]))

    def start(bi, h, slot):
        for c in copies(bi, h, slot):
            c.start()

    @pl.when(b == 0)
    def _():
        start(b, 0, 0)

    lam = _diff_lambda(lam_ref, lam_init)
    for h in range(heads):
        slot = h % 2
        if h + 1 < heads:
            start(b, h + 1, 1 - slot)
        else:
            @pl.when(b + 1 < nb)
            def _():
                start(b + 1, 0, 1 - slot)
        for c in copies(b, h, slot):
            c.wait()
        q2 = _stack_diff_queries(q_ref[0, h])
        sc = _dot_nt(q2, kbuf[slot].astype(BF16))
        sn = _dot_nt(q2, kn_ref[0, h].astype(BF16))
        m = jnp.maximum(jnp.max(sc, axis=-1, keepdims=True), jnp.max(sn, axis=-1, keepdims=True))
        pc = jnp.exp(sc - m)
        pn = jnp.exp(sn - m)
        l = jnp.sum(pc, axis=-1, keepdims=True) + jnp.sum(pn, axis=-1, keepdims=True)
        acc = _dot(pc.astype(BF16), vbuf[slot].astype(BF16)) + _dot(pn.astype(BF16), vn_ref[0, h].astype(BF16))
        o_ref[0, h] = _diff_finish(l, acc, lam, g_ref[...], 1.0 - lam_init, t).astype(o_ref.dtype)


def attn_b_sample(qb, kb, vb, cache_k, cache_v, layer, lam_params, subln_g, lam_init, b, t):
    hd = 2 * DIFF_HD
    heads = qb.shape[1] // hd
    assert heads % 2 == 0
    past = cache_k.shape[2]
    head_spec = pl.BlockSpec((1, heads, t, hd), lambda bi: (bi, 0, 0, 0))
    vmem = 2 * 2 * past * hd * 4 + 2 * 3 * heads * t * hd * 4 + 12 * 2 * t * past * 4
    out = pl.pallas_call(
        functools.partial(_attn_b_sample_kernel, layer=layer, heads=heads, t=t, lam_init=lam_init),
        grid=(b,),
        in_specs=[pl.BlockSpec(lam_params.shape, lambda bi: (0, 0)), head_spec, head_spec, head_spec,
                  pl.BlockSpec((1, hd), lambda bi: (0, 0)),
                  pl.BlockSpec(memory_space=pl.ANY), pl.BlockSpec(memory_space=pl.ANY)],
        out_specs=head_spec,
        out_shape=jax.ShapeDtypeStruct((b, heads, t, hd), BF16),
        scratch_shapes=[pltpu.VMEM((2, past, hd), F32), pltpu.VMEM((2, past, hd), F32),
                        pltpu.SemaphoreType.DMA((2, 2))],
        compiler_params=_compiler_params(("arbitrary",), vmem),
        name="attn_b_sample",
    )(lam_params, *(_by_head(a, b, t, heads, hd) for a in (qb, kb, vb)), subln_g.reshape(1, hd), cache_k, cache_v)
    return out.transpose(0, 2, 1, 3).reshape(b * t, heads * hd)


def _attn_m_kernel(q_ref, k_ref, v_ref, o_ref, *, heads, hd):
    scale = hd ** -0.5
    for h in range(heads):
        sl = slice(h * hd, (h + 1) * hd)
        s = _dot_nt(q_ref[0, :, sl], k_ref[0, :, sl].astype(BF16)) * scale
        o, l = _softmax_pv(s, v_ref[0, :, sl].astype(BF16))
        o_ref[0, :, sl] = (o / l).astype(o_ref.dtype)


def attn_m_prompt(qm, mem_k, mem_v, b, t, *, tq=512):
    width = qm.shape[1]
    hd = width // H_M
    tq = min(tq, t)
    assert t % tq == 0
    n = mem_k.shape[0] // b
    q_spec = pl.BlockSpec((1, tq, width), lambda bi, qi: (bi, qi, 0))
    kv_spec = pl.BlockSpec((1, n, width), lambda bi, qi: (bi, 0, 0))
    vmem = 2 * (2 * n * width * 4 + 2 * tq * width * 2) + 8 * tq * n * 4
    out = pl.pallas_call(
        functools.partial(_attn_m_kernel, heads=H_M, hd=hd),
        grid=(b, t // tq),
        in_specs=[q_spec, kv_spec, kv_spec],
        out_specs=q_spec,
        out_shape=jax.ShapeDtypeStruct((b, t, width), BF16),
        compiler_params=_compiler_params(("parallel", "arbitrary"), vmem),
        name="attn_m_prompt",
    )(qm.reshape(b, t, width), mem_k.reshape(b, n, width), mem_v.reshape(b, n, width))
    return out.reshape(b * t, width)


def _attn_m_sample_kernel(q_ref, kc_hbm, vc_hbm, o_ref, kbuf, vbuf, sem, *, layer, heads, hd):
    b = pl.program_id(0)
    nb = pl.num_programs(0)
    scale = hd ** -0.5

    def copies(bi, slot):
        out = []
        for h in range(heads):
            out.append(pltpu.make_async_copy(kc_hbm.at[layer, bi, :, h, :], kbuf.at[slot, h], sem.at[0, slot, h]))
            out.append(pltpu.make_async_copy(vc_hbm.at[layer, bi, :, h, :], vbuf.at[slot, h], sem.at[1, slot, h]))
        return out

    @pl.when(b == 0)
    def _():
        for c in copies(b, 0):
            c.start()

    slot = b % 2

    @pl.when(b + 1 < nb)
    def _():
        for c in copies(b + 1, 1 - slot):
            c.start()

    for c in copies(b, slot):
        c.wait()
    for h in range(heads):
        s = _dot_nt(q_ref[0, h], kbuf[slot, h].astype(BF16)) * scale
        o, l = _softmax_pv(s, vbuf[slot, h].astype(BF16))
        o_ref[0, h] = (o / l).astype(o_ref.dtype)


def attn_m_sample(qm, cache_k, cache_v, layer, b, t):
    n, heads, hd = cache_k.shape[2:]
    head_spec = pl.BlockSpec((1, heads, t, hd), lambda bi: (bi, 0, 0, 0))
    vmem = 2 * 2 * heads * n * hd * 4 + 2 * 2 * heads * t * hd * 2 + 8 * MIB
    out = pl.pallas_call(
        functools.partial(_attn_m_sample_kernel, layer=layer, heads=heads, hd=hd),
        grid=(b,),
        in_specs=[head_spec, pl.BlockSpec(memory_space=pl.ANY), pl.BlockSpec(memory_space=pl.ANY)],
        out_specs=head_spec,
        out_shape=jax.ShapeDtypeStruct((b, heads, t, hd), BF16),
        scratch_shapes=[pltpu.VMEM((2, heads, n, hd), F32), pltpu.VMEM((2, heads, n, hd), F32),
                        pltpu.SemaphoreType.DMA((2, 2, heads))],
        compiler_params=_compiler_params(("arbitrary",), vmem),
        name="attn_m_sample",
    )(_by_head(qm, b, t, heads, hd), cache_k, cache_v)
    return out.transpose(0, 2, 1, 3).reshape(b * t, heads * hd)


def _merge_kernel(h_ref, oa_ref, ob_ref, om_ref, wga_ref, wgb_ref, wgm_ref, gb_ref, wbr_ref, o_ref):
    h = h_ref[...]
    acc = None
    for n, (o_n, wg_n) in enumerate(((oa_ref, wga_ref), (ob_ref, wgb_ref), (om_ref, wgm_ref))):
        gate = jax.nn.sigmoid(_dot(h, wg_n[...]) + gb_ref[n])
        term = gate * _dot(o_n[...], wbr_ref[n])
        acc = term if acc is None else acc + term
    o_ref[...] = acc.astype(o_ref.dtype)


def merge_branches(h, oa, ob, om, w_in, gate_col0, gate_b, w_br, *, tm, tn=512):
    m, bw = oa.shape
    d = w_br.shape[2]
    tm = _row_tile(m, tm)
    assert d % tn == 0 and gate_col0 % tn == 0
    nj = d // tn
    g0 = gate_col0 // tn
    row_spec = lambda width: pl.BlockSpec((tm, width), lambda i, j: (i, 0))
    gate_w_specs = [pl.BlockSpec((d, tn), functools.partial(lambda i, j, n: (0, g0 + n * nj + j), n=n))
                    for n in range(N_BRANCH)]
    vmem = 2 * (tm * d * 2 + 3 * tm * bw * 2 + 3 * d * tn * 2 + 3 * bw * tn * 2 + tm * tn * 2) + 6 * tm * tn * 4
    return pl.pallas_call(
        _merge_kernel,
        grid=(m // tm, nj),
        in_specs=[row_spec(d), row_spec(bw), row_spec(bw), row_spec(bw)] + gate_w_specs
                 + [pl.BlockSpec((N_BRANCH, 1, tn), lambda i, j: (0, 0, j)),
                    pl.BlockSpec((N_BRANCH, bw, tn), lambda i, j: (0, 0, j))],
        out_specs=pl.BlockSpec((tm, tn), lambda i, j: (i, j)),
        out_shape=jax.ShapeDtypeStruct((m, d), BF16),
        compiler_params=_compiler_params(("parallel", "arbitrary"), vmem),
        name="merge_branches",
    )(h, oa, ob, om, w_in, w_in, w_in, gate_b, w_br)


def _ffn_up_kernel(h_ref, wa_ref, wb_ref, cw_ref, cb_ref, st_ref, g_ref, cn_ref, carry, *, nb, tb, blocks_per_seq):
    i = pl.program_id(0)
    j = pl.program_id(1)
    tm, tn = g_ref.shape
    if nb == 1:
        @pl.when((i % blocks_per_seq) == 0)
        def _():
            carry[j] = st_ref[0]

        trow = lax.broadcasted_iota(jnp.int32, (ROW_TILE, PROJ_SLAB), 0)
    else:
        trow = lax.broadcasted_iota(jnp.int32, (nb, tb, PROJ_SLAB), 1).reshape(tm, PROJ_SLAB)
    slabs = [slice(c0, c0 + PROJ_SLAB) for c0 in range(0, tn, PROJ_SLAB)]

    def gated(a, am1, am2, bgate, cols):
        cw = cw_ref[:, cols]
        c = cb_ref[:, cols] + am2 * cw[0:1] + am1 * cw[1:2] + a * cw[2:3]
        gelu = 0.5 * c * (1.0 + lax.erf(c * (2.0 ** -0.5)))
        return (gelu * bgate).astype(g_ref.dtype)

    if nb == 1:
        parts = FFN_ROW_PARTS if tm % (FFN_ROW_PARTS * ROW_TILE) == 0 else 1
        rp = tm // parts
        units = [(r0, cols) for cols in slabs for r0 in range(0, tm, rp)]
        dots = [(_dot(h_ref[r0:r0 + rp, :], wa_ref[:, cols]), _dot(h_ref[r0:r0 + rp, :], wb_ref[:, cols]))
                for r0, cols in units]
        prev = {}
        for (r0, cols), (a, bgate) in zip(units, dots):
            g_ref[r0:r0 + rp, cols] = gated(a, pltpu.roll(a, 1, 0), pltpu.roll(a, 2, 0), bgate, cols)
            before = carry[j, :, cols] if r0 == 0 else prev[cols.start]
            top = a[0:ROW_TILE]
            p0 = jnp.broadcast_to(before[0:1], top.shape)
            p1 = jnp.broadcast_to(before[1:2], top.shape)
            am1 = jnp.where(trow == 0, p1, pltpu.roll(top, 1, 0))
            am2 = jnp.where(trow == 0, p0, jnp.where(trow == 1, p1, pltpu.roll(top, 2, 0)))
            g_ref[r0:r0 + ROW_TILE, cols] = gated(top, am1, am2, bgate[0:ROW_TILE], cols)
            prev[cols.start] = a[rp - 2:rp]
            if r0 + rp == tm:
                carry[j, :, cols] = a[rp - 2:rp]
                cn_ref[0, :, cols] = a[rp - 2:rp]
        return

    h = h_ref[...]
    dots = [(_dot(h, wa_ref[:, cols]), _dot(h, wb_ref[:, cols])) for cols in slabs]
    for cols, (a, bgate) in zip(slabs, dots):
        st = st_ref[:, :, cols]
        p0 = jnp.broadcast_to(st[:, 0:1, :], (nb, tb, PROJ_SLAB)).reshape(tm, PROJ_SLAB)
        p1 = jnp.broadcast_to(st[:, 1:2, :], (nb, tb, PROJ_SLAB)).reshape(tm, PROJ_SLAB)
        cn_ref[:, :, cols] = a.reshape(nb, tb, PROJ_SLAB)[:, tb - 2:tb, :]
        am1 = jnp.where(trow == 0, p1, pltpu.roll(a, 1, 0))
        am2 = jnp.where(trow == 0, p0, jnp.where(trow == 1, p1, pltpu.roll(a, 2, 0)))
        g_ref[:, cols] = gated(a, am1, am2, bgate, cols)


def ffn_up(h, w_a, w_b, conv_w, conv_b, state, b, t, *, tm=1024, tn=512):
    m, d = h.shape
    f = w_a.shape[1]
    tm = _row_tile(m, tm)
    assert f % tn == 0 and tn % PROJ_SLAB == 0
    if tm >= t:
        assert tm % t == 0
        nb, tb, blocks_per_seq = tm // t, t, 1
    else:
        assert t % tm == 0
        nb, tb, blocks_per_seq = 1, tm, t // tm
    if nb > 1:
        seq_map = lambda i, j: (i, 0, j)
    else:
        seq_map = lambda i, j: (i // blocks_per_seq, 0, j)
    tail_map = lambda i, j: (i, 0, j)
    w_spec = pl.BlockSpec((d, tn), lambda i, j: (0, j))
    vmem = 2 * (tm * d * 2 + 2 * d * tn * 2 + tm * tn * 2 + 2 * nb * 8 * tn * 4) + 8 * tm * tn * 4
    g, conv_new = pl.pallas_call(
        functools.partial(_ffn_up_kernel, nb=nb, tb=tb, blocks_per_seq=blocks_per_seq),
        grid=(m // tm, f // tn),
        in_specs=[pl.BlockSpec((tm, d), lambda i, j: (i, 0)), w_spec, w_spec,
                  pl.BlockSpec((CONV_W, tn), lambda i, j: (0, j)),
                  pl.BlockSpec((1, tn), lambda i, j: (0, j)),
                  pl.BlockSpec((nb, CONV_W - 1, tn), seq_map)],
        out_specs=[pl.BlockSpec((tm, tn), lambda i, j: (i, j)),
                   pl.BlockSpec((nb, CONV_W - 1, tn), tail_map)],
        out_shape=[jax.ShapeDtypeStruct((m, f), BF16),
                   jax.ShapeDtypeStruct((b * blocks_per_seq, CONV_W - 1, f), F32)],
        scratch_shapes=[pltpu.VMEM((f // tn, CONV_W - 1, tn), F32)],
        compiler_params=_compiler_params(("arbitrary", "arbitrary"), vmem),
        name="ffn_up",
    )(h, w_a, w_b, conv_w, conv_b, state)
    return g, conv_new.reshape(b, blocks_per_seq, CONV_W - 1, f)[:, -1]


def _pad_cols(a, f_pad):
    return jnp.pad(a, [(0, 0)] * (a.ndim - 1) + [(0, f_pad - a.shape[-1])])


def _cast_kernel(w_ref, o_ref, *, valid_rows, valid_cols):
    rb, cb = o_ref.shape
    rows = pl.program_id(0) * rb + lax.broadcasted_iota(jnp.int32, (rb, cb), 0)
    cols = pl.program_id(1) * cb + lax.broadcasted_iota(jnp.int32, (rb, cb), 1)
    ok = (rows < valid_rows) & (cols < valid_cols)
    o_ref[...] = jnp.where(ok, w_ref[...], 0.0).astype(o_ref.dtype)


def cast_weight(w, layer, *, rb, cb, col0=0, ncols=None, out_rows=None, out_cols=None):
    _, r, c = w.shape
    ncols = c - col0 if ncols is None else ncols
    out_rows = r if out_rows is None else out_rows
    out_cols = ncols if out_cols is None else out_cols
    assert col0 % cb == 0 and out_rows % rb == 0 and out_cols % cb == 0
    c0 = col0 // cb
    last_r = (r - 1) // rb
    last_c = (col0 + ncols - 1) // cb
    in_map = lambda i, j: (layer, jnp.minimum(i, last_r), jnp.minimum(j + c0, last_c))
    return pl.pallas_call(
        functools.partial(_cast_kernel, valid_rows=r, valid_cols=ncols),
        grid=(out_rows // rb, out_cols // cb),
        in_specs=[pl.BlockSpec((None, rb, cb), in_map)],
        out_specs=pl.BlockSpec((rb, cb), lambda i, j: (i, j)),
        out_shape=jax.ShapeDtypeStruct((out_rows, out_cols), BF16),
        compiler_params=_compiler_params(("parallel", "parallel"), 2 * rb * cb * 6),
        name="cast_weight",
    )(w)


def _cast_halves_kernel(wa_ref, wb_ref, oa_ref, ob_ref, *, valid_cols):
    rb, cb = oa_ref.shape
    cols = pl.program_id(0) * cb + lax.broadcasted_iota(jnp.int32, (rb, cb), 1)
    ok = cols < valid_cols
    oa_ref[...] = jnp.where(ok, wa_ref[...], 0.0).astype(oa_ref.dtype)
    ob_ref[...] = jnp.where(ok, wb_ref[...], 0.0).astype(ob_ref.dtype)


def cast_weight_halves(w, layer, out_cols, *, cb=V7X_LANES):
    _, r, c = w.shape
    half = c // 2
    assert half % cb == 0 and out_cols % cb == 0
    nb = half // cb
    in_spec = lambda first: pl.BlockSpec((None, r, cb), lambda j: (layer, 0, first + jnp.minimum(j, nb - 1)))
    out_spec = pl.BlockSpec((r, cb), lambda j: (0, j))
    out = jax.ShapeDtypeStruct((r, out_cols), BF16)
    return pl.pallas_call(
        functools.partial(_cast_halves_kernel, valid_cols=half),
        grid=(out_cols // cb,),
        in_specs=[in_spec(0), in_spec(nb)],
        out_specs=[out_spec, out_spec],
        out_shape=[out, out],
        compiler_params=_compiler_params(("parallel",), 2 * 2 * r * cb * 6),
        name="cast_weight_halves",
    )(w, w)


def _layer_weights(l, P):
    depth, d, _ = P['w_in'].shape
    d_ff = P['w_ffn_down'].shape[1]
    f_pad = -(-d_ff // 512) * 512
    bw = d // 2
    assert d_ff % V7X_LANES == 0
    w_br = P['w_branch'].reshape(depth, N_BRANCH * bw, d)
    w_up_a, w_up_b = cast_weight_halves(P['w_ffn_up'], l, f_pad)
    return {
        'w_up_a': w_up_a,
        'w_up_b': w_up_b,
        'w_in': cast_weight(P['w_in'], l, rb=d, cb=512),
        'w_mem_kv': cast_weight(P['w_mem_kv'], l, rb=d, cb=512),
        'w_branch': cast_weight(w_br, l, rb=bw, cb=d).reshape(N_BRANCH, bw, d),
        'w_out': cast_weight(P['w_out'], l, rb=d, cb=512),
        'w_down': cast_weight(P['w_ffn_down'], l, rb=512, cb=d, out_rows=f_pad),
        'conv_w': _pad_cols(P['ffn_conv_w'][l], f_pad),
        'conv_b': _pad_cols(P['ffn_conv_b'][l].reshape(1, d_ff), f_pad),
        'f_pad': f_pad,
        'd_ff': d_ff,
    }


def _mixer_inputs(h, W, P, l, rope_tab, tm, final=None):
    gains = (P['a_q_norm_g'][l], P['a_k_norm_g'][l], P['b_q_norm_g'][l], P['b_k_norm_g'][l], P['m_q_norm_g'][l])
    return proj_in(h, W['w_in'], gains, rope_tab, tm=tm, final=final)


def _finish_layer(x, h, outs, W, P, l, state, b, t, tm):
    d = x.shape[1]
    gate_b = P['gate_b'][l].reshape(N_BRANCH, 1, d)
    merged = merge_branches(h, *outs, W['w_in'], 7 * (d // 2), gate_b, W['w_branch'], tm=tm)
    res_spec = lambda tn: pl.BlockSpec((tm, tn), lambda i, j: (i, j))
    x, h = proj_out_norm(merged, W['w_out'], x, P['norm_ffn_g'][l], tm=tm)
    g, conv_new = ffn_up(h, W['w_up_a'], W['w_up_b'], W['conv_w'], W['conv_b'], state, b, t, tm=tm)
    x = _proj_call(_proj_residual_kernel, g, W['w_down'], 0, d, F32, [(x, res_spec(512))], tm=tm, name="ffn_down")
    return x, conv_new[:, :, :W['d_ff']]


def kernel(x_prompt, x_sample, cache_a_k, cache_a_v, cache_b_k, cache_b_v, cache_mem_k, cache_mem_v, state_ffn_conv, mem_prompt, norm_mix_g, w_in, a_q_norm_g, a_k_norm_g, a_rel_bias, b_q_norm_g, b_k_norm_g, b_lam_q1, b_lam_k1, b_lam_q2, b_lam_k2, b_subln_g, m_q_norm_g, m_k_norm_g, mem_norm_g, w_mem_kv, gate_b, w_branch, w_out, norm_ffn_g, w_ffn_up, ffn_conv_w, ffn_conv_b, w_ffn_down):
    P = {'w_in': w_in, 'a_q_norm_g': a_q_norm_g, 'a_k_norm_g': a_k_norm_g, 'b_q_norm_g': b_q_norm_g,
         'b_k_norm_g': b_k_norm_g, 'm_q_norm_g': m_q_norm_g, 'm_k_norm_g': m_k_norm_g, 'w_mem_kv': w_mem_kv,
         'gate_b': gate_b, 'w_branch': w_branch, 'w_out': w_out, 'norm_ffn_g': norm_ffn_g,
         'w_ffn_up': w_ffn_up, 'ffn_conv_w': ffn_conv_w, 'ffn_conv_b': ffn_conv_b, 'w_ffn_down': w_ffn_down}
    bp, tp, d = x_prompt.shape
    bs, ts, _ = x_sample.shape
    depth = w_in.shape[0]
    bw = d // 2
    past = cache_b_k.shape[2]
    a_len = cache_a_k.shape[2]
    n_mem = mem_prompt.shape[1]
    a_keep = min(BAND_PAST, tp)
    h_a = bw // HD_A
    h_b = bw // (2 * DIFF_HD)
    hd_m = bw // H_M
    mp, ms = bp * tp, bs * ts
    tm_p = _row_tile(mp, 1024)
    tm_s = _row_tile(ms, 1024)
    assert tm_p <= tp and tp % tm_p == 0 or tm_p % tp == 0

    pos_s = past + np.arange(ts)
    key_pos_a = np.concatenate([past - a_len + np.arange(a_len), pos_s])
    q_chunk_s = pos_s // CHUNK
    k_chunk_a = key_pos_a // CHUNK
    valid_a_s = (k_chunk_a[None, :] <= q_chunk_s[:, None]) & (k_chunk_a[None, :] >= q_chunk_s[:, None] - BAND_CHUNKS)
    mask_a_s = jnp.asarray(np.where(valid_a_s, 0.0, NEG_INF), F32)
    key_pos_b = np.concatenate([np.arange(past), pos_s])
    valid_b_s = (key_pos_b // CHUNK)[None, :] <= q_chunk_s[:, None]
    assert valid_b_s.all(), "sample queries are expected to see every cached and new differential key"

    rope_p = _rope_table(jnp.arange(max(tp, tm_p), dtype=jnp.int32) % tp)
    rope_s = _rope_table(past + (jnp.arange(max(ts, tm_s), dtype=jnp.int32) % ts))

    xp = x_prompt.reshape(mp, d)
    xs = x_sample.reshape(ms, d)
    mem2d = mem_prompt.reshape(bp * n_mem, d)
    outs = {k: [] for k in ('mk_p', 'mv_p', 'cv_p', 'ak_s', 'av_s', 'bk_s', 'bv_s', 'cv_s')}
    kv_prompt = None
    for l in range(depth):
        W = _layer_weights(l, P)
        lam_init = 0.8 - 0.6 * math.exp(-0.3 * l)
        lam_params = jnp.stack([b_lam_q1[l], b_lam_k1[l], b_lam_q2[l], b_lam_k2[l]]).astype(F32)
        bias_row = _rel_bias_row(a_rel_bias[l])

        h = rmsnorm_cast(xp, norm_mix_g[l])
        final = dict(batch=bp, seq_len=tp, a_keep=a_keep, layer=l, depth=depth, prev=kv_prompt)
        qa, ka, va, qb, kb, vb, qm, *kv_prompt = _mixer_inputs(h, W, P, l, rope_p, tm_p, final)
        oa = attn_a_prompt(qa, ka, va, bias_row, bp, tp)
        ob = attn_b_prompt(qb, kb, vb, lam_params, b_subln_g[l], lam_init, bp, tp)
        hm = rmsnorm_cast(mem2d, mem_norm_g[l])
        tm_m = _row_tile(bp * n_mem, 1024)
        mk = _proj_call(functools.partial(_proj_headnorm_kernel, hd=hd_m), hm, W['w_mem_kv'], 0, bw, F32,
                        [(m_k_norm_g[l].reshape(1, hd_m), pl.BlockSpec((1, hd_m), lambda i, j: (0, 0)))],
                        tm=tm_m, name="proj_mk")
        mv = _proj_call(_proj_plain_kernel, hm, W['w_mem_kv'], bw, bw, F32, [], tm=tm_m, name="proj_mv")
        om = attn_m_prompt(qm, mk, mv, bp, tp)
        zeros_state = jnp.zeros((bp, CONV_W - 1, W['f_pad']), F32)
        xp, conv_new = _finish_layer(xp, h, (oa, ob, om), W, P, l, zeros_state, bp, tp, tm_p)
        outs['mk_p'].append(mk.reshape(bp, n_mem, H_M, hd_m))
        outs['mv_p'].append(mv.reshape(bp, n_mem, H_M, hd_m))
        outs['cv_p'].append(conv_new)

        h = rmsnorm_cast(xs, norm_mix_g[l])
        qa, ka, va, qb, kb, vb, qm = _mixer_inputs(h, W, P, l, rope_s, tm_s)
        oa = attn_a_sample(qa, ka, va, cache_a_k, cache_a_v, l, bias_row, mask_a_s, bs, ts)
        ob = attn_b_sample(qb, kb, vb, cache_b_k, cache_b_v, l, lam_params, b_subln_g[l], lam_init, bs, ts)
        om = attn_m_sample(qm, cache_mem_k, cache_mem_v, l, bs, ts)
        state = _pad_cols(state_ffn_conv[l], W['f_pad'])
        xs, conv_new = _finish_layer(xs, h, (oa, ob, om), W, P, l, state, bs, ts, tm_s)
        outs['ak_s'].append(ka.reshape(bs, ts, h_a, HD_A))
        outs['av_s'].append(va.reshape(bs, ts, h_a, HD_A))
        outs['bk_s'].append(kb.reshape(bs, ts, h_b, 2 * DIFF_HD))
        outs['bv_s'].append(vb.reshape(bs, ts, h_b, 2 * DIFF_HD))
        outs['cv_s'].append(conv_new)

    stack = lambda k: jnp.stack(outs[k])
    ak_p, av_p, bk_p, bv_p = kv_prompt
    return (xp.reshape(bp, tp, d), xs.reshape(bs, ts, d),
            ak_p, av_p, bk_p, bv_p, stack('mk_p'), stack('mv_p'), stack('cv_p'),
            stack('ak_s'), stack('av_s'), stack('bk_s'), stack('bv_s'), stack('cv_s'))
```

```python
import functools
import math

import numpy as np
import jax
import jax.numpy as jnp
from jax import lax
from jax.experimental import pallas as pl
from jax.experimental.pallas import tpu as pltpu

F32 = jnp.float32
BF16 = jnp.bfloat16

CHUNK = 64
BAND_CHUNKS = 8
BAND_PAST = BAND_CHUNKS * CHUNK
REL_CLIP = 128
HD_A = 128
DIFF_HD = 64
ROT_DIM = DIFF_HD // 4
ROPE_THETA = 500000.0
H_M = 4
N_BRANCH = 3
CONV_W = 3
EPS = 1e-6
NEG_INF = -1e30
LOG2E = math.log2(math.e)

V7X_LANES = 128
ROW_TILE = 8

ROW_BLOCK = 1024
COL_BLOCK = 512
NORM_ROWS = 512
FFN_ROW_PARTS = 4
V7X_VMEM_BYTES = 64 * 1024 * 1024
MIB = 1024 * 1024


def _compiler_params(semantics, vmem_estimate_bytes):
    limit = min(int(vmem_estimate_bytes * 1.25) + 8 * MIB, V7X_VMEM_BYTES - 4 * MIB)
    return pltpu.CompilerParams(dimension_semantics=semantics, vmem_limit_bytes=limit)


def _row_tile(m, target):
    t = min(m, target)
    assert m % t == 0, (m, t)
    return t


def _rmsnorm_kernel(x_ref, g_ref, o_ref):
    x = x_ref[...]
    ms = jnp.mean(x * x, axis=-1, keepdims=True)
    o_ref[...] = (x * lax.rsqrt(ms + EPS) * g_ref[...]).astype(o_ref.dtype)


def rmsnorm_cast(x, g):
    m, d = x.shape
    tm = _row_tile(m, NORM_ROWS)
    return pl.pallas_call(
        _rmsnorm_kernel,
        grid=(m // tm,),
        in_specs=[pl.BlockSpec((tm, d), lambda i: (i, 0)), pl.BlockSpec((1, d), lambda i: (0, 0))],
        out_specs=pl.BlockSpec((tm, d), lambda i: (i, 0)),
        out_shape=jax.ShapeDtypeStruct((m, d), BF16),
        compiler_params=_compiler_params(("parallel",), 2 * tm * d * 6),
        name="rmsnorm_cast",
    )(x, g.reshape(1, d))


def _dot(a, b):
    return jnp.dot(a, b, preferred_element_type=F32)


def _dot_nt(a, b):
    return lax.dot_general(a, b, (((1,), (1,)), ((), ())), preferred_element_type=F32)


def _proj_plain_kernel(h_ref, w_ref, o_ref):
    o_ref[...] = _dot(h_ref[...], w_ref[...]).astype(o_ref.dtype)


def _headnorm_store(acc, g, o_ref, hd):
    for k in range(acc.shape[1] // hd):
        s = acc[:, k * hd:(k + 1) * hd]
        ms = jnp.mean(s * s, axis=-1, keepdims=True)
        o_ref[:, k * hd:(k + 1) * hd] = (s * lax.rsqrt(ms + EPS) * g).astype(o_ref.dtype)


def _norm_rope_store(acc, g, tab_ref, o_ref):
    rows, width = acc.shape
    grp_r = lax.broadcasted_iota(jnp.int32, (width, width), 0) // DIFF_HD
    grp_c = lax.broadcasted_iota(jnp.int32, (width, width), 1) // DIFF_HD
    ones_bd = jnp.where(grp_r == grp_c, 1.0, 0.0).astype(BF16)
    ms = _dot((acc * acc).astype(BF16), ones_bd) * (1.0 / DIFF_HD)
    y = acc * lax.rsqrt(ms + EPS)
    cos = tab_ref[:, 0:V7X_LANES]
    sin_up = tab_ref[:, V7X_LANES:2 * V7X_LANES]
    sin_dn = tab_ref[:, 2 * V7X_LANES:3 * V7X_LANES]
    half = ROT_DIM // 2
    for k in range(width // V7X_LANES):
        yk = y[:, k * V7X_LANES:(k + 1) * V7X_LANES] * g
        out = (yk * cos + pltpu.roll(yk, half, 1) * sin_up
               + pltpu.roll(yk, V7X_LANES - half, 1) * sin_dn)
        o_ref[:, k * V7X_LANES:(k + 1) * V7X_LANES] = out.astype(o_ref.dtype)


def _proj_headnorm_kernel(h_ref, w_ref, g_ref, o_ref, *, hd):
    _headnorm_store(_dot(h_ref[...], w_ref[...]), g_ref[...], o_ref, hd)


PROJ_SLAB = 256


FINAL_GROUPS = (1, 2, 4, 5)


def _proj_in_kernel(h_ref, w_ref, gqa_ref, gka_ref, gqb_ref, gkb_ref, gqm_ref, tab_ref, *rest,
                    blocks_per_group, hd_m, final):
    i = pl.program_id(0)
    j = pl.program_id(1)
    group = j // blocks_per_group
    tm = h_ref.shape[0]
    tn = w_ref.shape[1]
    n_prev = 0 if final is None or final['layer'] == 0 else 4
    qa_ref, ka_ref, va_ref, qb_ref, kb_ref, vb_ref, qm_ref = rest[n_prev:n_prev + 7]
    out_refs = {0: qa_ref, 1: ka_ref, 2: va_ref, 3: qb_ref, 4: kb_ref, 5: vb_ref, 6: qm_ref}
    if final is not None:
        final_refs = dict(zip(FINAL_GROUPS, rest[n_prev + 7:n_prev + 11]))
        stage, zeros, sem, zero_sem = rest[n_prev + 11:]
        heads_per_block = tn // V7X_LANES
        bps = final['seq_len'] // tm
        clear_slots = range(1, final['depth']) if final['layer'] == 0 else ()

        @pl.when((i == 0) & (j == 0))
        def _():
            zeros[...] = jnp.zeros(zeros.shape, zeros.dtype)

        def head_copies(g, c, slot):
            keep = final['a_keep'] if g in (1, 2) else final['seq_len']
            if keep >= tm:
                rows, r_lo, t0 = tm, 0, (i % bps) * tm - (final['seq_len'] - keep)
            else:
                rows, r_lo, t0 = keep, tm - keep, 0
            out = []
            for hh in range(heads_per_block):
                where = (i // bps, pl.ds(t0, rows), c * heads_per_block + hh, slice(None))
                src = stage.at[slot, pl.ds(r_lo, rows), pl.ds(hh * V7X_LANES, V7X_LANES)]
                out.append(pltpu.make_async_copy(src, final_refs[g].at[(final['layer'],) + where], sem.at[hh]))
                for p in clear_slots:
                    out.append(pltpu.make_async_copy(zeros.at[pl.ds(0, rows), :], final_refs[g].at[(p,) + where],
                                                     zero_sem.at[p - 1, hh]))
            return out

        def kept(g):
            keep = final['a_keep'] if g in (1, 2) else final['seq_len']
            return (i % bps) >= bps - max(keep // tm, 1)

        def wait_block(g, c, slot):
            @pl.when(kept(g))
            def _():
                for cp in head_copies(g, c, slot):
                    cp.wait()

    def run(g, epilogue):
        o_ref = out_refs[g]

        def body():
            slabs = [slice(c0, c0 + PROJ_SLAB) for c0 in range(0, tn, PROJ_SLAB)]
            accs = [_dot(h_ref[...], w_ref[:, cols]) for cols in slabs]
            for acc, cols in zip(accs, slabs):
                epilogue(acc, o_ref.at[:, cols])
            if final is None:
                return
            c = j - g * blocks_per_group
            slot = j % 2
            if g in FINAL_GROUPS:
                @pl.when(c > 0)
                def _():
                    wait_block(g, c - 1, 1 - slot)
            if g - 1 in FINAL_GROUPS:
                @pl.when(c == 0)
                def _():
                    wait_block(g - 1, blocks_per_group - 1, 1 - slot)
            if g in FINAL_GROUPS:
                @pl.when(kept(g))
                def _():
                    stage[slot] = o_ref[...]
                    for cp in head_copies(g, c, slot):
                        cp.start()
        return body

    headnorm = lambda g_ref, hd: (lambda acc, o: _headnorm_store(acc, g_ref[...], o, hd))
    norm_rope = lambda g_ref: (lambda acc, o: _norm_rope_store(acc, g_ref[...], tab_ref, o))
    plain = lambda acc, o: o.__setitem__(Ellipsis, acc)
    bodies = (
        run(0, headnorm(gqa_ref, HD_A)), run(1, headnorm(gka_ref, HD_A)), run(2, plain),
        run(3, norm_rope(gqb_ref)), run(4, norm_rope(gkb_ref)), run(5, plain),
        run(6, headnorm(gqm_ref, hd_m)),
    )
    for n, body in enumerate(bodies):
        pl.when(group == n)(body)


def proj_in(h, w_in, gains, rope_tab, *, tm, tn=COL_BLOCK, final=None):
    m, d = h.shape
    bw = d // 2
    hd_m = bw // H_M
    tm = _row_tile(m, tm)
    assert bw % tn == 0 and tn % PROJ_SLAB == 0 and PROJ_SLAB % hd_m == 0 and rope_tab.shape[0] % tm == 0
    bpg = bw // tn
    n_tab = rope_tab.shape[0] // tm
    g_a_q, g_a_k, g_b_q, g_b_k, g_m_q = gains
    tile2 = lambda g: jnp.tile(g.reshape(1, -1), (1, 2))
    gain_args = [g_a_q.reshape(1, HD_A), g_a_k.reshape(1, HD_A), tile2(g_b_q), tile2(g_b_k), g_m_q.reshape(1, hd_m)]
    const = lambda a: pl.BlockSpec(a.shape, lambda i, j: (0, 0))

    def out_spec(n):
        return pl.BlockSpec((tm, tn), lambda i, j: (i, jnp.clip(j - n * bpg, 0, bpg - 1)))

    dtypes = (BF16, F32, F32, BF16, F32, F32, BF16)
    out_specs = [out_spec(n) for n in range(7)]
    out_shape = [jax.ShapeDtypeStruct((m, bw), t) for t in dtypes]
    prev, scratch, aliases, semantics = [], [], {}, ("parallel", "arbitrary")
    n_fixed_inputs = 8
    if final is not None:
        t, keep, batch = final['seq_len'], final['a_keep'], final['batch']
        heads = bw // V7X_LANES
        assert HD_A == V7X_LANES and 2 * DIFF_HD == V7X_LANES and m == batch * t and t % tm == 0
        assert keep % tm == 0 or (keep < tm and keep % 8 == 0)
        prev = list(final['prev'] or ())
        assert len(prev) == (4 if final['layer'] else 0)
        depth = final['depth']
        out_shape += [jax.ShapeDtypeStruct((depth, batch, rows, heads, V7X_LANES), F32) for rows in (keep, keep, t, t)]
        out_specs += [pl.BlockSpec(memory_space=pl.ANY)] * 4
        hpb = tn // V7X_LANES
        scratch = [pltpu.VMEM((2, tm, tn), F32), pltpu.VMEM((tm, V7X_LANES), F32), pltpu.SemaphoreType.DMA((hpb,)),
                   pltpu.SemaphoreType.DMA((max(depth - 1, 1), hpb))]
        aliases = {n_fixed_inputs + k: 7 + k for k in range(len(prev))}
        semantics = ("arbitrary", "arbitrary")
        final = {k: v for k, v in final.items() if k != 'prev'}
    out_bytes = sum(tm * tn * jnp.dtype(t).itemsize for t in dtypes)
    vmem = 2 * (tm * d * 2 + d * tn * 2 + tm * 3 * V7X_LANES * 4 + out_bytes) + 4 * tm * tn * 4
    return pl.pallas_call(
        functools.partial(_proj_in_kernel, blocks_per_group=bpg, hd_m=hd_m, final=final),
        grid=(m // tm, 7 * bpg),
        in_specs=[pl.BlockSpec((tm, d), lambda i, j: (i, 0)), pl.BlockSpec((d, tn), lambda i, j: (0, j))]
                 + [const(g) for g in gain_args]
                 + [pl.BlockSpec((tm, 3 * V7X_LANES), lambda i, j: (i % n_tab, 0))]
                 + [pl.BlockSpec(memory_space=pl.ANY)] * len(prev),
        out_specs=out_specs,
        out_shape=out_shape,
        input_output_aliases=aliases,
        scratch_shapes=scratch,
        compiler_params=_compiler_params(semantics, vmem),
        name="proj_in",
    )(h, w_in, *gain_args, rope_tab, *prev)


def _proj_residual_kernel(h_ref, w_ref, x_ref, o_ref):
    o_ref[...] = x_ref[...] + _dot(h_ref[...], w_ref[...])


def _proj_out_norm_kernel(m_ref, w_ref, x_ref, g_ref, o_ref, h_ref, xrow):
    j = pl.program_id(1)
    y = x_ref[...] + _dot(m_ref[...], w_ref[...])
    o_ref[...] = y
    xrow[j] = y

    @pl.when(j == pl.num_programs(1) - 1)
    def _():
        nj, _, tn = xrow.shape
        ssq = None
        for jj in range(nj):
            xb = xrow[jj]
            part = jnp.sum(xb * xb, axis=-1, keepdims=True)
            ssq = part if ssq is None else ssq + part
        scale = lax.rsqrt(ssq * (1.0 / (nj * tn)) + EPS)
        for jj in range(nj):
            cols = slice(jj * tn, (jj + 1) * tn)
            h_ref[:, cols] = (xrow[jj] * scale * g_ref[:, cols]).astype(h_ref.dtype)


def proj_out_norm(merged, w_out, x, g, *, tm, tn=2 * COL_BLOCK):
    m, d = x.shape
    tm = _row_tile(m, tm)
    tn = min(tn, d)
    assert d % tn == 0
    row_spec = pl.BlockSpec((tm, d), lambda i, j: (i, 0))
    blk_spec = pl.BlockSpec((tm, tn), lambda i, j: (i, j))
    vmem = 2 * (tm * d * 2 + d * tn * 2 + 2 * tm * tn * 4 + tm * d * 2) + tm * d * 4 + 2 * tm * tn * 4
    return pl.pallas_call(
        _proj_out_norm_kernel,
        grid=(m // tm, d // tn),
        in_specs=[row_spec, pl.BlockSpec((d, tn), lambda i, j: (0, j)), blk_spec,
                  pl.BlockSpec((1, d), lambda i, j: (0, 0))],
        out_specs=[blk_spec, row_spec],
        out_shape=[jax.ShapeDtypeStruct((m, d), F32), jax.ShapeDtypeStruct((m, d), BF16)],
        scratch_shapes=[pltpu.VMEM((d // tn, tm, tn), F32)],
        compiler_params=_compiler_params(("parallel", "arbitrary"), vmem),
        name="proj_out_norm",
    )(merged, w_out, x, g.reshape(1, d))


def _proj_call(kernel_fn, h, w, col0, ncols, out_dtype, extras, *, tm=ROW_BLOCK, tn=COL_BLOCK, name):
    m, k = h.shape
    tm = _row_tile(m, tm)
    tn = min(tn, ncols)
    assert ncols % tn == 0 and col0 % tn == 0, (ncols, col0, tn)
    cb = col0 // tn
    in_specs = [pl.BlockSpec((tm, k), lambda i, j: (i, 0)),
                pl.BlockSpec((k, tn), lambda i, j: (0, j + cb))]
    in_specs += [spec for _, spec in extras]
    extra_bytes = sum(int(np.prod(spec.block_shape)) * a.dtype.itemsize for a, spec in extras)
    vmem = 2 * (tm * k * 2 + k * tn * 2 + tm * tn * 4 + extra_bytes) + tm * tn * 8
    return pl.pallas_call(
        kernel_fn,
        grid=(m // tm, ncols // tn),
        in_specs=in_specs,
        out_specs=pl.BlockSpec((tm, tn), lambda i, j: (i, j)),
        out_shape=jax.ShapeDtypeStruct((m, ncols), out_dtype),
        compiler_params=_compiler_params(("parallel", "arbitrary"), vmem),
        name=name,
    )(h, w, *[a for a, _ in extras])


def _rope_table(pos):
    half = ROT_DIM // 2
    inv_freq = jnp.exp(jnp.arange(half, dtype=F32) * (-2.0 * math.log(ROPE_THETA) / ROT_DIM))
    ang = pos.astype(F32)[:, None] * inv_freq[None, :]
    cos = jnp.cos(ang)
    sin = jnp.sin(ang)
    p = pos.shape[0]
    rest = DIFF_HD - ROT_DIM
    c64 = jnp.concatenate([cos, cos, jnp.ones((p, rest), F32)], axis=1)
    up64 = jnp.concatenate([jnp.zeros((p, half), F32), sin, jnp.zeros((p, rest), F32)], axis=1)
    dn64 = jnp.concatenate([-sin, jnp.zeros((p, half + rest), F32)], axis=1)
    return jnp.concatenate([c64, c64, up64, up64, dn64, dn64], axis=1)


A_QBLK = 4 * CHUNK
A_KBLK = BAND_PAST + A_QBLK
A_BIAS_W = 1024


def _rel_bias_row(tab):
    assert A_KBLK + A_QBLK - 1 <= A_BIAS_W
    lo = BAND_PAST - REL_CLIP
    hi = BAND_PAST + REL_CLIP + 1
    rep = lambda col, n: jnp.repeat(tab[:, col:col + 1], n, axis=1)
    row = jnp.concatenate([rep(0, lo), tab, rep(2 * REL_CLIP, A_KBLK - hi), rep(0, A_BIAS_W - A_KBLK)], axis=1)
    return row[:, None, :]


def _toeplitz_bias(row, rows, width):
    full = pltpu.roll(jnp.broadcast_to(row, (rows, A_BIAS_W)), 0, 1, stride=1, stride_axis=0)
    return full[:, :width]


def _softmax_pv(s, v):
    m = jnp.max(s, axis=-1, keepdims=True)
    p = jnp.exp(s - m)
    l = jnp.sum(p, axis=-1, keepdims=True)
    return _dot(p.astype(BF16), v), l


def _attn_a_prompt_kernel(q_ref, k_ref, v_ref, row_ref, o_ref, kb, vb, *, t):
    kscale = (HD_A ** -0.5) * LOG2E
    kb[...] = (k_ref[0] * kscale).astype(BF16)
    vb[:, :HD_A] = v_ref[0].astype(BF16)
    vb[:, HD_A:] = jnp.ones((t, HD_A), BF16)
    qc = lax.broadcasted_iota(jnp.int32, (A_QBLK, A_KBLK), 0) // CHUNK
    kc = lax.broadcasted_iota(jnp.int32, (A_QBLK, A_KBLK), 1) // CHUNK
    inband = (kc >= qc) & (kc <= qc + BAND_CHUNKS)
    bias = jnp.where(inband, _toeplitz_bias(row_ref[0], A_QBLK, A_KBLK) * LOG2E, NEG_INF)
    for i in range(t // A_QBLK):
        r0 = i * A_QBLK
        k0 = max(r0 - BAND_PAST, 0)
        k1 = r0 + A_QBLK
        q = q_ref[0, r0:k1, :]
        s = _dot_nt(q, kb[k0:k1, :]) + bias[:, A_KBLK - (k1 - k0):]
        p = jnp.exp2(s - jnp.max(s, axis=-1, keepdims=True))
        o = _dot(p.astype(BF16), vb[k0:k1, :])
        o_ref[0, r0:k1, :] = (o[:, :HD_A] / o[:, HD_A:]).astype(o_ref.dtype)


def attn_a_prompt(qa, ka, va, bias_row, b, t):
    h = qa.shape[1] // HD_A
    assert t % A_QBLK == 0 and A_QBLK % V7X_LANES == 0
    q3, k3, v3 = (a.reshape(b, t, h * HD_A) for a in (qa, ka, va))
    spec = pl.BlockSpec((1, t, HD_A), lambda bi, hi: (bi, 0, hi))
    vmem = 2 * t * HD_A * (2 + 4 + 4 + 2) + 2 * t * HD_A * 2 + 24 * MIB
    out = pl.pallas_call(
        functools.partial(_attn_a_prompt_kernel, t=t),
        grid=(b, h),
        in_specs=[spec, spec, spec, pl.BlockSpec((1, 1, A_BIAS_W), lambda bi, hi: (hi, 0, 0))],
        out_specs=spec,
        out_shape=jax.ShapeDtypeStruct((b, t, h * HD_A), BF16),
        scratch_shapes=[pltpu.VMEM((t, HD_A), BF16), pltpu.VMEM((t, 2 * HD_A), BF16)],
        compiler_params=_compiler_params(("parallel", "parallel"), vmem),
        name="attn_a_prompt",
    )(q3, k3, v3, bias_row)
    return out.reshape(b * t, h * HD_A)


def _attn_a_sample_kernel(q_ref, kn_ref, vn_ref, row_ref, mask_ref, kc_hbm, vc_hbm, o_ref, kbuf, vbuf, sem,
                          *, layer, heads, a_len):
    b = pl.program_id(0)
    nb = pl.num_programs(0)
    scale = HD_A ** -0.5
    t = q_ref.shape[2]

    def copies(bi, slot):
        out = []
        for h in range(heads):
            out.append(pltpu.make_async_copy(kc_hbm.at[layer, bi, :, h, :], kbuf.at[slot, h], sem.at[0, slot, h]))
            out.append(pltpu.make_async_copy(vc_hbm.at[layer, bi, :, h, :], vbuf.at[slot, h], sem.at[1, slot, h]))
        return out

    @pl.when(b == 0)
    def _():
        for c in copies(b, 0):
            c.start()

    slot = b % 2

    @pl.when(b + 1 < nb)
    def _():
        for c in copies(b + 1, 1 - slot):
            c.start()

    for c in copies(b, slot):
        c.wait()
    for h in range(heads):
        q = q_ref[0, h]
        bias = _toeplitz_bias(row_ref[h], t, a_len + t) + mask_ref[...]
        sc = _dot_nt(q, kbuf[slot, h].astype(BF16)) * scale + bias[:, :a_len]
        sn = _dot_nt(q, kn_ref[0, h].astype(BF16)) * scale + bias[:, a_len:]
        m = jnp.maximum(jnp.max(sc, axis=-1, keepdims=True), jnp.max(sn, axis=-1, keepdims=True))
        pc = jnp.exp(sc - m)
        pn = jnp.exp(sn - m)
        l = jnp.sum(pc, axis=-1, keepdims=True) + jnp.sum(pn, axis=-1, keepdims=True)
        o = _dot(pc.astype(BF16), vbuf[slot, h].astype(BF16)) + _dot(pn.astype(BF16), vn_ref[0, h].astype(BF16))
        o_ref[0, h] = (o / l).astype(o_ref.dtype)


def _by_head(a, b, t, heads, hd):
    return a.reshape(b, t, heads, hd).transpose(0, 2, 1, 3)


def attn_a_sample(qa, ka, va, cache_k, cache_v, layer, bias_row, mask, b, t):
    heads = qa.shape[1] // HD_A
    a_len = cache_k.shape[2]
    assert a_len == BAND_PAST and a_len + t <= A_KBLK
    head_spec = pl.BlockSpec((1, heads, t, HD_A), lambda bi: (bi, 0, 0, 0))
    vmem = 2 * 2 * heads * a_len * HD_A * 4 + 2 * 4 * heads * t * HD_A * 4 + 16 * MIB
    out = pl.pallas_call(
        functools.partial(_attn_a_sample_kernel, layer=layer, heads=heads, a_len=a_len),
        grid=(b,),
        in_specs=[head_spec, head_spec, head_spec,
                  pl.BlockSpec(bias_row.shape, lambda bi: (0, 0, 0)),
                  pl.BlockSpec(mask.shape, lambda bi: (0, 0)),
                  pl.BlockSpec(memory_space=pl.ANY), pl.BlockSpec(memory_space=pl.ANY)],
        out_specs=head_spec,
        out_shape=jax.ShapeDtypeStruct((b, heads, t, HD_A), BF16),
        scratch_shapes=[pltpu.VMEM((2, heads, a_len, HD_A), F32), pltpu.VMEM((2, heads, a_len, HD_A), F32),
                        pltpu.SemaphoreType.DMA((2, 2, heads))],
        compiler_params=_compiler_params(("arbitrary",), vmem),
        name="attn_a_sample",
    )(*(_by_head(a, b, t, heads, HD_A) for a in (qa, ka, va)), bias_row, mask, cache_k, cache_v)
    return out.transpose(0, 2, 1, 3).reshape(b * t, heads * HD_A)


def _diff_lambda(lam_ref, lam_init):
    v = lam_ref[...]
    d1 = jnp.sum(v[0:1] * v[1:2], axis=-1, keepdims=True)
    d2 = jnp.sum(v[2:3] * v[3:4], axis=-1, keepdims=True)
    return jnp.exp(d1) - jnp.exp(d2) + lam_init


def _split_diff_queries(q):
    lane = lax.broadcasted_iota(jnp.int32, q.shape, 1)
    qs = q * jnp.asarray(DIFF_HD ** -0.5, q.dtype)
    zero = jnp.zeros_like(qs)
    return jnp.where(lane < DIFF_HD, qs, zero), jnp.where(lane >= DIFF_HD, qs, zero)


def _stack_diff_queries(q):
    return jnp.concatenate(_split_diff_queries(q), axis=0)


def _diff_post(o, g, post_scale):
    ms = jnp.mean(o * o, axis=-1, keepdims=True)
    return (o * lax.rsqrt(ms + EPS) * g) * post_scale


def _diff_finish(l, acc, lam, g, post_scale, tq):
    o = acc[:tq] / l[:tq] - lam * (acc[tq:] / l[tq:])
    return _diff_post(o, g, post_scale)


B_TQ = 8 * CHUNK


def _online_step(carry, s, v_ext):
    m, acc = carry
    m_new = jnp.maximum(m, jnp.max(s, axis=-1, keepdims=True))
    alpha = jnp.exp2(m - m_new)
    p = jnp.exp2(s - m_new)
    acc = alpha * acc + _dot(p.astype(BF16), v_ext)
    return m_new, acc


def _attn_b_prompt_kernel(lam_ref, q_ref, k_ref, v_ref, g_ref, o_ref, kb, vb, *, t, lam_init):
    tq = B_TQ
    hd = 2 * DIFF_HD
    kb[...] = (k_ref[0] * LOG2E).astype(BF16)
    vb[:, :hd] = v_ref[0].astype(BF16)
    vb[:, hd:] = jnp.ones((t, hd), BF16)
    lam = _diff_lambda(lam_ref, lam_init)
    row = lax.broadcasted_iota(jnp.int32, (tq, tq), 0)
    col = lax.broadcasted_iota(jnp.int32, (tq, tq), 1)
    diag_ok = (col // CHUNK) <= (row // CHUNK)
    for qi in range(t // tq):
        q0 = qi * tq
        qs = _split_diff_queries(q_ref[0, q0:q0 + tq, :])
        spans = ([(0, q0, False)] if q0 else []) + [(q0, tq, True)]
        outs = []
        for c in range(2):
            carry = (jnp.full((tq, 1), NEG_INF, F32), jnp.zeros((tq, 2 * hd), F32))
            for k0, width, masked in spans:
                s = _dot_nt(qs[c], kb[k0:k0 + width, :])
                if masked:
                    s = jnp.where(diag_ok, s, NEG_INF)
                carry = _online_step(carry, s, vb[k0:k0 + width, :])
            outs.append(carry[1][:, :hd] / carry[1][:, hd:])
        o = outs[0] - lam * outs[1]
        o_ref[0, q0:q0 + tq, :] = _diff_post(o, g_ref[...], 1.0 - lam_init).astype(o_ref.dtype)


def attn_b_prompt(qb, kb, vb, lam_params, subln_g, lam_init, b, t):
    hd = 2 * DIFF_HD
    heads = qb.shape[1] // hd
    assert t % B_TQ == 0
    q3, k3, v3 = (a.reshape(b, t, heads * hd) for a in (qb, kb, vb))
    spec = pl.BlockSpec((1, t, hd), lambda bi, hi: (bi, 0, hi))
    vmem = 2 * t * hd * (2 + 4 + 4 + 2) + 2 * t * hd * 2 + 32 * MIB
    out = pl.pallas_call(
        functools.partial(_attn_b_prompt_kernel, t=t, lam_init=lam_init),
        grid=(b, heads),
        in_specs=[pl.BlockSpec(lam_params.shape, lambda bi, hi: (0, 0)), spec, spec, spec,
                  pl.BlockSpec((1, hd), lambda bi, hi: (0, 0))],
        out_specs=spec,
        out_shape=jax.ShapeDtypeStruct((b, t, heads * hd), BF16),
        scratch_shapes=[pltpu.VMEM((t, hd), BF16), pltpu.VMEM((t, 2 * hd), BF16)],
        compiler_params=_compiler_params(("parallel", "parallel"), vmem),
        name="attn_b_prompt",
    )(lam_params, q3, k3, v3, subln_g.reshape(1, hd))
    return out.reshape(b * t, heads * hd)


def _attn_b_sample_kernel(lam_ref, q_ref, kn_ref, vn_ref, g_ref, kc_hbm, vc_hbm, o_ref, kbuf, vbuf, sem,
                          *, layer, heads, t, lam_init):
    b = pl.program_id(0)
    nb = pl.num_programs(0)

    def copies(bi, h, slot):
        return (pltpu.make_async_copy(kc_hbm.at[layer, bi, :, h, :], kbuf.at[slot], sem.at[0, slot]),
                pltpu.make_async_copy(vc_hbm.at[layer, bi, :, h, :], vbuf.at[slot], sem.at[1, slot]))

    def start(bi, h, slot):
        for c in copies(bi, h, slot):
            c.start()

    @pl.when(b == 0)
    def _():
        start(b, 0, 0)

    lam = _diff_lambda(lam_ref, lam_init)
    for h in range(heads):
        slot = h % 2
        if h + 1 < heads:
            start(b, h + 1, 1 - slot)
        else:
            @pl.when(b + 1 < nb)
            def _():
                start(b + 1, 0, 1 - slot)
        for c in copies(b, h, slot):
            c.wait()
        q2 = _stack_diff_queries(q_ref[0, h])
        sc = _dot_nt(q2, kbuf[slot].astype(BF16))
        sn = _dot_nt(q2, kn_ref[0, h].astype(BF16))
        m = jnp.maximum(jnp.max(sc, axis=-1, keepdims=True), jnp.max(sn, axis=-1, keepdims=True))
        pc = jnp.exp(sc - m)
        pn = jnp.exp(sn - m)
        l = jnp.sum(pc, axis=-1, keepdims=True) + jnp.sum(pn, axis=-1, keepdims=True)
        acc = _dot(pc.astype(BF16), vbuf[slot].astype(BF16)) + _dot(pn.astype(BF16), vn_ref[0, h].astype(BF16))
        o_ref[0, h] = _diff_finish(l, acc, lam, g_ref[...], 1.0 - lam_init, t).astype(o_ref.dtype)


def attn_b_sample(qb, kb, vb, cache_k, cache_v, layer, lam_params, subln_g, lam_init, b, t):
    hd = 2 * DIFF_HD
    heads = qb.shape[1] // hd
    assert heads % 2 == 0
    past = cache_k.shape[2]
    head_spec = pl.BlockSpec((1, heads, t, hd), lambda bi: (bi, 0, 0, 0))
    vmem = 2 * 2 * past * hd * 4 + 2 * 3 * heads * t * hd * 4 + 12 * 2 * t * past * 4
    out = pl.pallas_call(
        functools.partial(_attn_b_sample_kernel, layer=layer, heads=heads, t=t, lam_init=lam_init),
        grid=(b,),
        in_specs=[pl.BlockSpec(lam_params.shape, lambda bi: (0, 0)), head_spec, head_spec, head_spec,
                  pl.BlockSpec((1, hd), lambda bi: (0, 0)),
                  pl.BlockSpec(memory_space=pl.ANY), pl.BlockSpec(memory_space=pl.ANY)],
        out_specs=head_spec,
        out_shape=jax.ShapeDtypeStruct((b, heads, t, hd), BF16),
        scratch_shapes=[pltpu.VMEM((2, past, hd), F32), pltpu.VMEM((2, past, hd), F32),
                        pltpu.SemaphoreType.DMA((2, 2))],
        compiler_params=_compiler_params(("arbitrary",), vmem),
        name="attn_b_sample",
    )(lam_params, *(_by_head(a, b, t, heads, hd) for a in (qb, kb, vb)), subln_g.reshape(1, hd), cache_k, cache_v)
    return out.transpose(0, 2, 1, 3).reshape(b * t, heads * hd)


def _attn_m_kernel(q_ref, k_ref, v_ref, o_ref, *, heads, hd):
    scale = hd ** -0.5
    for h in range(heads):
        sl = slice(h * hd, (h + 1) * hd)
        s = _dot_nt(q_ref[0, :, sl], k_ref[0, :, sl].astype(BF16)) * scale
        o, l = _softmax_pv(s, v_ref[0, :, sl].astype(BF16))
        o_ref[0, :, sl] = (o / l).astype(o_ref.dtype)


def attn_m_prompt(qm, mem_k, mem_v, b, t, *, tq=B_TQ):
    width = qm.shape[1]
    hd = width // H_M
    tq = min(tq, t)
    assert t % tq == 0
    n = mem_k.shape[0] // b
    q_spec = pl.BlockSpec((1, tq, width), lambda bi, qi: (bi, qi, 0))
    kv_spec = pl.BlockSpec((1, n, width), lambda bi, qi: (bi, 0, 0))
    vmem = 2 * (2 * n * width * 4 + 2 * tq * width * 2) + 8 * tq * n * 4
    out = pl.pallas_call(
        functools.partial(_attn_m_kernel, heads=H_M, hd=hd),
        grid=(b, t // tq),
        in_specs=[q_spec, kv_spec, kv_spec],
        out_specs=q_spec,
        out_shape=jax.ShapeDtypeStruct((b, t, width), BF16),
        compiler_params=_compiler_params(("parallel", "arbitrary"), vmem),
        name="attn_m_prompt",
    )(qm.reshape(b, t, width), mem_k.reshape(b, n, width), mem_v.reshape(b, n, width))
    return out.reshape(b * t, width)


def _attn_m_sample_kernel(q_ref, kc_hbm, vc_hbm, o_ref, kbuf, vbuf, sem, *, layer, heads, hd):
    b = pl.program_id(0)
    nb = pl.num_programs(0)
    scale = hd ** -0.5

    def copies(bi, slot):
        out = []
        for h in range(heads):
            out.append(pltpu.make_async_copy(kc_hbm.at[layer, bi, :, h, :], kbuf.at[slot, h], sem.at[0, slot, h]))
            out.append(pltpu.make_async_copy(vc_hbm.at[layer, bi, :, h, :], vbuf.at[slot, h], sem.at[1, slot, h]))
        return out

    @pl.when(b == 0)
    def _():
        for c in copies(b, 0):
            c.start()

    slot = b % 2

    @pl.when(b + 1 < nb)
    def _():
        for c in copies(b + 1, 1 - slot):
            c.start()

    for c in copies(b, slot):
        c.wait()
    for h in range(heads):
        s = _dot_nt(q_ref[0, h], kbuf[slot, h].astype(BF16)) * scale
        o, l = _softmax_pv(s, vbuf[slot, h].astype(BF16))
        o_ref[0, h] = (o / l).astype(o_ref.dtype)


def attn_m_sample(qm, cache_k, cache_v, layer, b, t):
    n, heads, hd = cache_k.shape[2:]
    head_spec = pl.BlockSpec((1, heads, t, hd), lambda bi: (bi, 0, 0, 0))
    vmem = 2 * 2 * heads * n * hd * 4 + 2 * 2 * heads * t * hd * 2 + 8 * MIB
    out = pl.pallas_call(
        functools.partial(_attn_m_sample_kernel, layer=layer, heads=heads, hd=hd),
        grid=(b,),
        in_specs=[head_spec, pl.BlockSpec(memory_space=pl.ANY), pl.BlockSpec(memory_space=pl.ANY)],
        out_specs=head_spec,
        out_shape=jax.ShapeDtypeStruct((b, heads, t, hd), BF16),
        scratch_shapes=[pltpu.VMEM((2, heads, n, hd), F32), pltpu.VMEM((2, heads, n, hd), F32),
                        pltpu.SemaphoreType.DMA((2, 2, heads))],
        compiler_params=_compiler_params(("arbitrary",), vmem),
        name="attn_m_sample",
    )(_by_head(qm, b, t, heads, hd), cache_k, cache_v)
    return out.transpose(0, 2, 1, 3).reshape(b * t, heads * hd)


def _merge_kernel(h_ref, oa_ref, ob_ref, om_ref, wga_ref, wgb_ref, wgm_ref, gb_ref, wbr_ref, o_ref):
    h = h_ref[...]
    acc = None
    for n, (o_n, wg_n) in enumerate(((oa_ref, wga_ref), (ob_ref, wgb_ref), (om_ref, wgm_ref))):
        gate = jax.nn.sigmoid(_dot(h, wg_n[...]) + gb_ref[n])
        term = gate * _dot(o_n[...], wbr_ref[n])
        acc = term if acc is None else acc + term
    o_ref[...] = acc.astype(o_ref.dtype)


def merge_branches(h, oa, ob, om, w_in, gate_col0, gate_b, w_br, *, tm, tn=COL_BLOCK):
    m, bw = oa.shape
    d = w_br.shape[2]
    tm = _row_tile(m, tm)
    assert d % tn == 0 and gate_col0 % tn == 0
    nj = d // tn
    g0 = gate_col0 // tn
    row_spec = lambda width: pl.BlockSpec((tm, width), lambda i, j: (i, 0))
    gate_w_specs = [pl.BlockSpec((d, tn), functools.partial(lambda i, j, n: (0, g0 + n * nj + j), n=n))
                    for n in range(N_BRANCH)]
    vmem = 2 * (tm * d * 2 + 3 * tm * bw * 2 + 3 * d * tn * 2 + 3 * bw * tn * 2 + tm * tn * 2) + 6 * tm * tn * 4
    return pl.pallas_call(
        _merge_kernel,
        grid=(m // tm, nj),
        in_specs=[row_spec(d), row_spec(bw), row_spec(bw), row_spec(bw)] + gate_w_specs
                 + [pl.BlockSpec((N_BRANCH, 1, tn), lambda i, j: (0, 0, j)),
                    pl.BlockSpec((N_BRANCH, bw, tn), lambda i, j: (0, 0, j))],
        out_specs=pl.BlockSpec((tm, tn), lambda i, j: (i, j)),
        out_shape=jax.ShapeDtypeStruct((m, d), BF16),
        compiler_params=_compiler_params(("parallel", "arbitrary"), vmem),
        name="merge_branches",
    )(h, oa, ob, om, w_in, w_in, w_in, gate_b, w_br)


def _ffn_up_kernel(h_ref, wa_ref, wb_ref, cw_ref, cb_ref, st_ref, g_ref, cn_ref, carry, *, nb, tb, blocks_per_seq):
    i = pl.program_id(0)
    j = pl.program_id(1)
    tm, tn = g_ref.shape
    if nb == 1:
        @pl.when((i % blocks_per_seq) == 0)
        def _():
            carry[j] = st_ref[0]

        trow = lax.broadcasted_iota(jnp.int32, (ROW_TILE, PROJ_SLAB), 0)
    else:
        trow = lax.broadcasted_iota(jnp.int32, (nb, tb, PROJ_SLAB), 1).reshape(tm, PROJ_SLAB)
    slabs = [slice(c0, c0 + PROJ_SLAB) for c0 in range(0, tn, PROJ_SLAB)]

    def gated(a, am1, am2, bgate, cols):
        cw = cw_ref[:, cols]
        c = cb_ref[:, cols] + am2 * cw[0:1] + am1 * cw[1:2] + a * cw[2:3]
        gelu = 0.5 * c * (1.0 + lax.erf(c * (2.0 ** -0.5)))
        return (gelu * bgate).astype(g_ref.dtype)

    if nb == 1:
        parts = FFN_ROW_PARTS if tm % (FFN_ROW_PARTS * ROW_TILE) == 0 else 1
        rp = tm // parts
        units = [(r0, cols) for cols in slabs for r0 in range(0, tm, rp)]
        dots = [(_dot(h_ref[r0:r0 + rp, :], wa_ref[:, cols]), _dot(h_ref[r0:r0 + rp, :], wb_ref[:, cols]))
                for r0, cols in units]
        prev = {}
        for (r0, cols), (a, bgate) in zip(units, dots):
            g_ref[r0:r0 + rp, cols] = gated(a, pltpu.roll(a, 1, 0), pltpu.roll(a, 2, 0), bgate, cols)
            before = carry[j, :, cols] if r0 == 0 else prev[cols.start]
            top = a[0:ROW_TILE]
            p0 = jnp.broadcast_to(before[0:1], top.shape)
            p1 = jnp.broadcast_to(before[1:2], top.shape)
            am1 = jnp.where(trow == 0, p1, pltpu.roll(top, 1, 0))
            am2 = jnp.where(trow == 0, p0, jnp.where(trow == 1, p1, pltpu.roll(top, 2, 0)))
            g_ref[r0:r0 + ROW_TILE, cols] = gated(top, am1, am2, bgate[0:ROW_TILE], cols)
            prev[cols.start] = a[rp - 2:rp]
            if r0 + rp == tm:
                carry[j, :, cols] = a[rp - 2:rp]
                cn_ref[0, :, cols] = a[rp - 2:rp]
        return

    h = h_ref[...]
    dots = [(_dot(h, wa_ref[:, cols]), _dot(h, wb_ref[:, cols])) for cols in slabs]
    for cols, (a, bgate) in zip(slabs, dots):
        st = st_ref[:, :, cols]
        p0 = jnp.broadcast_to(st[:, 0:1, :], (nb, tb, PROJ_SLAB)).reshape(tm, PROJ_SLAB)
        p1 = jnp.broadcast_to(st[:, 1:2, :], (nb, tb, PROJ_SLAB)).reshape(tm, PROJ_SLAB)
        cn_ref[:, :, cols] = a.reshape(nb, tb, PROJ_SLAB)[:, tb - 2:tb, :]
        am1 = jnp.where(trow == 0, p1, pltpu.roll(a, 1, 0))
        am2 = jnp.where(trow == 0, p0, jnp.where(trow == 1, p1, pltpu.roll(a, 2, 0)))
        g_ref[:, cols] = gated(a, am1, am2, bgate, cols)


def ffn_up(h, w_a, w_b, conv_w, conv_b, state, b, t, *, tm=ROW_BLOCK, tn=COL_BLOCK):
    m, d = h.shape
    f = w_a.shape[1]
    tm = _row_tile(m, tm)
    assert f % tn == 0 and tn % PROJ_SLAB == 0
    if tm >= t:
        assert tm % t == 0
        nb, tb, blocks_per_seq = tm // t, t, 1
    else:
        assert t % tm == 0
        nb, tb, blocks_per_seq = 1, tm, t // tm
    if nb > 1:
        seq_map = lambda i, j: (i, 0, j)
    else:
        seq_map = lambda i, j: (i // blocks_per_seq, 0, j)
    tail_map = lambda i, j: (i, 0, j)
    w_spec = pl.BlockSpec((d, tn), lambda i, j: (0, j))
    vmem = 2 * (tm * d * 2 + 2 * d * tn * 2 + tm * tn * 2 + 2 * nb * 8 * tn * 4) + 8 * tm * tn * 4
    g, conv_new = pl.pallas_call(
        functools.partial(_ffn_up_kernel, nb=nb, tb=tb, blocks_per_seq=blocks_per_seq),
        grid=(m // tm, f // tn),
        in_specs=[pl.BlockSpec((tm, d), lambda i, j: (i, 0)), w_spec, w_spec,
                  pl.BlockSpec((CONV_W, tn), lambda i, j: (0, j)),
                  pl.BlockSpec((1, tn), lambda i, j: (0, j)),
                  pl.BlockSpec((nb, CONV_W - 1, tn), seq_map)],
        out_specs=[pl.BlockSpec((tm, tn), lambda i, j: (i, j)),
                   pl.BlockSpec((nb, CONV_W - 1, tn), tail_map)],
        out_shape=[jax.ShapeDtypeStruct((m, f), BF16),
                   jax.ShapeDtypeStruct((b * blocks_per_seq, CONV_W - 1, f), F32)],
        scratch_shapes=[pltpu.VMEM((f // tn, CONV_W - 1, tn), F32)],
        compiler_params=_compiler_params(("arbitrary", "arbitrary"), vmem),
        name="ffn_up",
    )(h, w_a, w_b, conv_w, conv_b, state)
    return g, conv_new.reshape(b, blocks_per_seq, CONV_W - 1, f)[:, -1]


def _pad_cols(a, f_pad):
    return jnp.pad(a, [(0, 0)] * (a.ndim - 1) + [(0, f_pad - a.shape[-1])])


def _cast_kernel(w_ref, o_ref, *, valid_rows, valid_cols):
    rb, cb = o_ref.shape
    rows = pl.program_id(0) * rb + lax.broadcasted_iota(jnp.int32, (rb, cb), 0)
    cols = pl.program_id(1) * cb + lax.broadcasted_iota(jnp.int32, (rb, cb), 1)
    ok = (rows < valid_rows) & (cols < valid_cols)
    o_ref[...] = jnp.where(ok, w_ref[...], 0.0).astype(o_ref.dtype)


def cast_weight(w, layer, *, rb, cb, col0=0, ncols=None, out_rows=None, out_cols=None):
    _, r, c = w.shape
    ncols = c - col0 if ncols is None else ncols
    out_rows = r if out_rows is None else out_rows
    out_cols = ncols if out_cols is None else out_cols
    assert col0 % cb == 0 and out_rows % rb == 0 and out_cols % cb == 0
    c0 = col0 // cb
    last_r = (r - 1) // rb
    last_c = (col0 + ncols - 1) // cb
    in_map = lambda i, j: (layer, jnp.minimum(i, last_r), jnp.minimum(j + c0, last_c))
    return pl.pallas_call(
        functools.partial(_cast_kernel, valid_rows=r, valid_cols=ncols),
        grid=(out_rows // rb, out_cols // cb),
        in_specs=[pl.BlockSpec((None, rb, cb), in_map)],
        out_specs=pl.BlockSpec((rb, cb), lambda i, j: (i, j)),
        out_shape=jax.ShapeDtypeStruct((out_rows, out_cols), BF16),
        compiler_params=_compiler_params(("parallel", "parallel"), 2 * rb * cb * 6),
        name="cast_weight",
    )(w)


def _cast_halves_kernel(wa_ref, wb_ref, oa_ref, ob_ref, *, valid_cols):
    rb, cb = oa_ref.shape
    cols = pl.program_id(0) * cb + lax.broadcasted_iota(jnp.int32, (rb, cb), 1)
    ok = cols < valid_cols
    oa_ref[...] = jnp.where(ok, wa_ref[...], 0.0).astype(oa_ref.dtype)
    ob_ref[...] = jnp.where(ok, wb_ref[...], 0.0).astype(ob_ref.dtype)


def cast_weight_halves(w, layer, out_cols, *, cb=V7X_LANES):
    _, r, c = w.shape
    half = c // 2
    assert half % cb == 0 and out_cols % cb == 0
    nb = half // cb
    in_spec = lambda first: pl.BlockSpec((None, r, cb), lambda j: (layer, 0, first + jnp.minimum(j, nb - 1)))
    out_spec = pl.BlockSpec((r, cb), lambda j: (0, j))
    out = jax.ShapeDtypeStruct((r, out_cols), BF16)
    return pl.pallas_call(
        functools.partial(_cast_halves_kernel, valid_cols=half),
        grid=(out_cols // cb,),
        in_specs=[in_spec(0), in_spec(nb)],
        out_specs=[out_spec, out_spec],
        out_shape=[out, out],
        compiler_params=_compiler_params(("parallel",), 2 * 2 * r * cb * 6),
        name="cast_weight_halves",
    )(w, w)


def _layer_weights(l, P):
    depth, d, _ = P['w_in'].shape
    d_ff = P['w_ffn_down'].shape[1]
    f_pad = -(-d_ff // COL_BLOCK) * COL_BLOCK
    bw = d // 2
    assert d_ff % V7X_LANES == 0
    w_br = P['w_branch'].reshape(depth, N_BRANCH * bw, d)
    w_up_a, w_up_b = cast_weight_halves(P['w_ffn_up'], l, f_pad)
    return {
        'w_up_a': w_up_a,
        'w_up_b': w_up_b,
        'w_in': cast_weight(P['w_in'], l, rb=d, cb=COL_BLOCK),
        'w_mem_kv': cast_weight(P['w_mem_kv'], l, rb=d, cb=COL_BLOCK),
        'w_branch': cast_weight(w_br, l, rb=bw, cb=d).reshape(N_BRANCH, bw, d),
        'w_out': cast_weight(P['w_out'], l, rb=d, cb=COL_BLOCK),
        'w_down': cast_weight(P['w_ffn_down'], l, rb=COL_BLOCK, cb=d, out_rows=f_pad),
        'conv_w': _pad_cols(P['ffn_conv_w'][l], f_pad),
        'conv_b': _pad_cols(P['ffn_conv_b'][l].reshape(1, d_ff), f_pad),
        'f_pad': f_pad,
        'd_ff': d_ff,
    }


def _mixer_inputs(h, W, P, l, rope_tab, tm, final=None):
    gains = (P['a_q_norm_g'][l], P['a_k_norm_g'][l], P['b_q_norm_g'][l], P['b_k_norm_g'][l], P['m_q_norm_g'][l])
    return proj_in(h, W['w_in'], gains, rope_tab, tm=tm, final=final)


def _finish_layer(x, h, outs, W, P, l, state, b, t, tm):
    d = x.shape[1]
    gate_b = P['gate_b'][l].reshape(N_BRANCH, 1, d)
    merged = merge_branches(h, *outs, W['w_in'], 7 * (d // 2), gate_b, W['w_branch'], tm=tm)
    res_spec = lambda tn: pl.BlockSpec((tm, tn), lambda i, j: (i, j))
    x, h = proj_out_norm(merged, W['w_out'], x, P['norm_ffn_g'][l], tm=tm)
    g, conv_new = ffn_up(h, W['w_up_a'], W['w_up_b'], W['conv_w'], W['conv_b'], state, b, t, tm=tm)
    x = _proj_call(_proj_residual_kernel, g, W['w_down'], 0, d, F32, [(x, res_spec(COL_BLOCK))], tm=tm, name="ffn_down")
    return x, conv_new[:, :, :W['d_ff']]


def kernel(x_prompt, x_sample, cache_a_k, cache_a_v, cache_b_k, cache_b_v, cache_mem_k, cache_mem_v, state_ffn_conv, mem_prompt, norm_mix_g, w_in, a_q_norm_g, a_k_norm_g, a_rel_bias, b_q_norm_g, b_k_norm_g, b_lam_q1, b_lam_k1, b_lam_q2, b_lam_k2, b_subln_g, m_q_norm_g, m_k_norm_g, mem_norm_g, w_mem_kv, gate_b, w_branch, w_out, norm_ffn_g, w_ffn_up, ffn_conv_w, ffn_conv_b, w_ffn_down):
    P = {'w_in': w_in, 'a_q_norm_g': a_q_norm_g, 'a_k_norm_g': a_k_norm_g, 'b_q_norm_g': b_q_norm_g,
         'b_k_norm_g': b_k_norm_g, 'm_q_norm_g': m_q_norm_g, 'm_k_norm_g': m_k_norm_g, 'w_mem_kv': w_mem_kv,
         'gate_b': gate_b, 'w_branch': w_branch, 'w_out': w_out, 'norm_ffn_g': norm_ffn_g,
         'w_ffn_up': w_ffn_up, 'ffn_conv_w': ffn_conv_w, 'ffn_conv_b': ffn_conv_b, 'w_ffn_down': w_ffn_down}
    bp, tp, d = x_prompt.shape
    bs, ts, _ = x_sample.shape
    depth = w_in.shape[0]
    bw = d // 2
    past = cache_b_k.shape[2]
    a_len = cache_a_k.shape[2]
    n_mem = mem_prompt.shape[1]
    a_keep = min(BAND_PAST, tp)
    h_a = bw // HD_A
    h_b = bw // (2 * DIFF_HD)
    hd_m = bw // H_M
    mp, ms = bp * tp, bs * ts
    tm_p = _row_tile(mp, ROW_BLOCK)
    tm_s = _row_tile(ms, ROW_BLOCK)
    assert tm_p <= tp and tp % tm_p == 0 or tm_p % tp == 0

    pos_s = past + np.arange(ts)
    key_pos_a = np.concatenate([past - a_len + np.arange(a_len), pos_s])
    q_chunk_s = pos_s // CHUNK
    k_chunk_a = key_pos_a // CHUNK
    valid_a_s = (k_chunk_a[None, :] <= q_chunk_s[:, None]) & (k_chunk_a[None, :] >= q_chunk_s[:, None] - BAND_CHUNKS)
    mask_a_s = jnp.asarray(np.where(valid_a_s, 0.0, NEG_INF), F32)
    key_pos_b = np.concatenate([np.arange(past), pos_s])
    valid_b_s = (key_pos_b // CHUNK)[None, :] <= q_chunk_s[:, None]
    assert valid_b_s.all(), "sample queries are expected to see every cached and new differential key"

    rope_p = _rope_table(jnp.arange(max(tp, tm_p), dtype=jnp.int32) % tp)
    rope_s = _rope_table(past + (jnp.arange(max(ts, tm_s), dtype=jnp.int32) % ts))

    xp = x_prompt.reshape(mp, d)
    xs = x_sample.reshape(ms, d)
    mem2d = mem_prompt.reshape(bp * n_mem, d)
    outs = {k: [] for k in ('mk_p', 'mv_p', 'cv_p', 'ak_s', 'av_s', 'bk_s', 'bv_s', 'cv_s')}
    kv_prompt = None
    for l in range(depth):
        W = _layer_weights(l, P)
        lam_init = 0.8 - 0.6 * math.exp(-0.3 * l)
        lam_params = jnp.stack([b_lam_q1[l], b_lam_k1[l], b_lam_q2[l], b_lam_k2[l]]).astype(F32)
        bias_row = _rel_bias_row(a_rel_bias[l])

        h = rmsnorm_cast(xp, norm_mix_g[l])
        final = dict(batch=bp, seq_len=tp, a_keep=a_keep, layer=l, depth=depth, prev=kv_prompt)
        qa, ka, va, qb, kb, vb, qm, *kv_prompt = _mixer_inputs(h, W, P, l, rope_p, tm_p, final)
        oa = attn_a_prompt(qa, ka, va, bias_row, bp, tp)
        ob = attn_b_prompt(qb, kb, vb, lam_params, b_subln_g[l], lam_init, bp, tp)
        hm = rmsnorm_cast(mem2d, mem_norm_g[l])
        tm_m = _row_tile(bp * n_mem, ROW_BLOCK)
        mk = _proj_call(functools.partial(_proj_headnorm_kernel, hd=hd_m), hm, W['w_mem_kv'], 0, bw, F32,
                        [(m_k_norm_g[l].reshape(1, hd_m), pl.BlockSpec((1, hd_m), lambda i, j: (0, 0)))],
                        tm=tm_m, name="proj_mk")
        mv = _proj_call(_proj_plain_kernel, hm, W['w_mem_kv'], bw, bw, F32, [], tm=tm_m, name="proj_mv")
        om = attn_m_prompt(qm, mk, mv, bp, tp)
        zeros_state = jnp.zeros((bp, CONV_W - 1, W['f_pad']), F32)
        xp, conv_new = _finish_layer(xp, h, (oa, ob, om), W, P, l, zeros_state, bp, tp, tm_p)
        outs['mk_p'].append(mk.reshape(bp, n_mem, H_M, hd_m))
        outs['mv_p'].append(mv.reshape(bp, n_mem, H_M, hd_m))
        outs['cv_p'].append(conv_new)

        h = rmsnorm_cast(xs, norm_mix_g[l])
        qa, ka, va, qb, kb, vb, qm = _mixer_inputs(h, W, P, l, rope_s, tm_s)
        oa = attn_a_sample(qa, ka, va, cache_a_k, cache_a_v, l, bias_row, mask_a_s, bs, ts)
        ob = attn_b_sample(qb, kb, vb, cache_b_k, cache_b_v, l, lam_params, b_subln_g[l], lam_init, bs, ts)
        om = attn_m_sample(qm, cache_mem_k, cache_mem_v, l, bs, ts)
        state = _pad_cols(state_ffn_conv[l], W['f_pad'])
        xs, conv_new = _finish_layer(xs, h, (oa, ob, om), W, P, l, state, bs, ts, tm_s)
        outs['ak_s'].append(ka.reshape(bs, ts, h_a, HD_A))
        outs['av_s'].append(va.reshape(bs, ts, h_a, HD_A))
        outs['bk_s'].append(kb.reshape(bs, ts, h_b, 2 * DIFF_HD))
        outs['bv_s'].append(vb.reshape(bs, ts, h_b, 2 * DIFF_HD))
        outs['cv_s'].append(conv_new)

    stack = lambda k: jnp.stack(outs[k])
    ak_p, av_p, bk_p, bv_p = kv_prompt
    return (xp.reshape(bp, tp, d), xs.reshape(bs, ts, d),
            ak_p, av_p, bk_p, bv_p, stack('mk_p'), stack('mv_p'), stack('cv_p'),
            stack('ak_s'), stack('av_s'), stack('bk_s'), stack('bv_s'), stack('cv_s'))
```

```python
import functools
import math

import numpy as np
import jax
import jax.numpy as jnp
from jax import lax
from jax.experimental import pallas as pl
from jax.experimental.pallas import tpu as pltpu

F32 = jnp.float32
BF16 = jnp.bfloat16

CHUNK = 64
BAND_CHUNKS = 8
BAND_PAST = BAND_CHUNKS * CHUNK
REL_CLIP = 128
HD_A = 128
DIFF_HD = 64
ROT_DIM = DIFF_HD // 4
ROPE_THETA = 500000.0
H_M = 4
N_BRANCH = 3
CONV_W = 3
EPS = 1e-6
NEG_INF = -1e30
LOG2E = math.log2(math.e)

V7X_LANES = 128
ROW_TILE = 8

ROW_BLOCK = 1024
COL_BLOCK = 512
NORM_ROWS = 512
FFN_ROW_PARTS = 4
V7X_VMEM_BYTES = 64 * 1024 * 1024
MIB = 1024 * 1024


def _compiler_params(semantics, vmem_estimate_bytes):
    limit = min(int(vmem_estimate_bytes * 1.25) + 8 * MIB, V7X_VMEM_BYTES - 4 * MIB)
    return pltpu.CompilerParams(dimension_semantics=semantics, vmem_limit_bytes=limit)


def _row_tile(m, target):
    t = min(m, target)
    assert m % t == 0, (m, t)
    return t


def _rmsnorm_kernel(x_ref, g_ref, o_ref):
    x = x_ref[...]
    ms = jnp.mean(x * x, axis=-1, keepdims=True)
    o_ref[...] = (x * lax.rsqrt(ms + EPS) * g_ref[...]).astype(o_ref.dtype)


def rmsnorm_cast(x, g):
    m, d = x.shape
    tm = _row_tile(m, NORM_ROWS)
    return pl.pallas_call(
        _rmsnorm_kernel,
        grid=(m // tm,),
        in_specs=[pl.BlockSpec((tm, d), lambda i: (i, 0)), pl.BlockSpec((1, d), lambda i: (0, 0))],
        out_specs=pl.BlockSpec((tm, d), lambda i: (i, 0)),
        out_shape=jax.ShapeDtypeStruct((m, d), BF16),
        compiler_params=_compiler_params(("parallel",), 2 * tm * d * 6),
        name="rmsnorm_cast",
    )(x, g.reshape(1, d))


def _dot(a, b):
    return jnp.dot(a, b, preferred_element_type=F32)


def _dot_nt(a, b):
    return lax.dot_general(a, b, (((1,), (1,)), ((), ())), preferred_element_type=F32)


def _proj_plain_kernel(h_ref, w_ref, o_ref):
    o_ref[...] = _dot(h_ref[...], w_ref[...]).astype(o_ref.dtype)


def _headnorm_store(acc, g, o_ref, hd):
    for k in range(acc.shape[1] // hd):
        s = acc[:, k * hd:(k + 1) * hd]
        ms = jnp.mean(s * s, axis=-1, keepdims=True)
        o_ref[:, k * hd:(k + 1) * hd] = (s * lax.rsqrt(ms + EPS) * g).astype(o_ref.dtype)


def _norm_rope_store(acc, g, tab_ref, o_ref):
    rows, width = acc.shape
    grp_r = lax.broadcasted_iota(jnp.int32, (width, width), 0) // DIFF_HD
    grp_c = lax.broadcasted_iota(jnp.int32, (width, width), 1) // DIFF_HD
    ones_bd = jnp.where(grp_r == grp_c, 1.0, 0.0).astype(BF16)
    ms = _dot((acc * acc).astype(BF16), ones_bd) * (1.0 / DIFF_HD)
    y = acc * lax.rsqrt(ms + EPS)
    cos = tab_ref[:, 0:V7X_LANES]
    sin_up = tab_ref[:, V7X_LANES:2 * V7X_LANES]
    sin_dn = tab_ref[:, 2 * V7X_LANES:3 * V7X_LANES]
    half = ROT_DIM // 2
    for k in range(width // V7X_LANES):
        yk = y[:, k * V7X_LANES:(k + 1) * V7X_LANES] * g
        out = (yk * cos + pltpu.roll(yk, half, 1) * sin_up
               + pltpu.roll(yk, V7X_LANES - half, 1) * sin_dn)
        o_ref[:, k * V7X_LANES:(k + 1) * V7X_LANES] = out.astype(o_ref.dtype)


def _proj_headnorm_kernel(h_ref, w_ref, g_ref, o_ref, *, hd):
    _headnorm_store(_dot(h_ref[...], w_ref[...]), g_ref[...], o_ref, hd)


PROJ_SLAB = 256


FINAL_GROUPS = (1, 2, 4, 5)


def _proj_in_kernel(h_ref, w_ref, gqa_ref, gka_ref, gqb_ref, gkb_ref, gqm_ref, tab_ref, *rest,
                    blocks_per_group, hd_m, final):
    i = pl.program_id(0)
    j = pl.program_id(1)
    group = j // blocks_per_group
    tm = h_ref.shape[0]
    tn = w_ref.shape[1]
    n_prev = 0 if final is None or final['layer'] == 0 else 4
    qa_ref, ka_ref, va_ref, qb_ref, kb_ref, vb_ref, qm_ref = rest[n_prev:n_prev + 7]
    out_refs = {0: qa_ref, 1: ka_ref, 2: va_ref, 3: qb_ref, 4: kb_ref, 5: vb_ref, 6: qm_ref}
    if final is not None:
        final_refs = dict(zip(FINAL_GROUPS, rest[n_prev + 7:n_prev + 11]))
        stage, zeros, sem, zero_sem = rest[n_prev + 11:]
        heads_per_block = tn // V7X_LANES
        bps = final['seq_len'] // tm
        clear_slots = range(1, final['depth']) if final['layer'] == 0 else ()

        @pl.when((i == 0) & (j == 0))
        def _():
            zeros[...] = jnp.zeros(zeros.shape, zeros.dtype)

        def head_copies(g, c, slot):
            keep = final['a_keep'] if g in (1, 2) else final['seq_len']
            if keep >= tm:
                rows, r_lo, t0 = tm, 0, (i % bps) * tm - (final['seq_len'] - keep)
            else:
                rows, r_lo, t0 = keep, tm - keep, 0
            out = []
            for hh in range(heads_per_block):
                where = (i // bps, pl.ds(t0, rows), c * heads_per_block + hh, slice(None))
                src = stage.at[slot, pl.ds(r_lo, rows), pl.ds(hh * V7X_LANES, V7X_LANES)]
                out.append(pltpu.make_async_copy(src, final_refs[g].at[(final['layer'],) + where], sem.at[hh]))
                for p in clear_slots:
                    out.append(pltpu.make_async_copy(zeros.at[pl.ds(0, rows), :], final_refs[g].at[(p,) + where],
                                                     zero_sem.at[p - 1, hh]))
            return out

        def kept(g):
            keep = final['a_keep'] if g in (1, 2) else final['seq_len']
            return (i % bps) >= bps - max(keep // tm, 1)

        def wait_block(g, c, slot):
            @pl.when(kept(g))
            def _():
                for cp in head_copies(g, c, slot):
                    cp.wait()

    def run(g, epilogue):
        o_ref = out_refs[g]

        def body():
            slabs = [slice(c0, c0 + PROJ_SLAB) for c0 in range(0, tn, PROJ_SLAB)]
            accs = [_dot(h_ref[...], w_ref[:, cols]) for cols in slabs]
            for acc, cols in zip(accs, slabs):
                epilogue(acc, o_ref.at[:, cols])
            if final is None:
                return
            c = j - g * blocks_per_group
            slot = j % 2
            if g in FINAL_GROUPS:
                @pl.when(c > 0)
                def _():
                    wait_block(g, c - 1, 1 - slot)
            if g - 1 in FINAL_GROUPS:
                @pl.when(c == 0)
                def _():
                    wait_block(g - 1, blocks_per_group - 1, 1 - slot)
            if g in FINAL_GROUPS:
                @pl.when(kept(g))
                def _():
                    stage[slot] = o_ref[...]
                    for cp in head_copies(g, c, slot):
                        cp.start()
        return body

    headnorm = lambda g_ref, hd: (lambda acc, o: _headnorm_store(acc, g_ref[...], o, hd))
    norm_rope = lambda g_ref: (lambda acc, o: _norm_rope_store(acc, g_ref[...], tab_ref, o))
    plain = lambda acc, o: o.__setitem__(Ellipsis, acc)
    bodies = (
        run(0, headnorm(gqa_ref, HD_A)), run(1, headnorm(gka_ref, HD_A)), run(2, plain),
        run(3, norm_rope(gqb_ref)), run(4, norm_rope(gkb_ref)), run(5, plain),
        run(6, headnorm(gqm_ref, hd_m)),
    )
    for n, body in enumerate(bodies):
        pl.when(group == n)(body)


def proj_in(h, w_in, gains, rope_tab, *, tm, tn=COL_BLOCK, final=None):
    m, d = h.shape
    bw = d // 2
    hd_m = bw // H_M
    tm = _row_tile(m, tm)
    assert bw % tn == 0 and tn % PROJ_SLAB == 0 and PROJ_SLAB % hd_m == 0 and rope_tab.shape[0] % tm == 0
    bpg = bw // tn
    n_tab = rope_tab.shape[0] // tm
    g_a_q, g_a_k, g_b_q, g_b_k, g_m_q = gains
    tile2 = lambda g: jnp.tile(g.reshape(1, -1), (1, 2))
    gain_args = [g_a_q.reshape(1, HD_A), g_a_k.reshape(1, HD_A), tile2(g_b_q), tile2(g_b_k), g_m_q.reshape(1, hd_m)]
    const = lambda a: pl.BlockSpec(a.shape, lambda i, j: (0, 0))

    def out_spec(n):
        return pl.BlockSpec((tm, tn), lambda i, j: (i, jnp.clip(j - n * bpg, 0, bpg - 1)))

    dtypes = (BF16, F32, F32, BF16, F32, F32, BF16)
    out_specs = [out_spec(n) for n in range(7)]
    out_shape = [jax.ShapeDtypeStruct((m, bw), t) for t in dtypes]
    prev, scratch, aliases, semantics = [], [], {}, ("parallel", "arbitrary")
    n_fixed_inputs = 8
    if final is not None:
        t, keep, batch = final['seq_len'], final['a_keep'], final['batch']
        heads = bw // V7X_LANES
        assert HD_A == V7X_LANES and 2 * DIFF_HD == V7X_LANES and m == batch * t and t % tm == 0
        assert keep % tm == 0 or (keep < tm and keep % 8 == 0)
        prev = list(final['prev'] or ())
        assert len(prev) == (4 if final['layer'] else 0)
        depth = final['depth']
        out_shape += [jax.ShapeDtypeStruct((depth, batch, rows, heads, V7X_LANES), F32) for rows in (keep, keep, t, t)]
        out_specs += [pl.BlockSpec(memory_space=pl.ANY)] * 4
        hpb = tn // V7X_LANES
        scratch = [pltpu.VMEM((2, tm, tn), F32), pltpu.VMEM((tm, V7X_LANES), F32), pltpu.SemaphoreType.DMA((hpb,)),
                   pltpu.SemaphoreType.DMA((max(depth - 1, 1), hpb))]
        aliases = {n_fixed_inputs + k: 7 + k for k in range(len(prev))}
        semantics = ("arbitrary", "arbitrary")
        final = {k: v for k, v in final.items() if k != 'prev'}
    out_bytes = sum(tm * tn * jnp.dtype(t).itemsize for t in dtypes)
    vmem = 2 * (tm * d * 2 + d * tn * 2 + tm * 3 * V7X_LANES * 4 + out_bytes) + 4 * tm * tn * 4
    return pl.pallas_call(
        functools.partial(_proj_in_kernel, blocks_per_group=bpg, hd_m=hd_m, final=final),
        grid=(m // tm, 7 * bpg),
        in_specs=[pl.BlockSpec((tm, d), lambda i, j: (i, 0)), pl.BlockSpec((d, tn), lambda i, j: (0, j))]
                 + [const(g) for g in gain_args]
                 + [pl.BlockSpec((tm, 3 * V7X_LANES), lambda i, j: (i % n_tab, 0))]
                 + [pl.BlockSpec(memory_space=pl.ANY)] * len(prev),
        out_specs=out_specs,
        out_shape=out_shape,
        input_output_aliases=aliases,
        scratch_shapes=scratch,
        compiler_params=_compiler_params(semantics, vmem),
        name="proj_in",
    )(h, w_in, *gain_args, rope_tab, *prev)


def _proj_residual_kernel(h_ref, w_ref, x_ref, o_ref):
    o_ref[...] = x_ref[...] + _dot(h_ref[...], w_ref[...])


def _proj_out_norm_kernel(m_ref, w_ref, x_ref, g_ref, o_ref, h_ref, xrow):
    j = pl.program_id(1)
    y = x_ref[...] + _dot(m_ref[...], w_ref[...])
    o_ref[...] = y
    xrow[j] = y

    @pl.when(j == pl.num_programs(1) - 1)
    def _():
        nj, _, tn = xrow.shape
        ssq = None
        for jj in range(nj):
            xb = xrow[jj]
            part = jnp.sum(xb * xb, axis=-1, keepdims=True)
            ssq = part if ssq is None else ssq + part
        scale = lax.rsqrt(ssq * (1.0 / (nj * tn)) + EPS)
        for jj in range(nj):
            cols = slice(jj * tn, (jj + 1) * tn)
            h_ref[:, cols] = (xrow[jj] * scale * g_ref[:, cols]).astype(h_ref.dtype)


def proj_out_norm(merged, w_out, x, g, *, tm, tn=2 * COL_BLOCK):
    m, d = x.shape
    tm = _row_tile(m, tm)
    tn = min(tn, d)
    assert d % tn == 0
    row_spec = pl.BlockSpec((tm, d), lambda i, j: (i, 0))
    blk_spec = pl.BlockSpec((tm, tn), lambda i, j: (i, j))
    vmem = 2 * (tm * d * 2 + d * tn * 2 + 2 * tm * tn * 4 + tm * d * 2) + tm * d * 4 + 2 * tm * tn * 4
    return pl.pallas_call(
        _proj_out_norm_kernel,
        grid=(m // tm, d // tn),
        in_specs=[row_spec, pl.BlockSpec((d, tn), lambda i, j: (0, j)), blk_spec,
                  pl.BlockSpec((1, d), lambda i, j: (0, 0))],
        out_specs=[blk_spec, row_spec],
        out_shape=[jax.ShapeDtypeStruct((m, d), F32), jax.ShapeDtypeStruct((m, d), BF16)],
        scratch_shapes=[pltpu.VMEM((d // tn, tm, tn), F32)],
        compiler_params=_compiler_params(("parallel", "arbitrary"), vmem),
        name="proj_out_norm",
    )(merged, w_out, x, g.reshape(1, d))


def _proj_call(kernel_fn, h, w, col0, ncols, out_dtype, extras, *, tm=ROW_BLOCK, tn=COL_BLOCK, name):
    m, k = h.shape
    tm = _row_tile(m, tm)
    tn = min(tn, ncols)
    assert ncols % tn == 0 and col0 % tn == 0, (ncols, col0, tn)
    cb = col0 // tn
    in_specs = [pl.BlockSpec((tm, k), lambda i, j: (i, 0)),
                pl.BlockSpec((k, tn), lambda i, j: (0, j + cb))]
    in_specs += [spec for _, spec in extras]
    extra_bytes = sum(int(np.prod(spec.block_shape)) * a.dtype.itemsize for a, spec in extras)
    vmem = 2 * (tm * k * 2 + k * tn * 2 + tm * tn * 4 + extra_bytes) + tm * tn * 8
    return pl.pallas_call(
        kernel_fn,
        grid=(m // tm, ncols // tn),
        in_specs=in_specs,
        out_specs=pl.BlockSpec((tm, tn), lambda i, j: (i, j)),
        out_shape=jax.ShapeDtypeStruct((m, ncols), out_dtype),
        compiler_params=_compiler_params(("parallel", "arbitrary"), vmem),
        name=name,
    )(h, w, *[a for a, _ in extras])


def _rope_table(pos):
    half = ROT_DIM // 2
    inv_freq = jnp.exp(jnp.arange(half, dtype=F32) * (-2.0 * math.log(ROPE_THETA) / ROT_DIM))
    ang = pos.astype(F32)[:, None] * inv_freq[None, :]
    cos = jnp.cos(ang)
    sin = jnp.sin(ang)
    p = pos.shape[0]
    rest = DIFF_HD - ROT_DIM
    c64 = jnp.concatenate([cos, cos, jnp.ones((p, rest), F32)], axis=1)
    up64 = jnp.concatenate([jnp.zeros((p, half), F32), sin, jnp.zeros((p, rest), F32)], axis=1)
    dn64 = jnp.concatenate([-sin, jnp.zeros((p, half + rest), F32)], axis=1)
    return jnp.concatenate([c64, c64, up64, up64, dn64, dn64], axis=1)


A_QBLK = 4 * CHUNK
A_KBLK = BAND_PAST + A_QBLK
A_BIAS_W = 1024


def _rel_bias_row(tab):
    assert A_KBLK + A_QBLK - 1 <= A_BIAS_W
    lo = BAND_PAST - REL_CLIP
    hi = BAND_PAST + REL_CLIP + 1
    rep = lambda col, n: jnp.repeat(tab[:, col:col + 1], n, axis=1)
    row = jnp.concatenate([rep(0, lo), tab, rep(2 * REL_CLIP, A_KBLK - hi), rep(0, A_BIAS_W - A_KBLK)], axis=1)
    return row[:, None, :]


def _toeplitz_bias(row, rows, width):
    full = pltpu.roll(jnp.broadcast_to(row, (rows, A_BIAS_W)), 0, 1, stride=1, stride_axis=0)
    return full[:, :width]


def _softmax_pv(s, v):
    m = jnp.max(s, axis=-1, keepdims=True)
    p = jnp.exp(s - m)
    l = jnp.sum(p, axis=-1, keepdims=True)
    return _dot(p.astype(BF16), v), l


def _attn_a_prompt_kernel(q_ref, k_ref, v_ref, row_ref, o_ref, kb, vb, *, t):
    kscale = (HD_A ** -0.5) * LOG2E
    kb[...] = (k_ref[0] * kscale).astype(BF16)
    vb[:, :HD_A] = v_ref[0].astype(BF16)
    vb[:, HD_A:] = jnp.ones((t, HD_A), BF16)
    qc = lax.broadcasted_iota(jnp.int32, (A_QBLK, A_KBLK), 0) // CHUNK
    kc = lax.broadcasted_iota(jnp.int32, (A_QBLK, A_KBLK), 1) // CHUNK
    inband = (kc >= qc) & (kc <= qc + BAND_CHUNKS)
    bias = jnp.where(inband, _toeplitz_bias(row_ref[0], A_QBLK, A_KBLK) * LOG2E, NEG_INF)
    for i in range(t // A_QBLK):
        r0 = i * A_QBLK
        k0 = max(r0 - BAND_PAST, 0)
        k1 = r0 + A_QBLK
        q = q_ref[0, r0:k1, :]
        s = _dot_nt(q, kb[k0:k1, :]) + bias[:, A_KBLK - (k1 - k0):]
        p = jnp.exp2(s - jnp.max(s, axis=-1, keepdims=True))
        o = _dot(p.astype(BF16), vb[k0:k1, :])
        o_ref[0, r0:k1, :] = (o[:, :HD_A] / o[:, HD_A:]).astype(o_ref.dtype)


def attn_a_prompt(qa, ka, va, bias_row, b, t):
    h = qa.shape[1] // HD_A
    assert t % A_QBLK == 0 and A_QBLK % V7X_LANES == 0
    q3, k3, v3 = (a.reshape(b, t, h * HD_A) for a in (qa, ka, va))
    spec = pl.BlockSpec((1, t, HD_A), lambda bi, hi: (bi, 0, hi))
    vmem = 2 * t * HD_A * (2 + 4 + 4 + 2) + 2 * t * HD_A * 2 + 24 * MIB
    out = pl.pallas_call(
        functools.partial(_attn_a_prompt_kernel, t=t),
        grid=(b, h),
        in_specs=[spec, spec, spec, pl.BlockSpec((1, 1, A_BIAS_W), lambda bi, hi: (hi, 0, 0))],
        out_specs=spec,
        out_shape=jax.ShapeDtypeStruct((b, t, h * HD_A), BF16),
        scratch_shapes=[pltpu.VMEM((t, HD_A), BF16), pltpu.VMEM((t, 2 * HD_A), BF16)],
        compiler_params=_compiler_params(("parallel", "parallel"), vmem),
        name="attn_a_prompt",
    )(q3, k3, v3, bias_row)
    return out.reshape(b * t, h * HD_A)


def _attn_a_sample_kernel(q_ref, kn_ref, vn_ref, row_ref, mask_ref, kc_hbm, vc_hbm, o_ref, kbuf, vbuf, sem,
                          *, layer, heads, a_len):
    b = pl.program_id(0)
    nb = pl.num_programs(0)
    scale = HD_A ** -0.5
    t = q_ref.shape[2]

    def copies(bi, slot):
        out = []
        for h in range(heads):
            out.append(pltpu.make_async_copy(kc_hbm.at[layer, bi, :, h, :], kbuf.at[slot, h], sem.at[0, slot, h]))
            out.append(pltpu.make_async_copy(vc_hbm.at[layer, bi, :, h, :], vbuf.at[slot, h], sem.at[1, slot, h]))
        return out

    @pl.when(b == 0)
    def _():
        for c in copies(b, 0):
            c.start()

    slot = b % 2

    @pl.when(b + 1 < nb)
    def _():
        for c in copies(b + 1, 1 - slot):
            c.start()

    for c in copies(b, slot):
        c.wait()
    for h in range(heads):
        q = q_ref[0, h]
        bias = _toeplitz_bias(row_ref[h], t, a_len + t) + mask_ref[...]
        sc = _dot_nt(q, kbuf[slot, h].astype(BF16)) * scale + bias[:, :a_len]
        sn = _dot_nt(q, kn_ref[0, h].astype(BF16)) * scale + bias[:, a_len:]
        m = jnp.maximum(jnp.max(sc, axis=-1, keepdims=True), jnp.max(sn, axis=-1, keepdims=True))
        pc = jnp.exp(sc - m)
        pn = jnp.exp(sn - m)
        l = jnp.sum(pc, axis=-1, keepdims=True) + jnp.sum(pn, axis=-1, keepdims=True)
        o = _dot(pc.astype(BF16), vbuf[slot, h].astype(BF16)) + _dot(pn.astype(BF16), vn_ref[0, h].astype(BF16))
        o_ref[0, h] = (o / l).astype(o_ref.dtype)


def _by_head(a, b, t, heads, hd):
    return a.reshape(b, t, heads, hd).transpose(0, 2, 1, 3)


def attn_a_sample(qa, ka, va, cache_k, cache_v, layer, bias_row, mask, b, t):
    heads = qa.shape[1] // HD_A
    a_len = cache_k.shape[2]
    assert a_len == BAND_PAST and a_len + t <= A_KBLK
    head_spec = pl.BlockSpec((1, heads, t, HD_A), lambda bi: (bi, 0, 0, 0))
    vmem = 2 * 2 * heads * a_len * HD_A * 4 + 2 * 4 * heads * t * HD_A * 4 + 16 * MIB
    out = pl.pallas_call(
        functools.partial(_attn_a_sample_kernel, layer=layer, heads=heads, a_len=a_len),
        grid=(b,),
        in_specs=[head_spec, head_spec, head_spec,
                  pl.BlockSpec(bias_row.shape, lambda bi: (0, 0, 0)),
                  pl.BlockSpec(mask.shape, lambda bi: (0, 0)),
                  pl.BlockSpec(memory_space=pl.ANY), pl.BlockSpec(memory_space=pl.ANY)],
        out_specs=head_spec,
        out_shape=jax.ShapeDtypeStruct((b, heads, t, HD_A), BF16),
        scratch_shapes=[pltpu.VMEM((2, heads, a_len, HD_A), F32), pltpu.VMEM((2, heads, a_len, HD_A), F32),
                        pltpu.SemaphoreType.DMA((2, 2, heads))],
        compiler_params=_compiler_params(("arbitrary",), vmem),
        name="attn_a_sample",
    )(*(_by_head(a, b, t, heads, HD_A) for a in (qa, ka, va)), bias_row, mask, cache_k, cache_v)
    return out.transpose(0, 2, 1, 3).reshape(b * t, heads * HD_A)


def _diff_lambda(lam_ref, lam_init):
    v = lam_ref[...]
    d1 = jnp.sum(v[0:1] * v[1:2], axis=-1, keepdims=True)
    d2 = jnp.sum(v[2:3] * v[3:4], axis=-1, keepdims=True)
    return jnp.exp(d1) - jnp.exp(d2) + lam_init


def _split_diff_queries(q):
    lane = lax.broadcasted_iota(jnp.int32, q.shape, 1)
    qs = q * jnp.asarray(DIFF_HD ** -0.5, q.dtype)
    zero = jnp.zeros_like(qs)
    return jnp.where(lane < DIFF_HD, qs, zero), jnp.where(lane >= DIFF_HD, qs, zero)


def _stack_diff_queries(q):
    return jnp.concatenate(_split_diff_queries(q), axis=0)


def _diff_post(o, g, post_scale):
    ms = jnp.mean(o * o, axis=-1, keepdims=True)
    return (o * lax.rsqrt(ms + EPS) * g) * post_scale


def _diff_finish(l, acc, lam, g, post_scale, tq):
    o = acc[:tq] / l[:tq] - lam * (acc[tq:] / l[tq:])
    return _diff_post(o, g, post_scale)


B_TQ = 8 * CHUNK
B_SAMPLE_SLOTS = 4


def _online_step(carry, s, v_ext):
    m, acc = carry
    m_new = jnp.maximum(m, jnp.max(s, axis=-1, keepdims=True))
    alpha = jnp.exp2(m - m_new)
    p = jnp.exp2(s - m_new)
    acc = alpha * acc + _dot(p.astype(BF16), v_ext)
    return m_new, acc


def _attn_b_prompt_kernel(lam_ref, q_ref, k_ref, v_ref, g_ref, o_ref, kb, vb, *, t, lam_init):
    tq = B_TQ
    hd = 2 * DIFF_HD
    kb[...] = (k_ref[0] * LOG2E).astype(BF16)
    vb[:, :hd] = v_ref[0].astype(BF16)
    vb[:, hd:] = jnp.ones((t, hd), BF16)
    lam = _diff_lambda(lam_ref, lam_init)
    row = lax.broadcasted_iota(jnp.int32, (tq, tq), 0)
    col = lax.broadcasted_iota(jnp.int32, (tq, tq), 1)
    diag_ok = (col // CHUNK) <= (row // CHUNK)
    for qi in range(t // tq):
        q0 = qi * tq
        qs = _split_diff_queries(q_ref[0, q0:q0 + tq, :])
        spans = ([(0, q0, False)] if q0 else []) + [(q0, tq, True)]
        outs = []
        for c in range(2):
            carry = (jnp.full((tq, 1), NEG_INF, F32), jnp.zeros((tq, 2 * hd), F32))
            for k0, width, masked in spans:
                s = _dot_nt(qs[c], kb[k0:k0 + width, :])
                if masked:
                    s = jnp.where(diag_ok, s, NEG_INF)
                carry = _online_step(carry, s, vb[k0:k0 + width, :])
            outs.append(carry[1][:, :hd] / carry[1][:, hd:])
        o = outs[0] - lam * outs[1]
        o_ref[0, q0:q0 + tq, :] = _diff_post(o, g_ref[...], 1.0 - lam_init).astype(o_ref.dtype)


def attn_b_prompt(qb, kb, vb, lam_params, subln_g, lam_init, b, t):
    hd = 2 * DIFF_HD
    heads = qb.shape[1] // hd
    assert t % B_TQ == 0
    q3, k3, v3 = (a.reshape(b, t, heads * hd) for a in (qb, kb, vb))
    spec = pl.BlockSpec((1, t, hd), lambda bi, hi: (bi, 0, hi))
    vmem = 2 * t * hd * (2 + 4 + 4 + 2) + 2 * t * hd * 2 + 32 * MIB
    out = pl.pallas_call(
        functools.partial(_attn_b_prompt_kernel, t=t, lam_init=lam_init),
        grid=(b, heads),
        in_specs=[pl.BlockSpec(lam_params.shape, lambda bi, hi: (0, 0)), spec, spec, spec,
                  pl.BlockSpec((1, hd), lambda bi, hi: (0, 0))],
        out_specs=spec,
        out_shape=jax.ShapeDtypeStruct((b, t, heads * hd), BF16),
        scratch_shapes=[pltpu.VMEM((t, hd), BF16), pltpu.VMEM((t, 2 * hd), BF16)],
        compiler_params=_compiler_params(("parallel", "parallel"), vmem),
        name="attn_b_prompt",
    )(lam_params, q3, k3, v3, subln_g.reshape(1, hd))
    return out.reshape(b * t, heads * hd)


def _attn_b_sample_kernel(lam_ref, q_ref, kn_ref, vn_ref, g_ref, kc_hbm, vc_hbm, o_ref, kbuf, vbuf, sem,
                          *, layer, heads, t, lam_init):
    b = pl.program_id(0)
    nb = pl.num_programs(0)

    def copies(bi, h, slot):
        return (pltpu.make_async_copy(kc_hbm.at[layer, bi, :, h, :], kbuf.at[slot], sem.at[0, slot]),
                pltpu.make_async_copy(vc_hbm.at[layer, bi, :, h, :], vbuf.at[slot], sem.at[1, slot]))

    def start(bi, h, slot):
        for c in copies(bi, h, slot):
            c.start()

    ahead = B_SAMPLE_SLOTS - 1

    @pl.when(b == 0)
    def _():
        for h in range(ahead):
            start(b, h, h % B_SAMPLE_SLOTS)

    lam = _diff_lambda(lam_ref, lam_init)
    for h in range(heads):
        slot = h % B_SAMPLE_SLOTS
        nxt = h + ahead
        if nxt < heads:
            start(b, nxt, nxt % B_SAMPLE_SLOTS)
        else:
            @pl.when(b + 1 < nb)
            def _():
                start(b + 1, nxt - heads, nxt % B_SAMPLE_SLOTS)
        for c in copies(b, h, slot):
            c.wait()
        q2 = _stack_diff_queries(q_ref[0, h])
        sc = _dot_nt(q2, kbuf[slot].astype(BF16))
        sn = _dot_nt(q2, kn_ref[0, h].astype(BF16))
        m = jnp.maximum(jnp.max(sc, axis=-1, keepdims=True), jnp.max(sn, axis=-1, keepdims=True))
        pc = jnp.exp(sc - m)
        pn = jnp.exp(sn - m)
        l = jnp.sum(pc, axis=-1, keepdims=True) + jnp.sum(pn, axis=-1, keepdims=True)
        acc = _dot(pc.astype(BF16), vbuf[slot].astype(BF16)) + _dot(pn.astype(BF16), vn_ref[0, h].astype(BF16))
        o_ref[0, h] = _diff_finish(l, acc, lam, g_ref[...], 1.0 - lam_init, t).astype(o_ref.dtype)


def attn_b_sample(qb, kb, vb, cache_k, cache_v, layer, lam_params, subln_g, lam_init, b, t):
    hd = 2 * DIFF_HD
    heads = qb.shape[1] // hd
    assert heads % B_SAMPLE_SLOTS == 0 and heads >= B_SAMPLE_SLOTS
    past = cache_k.shape[2]
    head_spec = pl.BlockSpec((1, heads, t, hd), lambda bi: (bi, 0, 0, 0))
    vmem = 2 * B_SAMPLE_SLOTS * past * hd * 4 + 2 * 3 * heads * t * hd * 4 + 12 * 2 * t * past * 4
    out = pl.pallas_call(
        functools.partial(_attn_b_sample_kernel, layer=layer, heads=heads, t=t, lam_init=lam_init),
        grid=(b,),
        in_specs=[pl.BlockSpec(lam_params.shape, lambda bi: (0, 0)), head_spec, head_spec, head_spec,
                  pl.BlockSpec((1, hd), lambda bi: (0, 0)),
                  pl.BlockSpec(memory_space=pl.ANY), pl.BlockSpec(memory_space=pl.ANY)],
        out_specs=head_spec,
        out_shape=jax.ShapeDtypeStruct((b, heads, t, hd), BF16),
        scratch_shapes=[pltpu.VMEM((B_SAMPLE_SLOTS, past, hd), F32), pltpu.VMEM((B_SAMPLE_SLOTS, past, hd), F32),
                        pltpu.SemaphoreType.DMA((2, B_SAMPLE_SLOTS))],
        compiler_params=_compiler_params(("arbitrary",), vmem),
        name="attn_b_sample",
    )(lam_params, *(_by_head(a, b, t, heads, hd) for a in (qb, kb, vb)), subln_g.reshape(1, hd), cache_k, cache_v)
    return out.transpose(0, 2, 1, 3).reshape(b * t, heads * hd)


def _attn_m_kernel(q_ref, k_ref, v_ref, o_ref, *, heads, hd):
    scale = hd ** -0.5
    for h in range(heads):
        sl = slice(h * hd, (h + 1) * hd)
        s = _dot_nt(q_ref[0, :, sl], k_ref[0, :, sl].astype(BF16)) * scale
        o, l = _softmax_pv(s, v_ref[0, :, sl].astype(BF16))
        o_ref[0, :, sl] = (o / l).astype(o_ref.dtype)


def attn_m_prompt(qm, mem_k, mem_v, b, t, *, tq=B_TQ):
    width = qm.shape[1]
    hd = width // H_M
    tq = min(tq, t)
    assert t % tq == 0
    n = mem_k.shape[0] // b
    q_spec = pl.BlockSpec((1, tq, width), lambda bi, qi: (bi, qi, 0))
    kv_spec = pl.BlockSpec((1, n, width), lambda bi, qi: (bi, 0, 0))
    vmem = 2 * (2 * n * width * 4 + 2 * tq * width * 2) + 8 * tq * n * 4
    out = pl.pallas_call(
        functools.partial(_attn_m_kernel, heads=H_M, hd=hd),
        grid=(b, t // tq),
        in_specs=[q_spec, kv_spec, kv_spec],
        out_specs=q_spec,
        out_shape=jax.ShapeDtypeStruct((b, t, width), BF16),
        compiler_params=_compiler_params(("parallel", "arbitrary"), vmem),
        name="attn_m_prompt",
    )(qm.reshape(b, t, width), mem_k.reshape(b, n, width), mem_v.reshape(b, n, width))
    return out.reshape(b * t, width)


def _attn_m_sample_kernel(q_ref, kc_hbm, vc_hbm, o_ref, kbuf, vbuf, sem, *, layer, heads, hd):
    b = pl.program_id(0)
    nb = pl.num_programs(0)
    scale = hd ** -0.5

    def copies(bi, slot):
        out = []
        for h in range(heads):
            out.append(pltpu.make_async_copy(kc_hbm.at[layer, bi, :, h, :], kbuf.at[slot, h], sem.at[0, slot, h]))
            out.append(pltpu.make_async_copy(vc_hbm.at[layer, bi, :, h, :], vbuf.at[slot, h], sem.at[1, slot, h]))
        return out

    @pl.when(b == 0)
    def _():
        for c in copies(b, 0):
            c.start()

    slot = b % 2

    @pl.when(b + 1 < nb)
    def _():
        for c in copies(b + 1, 1 - slot):
            c.start()

    for c in copies(b, slot):
        c.wait()
    for h in range(heads):
        s = _dot_nt(q_ref[0, h], kbuf[slot, h].astype(BF16)) * scale
        o, l = _softmax_pv(s, vbuf[slot, h].astype(BF16))
        o_ref[0, h] = (o / l).astype(o_ref.dtype)


def attn_m_sample(qm, cache_k, cache_v, layer, b, t):
    n, heads, hd = cache_k.shape[2:]
    head_spec = pl.BlockSpec((1, heads, t, hd), lambda bi: (bi, 0, 0, 0))
    vmem = 2 * 2 * heads * n * hd * 4 + 2 * 2 * heads * t * hd * 2 + 8 * MIB
    out = pl.pallas_call(
        functools.partial(_attn_m_sample_kernel, layer=layer, heads=heads, hd=hd),
        grid=(b,),
        in_specs=[head_spec, pl.BlockSpec(memory_space=pl.ANY), pl.BlockSpec(memory_space=pl.ANY)],
        out_specs=head_spec,
        out_shape=jax.ShapeDtypeStruct((b, heads, t, hd), BF16),
        scratch_shapes=[pltpu.VMEM((2, heads, n, hd), F32), pltpu.VMEM((2, heads, n, hd), F32),
                        pltpu.SemaphoreType.DMA((2, 2, heads))],
        compiler_params=_compiler_params(("arbitrary",), vmem),
        name="attn_m_sample",
    )(_by_head(qm, b, t, heads, hd), cache_k, cache_v)
    return out.transpose(0, 2, 1, 3).reshape(b * t, heads * hd)


def _merge_kernel(h_ref, oa_ref, ob_ref, om_ref, wga_ref, wgb_ref, wgm_ref, gb_ref, wbr_ref, o_ref):
    h = h_ref[...]
    acc = None
    for n, (o_n, wg_n) in enumerate(((oa_ref, wga_ref), (ob_ref, wgb_ref), (om_ref, wgm_ref))):
        gate = jax.nn.sigmoid(_dot(h, wg_n[...]) + gb_ref[n])
        term = gate * _dot(o_n[...], wbr_ref[n])
        acc = term if acc is None else acc + term
    o_ref[...] = acc.astype(o_ref.dtype)


def merge_branches(h, oa, ob, om, w_in, gate_col0, gate_b, w_br, *, tm, tn=COL_BLOCK):
    m, bw = oa.shape
    d = w_br.shape[2]
    tm = _row_tile(m, tm)
    assert d % tn == 0 and gate_col0 % tn == 0
    nj = d // tn
    g0 = gate_col0 // tn
    row_spec = lambda width: pl.BlockSpec((tm, width), lambda i, j: (i, 0))
    gate_w_specs = [pl.BlockSpec((d, tn), functools.partial(lambda i, j, n: (0, g0 + n * nj + j), n=n))
                    for n in range(N_BRANCH)]
    vmem = 2 * (tm * d * 2 + 3 * tm * bw * 2 + 3 * d * tn * 2 + 3 * bw * tn * 2 + tm * tn * 2) + 6 * tm * tn * 4
    return pl.pallas_call(
        _merge_kernel,
        grid=(m // tm, nj),
        in_specs=[row_spec(d), row_spec(bw), row_spec(bw), row_spec(bw)] + gate_w_specs
                 + [pl.BlockSpec((N_BRANCH, 1, tn), lambda i, j: (0, 0, j)),
                    pl.BlockSpec((N_BRANCH, bw, tn), lambda i, j: (0, 0, j))],
        out_specs=pl.BlockSpec((tm, tn), lambda i, j: (i, j)),
        out_shape=jax.ShapeDtypeStruct((m, d), BF16),
        compiler_params=_compiler_params(("parallel", "arbitrary"), vmem),
        name="merge_branches",
    )(h, oa, ob, om, w_in, w_in, w_in, gate_b, w_br)


def _ffn_up_kernel(h_ref, wa_ref, wb_ref, cw_ref, cb_ref, st_ref, g_ref, cn_ref, carry, *, nb, tb, blocks_per_seq):
    i = pl.program_id(0)
    j = pl.program_id(1)
    tm, tn = g_ref.shape
    if nb == 1:
        @pl.when((i % blocks_per_seq) == 0)
        def _():
            carry[j] = st_ref[0]

        trow = lax.broadcasted_iota(jnp.int32, (ROW_TILE, PROJ_SLAB), 0)
    else:
        trow = lax.broadcasted_iota(jnp.int32, (nb, tb, PROJ_SLAB), 1).reshape(tm, PROJ_SLAB)
    slabs = [slice(c0, c0 + PROJ_SLAB) for c0 in range(0, tn, PROJ_SLAB)]

    def gated(a, am1, am2, bgate, cols):
        cw = cw_ref[:, cols]
        c = cb_ref[:, cols] + am2 * cw[0:1] + am1 * cw[1:2] + a * cw[2:3]
        gelu = 0.5 * c * (1.0 + lax.erf(c * (2.0 ** -0.5)))
        return (gelu * bgate).astype(g_ref.dtype)

    if nb == 1:
        parts = FFN_ROW_PARTS if tm % (FFN_ROW_PARTS * ROW_TILE) == 0 else 1
        rp = tm // parts
        units = [(r0, cols) for cols in slabs for r0 in range(0, tm, rp)]
        dots = [(_dot(h_ref[r0:r0 + rp, :], wa_ref[:, cols]), _dot(h_ref[r0:r0 + rp, :], wb_ref[:, cols]))
                for r0, cols in units]
        prev = {}
        for (r0, cols), (a, bgate) in zip(units, dots):
            g_ref[r0:r0 + rp, cols] = gated(a, pltpu.roll(a, 1, 0), pltpu.roll(a, 2, 0), bgate, cols)
            before = carry[j, :, cols] if r0 == 0 else prev[cols.start]
            top = a[0:ROW_TILE]
            p0 = jnp.broadcast_to(before[0:1], top.shape)
            p1 = jnp.broadcast_to(before[1:2], top.shape)
            am1 = jnp.where(trow == 0, p1, pltpu.roll(top, 1, 0))
            am2 = jnp.where(trow == 0, p0, jnp.where(trow == 1, p1, pltpu.roll(top, 2, 0)))
            g_ref[r0:r0 + ROW_TILE, cols] = gated(top, am1, am2, bgate[0:ROW_TILE], cols)
            prev[cols.start] = a[rp - 2:rp]
            if r0 + rp == tm:
                carry[j, :, cols] = a[rp - 2:rp]
                cn_ref[0, :, cols] = a[rp - 2:rp]
        return

    h = h_ref[...]
    dots = [(_dot(h, wa_ref[:, cols]), _dot(h, wb_ref[:, cols])) for cols in slabs]
    for cols, (a, bgate) in zip(slabs, dots):
        st = st_ref[:, :, cols]
        p0 = jnp.broadcast_to(st[:, 0:1, :], (nb, tb, PROJ_SLAB)).reshape(tm, PROJ_SLAB)
        p1 = jnp.broadcast_to(st[:, 1:2, :], (nb, tb, PROJ_SLAB)).reshape(tm, PROJ_SLAB)
        cn_ref[:, :, cols] = a.reshape(nb, tb, PROJ_SLAB)[:, tb - 2:tb, :]
        am1 = jnp.where(trow == 0, p1, pltpu.roll(a, 1, 0))
        am2 = jnp.where(trow == 0, p0, jnp.where(trow == 1, p1, pltpu.roll(a, 2, 0)))
        g_ref[:, cols] = gated(a, am1, am2, bgate, cols)


def ffn_up(h, w_a, w_b, conv_w, conv_b, state, b, t, *, tm=ROW_BLOCK, tn=COL_BLOCK):
    m, d = h.shape
    f = w_a.shape[1]
    tm = _row_tile(m, tm)
    assert f % tn == 0 and tn % PROJ_SLAB == 0
    if tm >= t:
        assert tm % t == 0
        nb, tb, blocks_per_seq = tm // t, t, 1
    else:
        assert t % tm == 0
        nb, tb, blocks_per_seq = 1, tm, t // tm
    if nb > 1:
        seq_map = lambda i, j: (i, 0, j)
    else:
        seq_map = lambda i, j: (i // blocks_per_seq, 0, j)
    tail_map = lambda i, j: (i, 0, j)
    w_spec = pl.BlockSpec((d, tn), lambda i, j: (0, j))
    vmem = 2 * (tm * d * 2 + 2 * d * tn * 2 + tm * tn * 2 + 2 * nb * 8 * tn * 4) + 8 * tm * tn * 4
    g, conv_new = pl.pallas_call(
        functools.partial(_ffn_up_kernel, nb=nb, tb=tb, blocks_per_seq=blocks_per_seq),
        grid=(m // tm, f // tn),
        in_specs=[pl.BlockSpec((tm, d), lambda i, j: (i, 0)), w_spec, w_spec,
                  pl.BlockSpec((CONV_W, tn), lambda i, j: (0, j)),
                  pl.BlockSpec((1, tn), lambda i, j: (0, j)),
                  pl.BlockSpec((nb, CONV_W - 1, tn), seq_map)],
        out_specs=[pl.BlockSpec((tm, tn), lambda i, j: (i, j)),
                   pl.BlockSpec((nb, CONV_W - 1, tn), tail_map)],
        out_shape=[jax.ShapeDtypeStruct((m, f), BF16),
                   jax.ShapeDtypeStruct((b * blocks_per_seq, CONV_W - 1, f), F32)],
        scratch_shapes=[pltpu.VMEM((f // tn, CONV_W - 1, tn), F32)],
        compiler_params=_compiler_params(("arbitrary", "arbitrary"), vmem),
        name="ffn_up",
    )(h, w_a, w_b, conv_w, conv_b, state)
    return g, conv_new.reshape(b, blocks_per_seq, CONV_W - 1, f)[:, -1]


def _pad_cols(a, f_pad):
    return jnp.pad(a, [(0, 0)] * (a.ndim - 1) + [(0, f_pad - a.shape[-1])])


def _cast_kernel(w_ref, o_ref, *, valid_rows, valid_cols):
    rb, cb = o_ref.shape
    rows = pl.program_id(0) * rb + lax.broadcasted_iota(jnp.int32, (rb, cb), 0)
    cols = pl.program_id(1) * cb + lax.broadcasted_iota(jnp.int32, (rb, cb), 1)
    ok = (rows < valid_rows) & (cols < valid_cols)
    o_ref[...] = jnp.where(ok, w_ref[...], 0.0).astype(o_ref.dtype)


def cast_weight(w, layer, *, rb, cb, col0=0, ncols=None, out_rows=None, out_cols=None):
    _, r, c = w.shape
    ncols = c - col0 if ncols is None else ncols
    out_rows = r if out_rows is None else out_rows
    out_cols = ncols if out_cols is None else out_cols
    assert col0 % cb == 0 and out_rows % rb == 0 and out_cols % cb == 0
    c0 = col0 // cb
    last_r = (r - 1) // rb
    last_c = (col0 + ncols - 1) // cb
    in_map = lambda i, j: (layer, jnp.minimum(i, last_r), jnp.minimum(j + c0, last_c))
    return pl.pallas_call(
        functools.partial(_cast_kernel, valid_rows=r, valid_cols=ncols),
        grid=(out_rows // rb, out_cols // cb),
        in_specs=[pl.BlockSpec((None, rb, cb), in_map)],
        out_specs=pl.BlockSpec((rb, cb), lambda i, j: (i, j)),
        out_shape=jax.ShapeDtypeStruct((out_rows, out_cols), BF16),
        compiler_params=_compiler_params(("parallel", "parallel"), 2 * rb * cb * 6),
        name="cast_weight",
    )(w)


def _cast_halves_kernel(wa_ref, wb_ref, oa_ref, ob_ref, *, valid_cols):
    rb, cb = oa_ref.shape
    cols = pl.program_id(0) * cb + lax.broadcasted_iota(jnp.int32, (rb, cb), 1)
    ok = cols < valid_cols
    oa_ref[...] = jnp.where(ok, wa_ref[...], 0.0).astype(oa_ref.dtype)
    ob_ref[...] = jnp.where(ok, wb_ref[...], 0.0).astype(ob_ref.dtype)


def cast_weight_halves(w, layer, out_cols, *, cb=V7X_LANES):
    _, r, c = w.shape
    half = c // 2
    assert half % cb == 0 and out_cols % cb == 0
    nb = half // cb
    in_spec = lambda first: pl.BlockSpec((None, r, cb), lambda j: (layer, 0, first + jnp.minimum(j, nb - 1)))
    out_spec = pl.BlockSpec((r, cb), lambda j: (0, j))
    out = jax.ShapeDtypeStruct((r, out_cols), BF16)
    return pl.pallas_call(
        functools.partial(_cast_halves_kernel, valid_cols=half),
        grid=(out_cols // cb,),
        in_specs=[in_spec(0), in_spec(nb)],
        out_specs=[out_spec, out_spec],
        out_shape=[out, out],
        compiler_params=_compiler_params(("parallel",), 2 * 2 * r * cb * 6),
        name="cast_weight_halves",
    )(w, w)


def _layer_weights(l, P):
    depth, d, _ = P['w_in'].shape
    d_ff = P['w_ffn_down'].shape[1]
    f_pad = -(-d_ff // COL_BLOCK) * COL_BLOCK
    bw = d // 2
    assert d_ff % V7X_LANES == 0
    w_br = P['w_branch'].reshape(depth, N_BRANCH * bw, d)
    w_up_a, w_up_b = cast_weight_halves(P['w_ffn_up'], l, f_pad)
    return {
        'w_up_a': w_up_a,
        'w_up_b': w_up_b,
        'w_in': cast_weight(P['w_in'], l, rb=d, cb=COL_BLOCK),
        'w_mem_kv': cast_weight(P['w_mem_kv'], l, rb=d, cb=COL_BLOCK),
        'w_branch': cast_weight(w_br, l, rb=bw, cb=d).reshape(N_BRANCH, bw, d),
        'w_out': cast_weight(P['w_out'], l, rb=d, cb=COL_BLOCK),
        'w_down': cast_weight(P['w_ffn_down'], l, rb=COL_BLOCK, cb=d, out_rows=f_pad),
        'conv_w': _pad_cols(P['ffn_conv_w'][l], f_pad),
        'conv_b': _pad_cols(P['ffn_conv_b'][l].reshape(1, d_ff), f_pad),
        'f_pad': f_pad,
        'd_ff': d_ff,
    }


def _mixer_inputs(h, W, P, l, rope_tab, tm, final=None):
    gains = (P['a_q_norm_g'][l], P['a_k_norm_g'][l], P['b_q_norm_g'][l], P['b_k_norm_g'][l], P['m_q_norm_g'][l])
    return proj_in(h, W['w_in'], gains, rope_tab, tm=tm, final=final)


def _finish_layer(x, h, outs, W, P, l, state, b, t, tm):
    d = x.shape[1]
    gate_b = P['gate_b'][l].reshape(N_BRANCH, 1, d)
    merged = merge_branches(h, *outs, W['w_in'], 7 * (d // 2), gate_b, W['w_branch'], tm=tm)
    res_spec = lambda tn: pl.BlockSpec((tm, tn), lambda i, j: (i, j))
    x, h = proj_out_norm(merged, W['w_out'], x, P['norm_ffn_g'][l], tm=tm)
    g, conv_new = ffn_up(h, W['w_up_a'], W['w_up_b'], W['conv_w'], W['conv_b'], state, b, t, tm=tm)
    x = _proj_call(_proj_residual_kernel, g, W['w_down'], 0, d, F32, [(x, res_spec(COL_BLOCK))], tm=tm, name="ffn_down")
    return x, conv_new[:, :, :W['d_ff']]


def kernel(x_prompt, x_sample, cache_a_k, cache_a_v, cache_b_k, cache_b_v, cache_mem_k, cache_mem_v, state_ffn_conv, mem_prompt, norm_mix_g, w_in, a_q_norm_g, a_k_norm_g, a_rel_bias, b_q_norm_g, b_k_norm_g, b_lam_q1, b_lam_k1, b_lam_q2, b_lam_k2, b_subln_g, m_q_norm_g, m_k_norm_g, mem_norm_g, w_mem_kv, gate_b, w_branch, w_out, norm_ffn_g, w_ffn_up, ffn_conv_w, ffn_conv_b, w_ffn_down):
    P = {'w_in': w_in, 'a_q_norm_g': a_q_norm_g, 'a_k_norm_g': a_k_norm_g, 'b_q_norm_g': b_q_norm_g,
         'b_k_norm_g': b_k_norm_g, 'm_q_norm_g': m_q_norm_g, 'm_k_norm_g': m_k_norm_g, 'w_mem_kv': w_mem_kv,
         'gate_b': gate_b, 'w_branch': w_branch, 'w_out': w_out, 'norm_ffn_g': norm_ffn_g,
         'w_ffn_up': w_ffn_up, 'ffn_conv_w': ffn_conv_w, 'ffn_conv_b': ffn_conv_b, 'w_ffn_down': w_ffn_down}
    bp, tp, d = x_prompt.shape
    bs, ts, _ = x_sample.shape
    depth = w_in.shape[0]
    bw = d // 2
    past = cache_b_k.shape[2]
    a_len = cache_a_k.shape[2]
    n_mem = mem_prompt.shape[1]
    a_keep = min(BAND_PAST, tp)
    h_a = bw // HD_A
    h_b = bw // (2 * DIFF_HD)
    hd_m = bw // H_M
    mp, ms = bp * tp, bs * ts
    tm_p = _row_tile(mp, ROW_BLOCK)
    tm_s = _row_tile(ms, ROW_BLOCK)
    assert tm_p <= tp and tp % tm_p == 0 or tm_p % tp == 0

    pos_s = past + np.arange(ts)
    key_pos_a = np.concatenate([past - a_len + np.arange(a_len), pos_s])
    q_chunk_s = pos_s // CHUNK
    k_chunk_a = key_pos_a // CHUNK
    valid_a_s = (k_chunk_a[None, :] <= q_chunk_s[:, None]) & (k_chunk_a[None, :] >= q_chunk_s[:, None] - BAND_CHUNKS)
    mask_a_s = jnp.asarray(np.where(valid_a_s, 0.0, NEG_INF), F32)
    key_pos_b = np.concatenate([np.arange(past), pos_s])
    valid_b_s = (key_pos_b // CHUNK)[None, :] <= q_chunk_s[:, None]
    assert valid_b_s.all(), "sample queries are expected to see every cached and new differential key"

    rope_p = _rope_table(jnp.arange(max(tp, tm_p), dtype=jnp.int32) % tp)
    rope_s = _rope_table(past + (jnp.arange(max(ts, tm_s), dtype=jnp.int32) % ts))

    xp = x_prompt.reshape(mp, d)
    xs = x_sample.reshape(ms, d)
    mem2d = mem_prompt.reshape(bp * n_mem, d)
    outs = {k: [] for k in ('mk_p', 'mv_p', 'cv_p', 'ak_s', 'av_s', 'bk_s', 'bv_s', 'cv_s')}
    kv_prompt = None
    for l in range(depth):
        W = _layer_weights(l, P)
        lam_init = 0.8 - 0.6 * math.exp(-0.3 * l)
        lam_params = jnp.stack([b_lam_q1[l], b_lam_k1[l], b_lam_q2[l], b_lam_k2[l]]).astype(F32)
        bias_row = _rel_bias_row(a_rel_bias[l])

        h = rmsnorm_cast(xp, norm_mix_g[l])
        final = dict(batch=bp, seq_len=tp, a_keep=a_keep, layer=l, depth=depth, prev=kv_prompt)
        qa, ka, va, qb, kb, vb, qm, *kv_prompt = _mixer_inputs(h, W, P, l, rope_p, tm_p, final)
        oa = attn_a_prompt(qa, ka, va, bias_row, bp, tp)
        ob = attn_b_prompt(qb, kb, vb, lam_params, b_subln_g[l], lam_init, bp, tp)
        hm = rmsnorm_cast(mem2d, mem_norm_g[l])
        tm_m = _row_tile(bp * n_mem, ROW_BLOCK)
        mk = _proj_call(functools.partial(_proj_headnorm_kernel, hd=hd_m), hm, W['w_mem_kv'], 0, bw, F32,
                        [(m_k_norm_g[l].reshape(1, hd_m), pl.BlockSpec((1, hd_m), lambda i, j: (0, 0)))],
                        tm=tm_m, name="proj_mk")
        mv = _proj_call(_proj_plain_kernel, hm, W['w_mem_kv'], bw, bw, F32, [], tm=tm_m, name="proj_mv")
        om = attn_m_prompt(qm, mk, mv, bp, tp)
        zeros_state = jnp.zeros((bp, CONV_W - 1, W['f_pad']), F32)
        xp, conv_new = _finish_layer(xp, h, (oa, ob, om), W, P, l, zeros_state, bp, tp, tm_p)
        outs['mk_p'].append(mk.reshape(bp, n_mem, H_M, hd_m))
        outs['mv_p'].append(mv.reshape(bp, n_mem, H_M, hd_m))
        outs['cv_p'].append(conv_new)

        h = rmsnorm_cast(xs, norm_mix_g[l])
        qa, ka, va, qb, kb, vb, qm = _mixer_inputs(h, W, P, l, rope_s, tm_s)
        oa = attn_a_sample(qa, ka, va, cache_a_k, cache_a_v, l, bias_row, mask_a_s, bs, ts)
        ob = attn_b_sample(qb, kb, vb, cache_b_k, cache_b_v, l, lam_params, b_subln_g[l], lam_init, bs, ts)
        om = attn_m_sample(qm, cache_mem_k, cache_mem_v, l, bs, ts)
        state = _pad_cols(state_ffn_conv[l], W['f_pad'])
        xs, conv_new = _finish_layer(xs, h, (oa, ob, om), W, P, l, state, bs, ts, tm_s)
        outs['ak_s'].append(ka.reshape(bs, ts, h_a, HD_A))
        outs['av_s'].append(va.reshape(bs, ts, h_a, HD_A))
        outs['bk_s'].append(kb.reshape(bs, ts, h_b, 2 * DIFF_HD))
        outs['bv_s'].append(vb.reshape(bs, ts, h_b, 2 * DIFF_HD))
        outs['cv_s'].append(conv_new)

    stack = lambda k: jnp.stack(outs[k])
    ak_p, av_p, bk_p, bv_p = kv_prompt
    return (xp.reshape(bp, tp, d), xs.reshape(bs, ts, d),
            ak_p, av_p, bk_p, bv_p, stack('mk_p'), stack('mv_p'), stack('cv_p'),
            stack('ak_s'), stack('av_s'), stack('bk_s'), stack('bv_s'), stack('cv_s'))
```

```python
import functools
import math

import numpy as np
import jax
import jax.numpy as jnp
from jax import lax
from jax.experimental import pallas as pl
from jax.experimental.pallas import tpu as pltpu

F32 = jnp.float32
BF16 = jnp.bfloat16

CHUNK = 64
BAND_CHUNKS = 8
BAND_PAST = BAND_CHUNKS * CHUNK
REL_CLIP = 128
HD_A = 128
DIFF_HD = 64
ROT_DIM = DIFF_HD // 4
ROPE_THETA = 500000.0
H_M = 4
N_BRANCH = 3
CONV_W = 3
EPS = 1e-6
NEG_INF = -1e30
LOG2E = math.log2(math.e)

V7X_LANES = 128
ROW_TILE = 8

ROW_BLOCK = 1024
COL_BLOCK = 512
NORM_ROWS = 512
FFN_ROW_PARTS = 4
V7X_VMEM_BYTES = 64 * 1024 * 1024
MIB = 1024 * 1024


def _compiler_params(semantics, vmem_estimate_bytes):
    limit = min(int(vmem_estimate_bytes * 1.25) + 8 * MIB, V7X_VMEM_BYTES - 4 * MIB)
    return pltpu.CompilerParams(dimension_semantics=semantics, vmem_limit_bytes=limit)


def _row_tile(m, target):
    t = min(m, target)
    assert m % t == 0, (m, t)
    return t


def _rmsnorm_kernel(x_ref, g_ref, o_ref):
    x = x_ref[...]
    ms = jnp.mean(x * x, axis=-1, keepdims=True)
    o_ref[...] = (x * lax.rsqrt(ms + EPS) * g_ref[...]).astype(o_ref.dtype)


def rmsnorm_cast(x, g):
    m, d = x.shape
    tm = _row_tile(m, NORM_ROWS)
    return pl.pallas_call(
        _rmsnorm_kernel,
        grid=(m // tm,),
        in_specs=[pl.BlockSpec((tm, d), lambda i: (i, 0)), pl.BlockSpec((1, d), lambda i: (0, 0))],
        out_specs=pl.BlockSpec((tm, d), lambda i: (i, 0)),
        out_shape=jax.ShapeDtypeStruct((m, d), BF16),
        compiler_params=_compiler_params(("parallel",), 2 * tm * d * 6),
        name="rmsnorm_cast",
    )(x, g.reshape(1, d))


def _dot(a, b):
    return jnp.dot(a, b, preferred_element_type=F32)


def _dot_nt(a, b):
    return lax.dot_general(a, b, (((1,), (1,)), ((), ())), preferred_element_type=F32)


def _proj_plain_kernel(h_ref, w_ref, o_ref):
    o_ref[...] = _dot(h_ref[...], w_ref[...]).astype(o_ref.dtype)


def _headnorm_store(acc, g, o_ref, hd):
    for k in range(acc.shape[1] // hd):
        s = acc[:, k * hd:(k + 1) * hd]
        ms = jnp.mean(s * s, axis=-1, keepdims=True)
        o_ref[:, k * hd:(k + 1) * hd] = (s * lax.rsqrt(ms + EPS) * g).astype(o_ref.dtype)


def _norm_rope_store(acc, g, tab_ref, o_ref):
    rows, width = acc.shape
    grp_r = lax.broadcasted_iota(jnp.int32, (width, width), 0) // DIFF_HD
    grp_c = lax.broadcasted_iota(jnp.int32, (width, width), 1) // DIFF_HD
    ones_bd = jnp.where(grp_r == grp_c, 1.0, 0.0).astype(BF16)
    ms = _dot((acc * acc).astype(BF16), ones_bd) * (1.0 / DIFF_HD)
    y = acc * lax.rsqrt(ms + EPS)
    cos = tab_ref[:, 0:V7X_LANES]
    sin_up = tab_ref[:, V7X_LANES:2 * V7X_LANES]
    sin_dn = tab_ref[:, 2 * V7X_LANES:3 * V7X_LANES]
    half = ROT_DIM // 2
    for k in range(width // V7X_LANES):
        yk = y[:, k * V7X_LANES:(k + 1) * V7X_LANES] * g
        out = (yk * cos + pltpu.roll(yk, half, 1) * sin_up
               + pltpu.roll(yk, V7X_LANES - half, 1) * sin_dn)
        o_ref[:, k * V7X_LANES:(k + 1) * V7X_LANES] = out.astype(o_ref.dtype)


def _proj_headnorm_kernel(h_ref, w_ref, g_ref, o_ref, *, hd):
    _headnorm_store(_dot(h_ref[...], w_ref[...]), g_ref[...], o_ref, hd)


PROJ_SLAB = 256


FINAL_GROUPS = (1, 2, 4, 5)


def _proj_in_kernel(h_ref, w_ref, gqa_ref, gka_ref, gqb_ref, gkb_ref, gqm_ref, tab_ref, *rest,
                    blocks_per_group, hd_m, final):
    i = pl.program_id(0)
    j = pl.program_id(1)
    group = j // blocks_per_group
    tm = h_ref.shape[0]
    tn = w_ref.shape[1]
    n_prev = 0 if final is None or final['layer'] == 0 else 4
    qa_ref, ka_ref, va_ref, qb_ref, kb_ref, vb_ref, qm_ref = rest[n_prev:n_prev + 7]
    out_refs = {0: qa_ref, 1: ka_ref, 2: va_ref, 3: qb_ref, 4: kb_ref, 5: vb_ref, 6: qm_ref}
    if final is not None:
        final_refs = dict(zip(FINAL_GROUPS, rest[n_prev + 7:n_prev + 11]))
        stage, zeros, sem, zero_sem = rest[n_prev + 11:]
        heads_per_block = tn // V7X_LANES
        bps = final['seq_len'] // tm
        clear_slots = range(1, final['depth']) if final['layer'] == 0 else ()

        @pl.when((i == 0) & (j == 0))
        def _():
            zeros[...] = jnp.zeros(zeros.shape, zeros.dtype)

        def head_copies(g, c, slot):
            keep = final['a_keep'] if g in (1, 2) else final['seq_len']
            if keep >= tm:
                rows, r_lo, t0 = tm, 0, (i % bps) * tm - (final['seq_len'] - keep)
            else:
                rows, r_lo, t0 = keep, tm - keep, 0
            out = []
            for hh in range(heads_per_block):
                where = (i // bps, pl.ds(t0, rows), c * heads_per_block + hh, slice(None))
                src = stage.at[slot, pl.ds(r_lo, rows), pl.ds(hh * V7X_LANES, V7X_LANES)]
                out.append(pltpu.make_async_copy(src, final_refs[g].at[(final['layer'],) + where], sem.at[hh]))
                for p in clear_slots:
                    out.append(pltpu.make_async_copy(zeros.at[pl.ds(0, rows), :], final_refs[g].at[(p,) + where],
                                                     zero_sem.at[p - 1, hh]))
            return out

        def kept(g):
            keep = final['a_keep'] if g in (1, 2) else final['seq_len']
            return (i % bps) >= bps - max(keep // tm, 1)

        def wait_block(g, c, slot):
            @pl.when(kept(g))
            def _():
                for cp in head_copies(g, c, slot):
                    cp.wait()

    def run(g, epilogue):
        o_ref = out_refs[g]

        def body():
            slabs = [slice(c0, c0 + PROJ_SLAB) for c0 in range(0, tn, PROJ_SLAB)]
            accs = [_dot(h_ref[...], w_ref[:, cols]) for cols in slabs]
            for acc, cols in zip(accs, slabs):
                epilogue(acc, o_ref.at[:, cols])
            if final is None:
                return
            c = j - g * blocks_per_group
            slot = j % 2
            if g in FINAL_GROUPS:
                @pl.when(c > 0)
                def _():
                    wait_block(g, c - 1, 1 - slot)
            if g - 1 in FINAL_GROUPS:
                @pl.when(c == 0)
                def _():
                    wait_block(g - 1, blocks_per_group - 1, 1 - slot)
            if g in FINAL_GROUPS:
                @pl.when(kept(g))
                def _():
                    stage[slot] = o_ref[...]
                    for cp in head_copies(g, c, slot):
                        cp.start()
        return body

    headnorm = lambda g_ref, hd: (lambda acc, o: _headnorm_store(acc, g_ref[...], o, hd))
    norm_rope = lambda g_ref: (lambda acc, o: _norm_rope_store(acc, g_ref[...], tab_ref, o))
    plain = lambda acc, o: o.__setitem__(Ellipsis, acc)
    bodies = (
        run(0, headnorm(gqa_ref, HD_A)), run(1, headnorm(gka_ref, HD_A)), run(2, plain),
        run(3, norm_rope(gqb_ref)), run(4, norm_rope(gkb_ref)), run(5, plain),
        run(6, headnorm(gqm_ref, hd_m)),
    )
    for n, body in enumerate(bodies):
        pl.when(group == n)(body)


def proj_in(h, w_in, gains, rope_tab, *, tm, tn=COL_BLOCK, final=None):
    m, d = h.shape
    bw = d // 2
    hd_m = bw // H_M
    tm = _row_tile(m, tm)
    assert bw % tn == 0 and tn % PROJ_SLAB == 0 and PROJ_SLAB % hd_m == 0 and rope_tab.shape[0] % tm == 0
    bpg = bw // tn
    n_tab = rope_tab.shape[0] // tm
    g_a_q, g_a_k, g_b_q, g_b_k, g_m_q = gains
    tile2 = lambda g: jnp.tile(g.reshape(1, -1), (1, 2))
    gain_args = [g_a_q.reshape(1, HD_A), g_a_k.reshape(1, HD_A), tile2(g_b_q), tile2(g_b_k), g_m_q.reshape(1, hd_m)]
    const = lambda a: pl.BlockSpec(a.shape, lambda i, j: (0, 0))

    def out_spec(n):
        return pl.BlockSpec((tm, tn), lambda i, j: (i, jnp.clip(j - n * bpg, 0, bpg - 1)))

    dtypes = (BF16, F32, F32, BF16, F32, F32, BF16)
    out_specs = [out_spec(n) for n in range(7)]
    out_shape = [jax.ShapeDtypeStruct((m, bw), t) for t in dtypes]
    prev, scratch, aliases, semantics = [], [], {}, ("parallel", "arbitrary")
    n_fixed_inputs = 8
    if final is not None:
        t, keep, batch = final['seq_len'], final['a_keep'], final['batch']
        heads = bw // V7X_LANES
        assert HD_A == V7X_LANES and 2 * DIFF_HD == V7X_LANES and m == batch * t and t % tm == 0
        assert keep % tm == 0 or (keep < tm and keep % 8 == 0)
        prev = list(final['prev'] or ())
        assert len(prev) == (4 if final['layer'] else 0)
        depth = final['depth']
        out_shape += [jax.ShapeDtypeStruct((depth, batch, rows, heads, V7X_LANES), F32) for rows in (keep, keep, t, t)]
        out_specs += [pl.BlockSpec(memory_space=pl.ANY)] * 4
        hpb = tn // V7X_LANES
        scratch = [pltpu.VMEM((2, tm, tn), F32), pltpu.VMEM((tm, V7X_LANES), F32), pltpu.SemaphoreType.DMA((hpb,)),
                   pltpu.SemaphoreType.DMA((max(depth - 1, 1), hpb))]
        aliases = {n_fixed_inputs + k: 7 + k for k in range(len(prev))}
        semantics = ("arbitrary", "arbitrary")
        final = {k: v for k, v in final.items() if k != 'prev'}
    out_bytes = sum(tm * tn * jnp.dtype(t).itemsize for t in dtypes)
    vmem = 2 * (tm * d * 2 + d * tn * 2 + tm * 3 * V7X_LANES * 4 + out_bytes) + 4 * tm * tn * 4
    return pl.pallas_call(
        functools.partial(_proj_in_kernel, blocks_per_group=bpg, hd_m=hd_m, final=final),
        grid=(m // tm, 7 * bpg),
        in_specs=[pl.BlockSpec((tm, d), lambda i, j: (i, 0)), pl.BlockSpec((d, tn), lambda i, j: (0, j))]
                 + [const(g) for g in gain_args]
                 + [pl.BlockSpec((tm, 3 * V7X_LANES), lambda i, j: (i % n_tab, 0))]
                 + [pl.BlockSpec(memory_space=pl.ANY)] * len(prev),
        out_specs=out_specs,
        out_shape=out_shape,
        input_output_aliases=aliases,
        scratch_shapes=scratch,
        compiler_params=_compiler_params(semantics, vmem),
        name="proj_in",
    )(h, w_in, *gain_args, rope_tab, *prev)


def _proj_residual_kernel(h_ref, w_ref, x_ref, o_ref):
    o_ref[...] = x_ref[...] + _dot(h_ref[...], w_ref[...])


def _proj_out_norm_kernel(m_ref, w_ref, x_ref, g_ref, o_ref, h_ref, xrow):
    j = pl.program_id(1)
    y = x_ref[...] + _dot(m_ref[...], w_ref[...])
    o_ref[...] = y
    xrow[j] = y

    @pl.when(j == pl.num_programs(1) - 1)
    def _():
        nj, _, tn = xrow.shape
        ssq = None
        for jj in range(nj):
            xb = xrow[jj]
            part = jnp.sum(xb * xb, axis=-1, keepdims=True)
            ssq = part if ssq is None else ssq + part
        scale = lax.rsqrt(ssq * (1.0 / (nj * tn)) + EPS)
        for jj in range(nj):
            cols = slice(jj * tn, (jj + 1) * tn)
            h_ref[:, cols] = (xrow[jj] * scale * g_ref[:, cols]).astype(h_ref.dtype)


def proj_out_norm(merged, w_out, x, g, *, tm, tn=2 * COL_BLOCK):
    m, d = x.shape
    tm = _row_tile(m, tm)
    tn = min(tn, d)
    assert d % tn == 0
    row_spec = pl.BlockSpec((tm, d), lambda i, j: (i, 0))
    blk_spec = pl.BlockSpec((tm, tn), lambda i, j: (i, j))
    vmem = 2 * (tm * d * 2 + d * tn * 2 + 2 * tm * tn * 4 + tm * d * 2) + tm * d * 4 + 2 * tm * tn * 4
    return pl.pallas_call(
        _proj_out_norm_kernel,
        grid=(m // tm, d // tn),
        in_specs=[row_spec, pl.BlockSpec((d, tn), lambda i, j: (0, j)), blk_spec,
                  pl.BlockSpec((1, d), lambda i, j: (0, 0))],
        out_specs=[blk_spec, row_spec],
        out_shape=[jax.ShapeDtypeStruct((m, d), F32), jax.ShapeDtypeStruct((m, d), BF16)],
        scratch_shapes=[pltpu.VMEM((d // tn, tm, tn), F32)],
        compiler_params=_compiler_params(("parallel", "arbitrary"), vmem),
        name="proj_out_norm",
    )(merged, w_out, x, g.reshape(1, d))


def _proj_call(kernel_fn, h, w, col0, ncols, out_dtype, extras, *, tm=ROW_BLOCK, tn=COL_BLOCK, name):
    m, k = h.shape
    tm = _row_tile(m, tm)
    tn = min(tn, ncols)
    assert ncols % tn == 0 and col0 % tn == 0, (ncols, col0, tn)
    cb = col0 // tn
    in_specs = [pl.BlockSpec((tm, k), lambda i, j: (i, 0)),
                pl.BlockSpec((k, tn), lambda i, j: (0, j + cb))]
    in_specs += [spec for _, spec in extras]
    extra_bytes = sum(int(np.prod(spec.block_shape)) * a.dtype.itemsize for a, spec in extras)
    vmem = 2 * (tm * k * 2 + k * tn * 2 + tm * tn * 4 + extra_bytes) + tm * tn * 8
    return pl.pallas_call(
        kernel_fn,
        grid=(m // tm, ncols // tn),
        in_specs=in_specs,
        out_specs=pl.BlockSpec((tm, tn), lambda i, j: (i, j)),
        out_shape=jax.ShapeDtypeStruct((m, ncols), out_dtype),
        compiler_params=_compiler_params(("parallel", "arbitrary"), vmem),
        name=name,
    )(h, w, *[a for a, _ in extras])


def _rope_table(pos):
    half = ROT_DIM // 2
    inv_freq = jnp.exp(jnp.arange(half, dtype=F32) * (-2.0 * math.log(ROPE_THETA) / ROT_DIM))
    ang = pos.astype(F32)[:, None] * inv_freq[None, :]
    cos = jnp.cos(ang)
    sin = jnp.sin(ang)
    p = pos.shape[0]
    rest = DIFF_HD - ROT_DIM
    c64 = jnp.concatenate([cos, cos, jnp.ones((p, rest), F32)], axis=1)
    up64 = jnp.concatenate([jnp.zeros((p, half), F32), sin, jnp.zeros((p, rest), F32)], axis=1)
    dn64 = jnp.concatenate([-sin, jnp.zeros((p, half + rest), F32)], axis=1)
    return jnp.concatenate([c64, c64, up64, up64, dn64, dn64], axis=1)


A_QBLK = 4 * CHUNK
A_KBLK = BAND_PAST + A_QBLK
A_BIAS_W = 1024


def _rel_bias_row(tab):
    assert A_KBLK + A_QBLK - 1 <= A_BIAS_W
    lo = BAND_PAST - REL_CLIP
    hi = BAND_PAST + REL_CLIP + 1
    rep = lambda col, n: jnp.repeat(tab[:, col:col + 1], n, axis=1)
    row = jnp.concatenate([rep(0, lo), tab, rep(2 * REL_CLIP, A_KBLK - hi), rep(0, A_BIAS_W - A_KBLK)], axis=1)
    return row[:, None, :]


def _toeplitz_bias(row, rows, width):
    full = pltpu.roll(jnp.broadcast_to(row, (rows, A_BIAS_W)), 0, 1, stride=1, stride_axis=0)
    return full[:, :width]


def _softmax_pv(s, v):
    m = jnp.max(s, axis=-1, keepdims=True)
    p = jnp.exp(s - m)
    l = jnp.sum(p, axis=-1, keepdims=True)
    return _dot(p.astype(BF16), v), l


def _attn_a_prompt_kernel(q_ref, k_ref, v_ref, row_ref, o_ref, kb, vb, *, t):
    kscale = (HD_A ** -0.5) * LOG2E
    kb[...] = (k_ref[0] * kscale).astype(BF16)
    vb[:, :HD_A] = v_ref[0].astype(BF16)
    vb[:, HD_A:] = jnp.ones((t, HD_A), BF16)
    qc = lax.broadcasted_iota(jnp.int32, (A_QBLK, A_KBLK), 0) // CHUNK
    kc = lax.broadcasted_iota(jnp.int32, (A_QBLK, A_KBLK), 1) // CHUNK
    inband = (kc >= qc) & (kc <= qc + BAND_CHUNKS)
    bias = jnp.where(inband, _toeplitz_bias(row_ref[0], A_QBLK, A_KBLK) * LOG2E, NEG_INF)
    for i in range(t // A_QBLK):
        r0 = i * A_QBLK
        k0 = max(r0 - BAND_PAST, 0)
        k1 = r0 + A_QBLK
        q = q_ref[0, r0:k1, :]
        s = _dot_nt(q, kb[k0:k1, :]) + bias[:, A_KBLK - (k1 - k0):]
        p = jnp.exp2(s - jnp.max(s, axis=-1, keepdims=True))
        o = _dot(p.astype(BF16), vb[k0:k1, :])
        o_ref[0, r0:k1, :] = (o[:, :HD_A] / o[:, HD_A:]).astype(o_ref.dtype)


def attn_a_prompt(qa, ka, va, bias_row, b, t):
    h = qa.shape[1] // HD_A
    assert t % A_QBLK == 0 and A_QBLK % V7X_LANES == 0
    q3, k3, v3 = (a.reshape(b, t, h * HD_A) for a in (qa, ka, va))
    spec = pl.BlockSpec((1, t, HD_A), lambda bi, hi: (bi, 0, hi))
    vmem = 2 * t * HD_A * (2 + 4 + 4 + 2) + 2 * t * HD_A * 2 + 24 * MIB
    out = pl.pallas_call(
        functools.partial(_attn_a_prompt_kernel, t=t),
        grid=(b, h),
        in_specs=[spec, spec, spec, pl.BlockSpec((1, 1, A_BIAS_W), lambda bi, hi: (hi, 0, 0))],
        out_specs=spec,
        out_shape=jax.ShapeDtypeStruct((b, t, h * HD_A), BF16),
        scratch_shapes=[pltpu.VMEM((t, HD_A), BF16), pltpu.VMEM((t, 2 * HD_A), BF16)],
        compiler_params=_compiler_params(("parallel", "parallel"), vmem),
        name="attn_a_prompt",
    )(q3, k3, v3, bias_row)
    return out.reshape(b * t, h * HD_A)


def _attn_a_sample_kernel(q_ref, kn_ref, vn_ref, row_ref, mask_ref, kc_hbm, vc_hbm, o_ref, kbuf, vbuf, bias_scr, sem,
                          *, layer, heads, a_len):
    b = pl.program_id(0)
    nb = pl.num_programs(0)
    scale = HD_A ** -0.5
    t = q_ref.shape[2]

    def copies(bi, slot):
        out = []
        for h in range(heads):
            out.append(pltpu.make_async_copy(kc_hbm.at[layer, bi, :, h, :], kbuf.at[slot, h], sem.at[0, slot, h]))
            out.append(pltpu.make_async_copy(vc_hbm.at[layer, bi, :, h, :], vbuf.at[slot, h], sem.at[1, slot, h]))
        return out

    @pl.when(b == 0)
    def _():
        for c in copies(b, 0):
            c.start()
        for h in range(heads):
            bias_scr[h] = _toeplitz_bias(row_ref[h], t, a_len + t) + mask_ref[...]

    slot = b % 2

    @pl.when(b + 1 < nb)
    def _():
        for c in copies(b + 1, 1 - slot):
            c.start()

    for c in copies(b, slot):
        c.wait()
    for h in range(heads):
        q = q_ref[0, h]
        bias = bias_scr[h]
        sc = _dot_nt(q, kbuf[slot, h].astype(BF16)) * scale + bias[:, :a_len]
        sn = _dot_nt(q, kn_ref[0, h].astype(BF16)) * scale + bias[:, a_len:]
        m = jnp.maximum(jnp.max(sc, axis=-1, keepdims=True), jnp.max(sn, axis=-1, keepdims=True))
        pc = jnp.exp(sc - m)
        pn = jnp.exp(sn - m)
        l = jnp.sum(pc, axis=-1, keepdims=True) + jnp.sum(pn, axis=-1, keepdims=True)
        o = _dot(pc.astype(BF16), vbuf[slot, h].astype(BF16)) + _dot(pn.astype(BF16), vn_ref[0, h].astype(BF16))
        o_ref[0, h] = (o / l).astype(o_ref.dtype)


def _by_head(a, b, t, heads, hd):
    return a.reshape(b, t, heads, hd).transpose(0, 2, 1, 3)


def attn_a_sample(qa, ka, va, cache_k, cache_v, layer, bias_row, mask, b, t):
    heads = qa.shape[1] // HD_A
    a_len = cache_k.shape[2]
    assert a_len == BAND_PAST and a_len + t <= A_KBLK
    head_spec = pl.BlockSpec((1, heads, t, HD_A), lambda bi: (bi, 0, 0, 0))
    vmem = 2 * 2 * heads * a_len * HD_A * 4 + 2 * 4 * heads * t * HD_A * 4 + 16 * MIB
    out = pl.pallas_call(
        functools.partial(_attn_a_sample_kernel, layer=layer, heads=heads, a_len=a_len),
        grid=(b,),
        in_specs=[head_spec, head_spec, head_spec,
                  pl.BlockSpec(bias_row.shape, lambda bi: (0, 0, 0)),
                  pl.BlockSpec(mask.shape, lambda bi: (0, 0)),
                  pl.BlockSpec(memory_space=pl.ANY), pl.BlockSpec(memory_space=pl.ANY)],
        out_specs=head_spec,
        out_shape=jax.ShapeDtypeStruct((b, heads, t, HD_A), BF16),
        scratch_shapes=[pltpu.VMEM((2, heads, a_len, HD_A), F32), pltpu.VMEM((2, heads, a_len, HD_A), F32),
                        pltpu.VMEM((heads, t, a_len + t), F32), pltpu.SemaphoreType.DMA((2, 2, heads))],
        compiler_params=_compiler_params(("arbitrary",), vmem),
        name="attn_a_sample",
    )(*(_by_head(a, b, t, heads, HD_A) for a in (qa, ka, va)), bias_row, mask, cache_k, cache_v)
    return out.transpose(0, 2, 1, 3).reshape(b * t, heads * HD_A)


def _diff_lambda(lam_ref, lam_init):
    v = lam_ref[...]
    d1 = jnp.sum(v[0:1] * v[1:2], axis=-1, keepdims=True)
    d2 = jnp.sum(v[2:3] * v[3:4], axis=-1, keepdims=True)
    return jnp.exp(d1) - jnp.exp(d2) + lam_init


def _split_diff_queries(q):
    lane = lax.broadcasted_iota(jnp.int32, q.shape, 1)
    qs = q * jnp.asarray(DIFF_HD ** -0.5, q.dtype)
    zero = jnp.zeros_like(qs)
    return jnp.where(lane < DIFF_HD, qs, zero), jnp.where(lane >= DIFF_HD, qs, zero)


def _stack_diff_queries(q):
    return jnp.concatenate(_split_diff_queries(q), axis=0)


def _diff_post(o, g, post_scale):
    ms = jnp.mean(o * o, axis=-1, keepdims=True)
    return (o * lax.rsqrt(ms + EPS) * g) * post_scale


def _diff_finish(l, acc, lam, g, post_scale, tq):
    o = acc[:tq] / l[:tq] - lam * (acc[tq:] / l[tq:])
    return _diff_post(o, g, post_scale)


B_TQ = 8 * CHUNK
B_SAMPLE_SLOTS = 4


def _online_step(carry, s, v_ext):
    m, acc = carry
    m_new = jnp.maximum(m, jnp.max(s, axis=-1, keepdims=True))
    alpha = jnp.exp2(m - m_new)
    p = jnp.exp2(s - m_new)
    acc = alpha * acc + _dot(p.astype(BF16), v_ext)
    return m_new, acc


def _attn_b_prompt_kernel(lam_ref, q_ref, k_ref, v_ref, g_ref, o_ref, kb, vb, *, t, lam_init):
    tq = B_TQ
    hd = 2 * DIFF_HD
    kb[...] = (k_ref[0] * LOG2E).astype(BF16)
    vb[:, :hd] = v_ref[0].astype(BF16)
    vb[:, hd:] = jnp.ones((t, hd), BF16)
    lam = _diff_lambda(lam_ref, lam_init)
    row = lax.broadcasted_iota(jnp.int32, (tq, tq), 0)
    col = lax.broadcasted_iota(jnp.int32, (tq, tq), 1)
    diag_ok = (col // CHUNK) <= (row // CHUNK)
    for qi in range(t // tq):
        q0 = qi * tq
        qs = _split_diff_queries(q_ref[0, q0:q0 + tq, :])
        spans = ([(0, q0, False)] if q0 else []) + [(q0, tq, True)]
        outs = []
        for c in range(2):
            carry = (jnp.full((tq, 1), NEG_INF, F32), jnp.zeros((tq, 2 * hd), F32))
            for k0, width, masked in spans:
                s = _dot_nt(qs[c], kb[k0:k0 + width, :])
                if masked:
                    s = jnp.where(diag_ok, s, NEG_INF)
                carry = _online_step(carry, s, vb[k0:k0 + width, :])
            outs.append(carry[1][:, :hd] / carry[1][:, hd:])
        o = outs[0] - lam * outs[1]
        o_ref[0, q0:q0 + tq, :] = _diff_post(o, g_ref[...], 1.0 - lam_init).astype(o_ref.dtype)


def attn_b_prompt(qb, kb, vb, lam_params, subln_g, lam_init, b, t):
    hd = 2 * DIFF_HD
    heads = qb.shape[1] // hd
    assert t % B_TQ == 0
    q3, k3, v3 = (a.reshape(b, t, heads * hd) for a in (qb, kb, vb))
    spec = pl.BlockSpec((1, t, hd), lambda bi, hi: (bi, 0, hi))
    vmem = 2 * t * hd * (2 + 4 + 4 + 2) + 2 * t * hd * 2 + 32 * MIB
    out = pl.pallas_call(
        functools.partial(_attn_b_prompt_kernel, t=t, lam_init=lam_init),
        grid=(b, heads),
        in_specs=[pl.BlockSpec(lam_params.shape, lambda bi, hi: (0, 0)), spec, spec, spec,
                  pl.BlockSpec((1, hd), lambda bi, hi: (0, 0))],
        out_specs=spec,
        out_shape=jax.ShapeDtypeStruct((b, t, heads * hd), BF16),
        scratch_shapes=[pltpu.VMEM((t, hd), BF16), pltpu.VMEM((t, 2 * hd), BF16)],
        compiler_params=_compiler_params(("parallel", "parallel"), vmem),
        name="attn_b_prompt",
    )(lam_params, q3, k3, v3, subln_g.reshape(1, hd))
    return out.reshape(b * t, heads * hd)


def _attn_b_sample_kernel(lam_ref, q_ref, kn_ref, vn_ref, g_ref, kc_hbm, vc_hbm, o_ref, kbuf, vbuf, sem,
                          *, layer, heads, t, lam_init):
    b = pl.program_id(0)
    nb = pl.num_programs(0)

    def copies(bi, h, slot):
        return (pltpu.make_async_copy(kc_hbm.at[layer, bi, :, h, :], kbuf.at[slot], sem.at[0, slot]),
                pltpu.make_async_copy(vc_hbm.at[layer, bi, :, h, :], vbuf.at[slot], sem.at[1, slot]))

    def start(bi, h, slot):
        for c in copies(bi, h, slot):
            c.start()

    ahead = B_SAMPLE_SLOTS - 1

    @pl.when(b == 0)
    def _():
        for h in range(ahead):
            start(b, h, h % B_SAMPLE_SLOTS)

    lam = _diff_lambda(lam_ref, lam_init)
    for h in range(heads):
        slot = h % B_SAMPLE_SLOTS
        nxt = h + ahead
        if nxt < heads:
            start(b, nxt, nxt % B_SAMPLE_SLOTS)
        else:
            @pl.when(b + 1 < nb)
            def _():
                start(b + 1, nxt - heads, nxt % B_SAMPLE_SLOTS)
        for c in copies(b, h, slot):
            c.wait()
        q2 = _stack_diff_queries(q_ref[0, h])
        sc = _dot_nt(q2, kbuf[slot].astype(BF16))
        sn = _dot_nt(q2, kn_ref[0, h].astype(BF16))
        m = jnp.maximum(jnp.max(sc, axis=-1, keepdims=True), jnp.max(sn, axis=-1, keepdims=True))
        pc = jnp.exp(sc - m)
        pn = jnp.exp(sn - m)
        l = jnp.sum(pc, axis=-1, keepdims=True) + jnp.sum(pn, axis=-1, keepdims=True)
        acc = _dot(pc.astype(BF16), vbuf[slot].astype(BF16)) + _dot(pn.astype(BF16), vn_ref[0, h].astype(BF16))
        o_ref[0, h] = _diff_finish(l, acc, lam, g_ref[...], 1.0 - lam_init, t).astype(o_ref.dtype)


def attn_b_sample(qb, kb, vb, cache_k, cache_v, layer, lam_params, subln_g, lam_init, b, t):
    hd = 2 * DIFF_HD
    heads = qb.shape[1] // hd
    assert heads % B_SAMPLE_SLOTS == 0 and heads >= B_SAMPLE_SLOTS
    past = cache_k.shape[2]
    head_spec = pl.BlockSpec((1, heads, t, hd), lambda bi: (bi, 0, 0, 0))
    vmem = 2 * B_SAMPLE_SLOTS * past * hd * 4 + 2 * 3 * heads * t * hd * 4 + 12 * 2 * t * past * 4
    out = pl.pallas_call(
        functools.partial(_attn_b_sample_kernel, layer=layer, heads=heads, t=t, lam_init=lam_init),
        grid=(b,),
        in_specs=[pl.BlockSpec(lam_params.shape, lambda bi: (0, 0)), head_spec, head_spec, head_spec,
                  pl.BlockSpec((1, hd), lambda bi: (0, 0)),
                  pl.BlockSpec(memory_space=pl.ANY), pl.BlockSpec(memory_space=pl.ANY)],
        out_specs=head_spec,
        out_shape=jax.ShapeDtypeStruct((b, heads, t, hd), BF16),
        scratch_shapes=[pltpu.VMEM((B_SAMPLE_SLOTS, past, hd), F32), pltpu.VMEM((B_SAMPLE_SLOTS, past, hd), F32),
                        pltpu.SemaphoreType.DMA((2, B_SAMPLE_SLOTS))],
        compiler_params=_compiler_params(("arbitrary",), vmem),
        name="attn_b_sample",
    )(lam_params, *(_by_head(a, b, t, heads, hd) for a in (qb, kb, vb)), subln_g.reshape(1, hd), cache_k, cache_v)
    return out.transpose(0, 2, 1, 3).reshape(b * t, heads * hd)


def _attn_m_kernel(q_ref, k_ref, v_ref, o_ref, *, heads, hd):
    scale = hd ** -0.5
    for h in range(heads):
        sl = slice(h * hd, (h + 1) * hd)
        s = _dot_nt(q_ref[0, :, sl], k_ref[0, :, sl].astype(BF16)) * scale
        o, l = _softmax_pv(s, v_ref[0, :, sl].astype(BF16))
        o_ref[0, :, sl] = (o / l).astype(o_ref.dtype)


def attn_m_prompt(qm, mem_k, mem_v, b, t, *, tq=B_TQ):
    width = qm.shape[1]
    hd = width // H_M
    tq = min(tq, t)
    assert t % tq == 0
    n = mem_k.shape[0] // b
    q_spec = pl.BlockSpec((1, tq, width), lambda bi, qi: (bi, qi, 0))
    kv_spec = pl.BlockSpec((1, n, width), lambda bi, qi: (bi, 0, 0))
    vmem = 2 * (2 * n * width * 4 + 2 * tq * width * 2) + 8 * tq * n * 4
    out = pl.pallas_call(
        functools.partial(_attn_m_kernel, heads=H_M, hd=hd),
        grid=(b, t // tq),
        in_specs=[q_spec, kv_spec, kv_spec],
        out_specs=q_spec,
        out_shape=jax.ShapeDtypeStruct((b, t, width), BF16),
        compiler_params=_compiler_params(("parallel", "arbitrary"), vmem),
        name="attn_m_prompt",
    )(qm.reshape(b, t, width), mem_k.reshape(b, n, width), mem_v.reshape(b, n, width))
    return out.reshape(b * t, width)


def _attn_m_sample_kernel(q_ref, kc_hbm, vc_hbm, o_ref, kbuf, vbuf, sem, *, layer, heads, hd):
    b = pl.program_id(0)
    nb = pl.num_programs(0)
    scale = hd ** -0.5

    def copies(bi, slot):
        out = []
        for h in range(heads):
            out.append(pltpu.make_async_copy(kc_hbm.at[layer, bi, :, h, :], kbuf.at[slot, h], sem.at[0, slot, h]))
            out.append(pltpu.make_async_copy(vc_hbm.at[layer, bi, :, h, :], vbuf.at[slot, h], sem.at[1, slot, h]))
        return out

    @pl.when(b == 0)
    def _():
        for c in copies(b, 0):
            c.start()

    slot = b % 2

    @pl.when(b + 1 < nb)
    def _():
        for c in copies(b + 1, 1 - slot):
            c.start()

    for c in copies(b, slot):
        c.wait()
    for h in range(heads):
        s = _dot_nt(q_ref[0, h], kbuf[slot, h].astype(BF16)) * scale
        o, l = _softmax_pv(s, vbuf[slot, h].astype(BF16))
        o_ref[0, h] = (o / l).astype(o_ref.dtype)


def attn_m_sample(qm, cache_k, cache_v, layer, b, t):
    n, heads, hd = cache_k.shape[2:]
    head_spec = pl.BlockSpec((1, heads, t, hd), lambda bi: (bi, 0, 0, 0))
    vmem = 2 * 2 * heads * n * hd * 4 + 2 * 2 * heads * t * hd * 2 + 8 * MIB
    out = pl.pallas_call(
        functools.partial(_attn_m_sample_kernel, layer=layer, heads=heads, hd=hd),
        grid=(b,),
        in_specs=[head_spec, pl.BlockSpec(memory_space=pl.ANY), pl.BlockSpec(memory_space=pl.ANY)],
        out_specs=head_spec,
        out_shape=jax.ShapeDtypeStruct((b, heads, t, hd), BF16),
        scratch_shapes=[pltpu.VMEM((2, heads, n, hd), F32), pltpu.VMEM((2, heads, n, hd), F32),
                        pltpu.SemaphoreType.DMA((2, 2, heads))],
        compiler_params=_compiler_params(("arbitrary",), vmem),
        name="attn_m_sample",
    )(_by_head(qm, b, t, heads, hd), cache_k, cache_v)
    return out.transpose(0, 2, 1, 3).reshape(b * t, heads * hd)


def _merge_kernel(h_ref, oa_ref, ob_ref, om_ref, wga_ref, wgb_ref, wgm_ref, gb_ref, wbr_ref, o_ref):
    h = h_ref[...]
    acc = None
    for n, (o_n, wg_n) in enumerate(((oa_ref, wga_ref), (ob_ref, wgb_ref), (om_ref, wgm_ref))):
        gate = jax.nn.sigmoid(_dot(h, wg_n[...]) + gb_ref[n])
        term = gate * _dot(o_n[...], wbr_ref[n])
        acc = term if acc is None else acc + term
    o_ref[...] = acc.astype(o_ref.dtype)


def merge_branches(h, oa, ob, om, w_in, gate_col0, gate_b, w_br, *, tm, tn=COL_BLOCK):
    m, bw = oa.shape
    d = w_br.shape[2]
    tm = _row_tile(m, tm)
    assert d % tn == 0 and gate_col0 % tn == 0
    nj = d // tn
    g0 = gate_col0 // tn
    row_spec = lambda width: pl.BlockSpec((tm, width), lambda i, j: (i, 0))
    gate_w_specs = [pl.BlockSpec((d, tn), functools.partial(lambda i, j, n: (0, g0 + n * nj + j), n=n))
                    for n in range(N_BRANCH)]
    vmem = 2 * (tm * d * 2 + 3 * tm * bw * 2 + 3 * d * tn * 2 + 3 * bw * tn * 2 + tm * tn * 2) + 6 * tm * tn * 4
    return pl.pallas_call(
        _merge_kernel,
        grid=(m // tm, nj),
        in_specs=[row_spec(d), row_spec(bw), row_spec(bw), row_spec(bw)] + gate_w_specs
                 + [pl.BlockSpec((N_BRANCH, 1, tn), lambda i, j: (0, 0, j)),
                    pl.BlockSpec((N_BRANCH, bw, tn), lambda i, j: (0, 0, j))],
        out_specs=pl.BlockSpec((tm, tn), lambda i, j: (i, j)),
        out_shape=jax.ShapeDtypeStruct((m, d), BF16),
        compiler_params=_compiler_params(("parallel", "arbitrary"), vmem),
        name="merge_branches",
    )(h, oa, ob, om, w_in, w_in, w_in, gate_b, w_br)


def _ffn_up_kernel(h_ref, wa_ref, wb_ref, cw_ref, cb_ref, st_ref, g_ref, cn_ref, carry, *, nb, tb, blocks_per_seq):
    i = pl.program_id(0)
    j = pl.program_id(1)
    tm, tn = g_ref.shape
    if nb == 1:
        @pl.when((i % blocks_per_seq) == 0)
        def _():
            carry[j] = st_ref[0]

        trow = lax.broadcasted_iota(jnp.int32, (ROW_TILE, PROJ_SLAB), 0)
    else:
        trow = lax.broadcasted_iota(jnp.int32, (nb, tb, PROJ_SLAB), 1).reshape(tm, PROJ_SLAB)
    slabs = [slice(c0, c0 + PROJ_SLAB) for c0 in range(0, tn, PROJ_SLAB)]

    def gated(a, am1, am2, bgate, cols):
        cw = cw_ref[:, cols]
        c = cb_ref[:, cols] + am2 * cw[0:1] + am1 * cw[1:2] + a * cw[2:3]
        gelu = 0.5 * c * (1.0 + lax.erf(c * (2.0 ** -0.5)))
        return (gelu * bgate).astype(g_ref.dtype)

    if nb == 1:
        parts = FFN_ROW_PARTS if tm % (FFN_ROW_PARTS * ROW_TILE) == 0 else 1
        rp = tm // parts
        units = [(r0, cols) for cols in slabs for r0 in range(0, tm, rp)]
        dots = [(_dot(h_ref[r0:r0 + rp, :], wa_ref[:, cols]), _dot(h_ref[r0:r0 + rp, :], wb_ref[:, cols]))
                for r0, cols in units]
        prev = {}
        for (r0, cols), (a, bgate) in zip(units, dots):
            g_ref[r0:r0 + rp, cols] = gated(a, pltpu.roll(a, 1, 0), pltpu.roll(a, 2, 0), bgate, cols)
            before = carry[j, :, cols] if r0 == 0 else prev[cols.start]
            top = a[0:ROW_TILE]
            p0 = jnp.broadcast_to(before[0:1], top.shape)
            p1 = jnp.broadcast_to(before[1:2], top.shape)
            am1 = jnp.where(trow == 0, p1, pltpu.roll(top, 1, 0))
            am2 = jnp.where(trow == 0, p0, jnp.where(trow == 1, p1, pltpu.roll(top, 2, 0)))
            g_ref[r0:r0 + ROW_TILE, cols] = gated(top, am1, am2, bgate[0:ROW_TILE], cols)
            prev[cols.start] = a[rp - 2:rp]
            if r0 + rp == tm:
                carry[j, :, cols] = a[rp - 2:rp]
                cn_ref[0, :, cols] = a[rp - 2:rp]
        return

    h = h_ref[...]
    dots = [(_dot(h, wa_ref[:, cols]), _dot(h, wb_ref[:, cols])) for cols in slabs]
    for cols, (a, bgate) in zip(slabs, dots):
        st = st_ref[:, :, cols]
        p0 = jnp.broadcast_to(st[:, 0:1, :], (nb, tb, PROJ_SLAB)).reshape(tm, PROJ_SLAB)
        p1 = jnp.broadcast_to(st[:, 1:2, :], (nb, tb, PROJ_SLAB)).reshape(tm, PROJ_SLAB)
        cn_ref[:, :, cols] = a.reshape(nb, tb, PROJ_SLAB)[:, tb - 2:tb, :]
        am1 = jnp.where(trow == 0, p1, pltpu.roll(a, 1, 0))
        am2 = jnp.where(trow == 0, p0, jnp.where(trow == 1, p1, pltpu.roll(a, 2, 0)))
        g_ref[:, cols] = gated(a, am1, am2, bgate, cols)


def ffn_up(h, w_a, w_b, conv_w, conv_b, state, b, t, *, tm=ROW_BLOCK, tn=COL_BLOCK):
    m, d = h.shape
    f = w_a.shape[1]
    tm = _row_tile(m, tm)
    assert f % tn == 0 and tn % PROJ_SLAB == 0
    if tm >= t:
        assert tm % t == 0
        nb, tb, blocks_per_seq = tm // t, t, 1
    else:
        assert t % tm == 0
        nb, tb, blocks_per_seq = 1, tm, t // tm
    if nb > 1:
        seq_map = lambda i, j: (i, 0, j)
    else:
        seq_map = lambda i, j: (i // blocks_per_seq, 0, j)
    tail_map = lambda i, j: (i, 0, j)
    w_spec = pl.BlockSpec((d, tn), lambda i, j: (0, j))
    vmem = 2 * (tm * d * 2 + 2 * d * tn * 2 + tm * tn * 2 + 2 * nb * 8 * tn * 4) + 8 * tm * tn * 4
    g, conv_new = pl.pallas_call(
        functools.partial(_ffn_up_kernel, nb=nb, tb=tb, blocks_per_seq=blocks_per_seq),
        grid=(m // tm, f // tn),
        in_specs=[pl.BlockSpec((tm, d), lambda i, j: (i, 0)), w_spec, w_spec,
                  pl.BlockSpec((CONV_W, tn), lambda i, j: (0, j)),
                  pl.BlockSpec((1, tn), lambda i, j: (0, j)),
                  pl.BlockSpec((nb, CONV_W - 1, tn), seq_map)],
        out_specs=[pl.BlockSpec((tm, tn), lambda i, j: (i, j)),
                   pl.BlockSpec((nb, CONV_W - 1, tn), tail_map)],
        out_shape=[jax.ShapeDtypeStruct((m, f), BF16),
                   jax.ShapeDtypeStruct((b * blocks_per_seq, CONV_W - 1, f), F32)],
        scratch_shapes=[pltpu.VMEM((f // tn, CONV_W - 1, tn), F32)],
        compiler_params=_compiler_params(("arbitrary", "arbitrary"), vmem),
        name="ffn_up",
    )(h, w_a, w_b, conv_w, conv_b, state)
    return g, conv_new.reshape(b, blocks_per_seq, CONV_W - 1, f)[:, -1]


def _pad_cols(a, f_pad):
    return jnp.pad(a, [(0, 0)] * (a.ndim - 1) + [(0, f_pad - a.shape[-1])])


def _cast_kernel(w_ref, o_ref, *, valid_rows, valid_cols):
    rb, cb = o_ref.shape
    rows = pl.program_id(0) * rb + lax.broadcasted_iota(jnp.int32, (rb, cb), 0)
    cols = pl.program_id(1) * cb + lax.broadcasted_iota(jnp.int32, (rb, cb), 1)
    ok = (rows < valid_rows) & (cols < valid_cols)
    o_ref[...] = jnp.where(ok, w_ref[...], 0.0).astype(o_ref.dtype)


def cast_weight(w, layer, *, rb, cb, col0=0, ncols=None, out_rows=None, out_cols=None):
    _, r, c = w.shape
    ncols = c - col0 if ncols is None else ncols
    out_rows = r if out_rows is None else out_rows
    out_cols = ncols if out_cols is None else out_cols
    assert col0 % cb == 0 and out_rows % rb == 0 and out_cols % cb == 0
    c0 = col0 // cb
    last_r = (r - 1) // rb
    last_c = (col0 + ncols - 1) // cb
    in_map = lambda i, j: (layer, jnp.minimum(i, last_r), jnp.minimum(j + c0, last_c))
    return pl.pallas_call(
        functools.partial(_cast_kernel, valid_rows=r, valid_cols=ncols),
        grid=(out_rows // rb, out_cols // cb),
        in_specs=[pl.BlockSpec((None, rb, cb), in_map)],
        out_specs=pl.BlockSpec((rb, cb), lambda i, j: (i, j)),
        out_shape=jax.ShapeDtypeStruct((out_rows, out_cols), BF16),
        compiler_params=_compiler_params(("parallel", "parallel"), 2 * rb * cb * 6),
        name="cast_weight",
    )(w)


def _cast_halves_kernel(wa_ref, wb_ref, oa_ref, ob_ref, *, valid_cols):
    rb, cb = oa_ref.shape
    cols = pl.program_id(0) * cb + lax.broadcasted_iota(jnp.int32, (rb, cb), 1)
    ok = cols < valid_cols
    oa_ref[...] = jnp.where(ok, wa_ref[...], 0.0).astype(oa_ref.dtype)
    ob_ref[...] = jnp.where(ok, wb_ref[...], 0.0).astype(ob_ref.dtype)


def cast_weight_halves(w, layer, out_cols, *, cb=V7X_LANES):
    _, r, c = w.shape
    half = c // 2
    assert half % cb == 0 and out_cols % cb == 0
    nb = half // cb
    in_spec = lambda first: pl.BlockSpec((None, r, cb), lambda j: (layer, 0, first + jnp.minimum(j, nb - 1)))
    out_spec = pl.BlockSpec((r, cb), lambda j: (0, j))
    out = jax.ShapeDtypeStruct((r, out_cols), BF16)
    return pl.pallas_call(
        functools.partial(_cast_halves_kernel, valid_cols=half),
        grid=(out_cols // cb,),
        in_specs=[in_spec(0), in_spec(nb)],
        out_specs=[out_spec, out_spec],
        out_shape=[out, out],
        compiler_params=_compiler_params(("parallel",), 2 * 2 * r * cb * 6),
        name="cast_weight_halves",
    )(w, w)


def _layer_weights(l, P):
    depth, d, _ = P['w_in'].shape
    d_ff = P['w_ffn_down'].shape[1]
    f_pad = -(-d_ff // COL_BLOCK) * COL_BLOCK
    bw = d // 2
    assert d_ff % V7X_LANES == 0
    w_br = P['w_branch'].reshape(depth, N_BRANCH * bw, d)
    w_up_a, w_up_b = cast_weight_halves(P['w_ffn_up'], l, f_pad)
    return {
        'w_up_a': w_up_a,
        'w_up_b': w_up_b,
        'w_in': cast_weight(P['w_in'], l, rb=d, cb=COL_BLOCK),
        'w_mem_kv': cast_weight(P['w_mem_kv'], l, rb=d, cb=COL_BLOCK),
        'w_branch': cast_weight(w_br, l, rb=bw, cb=d).reshape(N_BRANCH, bw, d),
        'w_out': cast_weight(P['w_out'], l, rb=d, cb=COL_BLOCK),
        'w_down': cast_weight(P['w_ffn_down'], l, rb=COL_BLOCK, cb=d, out_rows=f_pad),
        'conv_w': _pad_cols(P['ffn_conv_w'][l], f_pad),
        'conv_b': _pad_cols(P['ffn_conv_b'][l].reshape(1, d_ff), f_pad),
        'f_pad': f_pad,
        'd_ff': d_ff,
    }


def _mixer_inputs(h, W, P, l, rope_tab, tm, final=None):
    gains = (P['a_q_norm_g'][l], P['a_k_norm_g'][l], P['b_q_norm_g'][l], P['b_k_norm_g'][l], P['m_q_norm_g'][l])
    return proj_in(h, W['w_in'], gains, rope_tab, tm=tm, final=final)


def _finish_layer(x, h, outs, W, P, l, state, b, t, tm):
    d = x.shape[1]
    gate_b = P['gate_b'][l].reshape(N_BRANCH, 1, d)
    merged = merge_branches(h, *outs, W['w_in'], 7 * (d // 2), gate_b, W['w_branch'], tm=tm)
    res_spec = lambda tn: pl.BlockSpec((tm, tn), lambda i, j: (i, j))
    x, h = proj_out_norm(merged, W['w_out'], x, P['norm_ffn_g'][l], tm=tm)
    g, conv_new = ffn_up(h, W['w_up_a'], W['w_up_b'], W['conv_w'], W['conv_b'], state, b, t, tm=tm)
    x = _proj_call(_proj_residual_kernel, g, W['w_down'], 0, d, F32, [(x, res_spec(COL_BLOCK))], tm=tm, name="ffn_down")
    return x, conv_new[:, :, :W['d_ff']]


def kernel(x_prompt, x_sample, cache_a_k, cache_a_v, cache_b_k, cache_b_v, cache_mem_k, cache_mem_v, state_ffn_conv, mem_prompt, norm_mix_g, w_in, a_q_norm_g, a_k_norm_g, a_rel_bias, b_q_norm_g, b_k_norm_g, b_lam_q1, b_lam_k1, b_lam_q2, b_lam_k2, b_subln_g, m_q_norm_g, m_k_norm_g, mem_norm_g, w_mem_kv, gate_b, w_branch, w_out, norm_ffn_g, w_ffn_up, ffn_conv_w, ffn_conv_b, w_ffn_down):
    P = {'w_in': w_in, 'a_q_norm_g': a_q_norm_g, 'a_k_norm_g': a_k_norm_g, 'b_q_norm_g': b_q_norm_g,
         'b_k_norm_g': b_k_norm_g, 'm_q_norm_g': m_q_norm_g, 'm_k_norm_g': m_k_norm_g, 'w_mem_kv': w_mem_kv,
         'gate_b': gate_b, 'w_branch': w_branch, 'w_out': w_out, 'norm_ffn_g': norm_ffn_g,
         'w_ffn_up': w_ffn_up, 'ffn_conv_w': ffn_conv_w, 'ffn_conv_b': ffn_conv_b, 'w_ffn_down': w_ffn_down}
    bp, tp, d = x_prompt.shape
    bs, ts, _ = x_sample.shape
    depth = w_in.shape[0]
    bw = d // 2
    past = cache_b_k.shape[2]
    a_len = cache_a_k.shape[2]
    n_mem = mem_prompt.shape[1]
    a_keep = min(BAND_PAST, tp)
    h_a = bw // HD_A
    h_b = bw // (2 * DIFF_HD)
    hd_m = bw // H_M
    mp, ms = bp * tp, bs * ts
    tm_p = _row_tile(mp, ROW_BLOCK)
    tm_s = _row_tile(ms, ROW_BLOCK)
    assert tm_p <= tp and tp % tm_p == 0 or tm_p % tp == 0

    pos_s = past + np.arange(ts)
    key_pos_a = np.concatenate([past - a_len + np.arange(a_len), pos_s])
    q_chunk_s = pos_s // CHUNK
    k_chunk_a = key_pos_a // CHUNK
    valid_a_s = (k_chunk_a[None, :] <= q_chunk_s[:, None]) & (k_chunk_a[None, :] >= q_chunk_s[:, None] - BAND_CHUNKS)
    mask_a_s = jnp.asarray(np.where(valid_a_s, 0.0, NEG_INF), F32)
    key_pos_b = np.concatenate([np.arange(past), pos_s])
    valid_b_s = (key_pos_b // CHUNK)[None, :] <= q_chunk_s[:, None]
    assert valid_b_s.all(), "sample queries are expected to see every cached and new differential key"

    rope_p = _rope_table(jnp.arange(max(tp, tm_p), dtype=jnp.int32) % tp)
    rope_s = _rope_table(past + (jnp.arange(max(ts, tm_s), dtype=jnp.int32) % ts))

    xp = x_prompt.reshape(mp, d)
    xs = x_sample.reshape(ms, d)
    mem2d = mem_prompt.reshape(bp * n_mem, d)
    outs = {k: [] for k in ('mk_p', 'mv_p', 'cv_p', 'ak_s', 'av_s', 'bk_s', 'bv_s', 'cv_s')}
    kv_prompt = None
    for l in range(depth):
        W = _layer_weights(l, P)
        lam_init = 0.8 - 0.6 * math.exp(-0.3 * l)
        lam_params = jnp.stack([b_lam_q1[l], b_lam_k1[l], b_lam_q2[l], b_lam_k2[l]]).astype(F32)
        bias_row = _rel_bias_row(a_rel_bias[l])

        h = rmsnorm_cast(xp, norm_mix_g[l])
        final = dict(batch=bp, seq_len=tp, a_keep=a_keep, layer=l, depth=depth, prev=kv_prompt)
        qa, ka, va, qb, kb, vb, qm, *kv_prompt = _mixer_inputs(h, W, P, l, rope_p, tm_p, final)
        oa = attn_a_prompt(qa, ka, va, bias_row, bp, tp)
        ob = attn_b_prompt(qb, kb, vb, lam_params, b_subln_g[l], lam_init, bp, tp)
        hm = rmsnorm_cast(mem2d, mem_norm_g[l])
        tm_m = _row_tile(bp * n_mem, ROW_BLOCK)
        mk = _proj_call(functools.partial(_proj_headnorm_kernel, hd=hd_m), hm, W['w_mem_kv'], 0, bw, F32,
                        [(m_k_norm_g[l].reshape(1, hd_m), pl.BlockSpec((1, hd_m), lambda i, j: (0, 0)))],
                        tm=tm_m, name="proj_mk")
        mv = _proj_call(_proj_plain_kernel, hm, W['w_mem_kv'], bw, bw, F32, [], tm=tm_m, name="proj_mv")
        om = attn_m_prompt(qm, mk, mv, bp, tp)
        zeros_state = jnp.zeros((bp, CONV_W - 1, W['f_pad']), F32)
        xp, conv_new = _finish_layer(xp, h, (oa, ob, om), W, P, l, zeros_state, bp, tp, tm_p)
        outs['mk_p'].append(mk.reshape(bp, n_mem, H_M, hd_m))
        outs['mv_p'].append(mv.reshape(bp, n_mem, H_M, hd_m))
        outs['cv_p'].append(conv_new)

        h = rmsnorm_cast(xs, norm_mix_g[l])
        qa, ka, va, qb, kb, vb, qm = _mixer_inputs(h, W, P, l, rope_s, tm_s)
        oa = attn_a_sample(qa, ka, va, cache_a_k, cache_a_v, l, bias_row, mask_a_s, bs, ts)
        ob = attn_b_sample(qb, kb, vb, cache_b_k, cache_b_v, l, lam_params, b_subln_g[l], lam_init, bs, ts)
        om = attn_m_sample(qm, cache_mem_k, cache_mem_v, l, bs, ts)
        state = _pad_cols(state_ffn_conv[l], W['f_pad'])
        xs, conv_new = _finish_layer(xs, h, (oa, ob, om), W, P, l, state, bs, ts, tm_s)
        outs['ak_s'].append(ka.reshape(bs, ts, h_a, HD_A))
        outs['av_s'].append(va.reshape(bs, ts, h_a, HD_A))
        outs['bk_s'].append(kb.reshape(bs, ts, h_b, 2 * DIFF_HD))
        outs['bv_s'].append(vb.reshape(bs, ts, h_b, 2 * DIFF_HD))
        outs['cv_s'].append(conv_new)

    stack = lambda k: jnp.stack(outs[k])
    ak_p, av_p, bk_p, bv_p = kv_prompt
    return (xp.reshape(bp, tp, d), xs.reshape(bs, ts, d),
            ak_p, av_p, bk_p, bv_p, stack('mk_p'), stack('mv_p'), stack('cv_p'),
            stack('ak_s'), stack('av_s'), stack('bk_s'), stack('bv_s'), stack('cv_s'))
```

```python
import functools
import math

import numpy as np
import jax
import jax.numpy as jnp
from jax import lax
from jax.experimental import pallas as pl
from jax.experimental.pallas import tpu as pltpu

F32 = jnp.float32
BF16 = jnp.bfloat16

CHUNK = 64
BAND_CHUNKS = 8
BAND_PAST = BAND_CHUNKS * CHUNK
REL_CLIP = 128
HD_A = 128
DIFF_HD = 64
ROT_DIM = DIFF_HD // 4
ROPE_THETA = 500000.0
H_M = 4
N_BRANCH = 3
CONV_W = 3
EPS = 1e-6
NEG_INF = -1e30
LOG2E = math.log2(math.e)

V7X_LANES = 128
ROW_TILE = 8

ROW_BLOCK = 1024
COL_BLOCK = 512
NORM_ROWS = 512
FFN_ROW_PARTS = 4
V7X_VMEM_BYTES = 64 * 1024 * 1024
MIB = 1024 * 1024


def _compiler_params(semantics, vmem_estimate_bytes):
    limit = min(int(vmem_estimate_bytes * 1.25) + 8 * MIB, V7X_VMEM_BYTES - 4 * MIB)
    return pltpu.CompilerParams(dimension_semantics=semantics, vmem_limit_bytes=limit)


def _row_tile(m, target):
    t = min(m, target)
    assert m % t == 0, (m, t)
    return t


def _rmsnorm_kernel(x_ref, g_ref, o_ref):
    x = x_ref[...]
    ms = jnp.mean(x * x, axis=-1, keepdims=True)
    o_ref[...] = (x * lax.rsqrt(ms + EPS) * g_ref[...]).astype(o_ref.dtype)


def rmsnorm_cast(x, g):
    m, d = x.shape
    tm = _row_tile(m, NORM_ROWS)
    return pl.pallas_call(
        _rmsnorm_kernel,
        grid=(m // tm,),
        in_specs=[pl.BlockSpec((tm, d), lambda i: (i, 0)), pl.BlockSpec((1, d), lambda i: (0, 0))],
        out_specs=pl.BlockSpec((tm, d), lambda i: (i, 0)),
        out_shape=jax.ShapeDtypeStruct((m, d), BF16),
        compiler_params=_compiler_params(("parallel",), 2 * tm * d * 6),
        name="rmsnorm_cast",
    )(x, g.reshape(1, d))


def _dot(a, b):
    return jnp.dot(a, b, preferred_element_type=F32)


def _dot_nt(a, b):
    return lax.dot_general(a, b, (((1,), (1,)), ((), ())), preferred_element_type=F32)


def _proj_plain_kernel(h_ref, w_ref, o_ref):
    o_ref[...] = _dot(h_ref[...], w_ref[...]).astype(o_ref.dtype)


def _headnorm_store(acc, g, o_ref, hd):
    for k in range(acc.shape[1] // hd):
        s = acc[:, k * hd:(k + 1) * hd]
        ms = jnp.mean(s * s, axis=-1, keepdims=True)
        o_ref[:, k * hd:(k + 1) * hd] = (s * lax.rsqrt(ms + EPS) * g).astype(o_ref.dtype)


def _norm_rope_store(acc, g, tab_ref, o_ref):
    rows, width = acc.shape
    grp_r = lax.broadcasted_iota(jnp.int32, (width, width), 0) // DIFF_HD
    grp_c = lax.broadcasted_iota(jnp.int32, (width, width), 1) // DIFF_HD
    ones_bd = jnp.where(grp_r == grp_c, 1.0, 0.0).astype(BF16)
    ms = _dot((acc * acc).astype(BF16), ones_bd) * (1.0 / DIFF_HD)
    y = acc * lax.rsqrt(ms + EPS)
    cos = tab_ref[:, 0:V7X_LANES]
    sin_up = tab_ref[:, V7X_LANES:2 * V7X_LANES]
    sin_dn = tab_ref[:, 2 * V7X_LANES:3 * V7X_LANES]
    half = ROT_DIM // 2
    for k in range(width // V7X_LANES):
        yk = y[:, k * V7X_LANES:(k + 1) * V7X_LANES] * g
        out = (yk * cos + pltpu.roll(yk, half, 1) * sin_up
               + pltpu.roll(yk, V7X_LANES - half, 1) * sin_dn)
        o_ref[:, k * V7X_LANES:(k + 1) * V7X_LANES] = out.astype(o_ref.dtype)


def _proj_headnorm_kernel(h_ref, w_ref, g_ref, o_ref, *, hd):
    _headnorm_store(_dot(h_ref[...], w_ref[...]), g_ref[...], o_ref, hd)


PROJ_SLAB = 256


FINAL_GROUPS = (1, 2, 4, 5)


def _proj_in_kernel(h_ref, w_ref, gqa_ref, gka_ref, gqb_ref, gkb_ref, gqm_ref, tab_ref, *rest,
                    blocks_per_group, hd_m, final):
    i = pl.program_id(0)
    j = pl.program_id(1)
    group = j // blocks_per_group
    tm = h_ref.shape[0]
    tn = w_ref.shape[1]
    n_prev = 0 if final is None or final['layer'] == 0 else 4
    qa_ref, ka_ref, va_ref, qb_ref, kb_ref, vb_ref, qm_ref = rest[n_prev:n_prev + 7]
    out_refs = {0: qa_ref, 1: ka_ref, 2: va_ref, 3: qb_ref, 4: kb_ref, 5: vb_ref, 6: qm_ref}
    if final is not None:
        final_refs = dict(zip(FINAL_GROUPS, rest[n_prev + 7:n_prev + 11]))
        stage, zeros, sem, zero_sem = rest[n_prev + 11:]
        heads_per_block = tn // V7X_LANES
        bps = final['seq_len'] // tm
        clear_slots = range(1, final['depth']) if final['layer'] == 0 else ()

        @pl.when((i == 0) & (j == 0))
        def _():
            zeros[...] = jnp.zeros(zeros.shape, zeros.dtype)

        def head_copies(g, c, slot):
            keep = final['a_keep'] if g in (1, 2) else final['seq_len']
            if keep >= tm:
                rows, r_lo, t0 = tm, 0, (i % bps) * tm - (final['seq_len'] - keep)
            else:
                rows, r_lo, t0 = keep, tm - keep, 0
            out = []
            for hh in range(heads_per_block):
                where = (i // bps, pl.ds(t0, rows), c * heads_per_block + hh, slice(None))
                src = stage.at[slot, pl.ds(r_lo, rows), pl.ds(hh * V7X_LANES, V7X_LANES)]
                out.append(pltpu.make_async_copy(src, final_refs[g].at[(final['layer'],) + where], sem.at[hh]))
                for p in clear_slots:
                    out.append(pltpu.make_async_copy(zeros.at[pl.ds(0, rows), :], final_refs[g].at[(p,) + where],
                                                     zero_sem.at[p - 1, hh]))
            return out

        def kept(g):
            keep = final['a_keep'] if g in (1, 2) else final['seq_len']
            return (i % bps) >= bps - max(keep // tm, 1)

        def wait_block(g, c, slot):
            @pl.when(kept(g))
            def _():
                for cp in head_copies(g, c, slot):
                    cp.wait()

    def run(g, epilogue):
        o_ref = out_refs[g]

        def body():
            slabs = [slice(c0, c0 + PROJ_SLAB) for c0 in range(0, tn, PROJ_SLAB)]
            accs = [_dot(h_ref[...], w_ref[:, cols]) for cols in slabs]
            for acc, cols in zip(accs, slabs):
                epilogue(acc, o_ref.at[:, cols])
            if final is None:
                return
            c = j - g * blocks_per_group
            slot = j % 2
            if g in FINAL_GROUPS:
                @pl.when(c > 0)
                def _():
                    wait_block(g, c - 1, 1 - slot)
            if g - 1 in FINAL_GROUPS:
                @pl.when(c == 0)
                def _():
                    wait_block(g - 1, blocks_per_group - 1, 1 - slot)
            if g in FINAL_GROUPS:
                @pl.when(kept(g))
                def _():
                    stage[slot] = o_ref[...]
                    for cp in head_copies(g, c, slot):
                        cp.start()
        return body

    headnorm = lambda g_ref, hd: (lambda acc, o: _headnorm_store(acc, g_ref[...], o, hd))
    norm_rope = lambda g_ref: (lambda acc, o: _norm_rope_store(acc, g_ref[...], tab_ref, o))
    plain = lambda acc, o: o.__setitem__(Ellipsis, acc)
    bodies = (
        run(0, headnorm(gqa_ref, HD_A)), run(1, headnorm(gka_ref, HD_A)), run(2, plain),
        run(3, norm_rope(gqb_ref)), run(4, norm_rope(gkb_ref)), run(5, plain),
        run(6, headnorm(gqm_ref, hd_m)),
    )
    for n, body in enumerate(bodies):
        pl.when(group == n)(body)


def proj_in(h, w_in, gains, rope_tab, *, tm, tn=COL_BLOCK, final=None):
    m, d = h.shape
    bw = d // 2
    hd_m = bw // H_M
    tm = _row_tile(m, tm)
    assert bw % tn == 0 and tn % PROJ_SLAB == 0 and PROJ_SLAB % hd_m == 0 and rope_tab.shape[0] % tm == 0
    bpg = bw // tn
    n_tab = rope_tab.shape[0] // tm
    g_a_q, g_a_k, g_b_q, g_b_k, g_m_q = gains
    tile2 = lambda g: jnp.tile(g.reshape(1, -1), (1, 2))
    gain_args = [g_a_q.reshape(1, HD_A), g_a_k.reshape(1, HD_A), tile2(g_b_q), tile2(g_b_k), g_m_q.reshape(1, hd_m)]
    const = lambda a: pl.BlockSpec(a.shape, lambda i, j: (0, 0))

    def out_spec(n):
        return pl.BlockSpec((tm, tn), lambda i, j: (i, jnp.clip(j - n * bpg, 0, bpg - 1)))

    dtypes = (BF16, F32, F32, BF16, F32, F32, BF16)
    out_specs = [out_spec(n) for n in range(7)]
    out_shape = [jax.ShapeDtypeStruct((m, bw), t) for t in dtypes]
    prev, scratch, aliases, semantics = [], [], {}, ("parallel", "arbitrary")
    n_fixed_inputs = 8
    if final is not None:
        t, keep, batch = final['seq_len'], final['a_keep'], final['batch']
        heads = bw // V7X_LANES
        assert HD_A == V7X_LANES and 2 * DIFF_HD == V7X_LANES and m == batch * t and t % tm == 0
        assert keep % tm == 0 or (keep < tm and keep % 8 == 0)
        prev = list(final['prev'] or ())
        assert len(prev) == (4 if final['layer'] else 0)
        depth = final['depth']
        out_shape += [jax.ShapeDtypeStruct((depth, batch, rows, heads, V7X_LANES), F32) for rows in (keep, keep, t, t)]
        out_specs += [pl.BlockSpec(memory_space=pl.ANY)] * 4
        hpb = tn // V7X_LANES
        scratch = [pltpu.VMEM((2, tm, tn), F32), pltpu.VMEM((tm, V7X_LANES), F32), pltpu.SemaphoreType.DMA((hpb,)),
                   pltpu.SemaphoreType.DMA((max(depth - 1, 1), hpb))]
        aliases = {n_fixed_inputs + k: 7 + k for k in range(len(prev))}
        semantics = ("arbitrary", "arbitrary")
        final = {k: v for k, v in final.items() if k != 'prev'}
    out_bytes = sum(tm * tn * jnp.dtype(t).itemsize for t in dtypes)
    vmem = 2 * (tm * d * 2 + d * tn * 2 + tm * 3 * V7X_LANES * 4 + out_bytes) + 4 * tm * tn * 4
    return pl.pallas_call(
        functools.partial(_proj_in_kernel, blocks_per_group=bpg, hd_m=hd_m, final=final),
        grid=(m // tm, 7 * bpg),
        in_specs=[pl.BlockSpec((tm, d), lambda i, j: (i, 0)), pl.BlockSpec((d, tn), lambda i, j: (0, j))]
                 + [const(g) for g in gain_args]
                 + [pl.BlockSpec((tm, 3 * V7X_LANES), lambda i, j: (i % n_tab, 0))]
                 + [pl.BlockSpec(memory_space=pl.ANY)] * len(prev),
        out_specs=out_specs,
        out_shape=out_shape,
        input_output_aliases=aliases,
        scratch_shapes=scratch,
        compiler_params=_compiler_params(semantics, vmem),
        name="proj_in",
    )(h, w_in, *gain_args, rope_tab, *prev)


def _proj_residual_kernel(h_ref, w_ref, x_ref, o_ref):
    o_ref[...] = x_ref[...] + _dot(h_ref[...], w_ref[...])


def _proj_out_norm_kernel(m_ref, w_ref, x_ref, g_ref, o_ref, h_ref, xrow):
    j = pl.program_id(1)
    y = x_ref[...] + _dot(m_ref[...], w_ref[...])
    o_ref[...] = y
    xrow[j] = y

    @pl.when(j == pl.num_programs(1) - 1)
    def _():
        nj, _, tn = xrow.shape
        ssq = None
        for jj in range(nj):
            xb = xrow[jj]
            part = jnp.sum(xb * xb, axis=-1, keepdims=True)
            ssq = part if ssq is None else ssq + part
        scale = lax.rsqrt(ssq * (1.0 / (nj * tn)) + EPS)
        for jj in range(nj):
            cols = slice(jj * tn, (jj + 1) * tn)
            h_ref[:, cols] = (xrow[jj] * scale * g_ref[:, cols]).astype(h_ref.dtype)


def proj_out_norm(merged, w_out, x, g, *, tm, tn=2 * COL_BLOCK):
    m, d = x.shape
    tm = _row_tile(m, tm)
    tn = min(tn, d)
    assert d % tn == 0
    row_spec = pl.BlockSpec((tm, d), lambda i, j: (i, 0))
    blk_spec = pl.BlockSpec((tm, tn), lambda i, j: (i, j))
    vmem = 2 * (tm * d * 2 + d * tn * 2 + 2 * tm * tn * 4 + tm * d * 2) + tm * d * 4 + 2 * tm * tn * 4
    return pl.pallas_call(
        _proj_out_norm_kernel,
        grid=(m // tm, d // tn),
        in_specs=[row_spec, pl.BlockSpec((d, tn), lambda i, j: (0, j)), blk_spec,
                  pl.BlockSpec((1, d), lambda i, j: (0, 0))],
        out_specs=[blk_spec, row_spec],
        out_shape=[jax.ShapeDtypeStruct((m, d), F32), jax.ShapeDtypeStruct((m, d), BF16)],
        scratch_shapes=[pltpu.VMEM((d // tn, tm, tn), F32)],
        compiler_params=_compiler_params(("parallel", "arbitrary"), vmem),
        name="proj_out_norm",
    )(merged, w_out, x, g.reshape(1, d))


def _proj_call(kernel_fn, h, w, col0, ncols, out_dtype, extras, *, tm=ROW_BLOCK, tn=COL_BLOCK, name):
    m, k = h.shape
    tm = _row_tile(m, tm)
    tn = min(tn, ncols)
    assert ncols % tn == 0 and col0 % tn == 0, (ncols, col0, tn)
    cb = col0 // tn
    in_specs = [pl.BlockSpec((tm, k), lambda i, j: (i, 0)),
                pl.BlockSpec((k, tn), lambda i, j: (0, j + cb))]
    in_specs += [spec for _, spec in extras]
    extra_bytes = sum(int(np.prod(spec.block_shape)) * a.dtype.itemsize for a, spec in extras)
    vmem = 2 * (tm * k * 2 + k * tn * 2 + tm * tn * 4 + extra_bytes) + tm * tn * 8
    return pl.pallas_call(
        kernel_fn,
        grid=(m // tm, ncols // tn),
        in_specs=in_specs,
        out_specs=pl.BlockSpec((tm, tn), lambda i, j: (i, j)),
        out_shape=jax.ShapeDtypeStruct((m, ncols), out_dtype),
        compiler_params=_compiler_params(("parallel", "arbitrary"), vmem),
        name=name,
    )(h, w, *[a for a, _ in extras])


def _rope_table(pos):
    half = ROT_DIM // 2
    inv_freq = jnp.exp(jnp.arange(half, dtype=F32) * (-2.0 * math.log(ROPE_THETA) / ROT_DIM))
    ang = pos.astype(F32)[:, None] * inv_freq[None, :]
    cos = jnp.cos(ang)
    sin = jnp.sin(ang)
    p = pos.shape[0]
    rest = DIFF_HD - ROT_DIM
    c64 = jnp.concatenate([cos, cos, jnp.ones((p, rest), F32)], axis=1)
    up64 = jnp.concatenate([jnp.zeros((p, half), F32), sin, jnp.zeros((p, rest), F32)], axis=1)
    dn64 = jnp.concatenate([-sin, jnp.zeros((p, half + rest), F32)], axis=1)
    return jnp.concatenate([c64, c64, up64, up64, dn64, dn64], axis=1)


A_QBLK = 4 * CHUNK
A_KBLK = BAND_PAST + A_QBLK
A_BIAS_W = 1024


def _rel_bias_row(tab):
    assert A_KBLK + A_QBLK - 1 <= A_BIAS_W
    lo = BAND_PAST - REL_CLIP
    hi = BAND_PAST + REL_CLIP + 1
    rep = lambda col, n: jnp.repeat(tab[:, col:col + 1], n, axis=1)
    row = jnp.concatenate([rep(0, lo), tab, rep(2 * REL_CLIP, A_KBLK - hi), rep(0, A_BIAS_W - A_KBLK)], axis=1)
    return row[:, None, :]


def _toeplitz_bias(row, rows, width):
    full = pltpu.roll(jnp.broadcast_to(row, (rows, A_BIAS_W)), 0, 1, stride=1, stride_axis=0)
    return full[:, :width]


def _softmax_pv(s, v):
    m = jnp.max(s, axis=-1, keepdims=True)
    p = jnp.exp(s - m)
    l = jnp.sum(p, axis=-1, keepdims=True)
    return _dot(p.astype(BF16), v), l


def _attn_a_prompt_kernel(q_ref, k_ref, v_ref, row_ref, o_ref, kb, vb, *, t):
    kscale = (HD_A ** -0.5) * LOG2E
    kb[...] = (k_ref[0] * kscale).astype(BF16)
    vb[:, :HD_A] = v_ref[0].astype(BF16)
    vb[:, HD_A:] = jnp.ones((t, HD_A), BF16)
    qc = lax.broadcasted_iota(jnp.int32, (A_QBLK, A_KBLK), 0) // CHUNK
    kc = lax.broadcasted_iota(jnp.int32, (A_QBLK, A_KBLK), 1) // CHUNK
    inband = (kc >= qc) & (kc <= qc + BAND_CHUNKS)
    bias = jnp.where(inband, _toeplitz_bias(row_ref[0], A_QBLK, A_KBLK) * LOG2E, NEG_INF)
    for i in range(t // A_QBLK):
        r0 = i * A_QBLK
        k0 = max(r0 - BAND_PAST, 0)
        k1 = r0 + A_QBLK
        q = q_ref[0, r0:k1, :]
        s = _dot_nt(q, kb[k0:k1, :]) + bias[:, A_KBLK - (k1 - k0):]
        p = jnp.exp2(s - jnp.max(s, axis=-1, keepdims=True))
        o = _dot(p.astype(BF16), vb[k0:k1, :])
        o_ref[0, r0:k1, :] = (o[:, :HD_A] / o[:, HD_A:]).astype(o_ref.dtype)


def attn_a_prompt(qa, ka, va, bias_row, b, t):
    h = qa.shape[1] // HD_A
    assert t % A_QBLK == 0 and A_QBLK % V7X_LANES == 0
    q3, k3, v3 = (a.reshape(b, t, h * HD_A) for a in (qa, ka, va))
    spec = pl.BlockSpec((1, t, HD_A), lambda bi, hi: (bi, 0, hi))
    vmem = 2 * t * HD_A * (2 + 4 + 4 + 2) + 2 * t * HD_A * 2 + 24 * MIB
    out = pl.pallas_call(
        functools.partial(_attn_a_prompt_kernel, t=t),
        grid=(b, h),
        in_specs=[spec, spec, spec, pl.BlockSpec((1, 1, A_BIAS_W), lambda bi, hi: (hi, 0, 0))],
        out_specs=spec,
        out_shape=jax.ShapeDtypeStruct((b, t, h * HD_A), BF16),
        scratch_shapes=[pltpu.VMEM((t, HD_A), BF16), pltpu.VMEM((t, 2 * HD_A), BF16)],
        compiler_params=_compiler_params(("parallel", "parallel"), vmem),
        name="attn_a_prompt",
    )(q3, k3, v3, bias_row)
    return out.reshape(b * t, h * HD_A)


def _attn_a_sample_kernel(q_ref, kn_ref, vn_ref, row_ref, mask_ref, kc_hbm, vc_hbm, o_ref, kbuf, vbuf, bias_scr, sem,
                          *, layer, heads, a_len):
    b = pl.program_id(0)
    nb = pl.num_programs(0)
    scale = HD_A ** -0.5
    t = q_ref.shape[2]

    def copies(bi, slot):
        out = []
        for h in range(heads):
            out.append(pltpu.make_async_copy(kc_hbm.at[layer, bi, :, h, :], kbuf.at[slot, h], sem.at[0, slot, h]))
            out.append(pltpu.make_async_copy(vc_hbm.at[layer, bi, :, h, :], vbuf.at[slot, h], sem.at[1, slot, h]))
        return out

    @pl.when(b == 0)
    def _():
        for k, c in enumerate(copies(b, 0)):
            c.start(priority=k % 2)
        for h in range(heads):
            bias_scr[h] = _toeplitz_bias(row_ref[h], t, a_len + t) + mask_ref[...]

    slot = b % 2

    @pl.when(b + 1 < nb)
    def _():
        for k, c in enumerate(copies(b + 1, 1 - slot)):
            c.start(priority=k % 2)

    for c in copies(b, slot):
        c.wait()
    for h in range(heads):
        q = q_ref[0, h]
        bias = bias_scr[h]
        sc = _dot_nt(q, kbuf[slot, h].astype(BF16)) * scale + bias[:, :a_len]
        sn = _dot_nt(q, kn_ref[0, h].astype(BF16)) * scale + bias[:, a_len:]
        m = jnp.maximum(jnp.max(sc, axis=-1, keepdims=True), jnp.max(sn, axis=-1, keepdims=True))
        pc = jnp.exp(sc - m)
        pn = jnp.exp(sn - m)
        l = jnp.sum(pc, axis=-1, keepdims=True) + jnp.sum(pn, axis=-1, keepdims=True)
        o = _dot(pc.astype(BF16), vbuf[slot, h].astype(BF16)) + _dot(pn.astype(BF16), vn_ref[0, h].astype(BF16))
        o_ref[0, h] = (o / l).astype(o_ref.dtype)


def _by_head(a, b, t, heads, hd):
    return a.reshape(b, t, heads, hd).transpose(0, 2, 1, 3)


def attn_a_sample(qa, ka, va, cache_k, cache_v, layer, bias_row, mask, b, t):
    heads = qa.shape[1] // HD_A
    a_len = cache_k.shape[2]
    assert a_len == BAND_PAST and a_len + t <= A_KBLK
    head_spec = pl.BlockSpec((1, heads, t, HD_A), lambda bi: (bi, 0, 0, 0))
    vmem = 2 * 2 * heads * a_len * HD_A * 4 + 2 * 4 * heads * t * HD_A * 4 + 16 * MIB
    out = pl.pallas_call(
        functools.partial(_attn_a_sample_kernel, layer=layer, heads=heads, a_len=a_len),
        grid=(b,),
        in_specs=[head_spec, head_spec, head_spec,
                  pl.BlockSpec(bias_row.shape, lambda bi: (0, 0, 0)),
                  pl.BlockSpec(mask.shape, lambda bi: (0, 0)),
                  pl.BlockSpec(memory_space=pl.ANY), pl.BlockSpec(memory_space=pl.ANY)],
        out_specs=head_spec,
        out_shape=jax.ShapeDtypeStruct((b, heads, t, HD_A), BF16),
        scratch_shapes=[pltpu.VMEM((2, heads, a_len, HD_A), F32), pltpu.VMEM((2, heads, a_len, HD_A), F32),
                        pltpu.VMEM((heads, t, a_len + t), F32), pltpu.SemaphoreType.DMA((2, 2, heads))],
        compiler_params=_compiler_params(("arbitrary",), vmem),
        name="attn_a_sample",
    )(*(_by_head(a, b, t, heads, HD_A) for a in (qa, ka, va)), bias_row, mask, cache_k, cache_v)
    return out.transpose(0, 2, 1, 3).reshape(b * t, heads * HD_A)


def _diff_lambda(lam_ref, lam_init):
    v = lam_ref[...]
    d1 = jnp.sum(v[0:1] * v[1:2], axis=-1, keepdims=True)
    d2 = jnp.sum(v[2:3] * v[3:4], axis=-1, keepdims=True)
    return jnp.exp(d1) - jnp.exp(d2) + lam_init


def _split_diff_queries(q):
    lane = lax.broadcasted_iota(jnp.int32, q.shape, 1)
    qs = q * jnp.asarray(DIFF_HD ** -0.5, q.dtype)
    zero = jnp.zeros_like(qs)
    return jnp.where(lane < DIFF_HD, qs, zero), jnp.where(lane >= DIFF_HD, qs, zero)


def _stack_diff_queries(q):
    return jnp.concatenate(_split_diff_queries(q), axis=0)


def _diff_post(o, g, post_scale):
    ms = jnp.mean(o * o, axis=-1, keepdims=True)
    return (o * lax.rsqrt(ms + EPS) * g) * post_scale


def _diff_finish(l, acc, lam, g, post_scale, tq):
    o = acc[:tq] / l[:tq] - lam * (acc[tq:] / l[tq:])
    return _diff_post(o, g, post_scale)


B_TQ = 8 * CHUNK
B_SAMPLE_SLOTS = 4


def _online_step(carry, s, v_ext):
    m, acc = carry
    m_new = jnp.maximum(m, jnp.max(s, axis=-1, keepdims=True))
    alpha = jnp.exp2(m - m_new)
    p = jnp.exp2(s - m_new)
    acc = alpha * acc + _dot(p.astype(BF16), v_ext)
    return m_new, acc


def _attn_b_prompt_kernel(lam_ref, q_ref, k_ref, v_ref, g_ref, o_ref, kb, vb, *, t, lam_init):
    tq = B_TQ
    hd = 2 * DIFF_HD
    kb[...] = (k_ref[0] * LOG2E).astype(BF16)
    vb[:, :hd] = v_ref[0].astype(BF16)
    vb[:, hd:] = jnp.ones((t, hd), BF16)
    lam = _diff_lambda(lam_ref, lam_init)
    row = lax.broadcasted_iota(jnp.int32, (tq, tq), 0)
    col = lax.broadcasted_iota(jnp.int32, (tq, tq), 1)
    diag_ok = (col // CHUNK) <= (row // CHUNK)
    for qi in range(t // tq):
        q0 = qi * tq
        qs = _split_diff_queries(q_ref[0, q0:q0 + tq, :])
        spans = ([(0, q0, False)] if q0 else []) + [(q0, tq, True)]
        outs = []
        for c in range(2):
            carry = (jnp.full((tq, 1), NEG_INF, F32), jnp.zeros((tq, 2 * hd), F32))
            for k0, width, masked in spans:
                s = _dot_nt(qs[c], kb[k0:k0 + width, :])
                if masked:
                    s = jnp.where(diag_ok, s, NEG_INF)
                carry = _online_step(carry, s, vb[k0:k0 + width, :])
            outs.append(carry[1][:, :hd] / carry[1][:, hd:])
        o = outs[0] - lam * outs[1]
        o_ref[0, q0:q0 + tq, :] = _diff_post(o, g_ref[...], 1.0 - lam_init).astype(o_ref.dtype)


def attn_b_prompt(qb, kb, vb, lam_params, subln_g, lam_init, b, t):
    hd = 2 * DIFF_HD
    heads = qb.shape[1] // hd
    assert t % B_TQ == 0
    q3, k3, v3 = (a.reshape(b, t, heads * hd) for a in (qb, kb, vb))
    spec = pl.BlockSpec((1, t, hd), lambda bi, hi: (bi, 0, hi))
    vmem = 2 * t * hd * (2 + 4 + 4 + 2) + 2 * t * hd * 2 + 32 * MIB
    out = pl.pallas_call(
        functools.partial(_attn_b_prompt_kernel, t=t, lam_init=lam_init),
        grid=(b, heads),
        in_specs=[pl.BlockSpec(lam_params.shape, lambda bi, hi: (0, 0)), spec, spec, spec,
                  pl.BlockSpec((1, hd), lambda bi, hi: (0, 0))],
        out_specs=spec,
        out_shape=jax.ShapeDtypeStruct((b, t, heads * hd), BF16),
        scratch_shapes=[pltpu.VMEM((t, hd), BF16), pltpu.VMEM((t, 2 * hd), BF16)],
        compiler_params=_compiler_params(("parallel", "parallel"), vmem),
        name="attn_b_prompt",
    )(lam_params, q3, k3, v3, subln_g.reshape(1, hd))
    return out.reshape(b * t, heads * hd)


def _attn_b_sample_kernel(lam_ref, q_ref, kn_ref, vn_ref, g_ref, kc_hbm, vc_hbm, o_ref, kbuf, vbuf, sem,
                          *, layer, heads, t, lam_init):
    b = pl.program_id(0)
    nb = pl.num_programs(0)

    def copies(bi, h, slot):
        return (pltpu.make_async_copy(kc_hbm.at[layer, bi, :, h, :], kbuf.at[slot], sem.at[0, slot]),
                pltpu.make_async_copy(vc_hbm.at[layer, bi, :, h, :], vbuf.at[slot], sem.at[1, slot]))

    def start(bi, h, slot):
        for c in copies(bi, h, slot):
            c.start()

    ahead = B_SAMPLE_SLOTS - 1

    @pl.when(b == 0)
    def _():
        for h in range(ahead):
            start(b, h, h % B_SAMPLE_SLOTS)

    lam = _diff_lambda(lam_ref, lam_init)
    for h in range(heads):
        slot = h % B_SAMPLE_SLOTS
        nxt = h + ahead
        if nxt < heads:
            start(b, nxt, nxt % B_SAMPLE_SLOTS)
        else:
            @pl.when(b + 1 < nb)
            def _():
                start(b + 1, nxt - heads, nxt % B_SAMPLE_SLOTS)
        for c in copies(b, h, slot):
            c.wait()
        q2 = _stack_diff_queries(q_ref[0, h])
        sc = _dot_nt(q2, kbuf[slot].astype(BF16))
        sn = _dot_nt(q2, kn_ref[0, h].astype(BF16))
        m = jnp.maximum(jnp.max(sc, axis=-1, keepdims=True), jnp.max(sn, axis=-1, keepdims=True))
        pc = jnp.exp(sc - m)
        pn = jnp.exp(sn - m)
        l = jnp.sum(pc, axis=-1, keepdims=True) + jnp.sum(pn, axis=-1, keepdims=True)
        acc = _dot(pc.astype(BF16), vbuf[slot].astype(BF16)) + _dot(pn.astype(BF16), vn_ref[0, h].astype(BF16))
        o_ref[0, h] = _diff_finish(l, acc, lam, g_ref[...], 1.0 - lam_init, t).astype(o_ref.dtype)


def attn_b_sample(qb, kb, vb, cache_k, cache_v, layer, lam_params, subln_g, lam_init, b, t):
    hd = 2 * DIFF_HD
    heads = qb.shape[1] // hd
    assert heads % B_SAMPLE_SLOTS == 0 and heads >= B_SAMPLE_SLOTS
    past = cache_k.shape[2]
    head_spec = pl.BlockSpec((1, heads, t, hd), lambda bi: (bi, 0, 0, 0))
    vmem = 2 * B_SAMPLE_SLOTS * past * hd * 4 + 2 * 3 * heads * t * hd * 4 + 12 * 2 * t * past * 4
    out = pl.pallas_call(
        functools.partial(_attn_b_sample_kernel, layer=layer, heads=heads, t=t, lam_init=lam_init),
        grid=(b,),
        in_specs=[pl.BlockSpec(lam_params.shape, lambda bi: (0, 0)), head_spec, head_spec, head_spec,
                  pl.BlockSpec((1, hd), lambda bi: (0, 0)),
                  pl.BlockSpec(memory_space=pl.ANY), pl.BlockSpec(memory_space=pl.ANY)],
        out_specs=head_spec,
        out_shape=jax.ShapeDtypeStruct((b, heads, t, hd), BF16),
        scratch_shapes=[pltpu.VMEM((B_SAMPLE_SLOTS, past, hd), F32), pltpu.VMEM((B_SAMPLE_SLOTS, past, hd), F32),
                        pltpu.SemaphoreType.DMA((2, B_SAMPLE_SLOTS))],
        compiler_params=_compiler_params(("arbitrary",), vmem),
        name="attn_b_sample",
    )(lam_params, *(_by_head(a, b, t, heads, hd) for a in (qb, kb, vb)), subln_g.reshape(1, hd), cache_k, cache_v)
    return out.transpose(0, 2, 1, 3).reshape(b * t, heads * hd)


def _attn_m_kernel(q_ref, k_ref, v_ref, o_ref, *, heads, hd):
    scale = hd ** -0.5
    for h in range(heads):
        sl = slice(h * hd, (h + 1) * hd)
        s = _dot_nt(q_ref[0, :, sl], k_ref[0, :, sl].astype(BF16)) * scale
        o, l = _softmax_pv(s, v_ref[0, :, sl].astype(BF16))
        o_ref[0, :, sl] = (o / l).astype(o_ref.dtype)


def attn_m_prompt(qm, mem_k, mem_v, b, t, *, tq=B_TQ):
    width = qm.shape[1]
    hd = width // H_M
    tq = min(tq, t)
    assert t % tq == 0
    n = mem_k.shape[0] // b
    q_spec = pl.BlockSpec((1, tq, width), lambda bi, qi: (bi, qi, 0))
    kv_spec = pl.BlockSpec((1, n, width), lambda bi, qi: (bi, 0, 0))
    vmem = 2 * (2 * n * width * 4 + 2 * tq * width * 2) + 8 * tq * n * 4
    out = pl.pallas_call(
        functools.partial(_attn_m_kernel, heads=H_M, hd=hd),
        grid=(b, t // tq),
        in_specs=[q_spec, kv_spec, kv_spec],
        out_specs=q_spec,
        out_shape=jax.ShapeDtypeStruct((b, t, width), BF16),
        compiler_params=_compiler_params(("parallel", "arbitrary"), vmem),
        name="attn_m_prompt",
    )(qm.reshape(b, t, width), mem_k.reshape(b, n, width), mem_v.reshape(b, n, width))
    return out.reshape(b * t, width)


def _attn_m_sample_kernel(q_ref, kc_hbm, vc_hbm, o_ref, kbuf, vbuf, sem, *, layer, heads, hd):
    b = pl.program_id(0)
    nb = pl.num_programs(0)
    scale = hd ** -0.5

    def copies(bi, slot):
        out = []
        for h in range(heads):
            out.append(pltpu.make_async_copy(kc_hbm.at[layer, bi, :, h, :], kbuf.at[slot, h], sem.at[0, slot, h]))
            out.append(pltpu.make_async_copy(vc_hbm.at[layer, bi, :, h, :], vbuf.at[slot, h], sem.at[1, slot, h]))
        return out

    @pl.when(b == 0)
    def _():
        for k, c in enumerate(copies(b, 0)):
            c.start(priority=k % 2)

    slot = b % 2

    @pl.when(b + 1 < nb)
    def _():
        for k, c in enumerate(copies(b + 1, 1 - slot)):
            c.start(priority=k % 2)

    for c in copies(b, slot):
        c.wait()
    for h in range(heads):
        s = _dot_nt(q_ref[0, h], kbuf[slot, h].astype(BF16)) * scale
        o, l = _softmax_pv(s, vbuf[slot, h].astype(BF16))
        o_ref[0, h] = (o / l).astype(o_ref.dtype)


def attn_m_sample(qm, cache_k, cache_v, layer, b, t):
    n, heads, hd = cache_k.shape[2:]
    head_spec = pl.BlockSpec((1, heads, t, hd), lambda bi: (bi, 0, 0, 0))
    vmem = 2 * 2 * heads * n * hd * 4 + 2 * 2 * heads * t * hd * 2 + 8 * MIB
    out = pl.pallas_call(
        functools.partial(_attn_m_sample_kernel, layer=layer, heads=heads, hd=hd),
        grid=(b,),
        in_specs=[head_spec, pl.BlockSpec(memory_space=pl.ANY), pl.BlockSpec(memory_space=pl.ANY)],
        out_specs=head_spec,
        out_shape=jax.ShapeDtypeStruct((b, heads, t, hd), BF16),
        scratch_shapes=[pltpu.VMEM((2, heads, n, hd), F32), pltpu.VMEM((2, heads, n, hd), F32),
                        pltpu.SemaphoreType.DMA((2, 2, heads))],
        compiler_params=_compiler_params(("arbitrary",), vmem),
        name="attn_m_sample",
    )(_by_head(qm, b, t, heads, hd), cache_k, cache_v)
    return out.transpose(0, 2, 1, 3).reshape(b * t, heads * hd)


def _merge_kernel(h_ref, oa_ref, ob_ref, om_ref, wga_ref, wgb_ref, wgm_ref, gb_ref, wbr_ref, o_ref):
    h = h_ref[...]
    acc = None
    for n, (o_n, wg_n) in enumerate(((oa_ref, wga_ref), (ob_ref, wgb_ref), (om_ref, wgm_ref))):
        gate = jax.nn.sigmoid(_dot(h, wg_n[...]) + gb_ref[n])
        term = gate * _dot(o_n[...], wbr_ref[n])
        acc = term if acc is None else acc + term
    o_ref[...] = acc.astype(o_ref.dtype)


def merge_branches(h, oa, ob, om, w_in, gate_col0, gate_b, w_br, *, tm, tn=COL_BLOCK):
    m, bw = oa.shape
    d = w_br.shape[2]
    tm = _row_tile(m, tm)
    assert d % tn == 0 and gate_col0 % tn == 0
    nj = d // tn
    g0 = gate_col0 // tn
    row_spec = lambda width: pl.BlockSpec((tm, width), lambda i, j: (i, 0))
    gate_w_specs = [pl.BlockSpec((d, tn), functools.partial(lambda i, j, n: (0, g0 + n * nj + j), n=n))
                    for n in range(N_BRANCH)]
    vmem = 2 * (tm * d * 2 + 3 * tm * bw * 2 + 3 * d * tn * 2 + 3 * bw * tn * 2 + tm * tn * 2) + 6 * tm * tn * 4
    return pl.pallas_call(
        _merge_kernel,
        grid=(m // tm, nj),
        in_specs=[row_spec(d), row_spec(bw), row_spec(bw), row_spec(bw)] + gate_w_specs
                 + [pl.BlockSpec((N_BRANCH, 1, tn), lambda i, j: (0, 0, j)),
                    pl.BlockSpec((N_BRANCH, bw, tn), lambda i, j: (0, 0, j))],
        out_specs=pl.BlockSpec((tm, tn), lambda i, j: (i, j)),
        out_shape=jax.ShapeDtypeStruct((m, d), BF16),
        compiler_params=_compiler_params(("parallel", "arbitrary"), vmem),
        name="merge_branches",
    )(h, oa, ob, om, w_in, w_in, w_in, gate_b, w_br)


def _ffn_up_kernel(h_ref, wa_ref, wb_ref, cw_ref, cb_ref, st_ref, g_ref, cn_ref, carry, *, nb, tb, blocks_per_seq):
    i = pl.program_id(0)
    j = pl.program_id(1)
    tm, tn = g_ref.shape
    if nb == 1:
        @pl.when((i % blocks_per_seq) == 0)
        def _():
            carry[j] = st_ref[0]

        trow = lax.broadcasted_iota(jnp.int32, (ROW_TILE, PROJ_SLAB), 0)
    else:
        trow = lax.broadcasted_iota(jnp.int32, (nb, tb, PROJ_SLAB), 1).reshape(tm, PROJ_SLAB)
    slabs = [slice(c0, c0 + PROJ_SLAB) for c0 in range(0, tn, PROJ_SLAB)]

    def gated(a, am1, am2, bgate, cols):
        cw = cw_ref[:, cols]
        c = cb_ref[:, cols] + am2 * cw[0:1] + am1 * cw[1:2] + a * cw[2:3]
        gelu = 0.5 * c * (1.0 + lax.erf(c * (2.0 ** -0.5)))
        return (gelu * bgate).astype(g_ref.dtype)

    if nb == 1:
        parts = FFN_ROW_PARTS if tm % (FFN_ROW_PARTS * ROW_TILE) == 0 else 1
        rp = tm // parts
        units = [(r0, cols) for cols in slabs for r0 in range(0, tm, rp)]
        dots = [(_dot(h_ref[r0:r0 + rp, :], wa_ref[:, cols]), _dot(h_ref[r0:r0 + rp, :], wb_ref[:, cols]))
                for r0, cols in units]
        prev = {}
        for (r0, cols), (a, bgate) in zip(units, dots):
            g_ref[r0:r0 + rp, cols] = gated(a, pltpu.roll(a, 1, 0), pltpu.roll(a, 2, 0), bgate, cols)
            before = carry[j, :, cols] if r0 == 0 else prev[cols.start]
            top = a[0:ROW_TILE]
            p0 = jnp.broadcast_to(before[0:1], top.shape)
            p1 = jnp.broadcast_to(before[1:2], top.shape)
            am1 = jnp.where(trow == 0, p1, pltpu.roll(top, 1, 0))
            am2 = jnp.where(trow == 0, p0, jnp.where(trow == 1, p1, pltpu.roll(top, 2, 0)))
            g_ref[r0:r0 + ROW_TILE, cols] = gated(top, am1, am2, bgate[0:ROW_TILE], cols)
            prev[cols.start] = a[rp - 2:rp]
            if r0 + rp == tm:
                carry[j, :, cols] = a[rp - 2:rp]
                cn_ref[0, :, cols] = a[rp - 2:rp]
        return

    h = h_ref[...]
    dots = [(_dot(h, wa_ref[:, cols]), _dot(h, wb_ref[:, cols])) for cols in slabs]
    for cols, (a, bgate) in zip(slabs, dots):
        st = st_ref[:, :, cols]
        p0 = jnp.broadcast_to(st[:, 0:1, :], (nb, tb, PROJ_SLAB)).reshape(tm, PROJ_SLAB)
        p1 = jnp.broadcast_to(st[:, 1:2, :], (nb, tb, PROJ_SLAB)).reshape(tm, PROJ_SLAB)
        cn_ref[:, :, cols] = a.reshape(nb, tb, PROJ_SLAB)[:, tb - 2:tb, :]
        am1 = jnp.where(trow == 0, p1, pltpu.roll(a, 1, 0))
        am2 = jnp.where(trow == 0, p0, jnp.where(trow == 1, p1, pltpu.roll(a, 2, 0)))
        g_ref[:, cols] = gated(a, am1, am2, bgate, cols)


def ffn_up(h, w_a, w_b, conv_w, conv_b, state, b, t, *, tm=ROW_BLOCK, tn=COL_BLOCK):
    m, d = h.shape
    f = w_a.shape[1]
    tm = _row_tile(m, tm)
    assert f % tn == 0 and tn % PROJ_SLAB == 0
    if tm >= t:
        assert tm % t == 0
        nb, tb, blocks_per_seq = tm // t, t, 1
    else:
        assert t % tm == 0
        nb, tb, blocks_per_seq = 1, tm, t // tm
    if nb > 1:
        seq_map = lambda i, j: (i, 0, j)
    else:
        seq_map = lambda i, j: (i // blocks_per_seq, 0, j)
    tail_map = lambda i, j: (i, 0, j)
    w_spec = pl.BlockSpec((d, tn), lambda i, j: (0, j))
    vmem = 2 * (tm * d * 2 + 2 * d * tn * 2 + tm * tn * 2 + 2 * nb * 8 * tn * 4) + 8 * tm * tn * 4
    g, conv_new = pl.pallas_call(
        functools.partial(_ffn_up_kernel, nb=nb, tb=tb, blocks_per_seq=blocks_per_seq),
        grid=(m // tm, f // tn),
        in_specs=[pl.BlockSpec((tm, d), lambda i, j: (i, 0)), w_spec, w_spec,
                  pl.BlockSpec((CONV_W, tn), lambda i, j: (0, j)),
                  pl.BlockSpec((1, tn), lambda i, j: (0, j)),
                  pl.BlockSpec((nb, CONV_W - 1, tn), seq_map)],
        out_specs=[pl.BlockSpec((tm, tn), lambda i, j: (i, j)),
                   pl.BlockSpec((nb, CONV_W - 1, tn), tail_map)],
        out_shape=[jax.ShapeDtypeStruct((m, f), BF16),
                   jax.ShapeDtypeStruct((b * blocks_per_seq, CONV_W - 1, f), F32)],
        scratch_shapes=[pltpu.VMEM((f // tn, CONV_W - 1, tn), F32)],
        compiler_params=_compiler_params(("arbitrary", "arbitrary"), vmem),
        name="ffn_up",
    )(h, w_a, w_b, conv_w, conv_b, state)
    return g, conv_new.reshape(b, blocks_per_seq, CONV_W - 1, f)[:, -1]


def _pad_cols(a, f_pad):
    return jnp.pad(a, [(0, 0)] * (a.ndim - 1) + [(0, f_pad - a.shape[-1])])


def _cast_kernel(w_ref, o_ref, *, valid_rows, valid_cols):
    rb, cb = o_ref.shape
    rows = pl.program_id(0) * rb + lax.broadcasted_iota(jnp.int32, (rb, cb), 0)
    cols = pl.program_id(1) * cb + lax.broadcasted_iota(jnp.int32, (rb, cb), 1)
    ok = (rows < valid_rows) & (cols < valid_cols)
    o_ref[...] = jnp.where(ok, w_ref[...], 0.0).astype(o_ref.dtype)


def cast_weight(w, layer, *, rb, cb, col0=0, ncols=None, out_rows=None, out_cols=None):
    _, r, c = w.shape
    ncols = c - col0 if ncols is None else ncols
    out_rows = r if out_rows is None else out_rows
    out_cols = ncols if out_cols is None else out_cols
    assert col0 % cb == 0 and out_rows % rb == 0 and out_cols % cb == 0
    c0 = col0 // cb
    last_r = (r - 1) // rb
    last_c = (col0 + ncols - 1) // cb
    in_map = lambda i, j: (layer, jnp.minimum(i, last_r), jnp.minimum(j + c0, last_c))
    return pl.pallas_call(
        functools.partial(_cast_kernel, valid_rows=r, valid_cols=ncols),
        grid=(out_rows // rb, out_cols // cb),
        in_specs=[pl.BlockSpec((None, rb, cb), in_map)],
        out_specs=pl.BlockSpec((rb, cb), lambda i, j: (i, j)),
        out_shape=jax.ShapeDtypeStruct((out_rows, out_cols), BF16),
        compiler_params=_compiler_params(("parallel", "parallel"), 2 * rb * cb * 6),
        name="cast_weight",
    )(w)


def _cast_halves_kernel(wa_ref, wb_ref, oa_ref, ob_ref, *, valid_cols):
    rb, cb = oa_ref.shape
    cols = pl.program_id(0) * cb + lax.broadcasted_iota(jnp.int32, (rb, cb), 1)
    ok = cols < valid_cols
    oa_ref[...] = jnp.where(ok, wa_ref[...], 0.0).astype(oa_ref.dtype)
    ob_ref[...] = jnp.where(ok, wb_ref[...], 0.0).astype(ob_ref.dtype)


def cast_weight_halves(w, layer, out_cols, *, cb=V7X_LANES):
    _, r, c = w.shape
    half = c // 2
    assert half % cb == 0 and out_cols % cb == 0
    nb = half // cb
    in_spec = lambda first: pl.BlockSpec((None, r, cb), lambda j: (layer, 0, first + jnp.minimum(j, nb - 1)))
    out_spec = pl.BlockSpec((r, cb), lambda j: (0, j))
    out = jax.ShapeDtypeStruct((r, out_cols), BF16)
    return pl.pallas_call(
        functools.partial(_cast_halves_kernel, valid_cols=half),
        grid=(out_cols // cb,),
        in_specs=[in_spec(0), in_spec(nb)],
        out_specs=[out_spec, out_spec],
        out_shape=[out, out],
        compiler_params=_compiler_params(("parallel",), 2 * 2 * r * cb * 6),
        name="cast_weight_halves",
    )(w, w)


def _layer_weights(l, P):
    depth, d, _ = P['w_in'].shape
    d_ff = P['w_ffn_down'].shape[1]
    f_pad = -(-d_ff // COL_BLOCK) * COL_BLOCK
    bw = d // 2
    assert d_ff % V7X_LANES == 0
    w_br = P['w_branch'].reshape(depth, N_BRANCH * bw, d)
    w_up_a, w_up_b = cast_weight_halves(P['w_ffn_up'], l, f_pad)
    return {
        'w_up_a': w_up_a,
        'w_up_b': w_up_b,
        'w_in': cast_weight(P['w_in'], l, rb=d, cb=COL_BLOCK),
        'w_mem_kv': cast_weight(P['w_mem_kv'], l, rb=d, cb=COL_BLOCK),
        'w_branch': cast_weight(w_br, l, rb=bw, cb=d).reshape(N_BRANCH, bw, d),
        'w_out': cast_weight(P['w_out'], l, rb=d, cb=COL_BLOCK),
        'w_down': cast_weight(P['w_ffn_down'], l, rb=COL_BLOCK, cb=d, out_rows=f_pad),
        'conv_w': _pad_cols(P['ffn_conv_w'][l], f_pad),
        'conv_b': _pad_cols(P['ffn_conv_b'][l].reshape(1, d_ff), f_pad),
        'f_pad': f_pad,
        'd_ff': d_ff,
    }


def _mixer_inputs(h, W, P, l, rope_tab, tm, final=None):
    gains = (P['a_q_norm_g'][l], P['a_k_norm_g'][l], P['b_q_norm_g'][l], P['b_k_norm_g'][l], P['m_q_norm_g'][l])
    return proj_in(h, W['w_in'], gains, rope_tab, tm=tm, final=final)


def _finish_layer(x, h, outs, W, P, l, state, b, t, tm):
    d = x.shape[1]
    gate_b = P['gate_b'][l].reshape(N_BRANCH, 1, d)
    merged = merge_branches(h, *outs, W['w_in'], 7 * (d // 2), gate_b, W['w_branch'], tm=tm)
    res_spec = lambda tn: pl.BlockSpec((tm, tn), lambda i, j: (i, j))
    x, h = proj_out_norm(merged, W['w_out'], x, P['norm_ffn_g'][l], tm=tm)
    g, conv_new = ffn_up(h, W['w_up_a'], W['w_up_b'], W['conv_w'], W['conv_b'], state, b, t, tm=tm)
    x = _proj_call(_proj_residual_kernel, g, W['w_down'], 0, d, F32, [(x, res_spec(COL_BLOCK))], tm=tm, name="ffn_down")
    return x, conv_new[:, :, :W['d_ff']]


def kernel(x_prompt, x_sample, cache_a_k, cache_a_v, cache_b_k, cache_b_v, cache_mem_k, cache_mem_v, state_ffn_conv, mem_prompt, norm_mix_g, w_in, a_q_norm_g, a_k_norm_g, a_rel_bias, b_q_norm_g, b_k_norm_g, b_lam_q1, b_lam_k1, b_lam_q2, b_lam_k2, b_subln_g, m_q_norm_g, m_k_norm_g, mem_norm_g, w_mem_kv, gate_b, w_branch, w_out, norm_ffn_g, w_ffn_up, ffn_conv_w, ffn_conv_b, w_ffn_down):
    P = {'w_in': w_in, 'a_q_norm_g': a_q_norm_g, 'a_k_norm_g': a_k_norm_g, 'b_q_norm_g': b_q_norm_g,
         'b_k_norm_g': b_k_norm_g, 'm_q_norm_g': m_q_norm_g, 'm_k_norm_g': m_k_norm_g, 'w_mem_kv': w_mem_kv,
         'gate_b': gate_b, 'w_branch': w_branch, 'w_out': w_out, 'norm_ffn_g': norm_ffn_g,
         'w_ffn_up': w_ffn_up, 'ffn_conv_w': ffn_conv_w, 'ffn_conv_b': ffn_conv_b, 'w_ffn_down': w_ffn_down}
    bp, tp, d = x_prompt.shape
    bs, ts, _ = x_sample.shape
    depth = w_in.shape[0]
    bw = d // 2
    past = cache_b_k.shape[2]
    a_len = cache_a_k.shape[2]
    n_mem = mem_prompt.shape[1]
    a_keep = min(BAND_PAST, tp)
    h_a = bw // HD_A
    h_b = bw // (2 * DIFF_HD)
    hd_m = bw // H_M
    mp, ms = bp * tp, bs * ts
    tm_p = _row_tile(mp, ROW_BLOCK)
    tm_s = _row_tile(ms, ROW_BLOCK)
    assert tm_p <= tp and tp % tm_p == 0 or tm_p % tp == 0

    pos_s = past + np.arange(ts)
    key_pos_a = np.concatenate([past - a_len + np.arange(a_len), pos_s])
    q_chunk_s = pos_s // CHUNK
    k_chunk_a = key_pos_a // CHUNK
    valid_a_s = (k_chunk_a[None, :] <= q_chunk_s[:, None]) & (k_chunk_a[None, :] >= q_chunk_s[:, None] - BAND_CHUNKS)
    mask_a_s = jnp.asarray(np.where(valid_a_s, 0.0, NEG_INF), F32)
    key_pos_b = np.concatenate([np.arange(past), pos_s])
    valid_b_s = (key_pos_b // CHUNK)[None, :] <= q_chunk_s[:, None]
    assert valid_b_s.all(), "sample queries are expected to see every cached and new differential key"

    rope_p = _rope_table(jnp.arange(max(tp, tm_p), dtype=jnp.int32) % tp)
    rope_s = _rope_table(past + (jnp.arange(max(ts, tm_s), dtype=jnp.int32) % ts))

    xp = x_prompt.reshape(mp, d)
    xs = x_sample.reshape(ms, d)
    mem2d = mem_prompt.reshape(bp * n_mem, d)
    outs = {k: [] for k in ('mk_p', 'mv_p', 'cv_p', 'ak_s', 'av_s', 'bk_s', 'bv_s', 'cv_s')}
    kv_prompt = None
    for l in range(depth):
        W = _layer_weights(l, P)
        lam_init = 0.8 - 0.6 * math.exp(-0.3 * l)
        lam_params = jnp.stack([b_lam_q1[l], b_lam_k1[l], b_lam_q2[l], b_lam_k2[l]]).astype(F32)
        bias_row = _rel_bias_row(a_rel_bias[l])

        h = rmsnorm_cast(xp, norm_mix_g[l])
        final = dict(batch=bp, seq_len=tp, a_keep=a_keep, layer=l, depth=depth, prev=kv_prompt)
        qa, ka, va, qb, kb, vb, qm, *kv_prompt = _mixer_inputs(h, W, P, l, rope_p, tm_p, final)
        oa = attn_a_prompt(qa, ka, va, bias_row, bp, tp)
        ob = attn_b_prompt(qb, kb, vb, lam_params, b_subln_g[l], lam_init, bp, tp)
        hm = rmsnorm_cast(mem2d, mem_norm_g[l])
        tm_m = _row_tile(bp * n_mem, ROW_BLOCK)
        mk = _proj_call(functools.partial(_proj_headnorm_kernel, hd=hd_m), hm, W['w_mem_kv'], 0, bw, F32,
                        [(m_k_norm_g[l].reshape(1, hd_m), pl.BlockSpec((1, hd_m), lambda i, j: (0, 0)))],
                        tm=tm_m, name="proj_mk")
        mv = _proj_call(_proj_plain_kernel, hm, W['w_mem_kv'], bw, bw, F32, [], tm=tm_m, name="proj_mv")
        om = attn_m_prompt(qm, mk, mv, bp, tp)
        zeros_state = jnp.zeros((bp, CONV_W - 1, W['f_pad']), F32)
        xp, conv_new = _finish_layer(xp, h, (oa, ob, om), W, P, l, zeros_state, bp, tp, tm_p)
        outs['mk_p'].append(mk.reshape(bp, n_mem, H_M, hd_m))
        outs['mv_p'].append(mv.reshape(bp, n_mem, H_M, hd_m))
        outs['cv_p'].append(conv_new)

        h = rmsnorm_cast(xs, norm_mix_g[l])
        qa, ka, va, qb, kb, vb, qm = _mixer_inputs(h, W, P, l, rope_s, tm_s)
        oa = attn_a_sample(qa, ka, va, cache_a_k, cache_a_v, l, bias_row, mask_a_s, bs, ts)
        ob = attn_b_sample(qb, kb, vb, cache_b_k, cache_b_v, l, lam_params, b_subln_g[l], lam_init, bs, ts)
        om = attn_m_sample(qm, cache_mem_k, cache_mem_v, l, bs, ts)
        state = _pad_cols(state_ffn_conv[l], W['f_pad'])
        xs, conv_new = _finish_layer(xs, h, (oa, ob, om), W, P, l, state, bs, ts, tm_s)
        outs['ak_s'].append(ka.reshape(bs, ts, h_a, HD_A))
        outs['av_s'].append(va.reshape(bs, ts, h_a, HD_A))
        outs['bk_s'].append(kb.reshape(bs, ts, h_b, 2 * DIFF_HD))
        outs['bv_s'].append(vb.reshape(bs, ts, h_b, 2 * DIFF_HD))
        outs['cv_s'].append(conv_new)

    stack = lambda k: jnp.stack(outs[k])
    ak_p, av_p, bk_p, bv_p = kv_prompt
    return (xp.reshape(bp, tp, d), xs.reshape(bs, ts, d),
            ak_p, av_p, bk_p, bv_p, stack('mk_p'), stack('mv_p'), stack('cv_p'),
            stack('ak_s'), stack('av_s'), stack('bk_s'), stack('bv_s'), stack('cv_s'))
```
